```python
import jax, jax.numpy as jnp
from jax import lax
import numpy as np

D_MODEL = 2048
BATCH = 4
SEQ = 2048
DEPTH = 4
DEC_BATCH = 8
DEC_SEQ = 1
PAST_LEN = 16384
PAGE_SIZE = 128

W_A = 1024
CONV_W = 3
W_B = 1024
CHUNK = 128
N_B_GROUPS = 4
N_C_HEADS = 4
HEAD_DIM = 128
ROT_DIM = HEAD_DIM // 4
ROPE_THETA = 500000.0
DIL_GROUPS = ((128, 1), (512, 4), (2048, 16))
N_DIL = len(DIL_GROUPS)
QB = 128
C_WIDTH = N_C_HEADS * HEAD_DIM
QKV_W = N_DIL * C_WIDTH
D_FF = ((-(-8 * D_MODEL // 3) + 255) // 256) * 256
IN_WIDTH = 3 * W_A + 2 * W_B + 3 * QKV_W + 3 * D_MODEL
EPS = 1e-6

kernel_name = "hybrid_gated_conv_gmlp_dilated_attn_decode_step"


def rmsnorm(x, g):
    xf = x.astype(jnp.float32)
    y = xf * lax.rsqrt(jnp.mean(xf * xf, axis=-1, keepdims=True) + EPS)
    return (y * g.astype(jnp.float32)).astype(x.dtype)


def layernorm(x, g, b):
    xf = x.astype(jnp.float32)
    mu = jnp.mean(xf, axis=-1, keepdims=True)
    xc = xf - mu
    var = jnp.mean(xc * xc, axis=-1, keepdims=True)
    y = xc * lax.rsqrt(var + EPS) * g.astype(jnp.float32) + b.astype(jnp.float32)
    return y.astype(x.dtype)


def rope(x, pos):
    inv_freq = ROPE_THETA ** (-jnp.arange(0, ROT_DIM, 2, dtype=jnp.float32) / ROT_DIM)
    ang = pos.astype(jnp.float32)[:, None] * inv_freq[None, :]
    cos = jnp.cos(ang)[None, :, None, :]
    sin = jnp.sin(ang)[None, :, None, :]
    xf = x.astype(jnp.float32)
    x1 = xf[..., : ROT_DIM // 2]
    x2 = xf[..., ROT_DIM // 2: ROT_DIM]
    out = jnp.concatenate([x1 * cos - x2 * sin, x2 * cos + x1 * sin, xf[..., ROT_DIM:]], axis=-1)
    return out.astype(x.dtype)


def split_proj(h, w_in):
    sizes = (W_A, W_A, W_A, W_B, W_B, QKV_W, QKV_W, QKV_W, D_MODEL, D_MODEL, D_MODEL)
    points, acc = [], 0
    for s in sizes[:-1]:
        acc += s
        points.append(acc)
    proj = jnp.einsum('bsd,de->bse', h, w_in)
    return jnp.split(proj, points, axis=-1)


def dwconv(z_ext, conv_w):
    T = z_ext.shape[1] - (CONV_W - 1)
    out = conv_w[0] * z_ext[:, 0:T]
    for j in range(1, CONV_W):
        out = out + conv_w[j] * z_ext[:, j:j + T]
    return out


def spatial_gate_prompt(vn, w_s, b_s):
    B, S, _ = vn.shape
    vc = vn.reshape(B, S // CHUNK, CHUNK, N_B_GROUPS, W_B // N_B_GROUPS)
    s = jnp.einsum('gij,bcjgd->bcigd', jnp.tril(w_s), vc) + b_s.T[None, None, :, :, None]
    return s.reshape(B, S, W_B)


def spatial_gate_sample(vn, w_s, b_s):
    B, T, _ = vn.shape
    vc = vn.reshape(B, T, N_B_GROUPS, W_B // N_B_GROUPS)
    s = jnp.einsum('gij,bjgd->bigd', jnp.tril(w_s[:, :T, :T]), vc) + b_s[:, :T].T[None, :, :, None]
    return s.reshape(B, T, W_B)


def banded_attn(q, k, v, n_back):
    N, L, H, hd = q.shape
    nb = L // QB
    qb = q.reshape(N, nb, QB, H, hd)

    def with_prev(t):
        tb = t.reshape(N, nb, QB, H, hd)
        tp = jnp.concatenate([jnp.zeros_like(tb[:, :1]), tb], axis=1)
        return jnp.concatenate([tp[:, :-1], tp[:, 1:]], axis=2)

    k2, v2 = with_prev(k), with_prev(v)
    s = jnp.einsum('ncqhd,nckhd->nhcqk', qb, k2, preferred_element_type=jnp.float32) * (hd ** -0.5)
    blk = jnp.arange(nb)[:, None]
    qpos = blk * QB + jnp.arange(QB)[None, :]
    kpos = (blk - 1) * QB + jnp.arange(2 * QB)[None, :]
    dist = qpos[:, :, None] - kpos[:, None, :]
    mask = (dist >= 0) & (dist <= n_back) & (kpos[:, None, :] >= 0)
    s = jnp.where(mask, s, -jnp.inf)
    lse = jax.nn.logsumexp(s, axis=-1)
    p = jnp.exp(s - lse[..., None])
    o = jnp.einsum('nhcqk,nckhd->ncqhd', p.astype(v.dtype), v2).reshape(N, L, H, hd)
    return o, lse.transpose(0, 2, 3, 1).reshape(N, L, H)


def dilated_attn_prompt(q, k, v, window, dil):
    B, S, H, hd = q.shape
    L = S // dil
    Lp = -(-L // QB) * QB

    def to_classes(t):
        t = t.reshape(B, L, dil, H, hd).transpose(0, 2, 1, 3, 4).reshape(B * dil, L, H, hd)
        return jnp.pad(t, ((0, 0), (0, Lp - L), (0, 0), (0, 0)))

    o, lse = banded_attn(to_classes(q), to_classes(k), to_classes(v), window // dil)
    o = o[:, :L].reshape(B, dil, L, H, hd).transpose(0, 2, 1, 3, 4).reshape(B, S, H, hd)
    lse = lse[:, :L].reshape(B, dil, L, H).transpose(0, 2, 1, 3).reshape(B, S, H)
    return o, lse


def dilated_attn_sample(q, k_new, v_new, kv_buf, window, dil):
    Lbuf = kv_buf.shape[1]
    T = q.shape[1]
    k_all = jnp.concatenate([kv_buf[:, :, 0].astype(k_new.dtype), k_new], axis=1)
    v_all = jnp.concatenate([kv_buf[:, :, 1].astype(v_new.dtype), v_new], axis=1)
    n_keys = window // dil + 1
    idx = Lbuf + jnp.arange(T)[:, None] - dil * jnp.arange(n_keys)[None, :]
    valid = idx >= 0
    idx = jnp.maximum(idx, 0)
    kg = jnp.take(k_all, idx, axis=1)
    vg = jnp.take(v_all, idx, axis=1)
    s = jnp.einsum('bthd,btjhd->bhtj', q, kg, preferred_element_type=jnp.float32) * (HEAD_DIM ** -0.5)
    s = jnp.where(valid[None, None], s, -jnp.inf)
    lse = jax.nn.logsumexp(s, axis=-1)
    p = jnp.exp(s - lse[..., None])
    o = jnp.einsum('bhtj,btjhd->bthd', p.astype(vg.dtype), vg)
    return o, lse.transpose(0, 2, 1)


def combine_groups(outs, lses):
    w = jax.nn.softmax(jnp.stack(lses, axis=0), axis=0)
    o = jnp.einsum('gbsh,gbshd->bshd', w, jnp.stack(outs, axis=0).astype(jnp.float32))
    B, S = o.shape[:2]
    return o.reshape(B, S, C_WIDTH).astype(outs[0].dtype)


def mixer_heads(q, k, v, pos):
    B, S, _ = q.shape
    q = rope(q.reshape(B, S, N_DIL * N_C_HEADS, HEAD_DIM), pos).reshape(B, S, N_DIL, N_C_HEADS, HEAD_DIM)
    k = rope(k.reshape(B, S, N_DIL * N_C_HEADS, HEAD_DIM), pos).reshape(B, S, N_DIL, N_C_HEADS, HEAD_DIM)
    v = v.reshape(B, S, N_DIL, N_C_HEADS, HEAD_DIM)
    return q, k, v


def merge_out(x, y_a, y_b, y_c, ga, gb, gc, lp):
    m = (jax.nn.sigmoid(ga) * (y_a @ lp['w_a_out'])
         + jax.nn.sigmoid(gb) * (y_b @ lp['w_b_out'])
         + jax.nn.sigmoid(gc) * (y_c @ lp['w_c_out']))
    return x + rmsnorm(m @ lp['w_o'], lp['g_post_mix'])


def ffn_block(x, lp):
    h = rmsnorm(x, lp['g_pre_ffn'])
    gate, up = jnp.split(h @ lp['w_ffn_in'], 2, axis=-1)
    f = (jax.nn.silu(gate) * up) @ lp['w_ffn_out']
    return x + rmsnorm(f, lp['g_post_ffn'])


def layer_prompt(x, lp):
    S = x.shape[1]
    h = rmsnorm(x, lp['g_pre_mix'])
    a_b, a_c, a_x, b_u, b_v, q, k, v, ga, gb, gc = split_proj(h, lp['w_in'])
    z = a_c * a_x
    z_ext = jnp.pad(z, ((0, 0), (CONV_W - 1, 0), (0, 0)))
    y_a = a_b * dwconv(z_ext, lp['conv_w'])
    conv_state = z_ext[:, -(CONV_W - 1):]
    vn = layernorm(b_v, lp['ln_g'], lp['ln_b'])
    y_b = b_u * spatial_gate_prompt(vn, lp['w_s'], lp['b_s'])
    q, k, v = mixer_heads(q, k, v, jnp.arange(S, dtype=jnp.int32))
    outs, lses, kv_states = [], [], []
    for g, (win, dil) in enumerate(DIL_GROUPS):
        o, lse = dilated_attn_prompt(q[:, :, g], k[:, :, g], v[:, :, g], win, dil)
        outs.append(o)
        lses.append(lse)
        keep = min(win, S)
        kv_states.append(jnp.stack([k[:, S - keep:, g], v[:, S - keep:, g]], axis=2))
    y_c = combine_groups(outs, lses)
    x = merge_out(x, y_a, y_b, y_c, ga, gb, gc, lp)
    x = ffn_block(x, lp)
    return x, conv_state, kv_states


def layer_sample(x, conv_buf, kv_bufs, lp):
    T = x.shape[1]
    h = rmsnorm(x, lp['g_pre_mix'])
    a_b, a_c, a_x, b_u, b_v, q, k, v, ga, gb, gc = split_proj(h, lp['w_in'])
    z = a_c * a_x
    z_ext = jnp.concatenate([conv_buf.astype(z.dtype), z], axis=1)
    y_a = a_b * dwconv(z_ext, lp['conv_w'])
    conv_state = z_ext[:, -(CONV_W - 1):]
    vn = layernorm(b_v, lp['ln_g'], lp['ln_b'])
    y_b = b_u * spatial_gate_sample(vn, lp['w_s'], lp['b_s'])
    q, k, v = mixer_heads(q, k, v, PAST_LEN + jnp.arange(T, dtype=jnp.int32))
    outs, lses, kv_rows = [], [], []
    for g, (win, dil) in enumerate(DIL_GROUPS):
        o, lse = dilated_attn_sample(q[:, :, g], k[:, :, g], v[:, :, g], kv_bufs[g], win, dil)
        outs.append(o)
        lses.append(lse)
        kv_rows.append(jnp.stack([k[:, :, g], v[:, :, g]], axis=2))
    y_c = combine_groups(outs, lses)
    x = merge_out(x, y_a, y_b, y_c, ga, gb, gc, lp)
    x = ffn_block(x, lp)
    return x, conv_state, kv_rows, vn


def setup_inputs(seed: int = 0) -> dict:
    key = jax.random.key(seed)
    ks = jax.random.split(key, 24)
    nrm = jax.random.normal
    f32 = jnp.float32
    d = D_MODEL
    inp = {}
    inp['x_prompt'] = nrm(ks[0], (BATCH, SEQ, d), f32)
    inp['x_sample'] = nrm(ks[1], (DEC_BATCH, DEC_SEQ, d), f32)
    inp['state_conv'] = nrm(ks[2], (DEPTH, DEC_BATCH, CONV_W - 1, W_A), f32)
    inp['cache_kv_w128'] = nrm(ks[3], (DEPTH, DEC_BATCH, min(DIL_GROUPS[0][0], PAST_LEN), 2, N_C_HEADS, HEAD_DIM), f32)
    inp['cache_kv_w512'] = nrm(ks[4], (DEPTH, DEC_BATCH, min(DIL_GROUPS[1][0], PAST_LEN), 2, N_C_HEADS, HEAD_DIM), f32)
    inp['cache_kv_w2048'] = nrm(ks[5], (DEPTH, DEC_BATCH, min(DIL_GROUPS[2][0], PAST_LEN), 2, N_C_HEADS, HEAD_DIM), f32)
    inp['g_pre_mix'] = 1.0 + 0.05 * nrm(ks[6], (DEPTH, d), f32)
    inp['w_in'] = nrm(ks[7], (DEPTH, d, IN_WIDTH), f32) * d ** -0.5
    inp['conv_w'] = nrm(ks[8], (DEPTH, CONV_W, W_A), f32) * CONV_W ** -0.5
    inp['ln_g'] = 1.0 + 0.05 * nrm(ks[9], (DEPTH, W_B), f32)
    inp['ln_b'] = 0.05 * nrm(ks[10], (DEPTH, W_B), f32)
    inp['w_s'] = nrm(ks[11], (DEPTH, N_B_GROUPS, CHUNK, CHUNK), f32) * CHUNK ** -0.5
    inp['b_s'] = 1.0 + 0.1 * nrm(ks[12], (DEPTH, N_B_GROUPS, CHUNK), f32)
    inp['w_a_out'] = nrm(ks[13], (DEPTH, W_A, d), f32) * W_A ** -0.5
    inp['w_b_out'] = nrm(ks[14], (DEPTH, W_B, d), f32) * W_B ** -0.5
    inp['w_c_out'] = nrm(ks[15], (DEPTH, C_WIDTH, d), f32) * C_WIDTH ** -0.5
    inp['w_o'] = nrm(ks[16], (DEPTH, d, d), f32) * d ** -0.5
    inp['g_post_mix'] = 1.0 + 0.05 * nrm(ks[17], (DEPTH, d), f32)
    inp['g_pre_ffn'] = 1.0 + 0.05 * nrm(ks[18], (DEPTH, d), f32)
    inp['w_ffn_in'] = nrm(ks[19], (DEPTH, d, 2 * D_FF), f32) * d ** -0.5
    inp['w_ffn_out'] = nrm(ks[20], (DEPTH, D_FF, d), f32) * D_FF ** -0.5
    inp['g_post_ffn'] = 1.0 + 0.05 * nrm(ks[21], (DEPTH, d), f32)
    return inp


def reference(x_prompt, x_sample, state_conv, cache_kv_w128, cache_kv_w512, cache_kv_w2048,
              g_pre_mix, w_in, conv_w, ln_g, ln_b, w_s, b_s, w_a_out, w_b_out, w_c_out, w_o,
              g_post_mix, g_pre_ffn, w_ffn_in, w_ffn_out, g_post_ffn):
    kv_caches = (cache_kv_w128, cache_kv_w512, cache_kv_w2048)
    xp, xs = x_prompt, x_sample
    conv_p, conv_s, vchunk_s = [], [], []
    kv_p = [[] for _ in DIL_GROUPS]
    kv_s = [[] for _ in DIL_GROUPS]
    for l in range(DEPTH):
        lp = dict(g_pre_mix=g_pre_mix[l], w_in=w_in[l], conv_w=conv_w[l], ln_g=ln_g[l], ln_b=ln_b[l],
                  w_s=w_s[l], b_s=b_s[l], w_a_out=w_a_out[l], w_b_out=w_b_out[l], w_c_out=w_c_out[l],
                  w_o=w_o[l], g_post_mix=g_post_mix[l], g_pre_ffn=g_pre_ffn[l], w_ffn_in=w_ffn_in[l],
                  w_ffn_out=w_ffn_out[l], g_post_ffn=g_post_ffn[l])
        xp, cst_p, kvst_p = layer_prompt(xp, lp)
        xs, cst_s, kvrow_s, vn_s = layer_sample(xs, state_conv[l], [c[l] for c in kv_caches], lp)
        conv_p.append(cst_p)
        conv_s.append(cst_s)
        vchunk_s.append(vn_s)
        for g in range(N_DIL):
            kv_p[g].append(kvst_p[g])
            kv_s[g].append(kvrow_s[g])
    y_prompt, y_sample = xp, xs
    conv_state_prompt = jnp.stack(conv_p, axis=0)
    kv_w128_prompt = jnp.stack(kv_p[0], axis=0)
    kv_w512_prompt = jnp.stack(kv_p[1], axis=0)
    kv_w2048_prompt = jnp.stack(kv_p[2], axis=0)
    conv_state_sample = jnp.stack(conv_s, axis=0)
    kv_w128_sample = jnp.stack(kv_s[0], axis=0)
    kv_w512_sample = jnp.stack(kv_s[1], axis=0)
    kv_w2048_sample = jnp.stack(kv_s[2], axis=0)
    v_chunk_sample = jnp.stack(vchunk_s, axis=0)
    return (y_prompt, y_sample, conv_state_prompt, kv_w128_prompt, kv_w512_prompt, kv_w2048_prompt,
            conv_state_sample, kv_w128_sample, kv_w512_sample, kv_w2048_sample, v_chunk_sample)
```

```python
import functools

import jax
import jax.numpy as jnp
from jax import lax
from jax.experimental import pallas as pl
from jax.experimental.pallas import tpu as pltpu

D_MODEL = 2048
BATCH = 4
SEQ = 2048
DEPTH = 4
DEC_BATCH = 8
PAST_LEN = 16384
W_A = 1024
CONV_W = 3
W_B = 1024
CHUNK = 128
N_B_GROUPS = 4
B_GROUP_W = W_B // N_B_GROUPS
N_C_HEADS = 4
HEAD_DIM = 128
ROT_DIM = HEAD_DIM // 4
ROPE_THETA = 500000.0
DIL_GROUPS = ((128, 1), (512, 4), (2048, 16))
N_DIL = len(DIL_GROUPS)
QB = 128
C_WIDTH = N_C_HEADS * HEAD_DIM
QKV_W = N_DIL * C_WIDTH
D_FF = ((-(-8 * D_MODEL // 3) + 255) // 256) * 256
IN_WIDTH = 3 * W_A + 2 * W_B + 3 * QKV_W + 3 * D_MODEL
EPS = 1e-6

GATE_W = 3 * D_MODEL
OFF_AB = GATE_W
OFF_Q = GATE_W + 3 * W_A + 2 * W_B
OFF_K = OFF_Q + QKV_W
OFF_V = OFF_K + QKV_W
ORIG_GATE_OFF = IN_WIDTH - GATE_W

V7X_VMEM_LIMIT = 60 * 1024 * 1024
NEG_BIG = -1e30

F32 = jnp.float32
BF16 = jnp.bfloat16


def _params(*sem):
    return pltpu.CompilerParams(dimension_semantics=sem, vmem_limit_bytes=V7X_VMEM_LIMIT)


def _rms(x, g):
    return x * lax.rsqrt(jnp.mean(x * x, axis=-1, keepdims=True) + EPS) * g


def _inproj_body(x_ref, g_ref, w_ref, c_ref, s1_ref, s2_ref, o_ref, h_ref, *, rope_lo, rope_hi):
    j = pl.program_id(1)

    @pl.when(j == 0)
    def _():
        h_ref[...] = _rms(x_ref[...], g_ref[...]).astype(BF16)

    acc = jnp.dot(h_ref[...], w_ref[...], preferred_element_type=F32)
    is_rope = jnp.logical_and(j >= rope_lo, j < rope_hi)

    @pl.when(is_rope)
    def _():
        for h in range(N_C_HEADS):
            sl = slice(h * HEAD_DIM, (h + 1) * HEAD_DIM)
            a = acc[:, sl]
            o_ref[:, sl] = (a * c_ref[...]
                            + pltpu.roll(a, HEAD_DIM - ROT_DIM // 2, 1) * s1_ref[...]
                            + pltpu.roll(a, ROT_DIM // 2, 1) * s2_ref[...])

    @pl.when(jnp.logical_not(is_rope))
    def _():
        o_ref[...] = acc


def _inproj(x, g, w, tabs, tm, pos_blocks):
    m = x.shape[0]
    tn = C_WIDTH
    nj = IN_WIDTH // tn
    tab_spec = pl.BlockSpec((tm, HEAD_DIM), lambda i, j: (i % pos_blocks, 0))
    return pl.pallas_call(
        functools.partial(_inproj_body, rope_lo=OFF_Q // tn, rope_hi=OFF_V // tn),
        grid=(m // tm, nj),
        in_specs=[
            pl.BlockSpec((tm, D_MODEL), lambda i, j: (i, 0)),
            pl.BlockSpec((1, D_MODEL), lambda i, j: (0, 0)),
            pl.BlockSpec((D_MODEL, tn), lambda i, j: (0, j)),
            tab_spec, tab_spec, tab_spec,
        ],
        out_specs=pl.BlockSpec((tm, tn), lambda i, j: (i, j)),
        out_shape=jax.ShapeDtypeStruct((m, IN_WIDTH), F32),
        scratch_shapes=[pltpu.VMEM((tm, D_MODEL), BF16)],
        compiler_params=_params("arbitrary", "arbitrary"),
        name="inproj",
    )(x, g, w, *tabs)


def _rope_tables(pos):
    inv_freq = ROPE_THETA ** (-jnp.arange(0, ROT_DIM, 2, dtype=jnp.float32) / ROT_DIM)
    ang = pos.astype(jnp.float32)[:, None] * inv_freq[None, :]
    cos, sin = jnp.cos(ang), jnp.sin(ang)
    n = pos.shape[0]
    half = ROT_DIM // 2
    rest = HEAD_DIM - ROT_DIM
    c = jnp.concatenate([cos, cos, jnp.ones((n, rest), F32)], axis=1)
    s1 = jnp.concatenate([-sin, jnp.zeros((n, half + rest), F32)], axis=1)
    s2 = jnp.concatenate([jnp.zeros((n, half), F32), sin, jnp.zeros((n, rest), F32)], axis=1)
    return c, s1, s2


def _mix_ab_body(ab_ref, ac_ref, ax_ref, bu_ref, bv_ref, cw_ref, lng_ref, lnb_ref, ws_ref, bs_ref,
                 ya_ref, yb_ref, cs_ref, zs_ref):
    ts = ab_ref.shape[0]

    @pl.when(pl.program_id(1) == 0)
    def _():
        zs_ref[0:8, :] = jnp.zeros((8, W_A), F32)

    z = ac_ref[...] * ax_ref[...]
    zs_ref[8:8 + ts, :] = z
    z1 = zs_ref[7:7 + ts, :]
    z2 = zs_ref[6:6 + ts, :]
    conv = cw_ref[0:1, :] * z2 + cw_ref[1:2, :] * z1 + cw_ref[2:3, :] * z
    ya_ref[...] = (ab_ref[...] * conv).astype(BF16)
    tail = zs_ref[ts:ts + 8, :]
    zs_ref[0:8, :] = tail
    cs_ref[0] = tail

    v = bv_ref[...]
    xc = v - jnp.mean(v, axis=-1, keepdims=True)
    var = jnp.mean(xc * xc, axis=-1, keepdims=True)
    vn = (xc * lax.rsqrt(var + EPS) * lng_ref[...] + lnb_ref[...]).astype(BF16)
    row = lax.broadcasted_iota(jnp.int32, (CHUNK, CHUNK), 0)
    col = lax.broadcasted_iota(jnp.int32, (CHUNK, CHUNK), 1)
    for g in range(N_B_GROUPS):
        wg = jnp.where(row >= col, ws_ref[g], 0.0).astype(BF16)
        bcol = bs_ref[:, g:g + 1]
        gs = slice(g * B_GROUP_W, (g + 1) * B_GROUP_W)
        for c in range(ts // CHUNK):
            rs = slice(c * CHUNK, (c + 1) * CHUNK)
            sg = jnp.dot(wg, vn[rs, gs], preferred_element_type=F32) + bcol
            yb_ref[rs, gs] = (bu_ref[rs, gs] * sg).astype(BF16)


def _mix_ab(proj, cw, lng, lnb, ws, bs_t, ts):
    nb = SEQ // ts
    blk = OFF_AB // W_A

    def seg(k):
        return pl.BlockSpec((ts, W_A), lambda b, s: (b * nb + s, blk + k))

    def full(shape):
        return pl.BlockSpec(shape, lambda b, s: (0,) * len(shape))

    rows = pl.BlockSpec((ts, W_A), lambda b, s: (b * nb + s, 0))
    return pl.pallas_call(
        _mix_ab_body,
        grid=(BATCH, nb),
        in_specs=[seg(0), seg(1), seg(2), seg(3), seg(4),
                  full((CONV_W, W_A)), full((1, W_B)), full((1, W_B)),
                  full((N_B_GROUPS, CHUNK, CHUNK)), full((CHUNK, N_B_GROUPS))],
        out_specs=[rows, rows, pl.BlockSpec((1, 8, W_A), lambda b, s: (b, 0, 0))],
        out_shape=[jax.ShapeDtypeStruct((BATCH * SEQ, W_A), BF16),
                   jax.ShapeDtypeStruct((BATCH * SEQ, W_B), BF16),
                   jax.ShapeDtypeStruct((BATCH, 8, W_A), F32)],
        scratch_shapes=[pltpu.VMEM((ts + 8, W_A), F32)],
        compiler_params=_params("arbitrary", "arbitrary"),
        name="mix_ab",
    )(proj, proj, proj, proj, proj, cw, lng, lnb, ws, bs_t)


def _attn_body(q_ref, kp_ref, kc_ref, vp_ref, vc_ref, o_ref, l_ref):
    c = pl.program_id(2)
    row = lax.broadcasted_iota(jnp.int32, (QB, QB), 0)
    col = lax.broadcasted_iota(jnp.int32, (QB, QB), 1)
    mask_c = col <= row
    mask_p = col >= row + jnp.where(c > 0, 0, QB)
    scale = HEAD_DIM ** -0.5
    nt = (((1,), (1,)), ((), ()))
    for h in range(N_C_HEADS):
        sl = slice(h * HEAD_DIM, (h + 1) * HEAD_DIM)
        q = q_ref[0, :, sl].astype(BF16)
        sp = lax.dot_general(q, kp_ref[0, :, sl].astype(BF16), nt, preferred_element_type=F32) * scale
        sc = lax.dot_general(q, kc_ref[0, :, sl].astype(BF16), nt, preferred_element_type=F32) * scale
        sp = jnp.where(mask_p, sp, NEG_BIG)
        sc = jnp.where(mask_c, sc, NEG_BIG)
        m = jnp.maximum(jnp.max(sp, axis=-1, keepdims=True), jnp.max(sc, axis=-1, keepdims=True))
        pp = jnp.exp(sp - m)
        pc = jnp.exp(sc - m)
        den = jnp.sum(pp, axis=-1, keepdims=True) + jnp.sum(pc, axis=-1, keepdims=True)
        o = (jnp.dot(pp.astype(BF16), vp_ref[0, :, sl].astype(BF16), preferred_element_type=F32)
             + jnp.dot(pc.astype(BF16), vc_ref[0, :, sl].astype(BF16), preferred_element_type=F32))
        o_ref[0, :, sl] = o / den
        l_ref[0, :, sl] = jnp.broadcast_to(m + jnp.log(den), (QB, HEAD_DIM))


def _attn(proj, g, dil):
    L = SEQ // dil
    nblk = IN_WIDTH // C_WIDTH
    pv = proj.reshape(BATCH, L, dil * IN_WIDTH)
    jq, jk, jv = (OFF_Q // C_WIDTH + g, OFF_K // C_WIDTH + g, OFF_V // C_WIDTH + g)

    def cur(jb):
        return pl.BlockSpec((1, QB, C_WIDTH), lambda b, r, c: (b, c, r * nblk + jb))

    def prev(jb):
        return pl.BlockSpec((1, QB, C_WIDTH), lambda b, r, c: (b, jnp.maximum(c - 1, 0), r * nblk + jb))

    out = pl.BlockSpec((1, QB, C_WIDTH), lambda b, r, c: (b, c, r))
    shp = jax.ShapeDtypeStruct((BATCH, L, dil * C_WIDTH), F32)
    o, lse = pl.pallas_call(
        _attn_body,
        grid=(BATCH, dil, L // QB),
        in_specs=[cur(jq), prev(jk), cur(jk), prev(jv), cur(jv)],
        out_specs=[out, out],
        out_shape=[shp, shp],
        compiler_params=_params("arbitrary", "arbitrary", "arbitrary"),
        name=f"attn_d{dil}",
    )(pv, pv, pv, pv, pv)
    return o.reshape(BATCH * SEQ, C_WIDTH), lse.reshape(BATCH * SEQ, C_WIDTH)


def _mix_sample_body(p_ref, pb_ref, st_ref, cw_ref, lng_ref, lnb_ref, wsc_ref, bsc_ref, c0_ref, c1_ref, c2_ref,
                     ya_ref, yb_ref, yc_ref, cs_ref, vn_ref):
    @pl.when(pl.program_id(0) == 0)
    def _():
        def seg(off, w):
            return p_ref[:, off:off + w]

        z = seg(OFF_AB + W_A, W_A) * seg(OFF_AB + 2 * W_A, W_A)
        st0 = st_ref[0, :, 0:W_A]
        st1 = st_ref[0, :, W_A:2 * W_A]
        conv = cw_ref[0:1, :] * st0 + cw_ref[1:2, :] * st1 + cw_ref[2:3, :] * z
        ya_ref[...] = (seg(OFF_AB, W_A) * conv).astype(BF16)
        cs_ref[:, 0:W_A] = st1
        cs_ref[:, W_A:2 * W_A] = z

        v = seg(OFF_AB + 3 * W_A + W_B, W_B)
        xc = v - jnp.mean(v, axis=-1, keepdims=True)
        var = jnp.mean(xc * xc, axis=-1, keepdims=True)
        vn = xc * lax.rsqrt(var + EPS) * lng_ref[...] + lnb_ref[...]
        vn_ref[...] = vn
        yb_ref[...] = (seg(OFF_AB + 3 * W_A, W_B) * (wsc_ref[...] * vn + bsc_ref[...])).astype(BF16)

    scale = HEAD_DIM ** -0.5
    caches = (c0_ref, c1_ref, c2_ref)
    for h in range(N_C_HEADS):
        outs, lses = [], []
        for g in range(N_DIL):
            off = g * C_WIDTH + h * HEAD_DIM
            q = pb_ref[0, :, OFF_Q + off:OFF_Q + off + HEAD_DIM]
            kn = pb_ref[0, :, OFF_K + off:OFF_K + off + HEAD_DIM]
            vnew = pb_ref[0, :, OFF_V + off:OFF_V + off + HEAD_DIM]
            kc = caches[g][0, 0, :, h * HEAD_DIM:(h + 1) * HEAD_DIM]
            vc = caches[g][0, 0, :, C_WIDTH + h * HEAD_DIM:C_WIDTH + (h + 1) * HEAD_DIM]
            s_c = jnp.sum(q * kc, axis=-1, keepdims=True) * scale
            s_n = jnp.sum(q * kn, axis=-1, keepdims=True) * scale
            m = jnp.maximum(jnp.max(s_c, axis=0, keepdims=True), s_n)
            p_c = jnp.exp(s_c - m)
            p_n = jnp.exp(s_n - m)
            den = jnp.sum(p_c, axis=0, keepdims=True) + p_n
            o = (jnp.sum(p_c * vc, axis=0, keepdims=True) + p_n * vnew) / den
            outs.append(o)
            lses.append(m + jnp.log(den))
        yc_ref[0, :, h * HEAD_DIM:(h + 1) * HEAD_DIM] = _combine_groups(*outs, *lses)


def _mix_sample(proj_s, st, cw, lng, lnb, wsc, bsc, caches, layer):
    def full(shape):
        return pl.BlockSpec(shape, lambda b: (0,) * len(shape))

    n_keys = DIL_GROUPS[0][0] // DIL_GROUPS[0][1]
    cache_specs = [pl.BlockSpec((1, 1, n_keys, 2 * C_WIDTH), lambda b: (layer, b, 0, 0))
                   for _ in caches]
    return pl.pallas_call(
        _mix_sample_body,
        grid=(DEC_BATCH,),
        in_specs=[full((DEC_BATCH, IN_WIDTH)),
                  pl.BlockSpec((1, 1, IN_WIDTH), lambda b: (b, 0, 0)),
                  pl.BlockSpec((1, DEC_BATCH, 2 * W_A), lambda b: (layer, 0, 0)),
                  full((CONV_W, W_A)), full((1, W_B)), full((1, W_B)), full((1, W_B)), full((1, W_B))]
                 + cache_specs,
        out_specs=[full((DEC_BATCH, W_A)), full((DEC_BATCH, W_B)),
                   pl.BlockSpec((1, 1, C_WIDTH), lambda b: (b, 0, 0)),
                   full((DEC_BATCH, 2 * W_A)), full((DEC_BATCH, W_B))],
        out_shape=[jax.ShapeDtypeStruct((DEC_BATCH, W_A), BF16),
                   jax.ShapeDtypeStruct((DEC_BATCH, W_B), BF16),
                   jax.ShapeDtypeStruct((DEC_BATCH, 1, C_WIDTH), F32),
                   jax.ShapeDtypeStruct((DEC_BATCH, 2 * W_A), F32),
                   jax.ShapeDtypeStruct((DEC_BATCH, W_B), F32)],
        compiler_params=_params("arbitrary"),
        name="mix_sample",
    )(proj_s, proj_s.reshape(DEC_BATCH, 1, IN_WIDTH), st, cw, lng, lnb, wsc, bsc, *caches)


def _combine_groups(o0, o1, o2, l0, l1, l2):
    m = jnp.maximum(jnp.maximum(l0, l1), l2)
    e0, e1, e2 = jnp.exp(l0 - m), jnp.exp(l1 - m), jnp.exp(l2 - m)
    return (e0 * o0 + e1 * o1 + e2 * o2) / (e0 + e1 + e2)


def _merge_tail(x, ga, gb, gc, ya, yb, yc, wa_ref, wb_ref, wc_ref, wo_ref, gp_ref):
    mm = (jax.nn.sigmoid(ga) * jnp.dot(ya, wa_ref[...], preferred_element_type=F32)
          + jax.nn.sigmoid(gb) * jnp.dot(yb, wb_ref[...], preferred_element_type=F32)
          + jax.nn.sigmoid(gc) * jnp.dot(yc, wc_ref[...], preferred_element_type=F32))
    r = jnp.dot(mm.astype(BF16), wo_ref[...], preferred_element_type=F32)
    return x + _rms(r, gp_ref[...])


def _merge_prompt_body(ga_ref, gb_ref, gc_ref, ya_ref, yb_ref, o0_ref, o1_ref, o2_ref, l0_ref, l1_ref, l2_ref,
                       x_ref, wa_ref, wb_ref, wc_ref, wo_ref, gp_ref, out_ref):
    yc = _combine_groups(o0_ref[...], o1_ref[...], o2_ref[...], l0_ref[...], l1_ref[...], l2_ref[...])
    out_ref[...] = _merge_tail(x_ref[...], ga_ref[...], gb_ref[...], gc_ref[...], ya_ref[...], yb_ref[...],
                               yc.astype(BF16), wa_ref, wb_ref, wc_ref, wo_ref, gp_ref)


def _merge_sample_body(ga_ref, gb_ref, gc_ref, ya_ref, yb_ref, yc_ref,
                       x_ref, wa_ref, wb_ref, wc_ref, wo_ref, gp_ref, out_ref):
    out_ref[...] = _merge_tail(x_ref[...], ga_ref[...], gb_ref[...], gc_ref[...], ya_ref[...], yb_ref[...],
                               yc_ref[...].astype(BF16), wa_ref, wb_ref, wc_ref, wo_ref, gp_ref)


def _resident(shape):
    return pl.BlockSpec(shape, lambda i: (0,) * len(shape), pipeline_mode=pl.Buffered(1))


def _merge_weight_specs():
    return [_resident((W_A, D_MODEL)), _resident((W_B, D_MODEL)), _resident((C_WIDTH, D_MODEL)),
            _resident((D_MODEL, D_MODEL)), _resident((1, D_MODEL))]


def _merge_prompt(proj, ya, yb, outs, lses, x, wa, wb, wc, wo, gp, tm):
    m = x.shape[0]

    def rows(w, k=0):
        return pl.BlockSpec((tm, w), lambda i: (i, k))

    return pl.pallas_call(
        _merge_prompt_body,
        grid=(m // tm,),
        in_specs=[rows(D_MODEL, 0), rows(D_MODEL, 1), rows(D_MODEL, 2), rows(W_A), rows(W_B)]
                 + [rows(C_WIDTH)] * 6 + [rows(D_MODEL)] + _merge_weight_specs(),
        out_specs=rows(D_MODEL),
        out_shape=jax.ShapeDtypeStruct((m, D_MODEL), F32),
        compiler_params=_params("arbitrary"),
        name="merge_prompt",
    )(proj, proj, proj, ya, yb, *outs, *lses, x, wa, wb, wc, wo, gp)


def _merge_sample(proj, ya, yb, yc, x, wa, wb, wc, wo, gp):
    m = x.shape[0]

    def rows(w, k=0):
        return pl.BlockSpec((m, w), lambda i: (0, k))

    return pl.pallas_call(
        _merge_sample_body,
        grid=(1,),
        in_specs=[rows(D_MODEL, 0), rows(D_MODEL, 1), rows(D_MODEL, 2), rows(W_A), rows(W_B), rows(C_WIDTH),
                  rows(D_MODEL)] + _merge_weight_specs(),
        out_specs=rows(D_MODEL),
        out_shape=jax.ShapeDtypeStruct((m, D_MODEL), F32),
        compiler_params=_params("arbitrary"),
        name="merge_sample",
    )(proj, proj, proj, ya, yb, yc, x, wa, wb, wc, wo, gp)


def _ffn_body(x_ref, g1_ref, wg_ref, wu_ref, wo_ref, g2_ref, out_ref, h_ref, acc_ref):
    j = pl.program_id(1)

    @pl.when(j == 0)
    def _():
        h_ref[...] = _rms(x_ref[...], g1_ref[...]).astype(BF16)
        acc_ref[...] = jnp.zeros_like(acc_ref)

    h = h_ref[...]
    gate = jnp.dot(h, wg_ref[...], preferred_element_type=F32)
    up = jnp.dot(h, wu_ref[...], preferred_element_type=F32)
    act = (gate * jax.nn.sigmoid(gate) * up).astype(BF16)
    acc_ref[...] += jnp.dot(act, wo_ref[...], preferred_element_type=F32)

    @pl.when(j == pl.num_programs(1) - 1)
    def _():
        out_ref[...] = x_ref[...] + _rms(acc_ref[...], g2_ref[...])


def _ffn(x, g1, w_in, w_out, g2, tm, tf):
    m = x.shape[0]
    nj = D_FF // tf
    return pl.pallas_call(
        _ffn_body,
        grid=(m // tm, nj),
        in_specs=[
            pl.BlockSpec((tm, D_MODEL), lambda i, j: (i, 0)),
            pl.BlockSpec((1, D_MODEL), lambda i, j: (0, 0)),
            pl.BlockSpec((D_MODEL, tf), lambda i, j: (0, j)),
            pl.BlockSpec((D_MODEL, tf), lambda i, j: (0, nj + j)),
            pl.BlockSpec((tf, D_MODEL), lambda i, j: (j, 0)),
            pl.BlockSpec((1, D_MODEL), lambda i, j: (0, 0)),
        ],
        out_specs=pl.BlockSpec((tm, D_MODEL), lambda i, j: (i, 0)),
        out_shape=jax.ShapeDtypeStruct((m, D_MODEL), F32),
        scratch_shapes=[pltpu.VMEM((tm, D_MODEL), BF16), pltpu.VMEM((tm, D_MODEL), F32)],
        compiler_params=_params("arbitrary", "arbitrary"),
        name="ffn",
    )(x, g1, w_in, w_in, w_out, g2)


TM_INPROJ = 1024
TS_MIX = 512
TM_MERGE = 256
TM_FFN = 512
TF_FFN = 512


def kernel(x_prompt, x_sample, state_conv, cache_kv_w128, cache_kv_w512, cache_kv_w2048, g_pre_mix, w_in, conv_w, ln_g, ln_b, w_s, b_s, w_a_out, w_b_out, w_c_out, w_o, g_post_mix, g_pre_ffn, w_ffn_in, w_ffn_out, g_post_ffn):
    w_in_b = [jnp.concatenate([w_in[l, :, ORIG_GATE_OFF:], w_in[l, :, :ORIG_GATE_OFF]], axis=-1).astype(BF16)
              for l in range(DEPTH)]
    wa_b, wb_b, wc_b, wo_b, wfi_b, wfo_b = (
        [w[l].astype(BF16) for l in range(DEPTH)]
        for w in (w_a_out, w_b_out, w_c_out, w_o, w_ffn_in, w_ffn_out))

    tabs_p = _rope_tables(jnp.arange(SEQ, dtype=jnp.int32))
    tabs_s = _rope_tables(jnp.full((DEC_BATCH,), PAST_LEN, dtype=jnp.int32))

    n_keys = DIL_GROUPS[0][0] // DIL_GROUPS[0][1]
    caches = tuple(c.reshape(DEPTH, DEC_BATCH, n_keys, dil * 2 * C_WIDTH)
                   for c, (_, dil) in zip((cache_kv_w128, cache_kv_w512, cache_kv_w2048), DIL_GROUPS))
    st_all = state_conv.reshape(DEPTH, DEC_BATCH, (CONV_W - 1) * W_A)
    bs_t = jnp.swapaxes(b_s, 1, 2)
    wsc = jnp.repeat(w_s[:, :, 0, 0], B_GROUP_W, axis=1)
    bsc = jnp.repeat(b_s[:, :, 0], B_GROUP_W, axis=1)

    xp = x_prompt.reshape(BATCH * SEQ, D_MODEL)
    xs = x_sample.reshape(DEC_BATCH, D_MODEL)
    conv_p, conv_s, vchunk_s = [], [], []
    kv_p = [[] for _ in DIL_GROUPS]
    kv_s = [[] for _ in DIL_GROUPS]

    def row(a, l):
        return a[l][None, :]

    for l in range(DEPTH):
        proj = _inproj(xp, row(g_pre_mix, l), w_in_b[l], tabs_p, TM_INPROJ, SEQ // TM_INPROJ)
        ya, yb, ctail = _mix_ab(proj, conv_w[l], row(ln_g, l), row(ln_b, l), w_s[l], bs_t[l], TS_MIX)
        outs, lses = [], []
        for g, (win, dil) in enumerate(DIL_GROUPS):
            o, lse = _attn(proj, g, dil)
            outs.append(o)
            lses.append(lse)
            keep = min(win, SEQ)
            k = proj[:, OFF_K + g * C_WIDTH:OFF_K + (g + 1) * C_WIDTH]
            v = proj[:, OFF_V + g * C_WIDTH:OFF_V + (g + 1) * C_WIDTH]
            kv = jnp.stack([k.reshape(BATCH, SEQ, N_C_HEADS, HEAD_DIM)[:, SEQ - keep:],
                            v.reshape(BATCH, SEQ, N_C_HEADS, HEAD_DIM)[:, SEQ - keep:]], axis=2)
            kv_p[g].append(kv)
        conv_p.append(ctail[:, 8 - (CONV_W - 1):])
        xp = _merge_prompt(proj, ya, yb, outs, lses, xp, wa_b[l], wb_b[l], wc_b[l], wo_b[l],
                           row(g_post_mix, l), TM_MERGE)
        xp = _ffn(xp, row(g_pre_ffn, l), wfi_b[l], wfo_b[l], row(g_post_ffn, l), TM_FFN, TF_FFN)

        proj_s = _inproj(xs, row(g_pre_mix, l), w_in_b[l], tabs_s, DEC_BATCH, 1)
        ya_s, yb_s, yc_s, cst, vn = _mix_sample(proj_s, st_all, conv_w[l], row(ln_g, l), row(ln_b, l),
                                                row(wsc, l), row(bsc, l), caches, l)
        for g in range(N_DIL):
            k = proj_s[:, OFF_K + g * C_WIDTH:OFF_K + (g + 1) * C_WIDTH]
            v = proj_s[:, OFF_V + g * C_WIDTH:OFF_V + (g + 1) * C_WIDTH]
            kv_s[g].append(jnp.stack([k.reshape(DEC_BATCH, 1, N_C_HEADS, HEAD_DIM),
                                      v.reshape(DEC_BATCH, 1, N_C_HEADS, HEAD_DIM)], axis=2))
        conv_s.append(cst.reshape(DEC_BATCH, CONV_W - 1, W_A))
        vchunk_s.append(vn.reshape(DEC_BATCH, 1, W_B))
        xs = _merge_sample(proj_s, ya_s, yb_s, yc_s.reshape(DEC_BATCH, C_WIDTH), xs, wa_b[l], wb_b[l], wc_b[l], wo_b[l],
                           row(g_post_mix, l))
        xs = _ffn(xs, row(g_pre_ffn, l), wfi_b[l], wfo_b[l], row(g_post_ffn, l), DEC_BATCH, TF_FFN)

    return (xp.reshape(BATCH, SEQ, D_MODEL), xs.reshape(DEC_BATCH, 1, D_MODEL),
            jnp.stack(conv_p, axis=0),
            jnp.stack(kv_p[0], axis=0), jnp.stack(kv_p[1], axis=0), jnp.stack(kv_p[2], axis=0),
            jnp.stack(conv_s, axis=0),
            jnp.stack(kv_s[0], axis=0), jnp.stack(kv_s[1], axis=0), jnp.stack(kv_s[2], axis=0),
            jnp.stack(vchunk_s, axis=0))
```

```python
import functools

import jax
import jax.numpy as jnp
from jax import lax
from jax.experimental import pallas as pl
from jax.experimental.pallas import tpu as pltpu

D_MODEL = 2048
BATCH = 4
SEQ = 2048
DEPTH = 4
DEC_BATCH = 8
PAST_LEN = 16384
W_A = 1024
CONV_W = 3
W_B = 1024
CHUNK = 128
N_B_GROUPS = 4
B_GROUP_W = W_B // N_B_GROUPS
N_C_HEADS = 4
HEAD_DIM = 128
ROT_DIM = HEAD_DIM // 4
ROPE_THETA = 500000.0
DIL_GROUPS = ((128, 1), (512, 4), (2048, 16))
N_DIL = len(DIL_GROUPS)
QB = 128
C_WIDTH = N_C_HEADS * HEAD_DIM
QKV_W = N_DIL * C_WIDTH
D_FF = ((-(-8 * D_MODEL // 3) + 255) // 256) * 256
IN_WIDTH = 3 * W_A + 2 * W_B + 3 * QKV_W + 3 * D_MODEL
EPS = 1e-6

GATE_W = 3 * D_MODEL
OFF_AB = GATE_W
OFF_Q = GATE_W + 3 * W_A + 2 * W_B
OFF_K = OFF_Q + QKV_W
OFF_V = OFF_K + QKV_W
ORIG_GATE_OFF = IN_WIDTH - GATE_W

V7X_VMEM_LIMIT = 60 * 1024 * 1024
NEG_BIG = -1e30

F32 = jnp.float32
BF16 = jnp.bfloat16


def _params(*sem):
    return pltpu.CompilerParams(dimension_semantics=sem, vmem_limit_bytes=V7X_VMEM_LIMIT)


def _rms(x, g):
    return x * lax.rsqrt(jnp.mean(x * x, axis=-1, keepdims=True) + EPS) * g


def _inproj_body(x_ref, g_ref, w_ref, c_ref, s1_ref, s2_ref, o_ref, h_ref, *, rope_lo, rope_hi):
    j = pl.program_id(1)

    @pl.when(j == 0)
    def _():
        h_ref[...] = _rms(x_ref[...], g_ref[...]).astype(BF16)

    acc = jnp.dot(h_ref[...], w_ref[...], preferred_element_type=F32)
    is_rope = jnp.logical_and(j >= rope_lo, j < rope_hi)

    @pl.when(is_rope)
    def _():
        for h in range(N_C_HEADS):
            sl = slice(h * HEAD_DIM, (h + 1) * HEAD_DIM)
            a = acc[:, sl]
            o_ref[:, sl] = (a * c_ref[...]
                            + pltpu.roll(a, HEAD_DIM - ROT_DIM // 2, 1) * s1_ref[...]
                            + pltpu.roll(a, ROT_DIM // 2, 1) * s2_ref[...])

    @pl.when(jnp.logical_not(is_rope))
    def _():
        o_ref[...] = acc


def _inproj(x, g, w, tabs, tm, pos_blocks):
    m = x.shape[0]
    tn = C_WIDTH
    nj = IN_WIDTH // tn
    tab_spec = pl.BlockSpec((tm, HEAD_DIM), lambda i, j: (i % pos_blocks, 0))
    return pl.pallas_call(
        functools.partial(_inproj_body, rope_lo=OFF_Q // tn, rope_hi=OFF_V // tn),
        grid=(m // tm, nj),
        in_specs=[
            pl.BlockSpec((tm, D_MODEL), lambda i, j: (i, 0)),
            pl.BlockSpec((1, D_MODEL), lambda i, j: (0, 0)),
            pl.BlockSpec((D_MODEL, tn), lambda i, j: (0, j)),
            tab_spec, tab_spec, tab_spec,
        ],
        out_specs=pl.BlockSpec((tm, tn), lambda i, j: (i, j)),
        out_shape=jax.ShapeDtypeStruct((m, IN_WIDTH), F32),
        scratch_shapes=[pltpu.VMEM((tm, D_MODEL), BF16)],
        compiler_params=_params("arbitrary", "arbitrary"),
        name="inproj",
    )(x, g, w, *tabs)


def _rope_tables(pos):
    inv_freq = ROPE_THETA ** (-jnp.arange(0, ROT_DIM, 2, dtype=jnp.float32) / ROT_DIM)
    ang = pos.astype(jnp.float32)[:, None] * inv_freq[None, :]
    cos, sin = jnp.cos(ang), jnp.sin(ang)
    n = pos.shape[0]
    half = ROT_DIM // 2
    rest = HEAD_DIM - ROT_DIM
    c = jnp.concatenate([cos, cos, jnp.ones((n, rest), F32)], axis=1)
    s1 = jnp.concatenate([-sin, jnp.zeros((n, half + rest), F32)], axis=1)
    s2 = jnp.concatenate([jnp.zeros((n, half), F32), sin, jnp.zeros((n, rest), F32)], axis=1)
    return c, s1, s2


def _mix_ab_body(ab_ref, ac_ref, ax_ref, bu_ref, bv_ref, cw_ref, lng_ref, lnb_ref, ws_ref, bs_ref,
                 ya_ref, yb_ref, cs_ref, zs_ref):
    ts = ab_ref.shape[0]

    @pl.when(pl.program_id(1) == 0)
    def _():
        zs_ref[0:8, :] = jnp.zeros((8, W_A), F32)

    z = ac_ref[...] * ax_ref[...]
    zs_ref[8:8 + ts, :] = z
    z1 = zs_ref[7:7 + ts, :]
    z2 = zs_ref[6:6 + ts, :]
    conv = cw_ref[0:1, :] * z2 + cw_ref[1:2, :] * z1 + cw_ref[2:3, :] * z
    ya_ref[...] = (ab_ref[...] * conv).astype(BF16)
    tail = zs_ref[ts:ts + 8, :]
    zs_ref[0:8, :] = tail
    cs_ref[0] = tail

    v = bv_ref[...]
    xc = v - jnp.mean(v, axis=-1, keepdims=True)
    var = jnp.mean(xc * xc, axis=-1, keepdims=True)
    vn = (xc * lax.rsqrt(var + EPS) * lng_ref[...] + lnb_ref[...]).astype(BF16)
    row = lax.broadcasted_iota(jnp.int32, (CHUNK, CHUNK), 0)
    col = lax.broadcasted_iota(jnp.int32, (CHUNK, CHUNK), 1)
    for g in range(N_B_GROUPS):
        wg = jnp.where(row >= col, ws_ref[g], 0.0).astype(BF16)
        bcol = bs_ref[:, g:g + 1]
        gs = slice(g * B_GROUP_W, (g + 1) * B_GROUP_W)
        for c in range(ts // CHUNK):
            rs = slice(c * CHUNK, (c + 1) * CHUNK)
            sg = jnp.dot(wg, vn[rs, gs], preferred_element_type=F32) + bcol
            yb_ref[rs, gs] = (bu_ref[rs, gs] * sg).astype(BF16)


def _mix_ab(proj, cw, lng, lnb, ws, bs_t, ts):
    nb = SEQ // ts
    blk = OFF_AB // W_A

    def seg(k):
        return pl.BlockSpec((ts, W_A), lambda b, s: (b * nb + s, blk + k))

    def full(shape):
        return pl.BlockSpec(shape, lambda b, s: (0,) * len(shape))

    rows = pl.BlockSpec((ts, W_A), lambda b, s: (b * nb + s, 0))
    return pl.pallas_call(
        _mix_ab_body,
        grid=(BATCH, nb),
        in_specs=[seg(0), seg(1), seg(2), seg(3), seg(4),
                  full((CONV_W, W_A)), full((1, W_B)), full((1, W_B)),
                  full((N_B_GROUPS, CHUNK, CHUNK)), full((CHUNK, N_B_GROUPS))],
        out_specs=[rows, rows, pl.BlockSpec((1, 8, W_A), lambda b, s: (b, 0, 0))],
        out_shape=[jax.ShapeDtypeStruct((BATCH * SEQ, W_A), BF16),
                   jax.ShapeDtypeStruct((BATCH * SEQ, W_B), BF16),
                   jax.ShapeDtypeStruct((BATCH, 8, W_A), F32)],
        scratch_shapes=[pltpu.VMEM((ts + 8, W_A), F32)],
        compiler_params=_params("arbitrary", "arbitrary"),
        name="mix_ab",
    )(proj, proj, proj, proj, proj, cw, lng, lnb, ws, bs_t)


def _attn_body(q0, q1, q2, k0, k1, k2, v0, v1, v2, y_ref, o0, o1, o2, l0, l1, l2):
    row = lax.broadcasted_iota(jnp.int32, (QB, QB), 0)
    col = lax.broadcasted_iota(jnp.int32, (QB, QB), 1)
    mask_c = col <= row
    mask_p = col >= row
    scale = HEAD_DIM ** -0.5
    nt = (((1,), (1,)), ((), ()))
    for (_, dil), q_ref, k_ref, v_ref, o_ref, l_ref in zip(
            DIL_GROUPS, (q0, q1, q2), (k0, k1, k2), (v0, v1, v2), (o0, o1, o2), (l0, l1, l2)):
        for r in range(dil):
            for c in range(SEQ // dil // QB):
                def rows(blk):
                    if dil == 1:
                        return pl.ds(blk * QB, QB)
                    return pl.ds(r + blk * QB * dil, QB, stride=dil)

                q = q_ref[rows(c), :].astype(BF16)
                s = lax.dot_general(q, k_ref[rows(c), :].astype(BF16), nt, preferred_element_type=F32) * scale
                s = jnp.where(mask_c, s, NEG_BIG)
                m = jnp.max(s, axis=-1, keepdims=True)
                if c > 0:
                    sp = lax.dot_general(q, k_ref[rows(c - 1), :].astype(BF16), nt,
                                         preferred_element_type=F32) * scale
                    sp = jnp.where(mask_p, sp, NEG_BIG)
                    m = jnp.maximum(m, jnp.max(sp, axis=-1, keepdims=True))
                p = jnp.exp(s - m)
                den = jnp.sum(p, axis=-1, keepdims=True)
                o = jnp.dot(p.astype(BF16), v_ref[rows(c), :].astype(BF16), preferred_element_type=F32)
                if c > 0:
                    pp = jnp.exp(sp - m)
                    den = den + jnp.sum(pp, axis=-1, keepdims=True)
                    o = o + jnp.dot(pp.astype(BF16), v_ref[rows(c - 1), :].astype(BF16),
                                    preferred_element_type=F32)
                o_ref[rows(c), :] = o / den
                l_ref[rows(c), :] = jnp.broadcast_to(m + jnp.log(den), (QB, HEAD_DIM))
    y_ref[...] = _combine_groups(o0[...], o1[...], o2[...], l0[...], l1[...], l2[...]).astype(BF16)


def _attn(proj):
    def seg(off, g):
        base = (off + g * C_WIDTH) // HEAD_DIM
        return pl.BlockSpec((SEQ, HEAD_DIM), lambda b, h: (b, base + h))

    specs = [seg(off, g) for off in (OFF_Q, OFF_K, OFF_V) for g in range(N_DIL)]
    return pl.pallas_call(
        _attn_body,
        grid=(BATCH, N_C_HEADS),
        in_specs=specs,
        out_specs=pl.BlockSpec((SEQ, HEAD_DIM), lambda b, h: (b, h)),
        out_shape=jax.ShapeDtypeStruct((BATCH * SEQ, C_WIDTH), BF16),
        scratch_shapes=[pltpu.VMEM((SEQ, HEAD_DIM), F32)] * (2 * N_DIL),
        compiler_params=_params("arbitrary", "arbitrary"),
        name="attn",
    )(*([proj] * (3 * N_DIL)))


def _mix_sample_body(p_ref, pb_ref, st_ref, cw_ref, lng_ref, lnb_ref, wsc_ref, bsc_ref, c0_ref, c1_ref, c2_ref,
                     ya_ref, yb_ref, yc_ref, cs_ref, vn_ref):
    @pl.when(pl.program_id(0) == 0)
    def _():
        def seg(off, w):
            return p_ref[:, off:off + w]

        z = seg(OFF_AB + W_A, W_A) * seg(OFF_AB + 2 * W_A, W_A)
        st0 = st_ref[0, :, 0:W_A]
        st1 = st_ref[0, :, W_A:2 * W_A]
        conv = cw_ref[0:1, :] * st0 + cw_ref[1:2, :] * st1 + cw_ref[2:3, :] * z
        ya_ref[...] = (seg(OFF_AB, W_A) * conv).astype(BF16)
        cs_ref[:, 0:W_A] = st1
        cs_ref[:, W_A:2 * W_A] = z

        v = seg(OFF_AB + 3 * W_A + W_B, W_B)
        xc = v - jnp.mean(v, axis=-1, keepdims=True)
        var = jnp.mean(xc * xc, axis=-1, keepdims=True)
        vn = xc * lax.rsqrt(var + EPS) * lng_ref[...] + lnb_ref[...]
        vn_ref[...] = vn
        yb_ref[...] = (seg(OFF_AB + 3 * W_A, W_B) * (wsc_ref[...] * vn + bsc_ref[...])).astype(BF16)

    scale = HEAD_DIM ** -0.5
    caches = (c0_ref, c1_ref, c2_ref)
    for h in range(N_C_HEADS):
        outs, lses = [], []
        for g in range(N_DIL):
            off = g * C_WIDTH + h * HEAD_DIM
            q = pb_ref[0, :, OFF_Q + off:OFF_Q + off + HEAD_DIM]
            kn = pb_ref[0, :, OFF_K + off:OFF_K + off + HEAD_DIM]
            vnew = pb_ref[0, :, OFF_V + off:OFF_V + off + HEAD_DIM]
            kc = caches[g][0, 0, :, h * HEAD_DIM:(h + 1) * HEAD_DIM]
            vc = caches[g][0, 0, :, C_WIDTH + h * HEAD_DIM:C_WIDTH + (h + 1) * HEAD_DIM]
            s_c = jnp.sum(q * kc, axis=-1, keepdims=True) * scale
            s_n = jnp.sum(q * kn, axis=-1, keepdims=True) * scale
            m = jnp.maximum(jnp.max(s_c, axis=0, keepdims=True), s_n)
            p_c = jnp.exp(s_c - m)
            p_n = jnp.exp(s_n - m)
            den = jnp.sum(p_c, axis=0, keepdims=True) + p_n
            o = (jnp.sum(p_c * vc, axis=0, keepdims=True) + p_n * vnew) / den
            outs.append(o)
            lses.append(m + jnp.log(den))
        yc_ref[0, :, h * HEAD_DIM:(h + 1) * HEAD_DIM] = _combine_groups(*outs, *lses)


def _mix_sample(proj_s, st, cw, lng, lnb, wsc, bsc, caches, layer):
    def full(shape):
        return pl.BlockSpec(shape, lambda b: (0,) * len(shape))

    n_keys = DIL_GROUPS[0][0] // DIL_GROUPS[0][1]
    cache_specs = [pl.BlockSpec((1, 1, n_keys, 2 * C_WIDTH), lambda b: (layer, b, 0, 0))
                   for _ in caches]
    return pl.pallas_call(
        _mix_sample_body,
        grid=(DEC_BATCH,),
        in_specs=[full((DEC_BATCH, IN_WIDTH)),
                  pl.BlockSpec((1, 1, IN_WIDTH), lambda b: (b, 0, 0)),
                  pl.BlockSpec((1, DEC_BATCH, 2 * W_A), lambda b: (layer, 0, 0)),
                  full((CONV_W, W_A)), full((1, W_B)), full((1, W_B)), full((1, W_B)), full((1, W_B))]
                 + cache_specs,
        out_specs=[full((DEC_BATCH, W_A)), full((DEC_BATCH, W_B)),
                   pl.BlockSpec((1, 1, C_WIDTH), lambda b: (b, 0, 0)),
                   full((DEC_BATCH, 2 * W_A)), full((DEC_BATCH, W_B))],
        out_shape=[jax.ShapeDtypeStruct((DEC_BATCH, W_A), BF16),
                   jax.ShapeDtypeStruct((DEC_BATCH, W_B), BF16),
                   jax.ShapeDtypeStruct((DEC_BATCH, 1, C_WIDTH), F32),
                   jax.ShapeDtypeStruct((DEC_BATCH, 2 * W_A), F32),
                   jax.ShapeDtypeStruct((DEC_BATCH, W_B), F32)],
        compiler_params=_params("arbitrary"),
        name="mix_sample",
    )(proj_s, proj_s.reshape(DEC_BATCH, 1, IN_WIDTH), st, cw, lng, lnb, wsc, bsc, *caches)


def _combine_groups(o0, o1, o2, l0, l1, l2):
    m = jnp.maximum(jnp.maximum(l0, l1), l2)
    e0, e1, e2 = jnp.exp(l0 - m), jnp.exp(l1 - m), jnp.exp(l2 - m)
    return (e0 * o0 + e1 * o1 + e2 * o2) / (e0 + e1 + e2)


def _merge_tail(x, ga, gb, gc, ya, yb, yc, wa_ref, wb_ref, wc_ref, wo_ref, gp_ref):
    mm = (jax.nn.sigmoid(ga) * jnp.dot(ya, wa_ref[...], preferred_element_type=F32)
          + jax.nn.sigmoid(gb) * jnp.dot(yb, wb_ref[...], preferred_element_type=F32)
          + jax.nn.sigmoid(gc) * jnp.dot(yc, wc_ref[...], preferred_element_type=F32))
    r = jnp.dot(mm.astype(BF16), wo_ref[...], preferred_element_type=F32)
    return x + _rms(r, gp_ref[...])


def _merge_body(ga_ref, gb_ref, gc_ref, ya_ref, yb_ref, yc_ref,
                x_ref, wa_ref, wb_ref, wc_ref, wo_ref, gp_ref, out_ref):
    out_ref[...] = _merge_tail(x_ref[...], ga_ref[...], gb_ref[...], gc_ref[...], ya_ref[...], yb_ref[...],
                               yc_ref[...].astype(BF16), wa_ref, wb_ref, wc_ref, wo_ref, gp_ref)


def _resident(shape):
    return pl.BlockSpec(shape, lambda i: (0,) * len(shape), pipeline_mode=pl.Buffered(1))


def _merge(proj, ya, yb, yc, x, wa, wb, wc, wo, gp, tm):
    m = x.shape[0]

    def rows(w, k=0):
        return pl.BlockSpec((tm, w), lambda i: (i, k))

    return pl.pallas_call(
        _merge_body,
        grid=(m // tm,),
        in_specs=[rows(D_MODEL, 0), rows(D_MODEL, 1), rows(D_MODEL, 2), rows(W_A), rows(W_B), rows(C_WIDTH),
                  rows(D_MODEL),
                  _resident((W_A, D_MODEL)), _resident((W_B, D_MODEL)), _resident((C_WIDTH, D_MODEL)),
                  _resident((D_MODEL, D_MODEL)), _resident((1, D_MODEL))],
        out_specs=rows(D_MODEL),
        out_shape=jax.ShapeDtypeStruct((m, D_MODEL), F32),
        compiler_params=_params("arbitrary"),
        name="merge",
    )(proj, proj, proj, ya, yb, yc, x, wa, wb, wc, wo, gp)


def _ffn_body(x_ref, g1_ref, wg_ref, wu_ref, wo_ref, g2_ref, out_ref, h_ref, acc_ref):
    j = pl.program_id(1)

    @pl.when(j == 0)
    def _():
        h_ref[...] = _rms(x_ref[...], g1_ref[...]).astype(BF16)
        acc_ref[...] = jnp.zeros_like(acc_ref)

    h = h_ref[...]
    gate = jnp.dot(h, wg_ref[...], preferred_element_type=F32)
    up = jnp.dot(h, wu_ref[...], preferred_element_type=F32)
    act = (gate * jax.nn.sigmoid(gate) * up).astype(BF16)
    acc_ref[...] += jnp.dot(act, wo_ref[...], preferred_element_type=F32)

    @pl.when(j == pl.num_programs(1) - 1)
    def _():
        out_ref[...] = x_ref[...] + _rms(acc_ref[...], g2_ref[...])


def _ffn(x, g1, w_in, w_out, g2, tm, tf):
    m = x.shape[0]
    nj = D_FF // tf
    return pl.pallas_call(
        _ffn_body,
        grid=(m // tm, nj),
        in_specs=[
            pl.BlockSpec((tm, D_MODEL), lambda i, j: (i, 0)),
            pl.BlockSpec((1, D_MODEL), lambda i, j: (0, 0)),
            pl.BlockSpec((D_MODEL, tf), lambda i, j: (0, j)),
            pl.BlockSpec((D_MODEL, tf), lambda i, j: (0, nj + j)),
            pl.BlockSpec((tf, D_MODEL), lambda i, j: (j, 0)),
            pl.BlockSpec((1, D_MODEL), lambda i, j: (0, 0)),
        ],
        out_specs=pl.BlockSpec((tm, D_MODEL), lambda i, j: (i, 0)),
        out_shape=jax.ShapeDtypeStruct((m, D_MODEL), F32),
        scratch_shapes=[pltpu.VMEM((tm, D_MODEL), BF16), pltpu.VMEM((tm, D_MODEL), F32)],
        compiler_params=_params("arbitrary", "arbitrary"),
        name="ffn",
    )(x, g1, w_in, w_in, w_out, g2)


TM_INPROJ = 1024
TS_MIX = 512
TM_MERGE = 256
TM_FFN = 512
TF_FFN = 512


def kernel(x_prompt, x_sample, state_conv, cache_kv_w128, cache_kv_w512, cache_kv_w2048, g_pre_mix, w_in, conv_w, ln_g, ln_b, w_s, b_s, w_a_out, w_b_out, w_c_out, w_o, g_post_mix, g_pre_ffn, w_ffn_in, w_ffn_out, g_post_ffn):
    w_in_b = [jnp.concatenate([w_in[l, :, ORIG_GATE_OFF:], w_in[l, :, :ORIG_GATE_OFF]], axis=-1).astype(BF16)
              for l in range(DEPTH)]
    wa_b, wb_b, wc_b, wo_b, wfi_b, wfo_b = (
        [w[l].astype(BF16) for l in range(DEPTH)]
        for w in (w_a_out, w_b_out, w_c_out, w_o, w_ffn_in, w_ffn_out))

    tabs_p = _rope_tables(jnp.arange(SEQ, dtype=jnp.int32))
    tabs_s = _rope_tables(jnp.full((DEC_BATCH,), PAST_LEN, dtype=jnp.int32))

    n_keys = DIL_GROUPS[0][0] // DIL_GROUPS[0][1]
    caches = tuple(c.reshape(DEPTH, DEC_BATCH, n_keys, dil * 2 * C_WIDTH)
                   for c, (_, dil) in zip((cache_kv_w128, cache_kv_w512, cache_kv_w2048), DIL_GROUPS))
    st_all = state_conv.reshape(DEPTH, DEC_BATCH, (CONV_W - 1) * W_A)
    bs_t = jnp.swapaxes(b_s, 1, 2)
    wsc = jnp.repeat(w_s[:, :, 0, 0], B_GROUP_W, axis=1)
    bsc = jnp.repeat(b_s[:, :, 0], B_GROUP_W, axis=1)

    xp = x_prompt.reshape(BATCH * SEQ, D_MODEL)
    xs = x_sample.reshape(DEC_BATCH, D_MODEL)
    conv_p, conv_s, vchunk_s = [], [], []
    kv_p = [[] for _ in DIL_GROUPS]
    kv_s = [[] for _ in DIL_GROUPS]

    def row(a, l):
        return a[l][None, :]

    for l in range(DEPTH):
        proj = _inproj(xp, row(g_pre_mix, l), w_in_b[l], tabs_p, TM_INPROJ, SEQ // TM_INPROJ)
        ya, yb, ctail = _mix_ab(proj, conv_w[l], row(ln_g, l), row(ln_b, l), w_s[l], bs_t[l], TS_MIX)
        yc = _attn(proj)
        for g, (win, dil) in enumerate(DIL_GROUPS):
            keep = min(win, SEQ)
            k = proj[:, OFF_K + g * C_WIDTH:OFF_K + (g + 1) * C_WIDTH]
            v = proj[:, OFF_V + g * C_WIDTH:OFF_V + (g + 1) * C_WIDTH]
            kv = jnp.stack([k.reshape(BATCH, SEQ, N_C_HEADS, HEAD_DIM)[:, SEQ - keep:],
                            v.reshape(BATCH, SEQ, N_C_HEADS, HEAD_DIM)[:, SEQ - keep:]], axis=2)
            kv_p[g].append(kv)
        conv_p.append(ctail[:, 8 - (CONV_W - 1):])
        xp = _merge(proj, ya, yb, yc, xp, wa_b[l], wb_b[l], wc_b[l], wo_b[l], row(g_post_mix, l), TM_MERGE)
        xp = _ffn(xp, row(g_pre_ffn, l), wfi_b[l], wfo_b[l], row(g_post_ffn, l), TM_FFN, TF_FFN)

        proj_s = _inproj(xs, row(g_pre_mix, l), w_in_b[l], tabs_s, DEC_BATCH, 1)
        ya_s, yb_s, yc_s, cst, vn = _mix_sample(proj_s, st_all, conv_w[l], row(ln_g, l), row(ln_b, l),
                                                row(wsc, l), row(bsc, l), caches, l)
        for g in range(N_DIL):
            k = proj_s[:, OFF_K + g * C_WIDTH:OFF_K + (g + 1) * C_WIDTH]
            v = proj_s[:, OFF_V + g * C_WIDTH:OFF_V + (g + 1) * C_WIDTH]
            kv_s[g].append(jnp.stack([k.reshape(DEC_BATCH, 1, N_C_HEADS, HEAD_DIM),
                                      v.reshape(DEC_BATCH, 1, N_C_HEADS, HEAD_DIM)], axis=2))
        conv_s.append(cst.reshape(DEC_BATCH, CONV_W - 1, W_A))
        vchunk_s.append(vn.reshape(DEC_BATCH, 1, W_B))
        xs = _merge(proj_s, ya_s, yb_s, yc_s.reshape(DEC_BATCH, C_WIDTH), xs, wa_b[l], wb_b[l], wc_b[l], wo_b[l],
                    row(g_post_mix, l), DEC_BATCH)
        xs = _ffn(xs, row(g_pre_ffn, l), wfi_b[l], wfo_b[l], row(g_post_ffn, l), DEC_BATCH, TF_FFN)

    return (xp.reshape(BATCH, SEQ, D_MODEL), xs.reshape(DEC_BATCH, 1, D_MODEL),
            jnp.stack(conv_p, axis=0),
            jnp.stack(kv_p[0], axis=0), jnp.stack(kv_p[1], axis=0), jnp.stack(kv_p[2], axis=0),
            jnp.stack(conv_s, axis=0),
            jnp.stack(kv_s[0], axis=0), jnp.stack(kv_s[1], axis=0), jnp.stack(kv_s[2], axis=0),
            jnp.stack(vchunk_s, axis=0))
```

```python
import functools

import jax
import jax.numpy as jnp
from jax import lax
from jax.experimental import pallas as pl
from jax.experimental.pallas import tpu as pltpu

D_MODEL = 2048
BATCH = 4
SEQ = 2048
DEPTH = 4
DEC_BATCH = 8
PAST_LEN = 16384
W_A = 1024
CONV_W = 3
W_B = 1024
CHUNK = 128
N_B_GROUPS = 4
B_GROUP_W = W_B // N_B_GROUPS
N_C_HEADS = 4
HEAD_DIM = 128
ROT_DIM = HEAD_DIM // 4
ROPE_THETA = 500000.0
DIL_GROUPS = ((128, 1), (512, 4), (2048, 16))
N_DIL = len(DIL_GROUPS)
QB = 128
C_WIDTH = N_C_HEADS * HEAD_DIM
QKV_W = N_DIL * C_WIDTH
D_FF = ((-(-8 * D_MODEL // 3) + 255) // 256) * 256
IN_WIDTH = 3 * W_A + 2 * W_B + 3 * QKV_W + 3 * D_MODEL
EPS = 1e-6

GATE_W = 3 * D_MODEL
OFF_AB = GATE_W
OFF_Q = GATE_W + 3 * W_A + 2 * W_B
OFF_K = OFF_Q + QKV_W
OFF_V = OFF_K + QKV_W
ORIG_GATE_OFF = IN_WIDTH - GATE_W

V7X_VMEM_LIMIT = 60 * 1024 * 1024
NEG_BIG = -1e30

F32 = jnp.float32
BF16 = jnp.bfloat16


def _params(*sem):
    return pltpu.CompilerParams(dimension_semantics=sem, vmem_limit_bytes=V7X_VMEM_LIMIT)


def _rms(x, g):
    return x * lax.rsqrt(jnp.mean(x * x, axis=-1, keepdims=True) + EPS) * g


def _inproj_body(x_ref, g_ref, w_ref, c_ref, s1_ref, s2_ref, o_ref, h_ref, *, rope_lo, rope_hi):
    j = pl.program_id(1)

    @pl.when(j == 0)
    def _():
        h_ref[...] = _rms(x_ref[...], g_ref[...]).astype(BF16)

    acc = jnp.dot(h_ref[...], w_ref[...], preferred_element_type=F32)
    is_rope = jnp.logical_and(j >= rope_lo, j < rope_hi)

    @pl.when(is_rope)
    def _():
        for h in range(N_C_HEADS):
            sl = slice(h * HEAD_DIM, (h + 1) * HEAD_DIM)
            a = acc[:, sl]
            o_ref[:, sl] = (a * c_ref[...]
                            + pltpu.roll(a, HEAD_DIM - ROT_DIM // 2, 1) * s1_ref[...]
                            + pltpu.roll(a, ROT_DIM // 2, 1) * s2_ref[...])

    @pl.when(jnp.logical_not(is_rope))
    def _():
        o_ref[...] = acc


def _inproj(x, g, w, layer, tabs, tm, pos_blocks):
    m = x.shape[0]
    tn = C_WIDTH
    nj = IN_WIDTH // tn
    rot = ORIG_GATE_OFF // tn
    tab_spec = pl.BlockSpec((tm, HEAD_DIM), lambda i, j: (i % pos_blocks, 0))
    return pl.pallas_call(
        functools.partial(_inproj_body, rope_lo=OFF_Q // tn, rope_hi=OFF_V // tn),
        grid=(m // tm, nj),
        in_specs=[
            pl.BlockSpec((tm, D_MODEL), lambda i, j: (i, 0)),
            pl.BlockSpec((None, 1, D_MODEL), lambda i, j: (layer, 0, 0)),
            pl.BlockSpec((None, D_MODEL, tn), lambda i, j: (layer, 0, (j + rot) % nj)),
            tab_spec, tab_spec, tab_spec,
        ],
        out_specs=pl.BlockSpec((tm, tn), lambda i, j: (i, j)),
        out_shape=jax.ShapeDtypeStruct((m, IN_WIDTH), F32),
        scratch_shapes=[pltpu.VMEM((tm, D_MODEL), BF16)],
        compiler_params=_params("arbitrary", "arbitrary"),
        name="inproj",
    )(x, g, w, *tabs)


def _rope_tables(pos):
    inv_freq = ROPE_THETA ** (-jnp.arange(0, ROT_DIM, 2, dtype=jnp.float32) / ROT_DIM)
    ang = pos.astype(jnp.float32)[:, None] * inv_freq[None, :]
    cos, sin = jnp.cos(ang), jnp.sin(ang)
    n = pos.shape[0]
    half = ROT_DIM // 2
    rest = HEAD_DIM - ROT_DIM
    c = jnp.concatenate([cos, cos, jnp.ones((n, rest), F32)], axis=1)
    s1 = jnp.concatenate([-sin, jnp.zeros((n, half + rest), F32)], axis=1)
    s2 = jnp.concatenate([jnp.zeros((n, half), F32), sin, jnp.zeros((n, rest), F32)], axis=1)
    return c, s1, s2


def _mix_ab_body(ab_ref, ac_ref, ax_ref, bu_ref, bv_ref, cw_ref, lng_ref, lnb_ref, ws_ref, bs_ref,
                 ya_ref, yb_ref, cs_ref, zs_ref):
    ts = ab_ref.shape[0]

    @pl.when(pl.program_id(1) == 0)
    def _():
        zs_ref[0:8, :] = jnp.zeros((8, W_A), F32)

    z = ac_ref[...] * ax_ref[...]
    zs_ref[8:8 + ts, :] = z
    z1 = zs_ref[7:7 + ts, :]
    z2 = zs_ref[6:6 + ts, :]
    conv = cw_ref[0:1, :] * z2 + cw_ref[1:2, :] * z1 + cw_ref[2:3, :] * z
    ya_ref[...] = (ab_ref[...] * conv).astype(BF16)
    tail = zs_ref[ts:ts + 8, :]
    zs_ref[0:8, :] = tail
    cs_ref[0] = tail

    v = bv_ref[...]
    xc = v - jnp.mean(v, axis=-1, keepdims=True)
    var = jnp.mean(xc * xc, axis=-1, keepdims=True)
    vn = (xc * lax.rsqrt(var + EPS) * lng_ref[...] + lnb_ref[...]).astype(BF16)
    row = lax.broadcasted_iota(jnp.int32, (CHUNK, CHUNK), 0)
    col = lax.broadcasted_iota(jnp.int32, (CHUNK, CHUNK), 1)
    for g in range(N_B_GROUPS):
        wg = jnp.where(row >= col, ws_ref[g], 0.0).astype(BF16)
        bcol = bs_ref[:, g:g + 1]
        gs = slice(g * B_GROUP_W, (g + 1) * B_GROUP_W)
        for c in range(ts // CHUNK):
            rs = slice(c * CHUNK, (c + 1) * CHUNK)
            sg = jnp.dot(wg, vn[rs, gs], preferred_element_type=F32) + bcol
            yb_ref[rs, gs] = (bu_ref[rs, gs] * sg).astype(BF16)


def _mix_ab(proj, cw, lng, lnb, ws, bs_t, ts):
    nb = SEQ // ts
    blk = OFF_AB // W_A

    def seg(k):
        return pl.BlockSpec((ts, W_A), lambda b, s: (b * nb + s, blk + k))

    def full(shape):
        return pl.BlockSpec(shape, lambda b, s: (0,) * len(shape))

    rows = pl.BlockSpec((ts, W_A), lambda b, s: (b * nb + s, 0))
    return pl.pallas_call(
        _mix_ab_body,
        grid=(BATCH, nb),
        in_specs=[seg(0), seg(1), seg(2), seg(3), seg(4),
                  full((CONV_W, W_A)), full((1, W_B)), full((1, W_B)),
                  full((N_B_GROUPS, CHUNK, CHUNK)), full((CHUNK, N_B_GROUPS))],
        out_specs=[rows, rows, pl.BlockSpec((1, 8, W_A), lambda b, s: (b, 0, 0))],
        out_shape=[jax.ShapeDtypeStruct((BATCH * SEQ, W_A), BF16),
                   jax.ShapeDtypeStruct((BATCH * SEQ, W_B), BF16),
                   jax.ShapeDtypeStruct((BATCH, 8, W_A), F32)],
        scratch_shapes=[pltpu.VMEM((ts + 8, W_A), F32)],
        compiler_params=_params("arbitrary", "arbitrary"),
        name="mix_ab",
    )(proj, proj, proj, proj, proj, cw, lng, lnb, ws, bs_t)


def _attn_body(q0, q1, q2, k0, k1, k2, v0, v1, v2, y_ref, o0, o1, o2, l0, l1, l2):
    row = lax.broadcasted_iota(jnp.int32, (QB, QB), 0)
    col = lax.broadcasted_iota(jnp.int32, (QB, QB), 1)
    mask_c = col <= row
    mask_p = col >= row
    scale = HEAD_DIM ** -0.5
    nt = (((1,), (1,)), ((), ()))
    for (_, dil), q_ref, k_ref, v_ref, o_ref, l_ref in zip(
            DIL_GROUPS, (q0, q1, q2), (k0, k1, k2), (v0, v1, v2), (o0, o1, o2), (l0, l1, l2)):
        for r in range(dil):
            for c in range(SEQ // dil // QB):
                def rows(blk):
                    if dil == 1:
                        return pl.ds(blk * QB, QB)
                    return pl.ds(r + blk * QB * dil, QB, stride=dil)

                q = q_ref[rows(c), :].astype(BF16)
                s = lax.dot_general(q, k_ref[rows(c), :].astype(BF16), nt, preferred_element_type=F32) * scale
                s = jnp.where(mask_c, s, NEG_BIG)
                m = jnp.max(s, axis=-1, keepdims=True)
                if c > 0:
                    sp = lax.dot_general(q, k_ref[rows(c - 1), :].astype(BF16), nt,
                                         preferred_element_type=F32) * scale
                    sp = jnp.where(mask_p, sp, NEG_BIG)
                    m = jnp.maximum(m, jnp.max(sp, axis=-1, keepdims=True))
                p = jnp.exp(s - m)
                den = jnp.sum(p, axis=-1, keepdims=True)
                o = jnp.dot(p.astype(BF16), v_ref[rows(c), :].astype(BF16), preferred_element_type=F32)
                if c > 0:
                    pp = jnp.exp(sp - m)
                    den = den + jnp.sum(pp, axis=-1, keepdims=True)
                    o = o + jnp.dot(pp.astype(BF16), v_ref[rows(c - 1), :].astype(BF16),
                                    preferred_element_type=F32)
                o_ref[rows(c), :] = o / den
                l_ref[rows(c), :] = jnp.broadcast_to(m + jnp.log(den), (QB, HEAD_DIM))
    y_ref[...] = _combine_groups(o0[...], o1[...], o2[...], l0[...], l1[...], l2[...]).astype(BF16)


def _attn(proj):
    def seg(off, g):
        base = (off + g * C_WIDTH) // HEAD_DIM
        return pl.BlockSpec((SEQ, HEAD_DIM), lambda b, h: (b, base + h))

    specs = [seg(off, g) for off in (OFF_Q, OFF_K, OFF_V) for g in range(N_DIL)]
    return pl.pallas_call(
        _attn_body,
        grid=(BATCH, N_C_HEADS),
        in_specs=specs,
        out_specs=pl.BlockSpec((SEQ, HEAD_DIM), lambda b, h: (b, h)),
        out_shape=jax.ShapeDtypeStruct((BATCH * SEQ, C_WIDTH), BF16),
        scratch_shapes=[pltpu.VMEM((SEQ, HEAD_DIM), F32)] * (2 * N_DIL),
        compiler_params=_params("arbitrary", "arbitrary"),
        name="attn",
    )(*([proj] * (3 * N_DIL)))


def _kv_pack_body(*refs):
    k_refs, v_refs, out_refs = refs[0:N_DIL], refs[N_DIL:2 * N_DIL], refs[-N_DIL:]
    for k_ref, v_ref, o_ref in zip(k_refs, v_refs, out_refs):
        for h in range(N_C_HEADS):
            sl = slice(h * HEAD_DIM, (h + 1) * HEAD_DIM)
            o_ref[:, 0, h, :] = k_ref[:, sl]
            o_ref[:, 1, h, :] = v_ref[:, sl]


def _kv_pack(proj, layer, prev):
    nb = SEQ // QB
    keeps = [min(win, SEQ) for win, _ in DIL_GROUPS]

    def src(off, g):
        return pl.BlockSpec((QB, C_WIDTH), lambda b, s: (b * nb + s, (off + g * C_WIDTH) // C_WIDTH))

    def dst(keep):
        first = (SEQ - keep) // QB
        return pl.BlockSpec((None, None, QB, 2, N_C_HEADS, HEAD_DIM),
                            lambda b, s: (layer, b, jnp.maximum(s - first, 0), 0, 0, 0))

    in_specs = [src(OFF_K, g) for g in range(N_DIL)] + [src(OFF_V, g) for g in range(N_DIL)]
    args = [proj] * (2 * N_DIL)
    aliases = {}
    if prev is not None:
        in_specs += [pl.BlockSpec(memory_space=pl.ANY)] * N_DIL
        aliases = {2 * N_DIL + g: g for g in range(N_DIL)}
        args += list(prev)
    return pl.pallas_call(
        _kv_pack_body,
        grid=(BATCH, nb),
        in_specs=in_specs,
        out_specs=[dst(keep) for keep in keeps],
        out_shape=[jax.ShapeDtypeStruct((DEPTH, BATCH, keep, 2, N_C_HEADS, HEAD_DIM), F32) for keep in keeps],
        input_output_aliases=aliases,
        compiler_params=_params("arbitrary", "arbitrary"),
        name="kv_pack",
    )(*args)


def _mix_sample_body(p_ref, qkv_ref, st_ref, cw_ref, lng_ref, lnb_ref, wsc_ref, bsc_ref, c0_ref, c1_ref, c2_ref,
                     ya_ref, yb_ref, yc_ref, cs_ref, vn_ref):
    @pl.when(pl.program_id(0) == 0)
    def _():
        def seg(off, w):
            return p_ref[:, off:off + w]

        z = seg(OFF_AB + W_A, W_A) * seg(OFF_AB + 2 * W_A, W_A)
        st0 = st_ref[0, :, 0:W_A]
        st1 = st_ref[0, :, W_A:2 * W_A]
        conv = cw_ref[0:1, :] * st0 + cw_ref[1:2, :] * st1 + cw_ref[2:3, :] * z
        ya_ref[...] = (seg(OFF_AB, W_A) * conv).astype(BF16)
        cs_ref[:, 0:W_A] = st1
        cs_ref[:, W_A:2 * W_A] = z

        v = seg(OFF_AB + 3 * W_A + W_B, W_B)
        xc = v - jnp.mean(v, axis=-1, keepdims=True)
        var = jnp.mean(xc * xc, axis=-1, keepdims=True)
        vn = xc * lax.rsqrt(var + EPS) * lng_ref[...] + lnb_ref[...]
        vn_ref[...] = vn
        yb_ref[...] = (seg(OFF_AB + 3 * W_A, W_B) * (wsc_ref[...] * vn + bsc_ref[...])).astype(BF16)

    scale = HEAD_DIM ** -0.5
    outs, lses = [], []
    for g, c_ref in enumerate((c0_ref, c1_ref, c2_ref)):
        q = qkv_ref[g]
        kn = qkv_ref[N_DIL + g]
        vnew = qkv_ref[2 * N_DIL + g]
        kc = c_ref[:, 0]
        vc = c_ref[:, 1]
        s_c = jnp.sum(q[None] * kc, axis=-1, keepdims=True) * scale
        s_n = jnp.sum(q * kn, axis=-1, keepdims=True) * scale
        m = jnp.maximum(jnp.max(s_c, axis=0), s_n)
        p_c = jnp.exp(s_c - m[None])
        p_n = jnp.exp(s_n - m)
        den = jnp.sum(p_c, axis=0) + p_n
        outs.append((jnp.sum(p_c * vc, axis=0) + p_n * vnew) / den)
        lses.append(m + jnp.log(den))
    yc_ref[...] = _combine_groups(*outs, *lses)


def _mix_sample(proj_s, st, cw, lng, lnb, wsc, bsc, caches, layer):
    def full(shape):
        return pl.BlockSpec(shape, lambda b: (0,) * len(shape))

    n_keys = DIL_GROUPS[0][0] // DIL_GROUPS[0][1]
    cache_specs = [pl.BlockSpec((None, None, n_keys, None, 2, N_C_HEADS, HEAD_DIM),
                                lambda b: (layer, b, 0, 0, 0, 0, 0)) for _ in caches]
    qkv = proj_s[:, OFF_Q:].reshape(DEC_BATCH, 3 * N_DIL, N_C_HEADS, HEAD_DIM)
    return pl.pallas_call(
        _mix_sample_body,
        grid=(DEC_BATCH,),
        in_specs=[full((DEC_BATCH, IN_WIDTH)),
                  pl.BlockSpec((None, 3 * N_DIL, N_C_HEADS, HEAD_DIM), lambda b: (b, 0, 0, 0)),
                  pl.BlockSpec((1, DEC_BATCH, 2 * W_A), lambda b: (layer, 0, 0)),
                  full((CONV_W, W_A)), full((1, W_B)), full((1, W_B)), full((1, W_B)), full((1, W_B))]
                 + cache_specs,
        out_specs=[full((DEC_BATCH, W_A)), full((DEC_BATCH, W_B)),
                   pl.BlockSpec((None, N_C_HEADS, HEAD_DIM), lambda b: (b, 0, 0)),
                   full((DEC_BATCH, 2 * W_A)), full((DEC_BATCH, W_B))],
        out_shape=[jax.ShapeDtypeStruct((DEC_BATCH, W_A), BF16),
                   jax.ShapeDtypeStruct((DEC_BATCH, W_B), BF16),
                   jax.ShapeDtypeStruct((DEC_BATCH, N_C_HEADS, HEAD_DIM), F32),
                   jax.ShapeDtypeStruct((DEC_BATCH, 2 * W_A), F32),
                   jax.ShapeDtypeStruct((DEC_BATCH, W_B), F32)],
        compiler_params=_params("arbitrary"),
        name="mix_sample",
    )(proj_s, qkv, st, cw, lng, lnb, wsc, bsc, *caches)


def _combine_groups(o0, o1, o2, l0, l1, l2):
    m = jnp.maximum(jnp.maximum(l0, l1), l2)
    e0, e1, e2 = jnp.exp(l0 - m), jnp.exp(l1 - m), jnp.exp(l2 - m)
    return (e0 * o0 + e1 * o1 + e2 * o2) / (e0 + e1 + e2)


def _merge_tail(x, ga, gb, gc, ya, yb, yc, wa_ref, wb_ref, wc_ref, wo_ref, gp_ref):
    mm = (jax.nn.sigmoid(ga) * jnp.dot(ya, wa_ref[...], preferred_element_type=F32)
          + jax.nn.sigmoid(gb) * jnp.dot(yb, wb_ref[...], preferred_element_type=F32)
          + jax.nn.sigmoid(gc) * jnp.dot(yc, wc_ref[...], preferred_element_type=F32))
    r = jnp.dot(mm.astype(BF16), wo_ref[...], preferred_element_type=F32)
    return x + _rms(r, gp_ref[...])


def _merge_body(ga_ref, gb_ref, gc_ref, ya_ref, yb_ref, yc_ref,
                x_ref, wa_ref, wb_ref, wc_ref, wo_ref, gp_ref, out_ref):
    out_ref[...] = _merge_tail(x_ref[...], ga_ref[...], gb_ref[...], gc_ref[...], ya_ref[...], yb_ref[...],
                               yc_ref[...].astype(BF16), wa_ref, wb_ref, wc_ref, wo_ref, gp_ref)


def _merge(proj, ya, yb, yc, x, wa, wb, wc, wo, gp, layer, tm):
    m = x.shape[0]

    def rows(w, k=0):
        return pl.BlockSpec((tm, w), lambda i: (i, k))

    def resident(k):
        return pl.BlockSpec((None, k, D_MODEL), lambda i: (layer, 0, 0), pipeline_mode=pl.Buffered(1))

    return pl.pallas_call(
        _merge_body,
        grid=(m // tm,),
        in_specs=[rows(D_MODEL, 0), rows(D_MODEL, 1), rows(D_MODEL, 2), rows(W_A), rows(W_B), rows(C_WIDTH),
                  rows(D_MODEL),
                  resident(W_A), resident(W_B), resident(C_WIDTH), resident(D_MODEL), resident(1)],
        out_specs=rows(D_MODEL),
        out_shape=jax.ShapeDtypeStruct((m, D_MODEL), F32),
        compiler_params=_params("arbitrary"),
        name="merge",
    )(proj, proj, proj, ya, yb, yc, x, wa, wb, wc, wo, gp)


def _ffn_body(x_ref, g1_ref, wg_ref, wu_ref, wo_ref, g2_ref, out_ref, h_ref, acc_ref):
    j = pl.program_id(1)

    @pl.when(j == 0)
    def _():
        h_ref[...] = _rms(x_ref[...], g1_ref[...]).astype(BF16)
        acc_ref[...] = jnp.zeros_like(acc_ref)

    h = h_ref[...]
    gate = jnp.dot(h, wg_ref[...], preferred_element_type=F32)
    up = jnp.dot(h, wu_ref[...], preferred_element_type=F32)
    act = (gate * jax.nn.sigmoid(gate) * up).astype(BF16)
    acc_ref[...] += jnp.dot(act, wo_ref[...], preferred_element_type=F32)

    @pl.when(j == pl.num_programs(1) - 1)
    def _():
        out_ref[...] = x_ref[...] + _rms(acc_ref[...], g2_ref[...])


def _ffn(x, g1, w_in, w_out, g2, layer, tm, tf):
    m = x.shape[0]
    nj = D_FF // tf
    return pl.pallas_call(
        _ffn_body,
        grid=(m // tm, nj),
        in_specs=[
            pl.BlockSpec((tm, D_MODEL), lambda i, j: (i, 0)),
            pl.BlockSpec((None, 1, D_MODEL), lambda i, j: (layer, 0, 0)),
            pl.BlockSpec((None, D_MODEL, tf), lambda i, j: (layer, 0, j)),
            pl.BlockSpec((None, D_MODEL, tf), lambda i, j: (layer, 0, nj + j)),
            pl.BlockSpec((None, tf, D_MODEL), lambda i, j: (layer, j, 0)),
            pl.BlockSpec((None, 1, D_MODEL), lambda i, j: (layer, 0, 0)),
        ],
        out_specs=pl.BlockSpec((tm, D_MODEL), lambda i, j: (i, 0)),
        out_shape=jax.ShapeDtypeStruct((m, D_MODEL), F32),
        scratch_shapes=[pltpu.VMEM((tm, D_MODEL), BF16), pltpu.VMEM((tm, D_MODEL), F32)],
        compiler_params=_params("arbitrary", "arbitrary"),
        name="ffn",
    )(x, g1, w_in, w_in, w_out, g2)


TM_INPROJ = 1024
TS_MIX = 512
TM_MERGE = 256
TM_FFN = 512
TF_FFN = 512


def kernel(x_prompt, x_sample, state_conv, cache_kv_w128, cache_kv_w512, cache_kv_w2048, g_pre_mix, w_in, conv_w, ln_g, ln_b, w_s, b_s, w_a_out, w_b_out, w_c_out, w_o, g_post_mix, g_pre_ffn, w_ffn_in, w_ffn_out, g_post_ffn):
    w_in_b, wa_b, wb_b, wc_b, wo_b, wfi_b, wfo_b = (
        w.astype(BF16) for w in (w_in, w_a_out, w_b_out, w_c_out, w_o, w_ffn_in, w_ffn_out))
    g_pre_mix, g_post_mix, g_pre_ffn, g_post_ffn = (
        g.reshape(DEPTH, 1, D_MODEL) for g in (g_pre_mix, g_post_mix, g_pre_ffn, g_post_ffn))

    tabs_p = _rope_tables(jnp.arange(SEQ, dtype=jnp.int32))
    tabs_s = _rope_tables(jnp.full((DEC_BATCH,), PAST_LEN, dtype=jnp.int32))

    n_keys = DIL_GROUPS[0][0] // DIL_GROUPS[0][1]
    caches = tuple(c.reshape(DEPTH, DEC_BATCH, n_keys, dil, 2, N_C_HEADS, HEAD_DIM)
                   for c, (_, dil) in zip((cache_kv_w128, cache_kv_w512, cache_kv_w2048), DIL_GROUPS))
    st_all = state_conv.reshape(DEPTH, DEC_BATCH, (CONV_W - 1) * W_A)
    bs_t = jnp.swapaxes(b_s, 1, 2)
    wsc = jnp.repeat(w_s[:, :, 0, 0], B_GROUP_W, axis=1)
    bsc = jnp.repeat(b_s[:, :, 0], B_GROUP_W, axis=1)

    xp = x_prompt.reshape(BATCH * SEQ, D_MODEL)
    xs = x_sample.reshape(DEC_BATCH, D_MODEL)
    conv_p, conv_s, vchunk_s = [], [], []
    kv_p = None
    kv_s = [[] for _ in DIL_GROUPS]

    def row(a, l):
        return a[l][None, :]

    for l in range(DEPTH):
        proj = _inproj(xp, g_pre_mix, w_in_b, l, tabs_p, TM_INPROJ, SEQ // TM_INPROJ)
        ya, yb, ctail = _mix_ab(proj, conv_w[l], row(ln_g, l), row(ln_b, l), w_s[l], bs_t[l], TS_MIX)
        yc = _attn(proj)
        kv_p = _kv_pack(proj, l, kv_p)
        conv_p.append(ctail[:, 8 - (CONV_W - 1):])
        xp = _merge(proj, ya, yb, yc, xp, wa_b, wb_b, wc_b, wo_b, g_post_mix, l, TM_MERGE)
        xp = _ffn(xp, g_pre_ffn, wfi_b, wfo_b, g_post_ffn, l, TM_FFN, TF_FFN)

        proj_s = _inproj(xs, g_pre_mix, w_in_b, l, tabs_s, DEC_BATCH, 1)
        ya_s, yb_s, yc_s, cst, vn = _mix_sample(proj_s, st_all, conv_w[l], row(ln_g, l), row(ln_b, l),
                                                row(wsc, l), row(bsc, l), caches, l)
        for g in range(N_DIL):
            k = proj_s[:, OFF_K + g * C_WIDTH:OFF_K + (g + 1) * C_WIDTH]
            v = proj_s[:, OFF_V + g * C_WIDTH:OFF_V + (g + 1) * C_WIDTH]
            kv_s[g].append(jnp.stack([k.reshape(DEC_BATCH, 1, N_C_HEADS, HEAD_DIM),
                                      v.reshape(DEC_BATCH, 1, N_C_HEADS, HEAD_DIM)], axis=2))
        conv_s.append(cst.reshape(DEC_BATCH, CONV_W - 1, W_A))
        vchunk_s.append(vn.reshape(DEC_BATCH, 1, W_B))
        xs = _merge(proj_s, ya_s, yb_s, yc_s.reshape(DEC_BATCH, C_WIDTH), xs, wa_b, wb_b, wc_b, wo_b,
                    g_post_mix, l, DEC_BATCH)
        xs = _ffn(xs, g_pre_ffn, wfi_b, wfo_b, g_post_ffn, l, DEC_BATCH, TF_FFN)

    return (xp.reshape(BATCH, SEQ, D_MODEL), xs.reshape(DEC_BATCH, 1, D_MODEL),
            jnp.stack(conv_p, axis=0),
            kv_p[0], kv_p[1], kv_p[2],
            jnp.stack(conv_s, axis=0),
            jnp.stack(kv_s[0], axis=0), jnp.stack(kv_s[1], axis=0), jnp.stack(kv_s[2], axis=0),
            jnp.stack(vchunk_s, axis=0))
```

```python
import functools

import jax
import jax.numpy as jnp
from jax import lax
from jax.experimental import pallas as pl
from jax.experimental.pallas import tpu as pltpu

D_MODEL = 2048
BATCH = 4
SEQ = 2048
DEPTH = 4
DEC_BATCH = 8
PAST_LEN = 16384
W_A = 1024
CONV_W = 3
W_B = 1024
CHUNK = 128
N_B_GROUPS = 4
B_GROUP_W = W_B // N_B_GROUPS
N_C_HEADS = 4
HEAD_DIM = 128
ROT_DIM = HEAD_DIM // 4
ROPE_THETA = 500000.0
DIL_GROUPS = ((128, 1), (512, 4), (2048, 16))
N_DIL = len(DIL_GROUPS)
QB = 128
C_WIDTH = N_C_HEADS * HEAD_DIM
QKV_W = N_DIL * C_WIDTH
D_FF = ((-(-8 * D_MODEL // 3) + 255) // 256) * 256
IN_WIDTH = 3 * W_A + 2 * W_B + 3 * QKV_W + 3 * D_MODEL
EPS = 1e-6

GATE_W = 3 * D_MODEL
OFF_AB = GATE_W
OFF_Q = GATE_W + 3 * W_A + 2 * W_B
OFF_K = OFF_Q + QKV_W
OFF_V = OFF_K + QKV_W
ORIG_GATE_OFF = IN_WIDTH - GATE_W

ROPE_ROWS = 256
V7X_VMEM_LIMIT = 60 * 1024 * 1024
NEG_BIG = -1e30

F32 = jnp.float32
BF16 = jnp.bfloat16


def _params(*sem):
    return pltpu.CompilerParams(dimension_semantics=sem, vmem_limit_bytes=V7X_VMEM_LIMIT)


def _rms(x, g):
    return x * lax.rsqrt(jnp.mean(x * x, axis=-1, keepdims=True) + EPS) * g


def _inproj_body(x_ref, g_ref, w_ref, c_ref, s1_ref, s2_ref, o_ref, h_ref, *, rope_lo, rope_hi):
    j = pl.program_id(1)

    @pl.when(j == 0)
    def _():
        h_ref[...] = _rms(x_ref[...], g_ref[...]).astype(BF16)

    is_rope = jnp.logical_and(j >= rope_lo, j < rope_hi)
    tm = x_ref.shape[0]
    rc = min(tm, ROPE_ROWS)

    @pl.when(is_rope)
    def _():
        for r in range(tm // rc):
            rs = slice(r * rc, (r + 1) * rc)
            acc = jnp.dot(h_ref[rs, :], w_ref[...], preferred_element_type=F32)
            for h in range(N_C_HEADS):
                sl = slice(h * HEAD_DIM, (h + 1) * HEAD_DIM)
                a = acc[:, sl]
                o_ref[rs, sl] = (a * c_ref[rs, :]
                                 + pltpu.roll(a, HEAD_DIM - ROT_DIM // 2, 1) * s1_ref[rs, :]
                                 + pltpu.roll(a, ROT_DIM // 2, 1) * s2_ref[rs, :])

    @pl.when(jnp.logical_not(is_rope))
    def _():
        o_ref[...] = jnp.dot(h_ref[...], w_ref[...], preferred_element_type=F32)


def _inproj(x, g, w, layer, tabs, tm, pos_blocks):
    m = x.shape[0]
    tn = C_WIDTH
    nj = IN_WIDTH // tn
    rot = ORIG_GATE_OFF // tn
    tab_spec = pl.BlockSpec((tm, HEAD_DIM), lambda i, j: (i % pos_blocks, 0))
    return pl.pallas_call(
        functools.partial(_inproj_body, rope_lo=OFF_Q // tn, rope_hi=OFF_V // tn),
        grid=(m // tm, nj),
        in_specs=[
            pl.BlockSpec((tm, D_MODEL), lambda i, j: (i, 0)),
            pl.BlockSpec((None, 1, D_MODEL), lambda i, j: (layer, 0, 0)),
            pl.BlockSpec((None, D_MODEL, tn), lambda i, j: (layer, 0, (j + rot) % nj)),
            tab_spec, tab_spec, tab_spec,
        ],
        out_specs=pl.BlockSpec((tm, tn), lambda i, j: (i, j)),
        out_shape=jax.ShapeDtypeStruct((m, IN_WIDTH), F32),
        scratch_shapes=[pltpu.VMEM((tm, D_MODEL), BF16)],
        compiler_params=_params("arbitrary", "arbitrary"),
        name="inproj",
    )(x, g, w, *tabs)


def _rope_tables(pos):
    inv_freq = ROPE_THETA ** (-jnp.arange(0, ROT_DIM, 2, dtype=jnp.float32) / ROT_DIM)
    ang = pos.astype(jnp.float32)[:, None] * inv_freq[None, :]
    cos, sin = jnp.cos(ang), jnp.sin(ang)
    n = pos.shape[0]
    half = ROT_DIM // 2
    rest = HEAD_DIM - ROT_DIM
    c = jnp.concatenate([cos, cos, jnp.ones((n, rest), F32)], axis=1)
    s1 = jnp.concatenate([-sin, jnp.zeros((n, half + rest), F32)], axis=1)
    s2 = jnp.concatenate([jnp.zeros((n, half), F32), sin, jnp.zeros((n, rest), F32)], axis=1)
    return c, s1, s2


def _mix_ab_body(ab_ref, ac_ref, ax_ref, bu_ref, bv_ref, cw_ref, lng_ref, lnb_ref, ws_ref, bs_ref,
                 ya_ref, yb_ref, cs_ref, zs_ref):
    ts = ab_ref.shape[0]

    @pl.when(pl.program_id(1) == 0)
    def _():
        zs_ref[0:8, :] = jnp.zeros((8, W_A), F32)

    z = ac_ref[...] * ax_ref[...]
    zs_ref[8:8 + ts, :] = z
    z1 = zs_ref[7:7 + ts, :]
    z2 = zs_ref[6:6 + ts, :]
    conv = cw_ref[0:1, :] * z2 + cw_ref[1:2, :] * z1 + cw_ref[2:3, :] * z
    ya_ref[...] = (ab_ref[...] * conv).astype(BF16)
    tail = zs_ref[ts:ts + 8, :]
    zs_ref[0:8, :] = tail
    cs_ref[0] = tail

    v = bv_ref[...]
    xc = v - jnp.mean(v, axis=-1, keepdims=True)
    var = jnp.mean(xc * xc, axis=-1, keepdims=True)
    vn = (xc * lax.rsqrt(var + EPS) * lng_ref[...] + lnb_ref[...]).astype(BF16)
    row = lax.broadcasted_iota(jnp.int32, (CHUNK, CHUNK), 0)
    col = lax.broadcasted_iota(jnp.int32, (CHUNK, CHUNK), 1)
    for g in range(N_B_GROUPS):
        wg = jnp.where(row >= col, ws_ref[g], 0.0).astype(BF16)
        bcol = bs_ref[:, g:g + 1]
        gs = slice(g * B_GROUP_W, (g + 1) * B_GROUP_W)
        for c in range(ts // CHUNK):
            rs = slice(c * CHUNK, (c + 1) * CHUNK)
            sg = jnp.dot(wg, vn[rs, gs], preferred_element_type=F32) + bcol
            yb_ref[rs, gs] = (bu_ref[rs, gs] * sg).astype(BF16)


def _mix_ab(proj, cw, lng, lnb, ws, bs_t, ts):
    nb = SEQ // ts
    blk = OFF_AB // W_A

    def seg(k):
        return pl.BlockSpec((ts, W_A), lambda b, s: (b * nb + s, blk + k))

    def full(shape):
        return pl.BlockSpec(shape, lambda b, s: (0,) * len(shape))

    rows = pl.BlockSpec((ts, W_A), lambda b, s: (b * nb + s, 0))
    return pl.pallas_call(
        _mix_ab_body,
        grid=(BATCH, nb),
        in_specs=[seg(0), seg(1), seg(2), seg(3), seg(4),
                  full((CONV_W, W_A)), full((1, W_B)), full((1, W_B)),
                  full((N_B_GROUPS, CHUNK, CHUNK)), full((CHUNK, N_B_GROUPS))],
        out_specs=[rows, rows, pl.BlockSpec((1, 8, W_A), lambda b, s: (b, 0, 0))],
        out_shape=[jax.ShapeDtypeStruct((BATCH * SEQ, W_A), BF16),
                   jax.ShapeDtypeStruct((BATCH * SEQ, W_B), BF16),
                   jax.ShapeDtypeStruct((BATCH, 8, W_A), F32)],
        scratch_shapes=[pltpu.VMEM((ts + 8, W_A), F32)],
        compiler_params=_params("arbitrary", "arbitrary"),
        name="mix_ab",
    )(proj, proj, proj, proj, proj, cw, lng, lnb, ws, bs_t)


def _attn_body(q0, q1, q2, k0, k1, k2, v0, v1, v2, y_ref,
               qc, kc, vc, s_scr, p_scr, inv_scr, o0, o1, o2, l0, l1, l2):
    row = lax.broadcasted_iota(jnp.int32, (QB, 2 * QB), 0)
    col = lax.broadcasted_iota(jnp.int32, (QB, 2 * QB), 1)
    band = jnp.logical_and(col >= row, col <= row + QB)
    band_first = jnp.logical_and(band, col >= QB)
    scale = HEAD_DIM ** -0.5
    nt = (((1,), (1,)), ((), ()))
    zero_blk = jnp.zeros((QB, HEAD_DIM), BF16)
    for (_, dil), q_ref, k_ref, v_ref, o_ref, l_ref in zip(
            DIL_GROUPS, (q0, q1, q2), (k0, k1, k2), (v0, v1, v2), (o0, o1, o2), (l0, l1, l2)):
        L = SEQ // dil
        nblk = L // QB

        def tok_rows(r, c):
            if dil == 1:
                return pl.ds(c * QB, QB)
            return pl.ds(r + c * QB * dil, QB, stride=dil)

        for r in range(dil):
            src = pl.ds(0, L) if dil == 1 else pl.ds(r, L, stride=dil)
            base = r * (L + QB)
            qc[r * L:(r + 1) * L, :] = q_ref[src, :].astype(BF16)
            kc[base:base + QB, :] = zero_blk
            vc[base:base + QB, :] = zero_blk
            kc[base + QB:base + QB + L, :] = k_ref[src, :].astype(BF16)
            vc[base + QB:base + QB + L, :] = v_ref[src, :].astype(BF16)

        for r in range(dil):
            for c in range(nblk):
                t = r * nblk + c
                kk = kc[r * (L + QB) + c * QB:r * (L + QB) + (c + 2) * QB, :]
                s = lax.dot_general(qc[t * QB:(t + 1) * QB, :], kk, nt, preferred_element_type=F32) * scale
                s_scr[t * QB:(t + 1) * QB, :] = jnp.where(band_first if c == 0 else band, s, NEG_BIG)

        for r in range(dil):
            for c in range(nblk):
                t = r * nblk + c
                s = s_scr[t * QB:(t + 1) * QB, :]
                m = jnp.max(s, axis=-1, keepdims=True)
                p = jnp.exp(s - m)
                den = jnp.sum(p, axis=-1, keepdims=True)
                p_scr[t * QB:(t + 1) * QB, :] = p.astype(BF16)
                inv_scr[t * QB:(t + 1) * QB, :] = jnp.broadcast_to(1.0 / den, (QB, HEAD_DIM))
                l_ref[tok_rows(r, c), :] = jnp.broadcast_to(m + jnp.log(den), (QB, HEAD_DIM))

        for r in range(dil):
            for c in range(nblk):
                t = r * nblk + c
                vv = vc[r * (L + QB) + c * QB:r * (L + QB) + (c + 2) * QB, :]
                o = jnp.dot(p_scr[t * QB:(t + 1) * QB, :], vv, preferred_element_type=F32)
                o_ref[tok_rows(r, c), :] = o * inv_scr[t * QB:(t + 1) * QB, :]
    y_ref[...] = _combine_groups(o0[...], o1[...], o2[...], l0[...], l1[...], l2[...]).astype(BF16)


def _attn(proj):
    def seg(off, g):
        base = (off + g * C_WIDTH) // HEAD_DIM
        return pl.BlockSpec((SEQ, HEAD_DIM), lambda b, h: (b, base + h))

    specs = [seg(off, g) for off in (OFF_Q, OFF_K, OFF_V) for g in range(N_DIL)]
    return pl.pallas_call(
        _attn_body,
        grid=(BATCH, N_C_HEADS),
        in_specs=specs,
        out_specs=pl.BlockSpec((SEQ, HEAD_DIM), lambda b, h: (b, h)),
        out_shape=jax.ShapeDtypeStruct((BATCH * SEQ, C_WIDTH), BF16),
        scratch_shapes=[pltpu.VMEM((SEQ, HEAD_DIM), BF16),
                        pltpu.VMEM((2 * SEQ, HEAD_DIM), BF16),
                        pltpu.VMEM((2 * SEQ, HEAD_DIM), BF16),
                        pltpu.VMEM((SEQ, 2 * QB), F32),
                        pltpu.VMEM((SEQ, 2 * QB), BF16),
                        pltpu.VMEM((SEQ, HEAD_DIM), F32)]
                       + [pltpu.VMEM((SEQ, HEAD_DIM), F32)] * (2 * N_DIL),
        compiler_params=_params("arbitrary", "arbitrary"),
        name="attn",
    )(*([proj] * (3 * N_DIL)))


def _kv_pack_body(*refs):
    k_refs, v_refs, out_refs = refs[0:N_DIL], refs[N_DIL:2 * N_DIL], refs[-N_DIL:]
    for k_ref, v_ref, o_ref in zip(k_refs, v_refs, out_refs):
        for h in range(N_C_HEADS):
            sl = slice(h * HEAD_DIM, (h + 1) * HEAD_DIM)
            o_ref[:, 0, h, :] = k_ref[:, sl]
            o_ref[:, 1, h, :] = v_ref[:, sl]


def _kv_pack(proj, layer, prev):
    nb = SEQ // QB
    keeps = [min(win, SEQ) for win, _ in DIL_GROUPS]

    def src(off, g):
        return pl.BlockSpec((QB, C_WIDTH), lambda b, s: (b * nb + s, (off + g * C_WIDTH) // C_WIDTH))

    def dst(keep):
        first = (SEQ - keep) // QB
        return pl.BlockSpec((None, None, QB, 2, N_C_HEADS, HEAD_DIM),
                            lambda b, s: (layer, b, jnp.maximum(s - first, 0), 0, 0, 0))

    in_specs = [src(OFF_K, g) for g in range(N_DIL)] + [src(OFF_V, g) for g in range(N_DIL)]
    args = [proj] * (2 * N_DIL)
    aliases = {}
    if prev is not None:
        in_specs += [pl.BlockSpec(memory_space=pl.ANY)] * N_DIL
        aliases = {2 * N_DIL + g: g for g in range(N_DIL)}
        args += list(prev)
    return pl.pallas_call(
        _kv_pack_body,
        grid=(BATCH, nb),
        in_specs=in_specs,
        out_specs=[dst(keep) for keep in keeps],
        out_shape=[jax.ShapeDtypeStruct((DEPTH, BATCH, keep, 2, N_C_HEADS, HEAD_DIM), F32) for keep in keeps],
        input_output_aliases=aliases,
        compiler_params=_params("arbitrary", "arbitrary"),
        name="kv_pack",
    )(*args)


def _mix_sample_body(p_ref, qkv_ref, st_ref, cw_ref, lng_ref, lnb_ref, wsc_ref, bsc_ref, c0_ref, c1_ref, c2_ref,
                     ya_ref, yb_ref, yc_ref, cs_ref, vn_ref):
    @pl.when(pl.program_id(0) == 0)
    def _():
        def seg(off, w):
            return p_ref[:, off:off + w]

        z = seg(OFF_AB + W_A, W_A) * seg(OFF_AB + 2 * W_A, W_A)
        st0 = st_ref[0, :, 0:W_A]
        st1 = st_ref[0, :, W_A:2 * W_A]
        conv = cw_ref[0:1, :] * st0 + cw_ref[1:2, :] * st1 + cw_ref[2:3, :] * z
        ya_ref[...] = (seg(OFF_AB, W_A) * conv).astype(BF16)
        cs_ref[:, 0:W_A] = st1
        cs_ref[:, W_A:2 * W_A] = z

        v = seg(OFF_AB + 3 * W_A + W_B, W_B)
        xc = v - jnp.mean(v, axis=-1, keepdims=True)
        var = jnp.mean(xc * xc, axis=-1, keepdims=True)
        vn = xc * lax.rsqrt(var + EPS) * lng_ref[...] + lnb_ref[...]
        vn_ref[...] = vn
        yb_ref[...] = (seg(OFF_AB + 3 * W_A, W_B) * (wsc_ref[...] * vn + bsc_ref[...])).astype(BF16)

    scale = HEAD_DIM ** -0.5
    outs, lses = [], []
    for g, c_ref in enumerate((c0_ref, c1_ref, c2_ref)):
        q = qkv_ref[g]
        kn = qkv_ref[N_DIL + g]
        vnew = qkv_ref[2 * N_DIL + g]
        kc = c_ref[:, 0]
        vc = c_ref[:, 1]
        s_c = jnp.sum(q[None] * kc, axis=-1, keepdims=True) * scale
        s_n = jnp.sum(q * kn, axis=-1, keepdims=True) * scale
        m = jnp.maximum(jnp.max(s_c, axis=0), s_n)
        p_c = jnp.exp(s_c - m[None])
        p_n = jnp.exp(s_n - m)
        den = jnp.sum(p_c, axis=0) + p_n
        outs.append((jnp.sum(p_c * vc, axis=0) + p_n * vnew) / den)
        lses.append(m + jnp.log(den))
    yc_ref[...] = _combine_groups(*outs, *lses)


def _mix_sample(proj_s, st, cw, lng, lnb, wsc, bsc, caches, layer):
    def full(shape):
        return pl.BlockSpec(shape, lambda b: (0,) * len(shape))

    n_keys = DIL_GROUPS[0][0] // DIL_GROUPS[0][1]
    cache_specs = [pl.BlockSpec((None, None, n_keys, None, 2, N_C_HEADS, HEAD_DIM),
                                lambda b: (layer, b, 0, 0, 0, 0, 0)) for _ in caches]
    qkv = proj_s[:, OFF_Q:].reshape(DEC_BATCH, 3 * N_DIL, N_C_HEADS, HEAD_DIM)
    return pl.pallas_call(
        _mix_sample_body,
        grid=(DEC_BATCH,),
        in_specs=[full((DEC_BATCH, IN_WIDTH)),
                  pl.BlockSpec((None, 3 * N_DIL, N_C_HEADS, HEAD_DIM), lambda b: (b, 0, 0, 0)),
                  pl.BlockSpec((1, DEC_BATCH, 2 * W_A), lambda b: (layer, 0, 0)),
                  full((CONV_W, W_A)), full((1, W_B)), full((1, W_B)), full((1, W_B)), full((1, W_B))]
                 + cache_specs,
        out_specs=[full((DEC_BATCH, W_A)), full((DEC_BATCH, W_B)),
                   pl.BlockSpec((None, N_C_HEADS, HEAD_DIM), lambda b: (b, 0, 0)),
                   full((DEC_BATCH, 2 * W_A)), full((DEC_BATCH, W_B))],
        out_shape=[jax.ShapeDtypeStruct((DEC_BATCH, W_A), BF16),
                   jax.ShapeDtypeStruct((DEC_BATCH, W_B), BF16),
                   jax.ShapeDtypeStruct((DEC_BATCH, N_C_HEADS, HEAD_DIM), F32),
                   jax.ShapeDtypeStruct((DEC_BATCH, 2 * W_A), F32),
                   jax.ShapeDtypeStruct((DEC_BATCH, W_B), F32)],
        compiler_params=_params("arbitrary"),
        name="mix_sample",
    )(proj_s, qkv, st, cw, lng, lnb, wsc, bsc, *caches)


def _combine_groups(o0, o1, o2, l0, l1, l2):
    m = jnp.maximum(jnp.maximum(l0, l1), l2)
    e0, e1, e2 = jnp.exp(l0 - m), jnp.exp(l1 - m), jnp.exp(l2 - m)
    return (e0 * o0 + e1 * o1 + e2 * o2) / (e0 + e1 + e2)


def _merge_tail(x, ga, gb, gc, ya, yb, yc, wa_ref, wb_ref, wc_ref, wo_ref, gp_ref):
    mm = (jax.nn.sigmoid(ga) * jnp.dot(ya, wa_ref[...], preferred_element_type=F32)
          + jax.nn.sigmoid(gb) * jnp.dot(yb, wb_ref[...], preferred_element_type=F32)
          + jax.nn.sigmoid(gc) * jnp.dot(yc, wc_ref[...], preferred_element_type=F32))
    r = jnp.dot(mm.astype(BF16), wo_ref[...], preferred_element_type=F32)
    return x + _rms(r, gp_ref[...])


def _merge_body(ga_ref, gb_ref, gc_ref, ya_ref, yb_ref, yc_ref,
                x_ref, wa_ref, wb_ref, wc_ref, wo_ref, gp_ref, out_ref):
    out_ref[...] = _merge_tail(x_ref[...], ga_ref[...], gb_ref[...], gc_ref[...], ya_ref[...], yb_ref[...],
                               yc_ref[...].astype(BF16), wa_ref, wb_ref, wc_ref, wo_ref, gp_ref)


def _merge(proj, ya, yb, yc, x, wa, wb, wc, wo, gp, layer, tm):
    m = x.shape[0]

    def rows(w, k=0):
        return pl.BlockSpec((tm, w), lambda i: (i, k))

    def resident(k):
        return pl.BlockSpec((None, k, D_MODEL), lambda i: (layer, 0, 0), pipeline_mode=pl.Buffered(1))

    return pl.pallas_call(
        _merge_body,
        grid=(m // tm,),
        in_specs=[rows(D_MODEL, 0), rows(D_MODEL, 1), rows(D_MODEL, 2), rows(W_A), rows(W_B), rows(C_WIDTH),
                  rows(D_MODEL),
                  resident(W_A), resident(W_B), resident(C_WIDTH), resident(D_MODEL), resident(1)],
        out_specs=rows(D_MODEL),
        out_shape=jax.ShapeDtypeStruct((m, D_MODEL), F32),
        compiler_params=_params("arbitrary"),
        name="merge",
    )(proj, proj, proj, ya, yb, yc, x, wa, wb, wc, wo, gp)


def _ffn_body(x_ref, g1_ref, wg_ref, wu_ref, wo_ref, g2_ref, out_ref, h_ref, acc_ref):
    j = pl.program_id(1)

    @pl.when(j == 0)
    def _():
        h_ref[...] = _rms(x_ref[...], g1_ref[...]).astype(BF16)
        acc_ref[...] = jnp.zeros_like(acc_ref)

    h = h_ref[...]
    gate = jnp.dot(h, wg_ref[...], preferred_element_type=F32)
    up = jnp.dot(h, wu_ref[...], preferred_element_type=F32)
    act = (gate * jax.nn.sigmoid(gate) * up).astype(BF16)
    acc_ref[...] += jnp.dot(act, wo_ref[...], preferred_element_type=F32)

    @pl.when(j == pl.num_programs(1) - 1)
    def _():
        out_ref[...] = x_ref[...] + _rms(acc_ref[...], g2_ref[...])


def _ffn(x, g1, w_in, w_out, g2, layer, tm, tf):
    m = x.shape[0]
    nj = D_FF // tf
    return pl.pallas_call(
        _ffn_body,
        grid=(m // tm, nj),
        in_specs=[
            pl.BlockSpec((tm, D_MODEL), lambda i, j: (i, 0)),
            pl.BlockSpec((None, 1, D_MODEL), lambda i, j: (layer, 0, 0)),
            pl.BlockSpec((None, D_MODEL, tf), lambda i, j: (layer, 0, j)),
            pl.BlockSpec((None, D_MODEL, tf), lambda i, j: (layer, 0, nj + j)),
            pl.BlockSpec((None, tf, D_MODEL), lambda i, j: (layer, j, 0)),
            pl.BlockSpec((None, 1, D_MODEL), lambda i, j: (layer, 0, 0)),
        ],
        out_specs=pl.BlockSpec((tm, D_MODEL), lambda i, j: (i, 0)),
        out_shape=jax.ShapeDtypeStruct((m, D_MODEL), F32),
        scratch_shapes=[pltpu.VMEM((tm, D_MODEL), BF16), pltpu.VMEM((tm, D_MODEL), F32)],
        compiler_params=_params("arbitrary", "arbitrary"),
        name="ffn",
    )(x, g1, w_in, w_in, w_out, g2)


TM_INPROJ = 1024
TS_MIX = 512
TM_MERGE = 256
TM_FFN = 512
TF_FFN = 512


def kernel(x_prompt, x_sample, state_conv, cache_kv_w128, cache_kv_w512, cache_kv_w2048, g_pre_mix, w_in, conv_w, ln_g, ln_b, w_s, b_s, w_a_out, w_b_out, w_c_out, w_o, g_post_mix, g_pre_ffn, w_ffn_in, w_ffn_out, g_post_ffn):
    w_in_b, wa_b, wb_b, wc_b, wo_b, wfi_b, wfo_b = (
        w.astype(BF16) for w in (w_in, w_a_out, w_b_out, w_c_out, w_o, w_ffn_in, w_ffn_out))
    g_pre_mix, g_post_mix, g_pre_ffn, g_post_ffn = (
        g.reshape(DEPTH, 1, D_MODEL) for g in (g_pre_mix, g_post_mix, g_pre_ffn, g_post_ffn))

    tabs_p = _rope_tables(jnp.arange(SEQ, dtype=jnp.int32))
    tabs_s = _rope_tables(jnp.full((DEC_BATCH,), PAST_LEN, dtype=jnp.int32))

    n_keys = DIL_GROUPS[0][0] // DIL_GROUPS[0][1]
    caches = tuple(c.reshape(DEPTH, DEC_BATCH, n_keys, dil, 2, N_C_HEADS, HEAD_DIM)
                   for c, (_, dil) in zip((cache_kv_w128, cache_kv_w512, cache_kv_w2048), DIL_GROUPS))
    st_all = state_conv.reshape(DEPTH, DEC_BATCH, (CONV_W - 1) * W_A)
    bs_t = jnp.swapaxes(b_s, 1, 2)
    wsc = jnp.repeat(w_s[:, :, 0, 0], B_GROUP_W, axis=1)
    bsc = jnp.repeat(b_s[:, :, 0], B_GROUP_W, axis=1)

    xp = x_prompt.reshape(BATCH * SEQ, D_MODEL)
    xs = x_sample.reshape(DEC_BATCH, D_MODEL)
    conv_p, conv_s, vchunk_s = [], [], []
    kv_p = None
    kv_s = [[] for _ in DIL_GROUPS]

    def row(a, l):
        return a[l][None, :]

    for l in range(DEPTH):
        proj = _inproj(xp, g_pre_mix, w_in_b, l, tabs_p, TM_INPROJ, SEQ // TM_INPROJ)
        ya, yb, ctail = _mix_ab(proj, conv_w[l], row(ln_g, l), row(ln_b, l), w_s[l], bs_t[l], TS_MIX)
        yc = _attn(proj)
        kv_p = _kv_pack(proj, l, kv_p)
        conv_p.append(ctail[:, 8 - (CONV_W - 1):])
        xp = _merge(proj, ya, yb, yc, xp, wa_b, wb_b, wc_b, wo_b, g_post_mix, l, TM_MERGE)
        xp = _ffn(xp, g_pre_ffn, wfi_b, wfo_b, g_post_ffn, l, TM_FFN, TF_FFN)

        proj_s = _inproj(xs, g_pre_mix, w_in_b, l, tabs_s, DEC_BATCH, 1)
        ya_s, yb_s, yc_s, cst, vn = _mix_sample(proj_s, st_all, conv_w[l], row(ln_g, l), row(ln_b, l),
                                                row(wsc, l), row(bsc, l), caches, l)
        for g in range(N_DIL):
            k = proj_s[:, OFF_K + g * C_WIDTH:OFF_K + (g + 1) * C_WIDTH]
            v = proj_s[:, OFF_V + g * C_WIDTH:OFF_V + (g + 1) * C_WIDTH]
            kv_s[g].append(jnp.stack([k.reshape(DEC_BATCH, 1, N_C_HEADS, HEAD_DIM),
                                      v.reshape(DEC_BATCH, 1, N_C_HEADS, HEAD_DIM)], axis=2))
        conv_s.append(cst.reshape(DEC_BATCH, CONV_W - 1, W_A))
        vchunk_s.append(vn.reshape(DEC_BATCH, 1, W_B))
        xs = _merge(proj_s, ya_s, yb_s, yc_s.reshape(DEC_BATCH, C_WIDTH), xs, wa_b, wb_b, wc_b, wo_b,
                    g_post_mix, l, DEC_BATCH)
        xs = _ffn(xs, g_pre_ffn, wfi_b, wfo_b, g_post_ffn, l, DEC_BATCH, TF_FFN)

    return (xp.reshape(BATCH, SEQ, D_MODEL), xs.reshape(DEC_BATCH, 1, D_MODEL),
            jnp.stack(conv_p, axis=0),
            kv_p[0], kv_p[1], kv_p[2],
            jnp.stack(conv_s, axis=0),
            jnp.stack(kv_s[0], axis=0), jnp.stack(kv_s[1], axis=0), jnp.stack(kv_s[2], axis=0),
            jnp.stack(vchunk_s, axis=0))
```

```python
import functools

import jax
import jax.numpy as jnp
from jax import lax
from jax.experimental import pallas as pl
from jax.experimental.pallas import tpu as pltpu

D_MODEL = 2048
BATCH = 4
SEQ = 2048
DEPTH = 4
DEC_BATCH = 8
PAST_LEN = 16384
W_A = 1024
CONV_W = 3
W_B = 1024
CHUNK = 128
N_B_GROUPS = 4
B_GROUP_W = W_B // N_B_GROUPS
N_C_HEADS = 4
HEAD_DIM = 128
ROT_DIM = HEAD_DIM // 4
ROPE_THETA = 500000.0
DIL_GROUPS = ((128, 1), (512, 4), (2048, 16))
N_DIL = len(DIL_GROUPS)
QB = 128
C_WIDTH = N_C_HEADS * HEAD_DIM
QKV_W = N_DIL * C_WIDTH
D_FF = ((-(-8 * D_MODEL // 3) + 255) // 256) * 256
IN_WIDTH = 3 * W_A + 2 * W_B + 3 * QKV_W + 3 * D_MODEL
EPS = 1e-6

GATE_W = 3 * D_MODEL
OFF_AB = GATE_W
OFF_Q = GATE_W + 3 * W_A + 2 * W_B
OFF_K = OFF_Q + QKV_W
OFF_V = OFF_K + QKV_W
ORIG_GATE_OFF = IN_WIDTH - GATE_W

ROPE_ROWS = 256
V7X_VMEM_LIMIT = 60 * 1024 * 1024
NEG_BIG = -1e30

F32 = jnp.float32
BF16 = jnp.bfloat16


def _params(*sem):
    return pltpu.CompilerParams(dimension_semantics=sem, vmem_limit_bytes=V7X_VMEM_LIMIT)


def _rms(x, g):
    return x * lax.rsqrt(jnp.mean(x * x, axis=-1, keepdims=True) + EPS) * g


def _inproj_body(x_ref, g_ref, w_ref, c_ref, s1_ref, s2_ref, o_ref, h_ref, *, rope_lo, rope_hi):
    j = pl.program_id(1)

    @pl.when(j == 0)
    def _():
        h_ref[...] = _rms(x_ref[...], g_ref[...]).astype(BF16)

    is_rope = jnp.logical_and(j >= rope_lo, j < rope_hi)
    tm = x_ref.shape[0]
    rc = min(tm, ROPE_ROWS)

    @pl.when(is_rope)
    def _():
        for r in range(tm // rc):
            rs = slice(r * rc, (r + 1) * rc)
            acc = jnp.dot(h_ref[rs, :], w_ref[...], preferred_element_type=F32)
            for h in range(N_C_HEADS):
                sl = slice(h * HEAD_DIM, (h + 1) * HEAD_DIM)
                a = acc[:, sl]
                o_ref[rs, sl] = (a * c_ref[rs, :]
                                 + pltpu.roll(a, HEAD_DIM - ROT_DIM // 2, 1) * s1_ref[rs, :]
                                 + pltpu.roll(a, ROT_DIM // 2, 1) * s2_ref[rs, :])

    @pl.when(jnp.logical_not(is_rope))
    def _():
        o_ref[...] = jnp.dot(h_ref[...], w_ref[...], preferred_element_type=F32)


def _inproj(x, g, w, layer, tabs, tm, pos_blocks):
    m = x.shape[0]
    tn = C_WIDTH
    nj = IN_WIDTH // tn
    rot = ORIG_GATE_OFF // tn
    tab_spec = pl.BlockSpec((tm, HEAD_DIM), lambda i, j: (i % pos_blocks, 0))
    return pl.pallas_call(
        functools.partial(_inproj_body, rope_lo=OFF_Q // tn, rope_hi=OFF_V // tn),
        grid=(m // tm, nj),
        in_specs=[
            pl.BlockSpec((tm, D_MODEL), lambda i, j: (i, 0)),
            pl.BlockSpec((None, 1, D_MODEL), lambda i, j: (layer, 0, 0)),
            pl.BlockSpec((None, None, D_MODEL, tn), lambda i, j: (layer, (j + rot) % nj, 0, 0)),
            tab_spec, tab_spec, tab_spec,
        ],
        out_specs=pl.BlockSpec((tm, tn), lambda i, j: (i, j)),
        out_shape=jax.ShapeDtypeStruct((m, IN_WIDTH), F32),
        scratch_shapes=[pltpu.VMEM((tm, D_MODEL), BF16)],
        compiler_params=_params("arbitrary", "arbitrary"),
        name="inproj",
    )(x, g, w, *tabs)


def _rope_tables(pos):
    inv_freq = ROPE_THETA ** (-jnp.arange(0, ROT_DIM, 2, dtype=jnp.float32) / ROT_DIM)
    ang = pos.astype(jnp.float32)[:, None] * inv_freq[None, :]
    cos, sin = jnp.cos(ang), jnp.sin(ang)
    n = pos.shape[0]
    half = ROT_DIM // 2
    rest = HEAD_DIM - ROT_DIM
    c = jnp.concatenate([cos, cos, jnp.ones((n, rest), F32)], axis=1)
    s1 = jnp.concatenate([-sin, jnp.zeros((n, half + rest), F32)], axis=1)
    s2 = jnp.concatenate([jnp.zeros((n, half), F32), sin, jnp.zeros((n, rest), F32)], axis=1)
    return c, s1, s2


def _mix_ab_body(ab_ref, ac_ref, ax_ref, bu_ref, bv_ref, cw_ref, lng_ref, lnb_ref, ws_ref, bs_ref,
                 ya_ref, yb_ref, cs_ref, zs_ref):
    ts = ab_ref.shape[0]

    @pl.when(pl.program_id(1) == 0)
    def _():
        zs_ref[0:8, :] = jnp.zeros((8, W_A), F32)

    z = ac_ref[...] * ax_ref[...]
    zs_ref[8:8 + ts, :] = z
    z1 = zs_ref[7:7 + ts, :]
    z2 = zs_ref[6:6 + ts, :]
    conv = cw_ref[0:1, :] * z2 + cw_ref[1:2, :] * z1 + cw_ref[2:3, :] * z
    ya_ref[...] = (ab_ref[...] * conv).astype(BF16)
    tail = zs_ref[ts:ts + 8, :]
    zs_ref[0:8, :] = tail
    cs_ref[0] = tail

    v = bv_ref[...]
    xc = v - jnp.mean(v, axis=-1, keepdims=True)
    var = jnp.mean(xc * xc, axis=-1, keepdims=True)
    vn = (xc * lax.rsqrt(var + EPS) * lng_ref[...] + lnb_ref[...]).astype(BF16)
    row = lax.broadcasted_iota(jnp.int32, (CHUNK, CHUNK), 0)
    col = lax.broadcasted_iota(jnp.int32, (CHUNK, CHUNK), 1)
    for g in range(N_B_GROUPS):
        wg = jnp.where(row >= col, ws_ref[g], 0.0).astype(BF16)
        bcol = bs_ref[:, g:g + 1]
        gs = slice(g * B_GROUP_W, (g + 1) * B_GROUP_W)
        for c in range(ts // CHUNK):
            rs = slice(c * CHUNK, (c + 1) * CHUNK)
            sg = jnp.dot(wg, vn[rs, gs], preferred_element_type=F32) + bcol
            yb_ref[rs, gs] = (bu_ref[rs, gs] * sg).astype(BF16)


def _mix_ab(proj, cw, lng, lnb, ws, bs_t, ts):
    nb = SEQ // ts
    blk = OFF_AB // W_A

    def seg(k):
        return pl.BlockSpec((ts, W_A), lambda b, s: (b * nb + s, blk + k))

    def full(shape):
        return pl.BlockSpec(shape, lambda b, s: (0,) * len(shape))

    rows = pl.BlockSpec((ts, W_A), lambda b, s: (b * nb + s, 0))
    return pl.pallas_call(
        _mix_ab_body,
        grid=(BATCH, nb),
        in_specs=[seg(0), seg(1), seg(2), seg(3), seg(4),
                  full((CONV_W, W_A)), full((1, W_B)), full((1, W_B)),
                  full((N_B_GROUPS, CHUNK, CHUNK)), full((CHUNK, N_B_GROUPS))],
        out_specs=[rows, rows, pl.BlockSpec((1, 8, W_A), lambda b, s: (b, 0, 0))],
        out_shape=[jax.ShapeDtypeStruct((BATCH * SEQ, W_A), BF16),
                   jax.ShapeDtypeStruct((BATCH * SEQ, W_B), BF16),
                   jax.ShapeDtypeStruct((BATCH, 8, W_A), F32)],
        scratch_shapes=[pltpu.VMEM((ts + 8, W_A), F32)],
        compiler_params=_params("arbitrary", "arbitrary"),
        name="mix_ab",
    )(proj, proj, proj, proj, proj, cw, lng, lnb, ws, bs_t)


def _attn_body(q0, q1, q2, k0, k1, k2, v0, v1, v2, y_ref,
               qc, kc, vc, s_scr, p_scr, inv_scr, o0, o1, o2, l0, l1, l2):
    row = lax.broadcasted_iota(jnp.int32, (QB, 2 * QB), 0)
    col = lax.broadcasted_iota(jnp.int32, (QB, 2 * QB), 1)
    band = jnp.logical_and(col >= row, col <= row + QB)
    band_first = jnp.logical_and(band, col >= QB)
    scale = HEAD_DIM ** -0.5
    nt = (((1,), (1,)), ((), ()))
    zero_blk = jnp.zeros((QB, HEAD_DIM), BF16)
    for (_, dil), q_ref, k_ref, v_ref, o_ref, l_ref in zip(
            DIL_GROUPS, (q0, q1, q2), (k0, k1, k2), (v0, v1, v2), (o0, o1, o2), (l0, l1, l2)):
        L = SEQ // dil
        nblk = L // QB

        def tok_rows(r, c):
            if dil == 1:
                return pl.ds(c * QB, QB)
            return pl.ds(r + c * QB * dil, QB, stride=dil)

        for r in range(dil):
            src = pl.ds(0, L) if dil == 1 else pl.ds(r, L, stride=dil)
            base = r * (L + QB)
            qc[r * L:(r + 1) * L, :] = q_ref[src, :].astype(BF16)
            kc[base:base + QB, :] = zero_blk
            vc[base:base + QB, :] = zero_blk
            kc[base + QB:base + QB + L, :] = k_ref[src, :].astype(BF16)
            vc[base + QB:base + QB + L, :] = v_ref[src, :].astype(BF16)

        for r in range(dil):
            for c in range(nblk):
                t = r * nblk + c
                kk = kc[r * (L + QB) + c * QB:r * (L + QB) + (c + 2) * QB, :]
                s = lax.dot_general(qc[t * QB:(t + 1) * QB, :], kk, nt, preferred_element_type=F32) * scale
                s_scr[t * QB:(t + 1) * QB, :] = jnp.where(band_first if c == 0 else band, s, NEG_BIG)

        for r in range(dil):
            for c in range(nblk):
                t = r * nblk + c
                s = s_scr[t * QB:(t + 1) * QB, :]
                m = jnp.max(s, axis=-1, keepdims=True)
                p = jnp.exp(s - m)
                den = jnp.sum(p, axis=-1, keepdims=True)
                p_scr[t * QB:(t + 1) * QB, :] = p.astype(BF16)
                inv_scr[t * QB:(t + 1) * QB, :] = jnp.broadcast_to(1.0 / den, (QB, HEAD_DIM))
                l_ref[tok_rows(r, c), :] = jnp.broadcast_to(m + jnp.log(den), (QB, HEAD_DIM))

        for r in range(dil):
            for c in range(nblk):
                t = r * nblk + c
                vv = vc[r * (L + QB) + c * QB:r * (L + QB) + (c + 2) * QB, :]
                o = jnp.dot(p_scr[t * QB:(t + 1) * QB, :], vv, preferred_element_type=F32)
                o_ref[tok_rows(r, c), :] = o * inv_scr[t * QB:(t + 1) * QB, :]
    y_ref[...] = _combine_groups(o0[...], o1[...], o2[...], l0[...], l1[...], l2[...]).astype(BF16)


def _attn(proj):
    def seg(off, g):
        base = (off + g * C_WIDTH) // HEAD_DIM
        return pl.BlockSpec((SEQ, HEAD_DIM), lambda b, h: (b, base + h))

    specs = [seg(off, g) for off in (OFF_Q, OFF_K, OFF_V) for g in range(N_DIL)]
    return pl.pallas_call(
        _attn_body,
        grid=(BATCH, N_C_HEADS),
        in_specs=specs,
        out_specs=pl.BlockSpec((SEQ, HEAD_DIM), lambda b, h: (b, h)),
        out_shape=jax.ShapeDtypeStruct((BATCH * SEQ, C_WIDTH), BF16),
        scratch_shapes=[pltpu.VMEM((SEQ, HEAD_DIM), BF16),
                        pltpu.VMEM((2 * SEQ, HEAD_DIM), BF16),
                        pltpu.VMEM((2 * SEQ, HEAD_DIM), BF16),
                        pltpu.VMEM((SEQ, 2 * QB), F32),
                        pltpu.VMEM((SEQ, 2 * QB), BF16),
                        pltpu.VMEM((SEQ, HEAD_DIM), F32)]
                       + [pltpu.VMEM((SEQ, HEAD_DIM), F32)] * (2 * N_DIL),
        compiler_params=_params("arbitrary", "arbitrary"),
        name="attn",
    )(*([proj] * (3 * N_DIL)))


def _kv_pack_body(*refs, fixed):
    k_refs, v_refs, out_refs = refs[0:N_DIL], refs[N_DIL:2 * N_DIL], refs[-N_DIL:]
    for k_ref, v_ref, o_ref, is_fixed in zip(k_refs, v_refs, out_refs, fixed):
        def pack(k_ref=k_ref, v_ref=v_ref, o_ref=o_ref):
            for h in range(N_C_HEADS):
                sl = slice(h * HEAD_DIM, (h + 1) * HEAD_DIM)
                o_ref[:, 0, h, :] = k_ref[:, sl]
                o_ref[:, 1, h, :] = v_ref[:, sl]

        if is_fixed:
            pl.when(pl.program_id(1) == 0)(pack)
        else:
            pack()


def _kv_pack(proj, layer, prev):
    ts = TS_KV
    keeps = [min(win, SEQ) for win, _ in DIL_GROUPS]

    def src(off, g):
        keep, col = keeps[g], (off + g * C_WIDTH) // C_WIDTH
        if keep <= ts:
            nb = SEQ // keep
            return pl.BlockSpec((keep, C_WIDTH), lambda b, s: (b * nb + nb - 1, col))
        nb, first = SEQ // ts, (SEQ - keep) // ts
        return pl.BlockSpec((ts, C_WIDTH), lambda b, s: (b * nb + jnp.maximum(s, first), col))

    def dst(keep):
        if keep <= ts:
            return pl.BlockSpec((None, None, keep, 2, N_C_HEADS, HEAD_DIM), lambda b, s: (layer, b, 0, 0, 0, 0))
        first = (SEQ - keep) // ts
        return pl.BlockSpec((None, None, ts, 2, N_C_HEADS, HEAD_DIM),
                            lambda b, s: (layer, b, jnp.maximum(s - first, 0), 0, 0, 0))

    in_specs = [src(OFF_K, g) for g in range(N_DIL)] + [src(OFF_V, g) for g in range(N_DIL)]
    args = [proj] * (2 * N_DIL)
    aliases = {}
    if prev is not None:
        in_specs += [pl.BlockSpec(memory_space=pl.ANY)] * N_DIL
        aliases = {2 * N_DIL + g: g for g in range(N_DIL)}
        args += list(prev)
    return pl.pallas_call(
        functools.partial(_kv_pack_body, fixed=tuple(keep <= ts for keep in keeps)),
        grid=(BATCH, SEQ // ts),
        in_specs=in_specs,
        out_specs=[dst(keep) for keep in keeps],
        out_shape=[jax.ShapeDtypeStruct((DEPTH, BATCH, keep, 2, N_C_HEADS, HEAD_DIM), F32) for keep in keeps],
        input_output_aliases=aliases,
        compiler_params=_params("arbitrary", "arbitrary"),
        name="kv_pack",
    )(*args)


def _mix_sample_body(p_ref, qkv_ref, st_ref, cw_ref, lng_ref, lnb_ref, wsc_ref, bsc_ref, c0_ref, c1_ref, c2_ref,
                     ya_ref, yb_ref, yc_ref, cs_ref, vn_ref):
    @pl.when(pl.program_id(0) == 0)
    def _():
        def seg(off, w):
            return p_ref[:, off:off + w]

        z = seg(OFF_AB + W_A, W_A) * seg(OFF_AB + 2 * W_A, W_A)
        st0 = st_ref[0, :, 0:W_A]
        st1 = st_ref[0, :, W_A:2 * W_A]
        conv = cw_ref[0:1, :] * st0 + cw_ref[1:2, :] * st1 + cw_ref[2:3, :] * z
        ya_ref[...] = (seg(OFF_AB, W_A) * conv).astype(BF16)
        cs_ref[:, 0:W_A] = st1
        cs_ref[:, W_A:2 * W_A] = z

        v = seg(OFF_AB + 3 * W_A + W_B, W_B)
        xc = v - jnp.mean(v, axis=-1, keepdims=True)
        var = jnp.mean(xc * xc, axis=-1, keepdims=True)
        vn = xc * lax.rsqrt(var + EPS) * lng_ref[...] + lnb_ref[...]
        vn_ref[...] = vn
        yb_ref[...] = (seg(OFF_AB + 3 * W_A, W_B) * (wsc_ref[...] * vn + bsc_ref[...])).astype(BF16)

    scale = HEAD_DIM ** -0.5
    outs, lses = [], []
    for g, c_ref in enumerate((c0_ref, c1_ref, c2_ref)):
        q = qkv_ref[g]
        kn = qkv_ref[N_DIL + g]
        vnew = qkv_ref[2 * N_DIL + g]
        kc = c_ref[:, 0]
        vc = c_ref[:, 1]
        s_c = jnp.sum(q[None] * kc, axis=-1, keepdims=True) * scale
        s_n = jnp.sum(q * kn, axis=-1, keepdims=True) * scale
        m = jnp.maximum(jnp.max(s_c, axis=0), s_n)
        p_c = jnp.exp(s_c - m[None])
        p_n = jnp.exp(s_n - m)
        den = jnp.sum(p_c, axis=0) + p_n
        outs.append((jnp.sum(p_c * vc, axis=0) + p_n * vnew) / den)
        lses.append(m + jnp.log(den))
    yc_ref[...] = _combine_groups(*outs, *lses)


def _mix_sample(proj_s, st, cw, lng, lnb, wsc, bsc, caches, layer):
    def full(shape):
        return pl.BlockSpec(shape, lambda b: (0,) * len(shape))

    n_keys = DIL_GROUPS[0][0] // DIL_GROUPS[0][1]
    cache_specs = [pl.BlockSpec((None, None, n_keys, None, 2, N_C_HEADS, HEAD_DIM),
                                lambda b: (layer, b, 0, 0, 0, 0, 0)) for _ in caches]
    qkv = proj_s[:, OFF_Q:].reshape(DEC_BATCH, 3 * N_DIL, N_C_HEADS, HEAD_DIM)
    return pl.pallas_call(
        _mix_sample_body,
        grid=(DEC_BATCH,),
        in_specs=[full((DEC_BATCH, IN_WIDTH)),
                  pl.BlockSpec((None, 3 * N_DIL, N_C_HEADS, HEAD_DIM), lambda b: (b, 0, 0, 0)),
                  pl.BlockSpec((1, DEC_BATCH, 2 * W_A), lambda b: (layer, 0, 0)),
                  full((CONV_W, W_A)), full((1, W_B)), full((1, W_B)), full((1, W_B)), full((1, W_B))]
                 + cache_specs,
        out_specs=[full((DEC_BATCH, W_A)), full((DEC_BATCH, W_B)),
                   pl.BlockSpec((None, N_C_HEADS, HEAD_DIM), lambda b: (b, 0, 0)),
                   full((DEC_BATCH, 2 * W_A)), full((DEC_BATCH, W_B))],
        out_shape=[jax.ShapeDtypeStruct((DEC_BATCH, W_A), BF16),
                   jax.ShapeDtypeStruct((DEC_BATCH, W_B), BF16),
                   jax.ShapeDtypeStruct((DEC_BATCH, N_C_HEADS, HEAD_DIM), F32),
                   jax.ShapeDtypeStruct((DEC_BATCH, 2 * W_A), F32),
                   jax.ShapeDtypeStruct((DEC_BATCH, W_B), F32)],
        compiler_params=_params("arbitrary"),
        name="mix_sample",
    )(proj_s, qkv, st, cw, lng, lnb, wsc, bsc, *caches)


def _combine_groups(o0, o1, o2, l0, l1, l2):
    m = jnp.maximum(jnp.maximum(l0, l1), l2)
    e0, e1, e2 = jnp.exp(l0 - m), jnp.exp(l1 - m), jnp.exp(l2 - m)
    return (e0 * o0 + e1 * o1 + e2 * o2) / (e0 + e1 + e2)


def _merge_tail(x, ga, gb, gc, ya, yb, yc, wa_ref, wb_ref, wc_ref, wo_ref, gp_ref):
    mm = (jax.nn.sigmoid(ga) * jnp.dot(ya, wa_ref[...], preferred_element_type=F32)
          + jax.nn.sigmoid(gb) * jnp.dot(yb, wb_ref[...], preferred_element_type=F32)
          + jax.nn.sigmoid(gc) * jnp.dot(yc, wc_ref[...], preferred_element_type=F32))
    r = jnp.dot(mm.astype(BF16), wo_ref[...], preferred_element_type=F32)
    return x + _rms(r, gp_ref[...])


def _merge_body(ga_ref, gb_ref, gc_ref, ya_ref, yb_ref, yc_ref,
                x_ref, wa_ref, wb_ref, wc_ref, wo_ref, gp_ref, out_ref):
    out_ref[...] = _merge_tail(x_ref[...], ga_ref[...], gb_ref[...], gc_ref[...], ya_ref[...], yb_ref[...],
                               yc_ref[...].astype(BF16), wa_ref, wb_ref, wc_ref, wo_ref, gp_ref)


def _merge(proj, ya, yb, yc, x, wa, wb, wc, wo, gp, layer, tm):
    m = x.shape[0]

    def rows(w, k=0):
        return pl.BlockSpec((tm, w), lambda i: (i, k))

    def resident(k):
        return pl.BlockSpec((None, k, D_MODEL), lambda i: (layer, 0, 0), pipeline_mode=pl.Buffered(1))

    return pl.pallas_call(
        _merge_body,
        grid=(m // tm,),
        in_specs=[rows(D_MODEL, 0), rows(D_MODEL, 1), rows(D_MODEL, 2), rows(W_A), rows(W_B), rows(C_WIDTH),
                  rows(D_MODEL),
                  resident(W_A), resident(W_B), resident(C_WIDTH), resident(D_MODEL), resident(1)],
        out_specs=rows(D_MODEL),
        out_shape=jax.ShapeDtypeStruct((m, D_MODEL), F32),
        compiler_params=_params("arbitrary"),
        name="merge",
    )(proj, proj, proj, ya, yb, yc, x, wa, wb, wc, wo, gp)


def _ffn_body(x_ref, g1_ref, wg_ref, wu_ref, wo_ref, g2_ref, out_ref, h_ref):
    j = pl.program_id(1)

    @pl.when(j == 0)
    def _():
        h_ref[...] = _rms(x_ref[...], g1_ref[...]).astype(BF16)
        out_ref[...] = jnp.zeros_like(out_ref)

    h = h_ref[...]
    gate = jnp.dot(h, wg_ref[...], preferred_element_type=F32)
    up = jnp.dot(h, wu_ref[...], preferred_element_type=F32)
    act = (gate * jax.nn.sigmoid(gate) * up).astype(BF16)
    out_ref[...] += jnp.dot(act, wo_ref[...], preferred_element_type=F32)

    @pl.when(j == pl.num_programs(1) - 1)
    def _():
        out_ref[...] = x_ref[...] + _rms(out_ref[...], g2_ref[...])


def _ffn(x, g1, w_in, w_out, g2, layer, tm, tf):
    m = x.shape[0]
    nj = D_FF // tf
    return pl.pallas_call(
        _ffn_body,
        grid=(m // tm, nj),
        in_specs=[
            pl.BlockSpec((tm, D_MODEL), lambda i, j: (i, 0)),
            pl.BlockSpec((None, 1, D_MODEL), lambda i, j: (layer, 0, 0)),
            pl.BlockSpec((None, None, D_MODEL, tf), lambda i, j: (layer, j, 0, 0)),
            pl.BlockSpec((None, None, D_MODEL, tf), lambda i, j: (layer, nj + j, 0, 0)),
            pl.BlockSpec((None, tf, D_MODEL), lambda i, j: (layer, j, 0)),
            pl.BlockSpec((None, 1, D_MODEL), lambda i, j: (layer, 0, 0)),
        ],
        out_specs=pl.BlockSpec((tm, D_MODEL), lambda i, j: (i, 0)),
        out_shape=jax.ShapeDtypeStruct((m, D_MODEL), F32),
        scratch_shapes=[pltpu.VMEM((tm, D_MODEL), BF16)],
        compiler_params=_params("arbitrary", "arbitrary"),
        name="ffn",
    )(x, g1, w_in, w_in, w_out, g2)


TM_INPROJ = 1024
TS_MIX = 512
TM_MERGE = 256
TM_FFN = 512
TF_FFN = 512
TS_KV = 512


def kernel(x_prompt, x_sample, state_conv, cache_kv_w128, cache_kv_w512, cache_kv_w2048, g_pre_mix, w_in, conv_w, ln_g, ln_b, w_s, b_s, w_a_out, w_b_out, w_c_out, w_o, g_post_mix, g_pre_ffn, w_ffn_in, w_ffn_out, g_post_ffn):
    wa_b, wb_b, wc_b, wo_b, wfo_b = (w.astype(BF16) for w in (w_a_out, w_b_out, w_c_out, w_o, w_ffn_out))

    def column_tiles(w, tn):
        d, k, n = w.shape
        return w.astype(BF16).reshape(d, k, n // tn, tn).transpose(0, 2, 1, 3)

    w_in_b = column_tiles(w_in, C_WIDTH)
    wfi_b = column_tiles(w_ffn_in, TF_FFN)
    g_pre_mix, g_post_mix, g_pre_ffn, g_post_ffn = (
        g.reshape(DEPTH, 1, D_MODEL) for g in (g_pre_mix, g_post_mix, g_pre_ffn, g_post_ffn))

    tabs_p = _rope_tables(jnp.arange(SEQ, dtype=jnp.int32))
    tabs_s = _rope_tables(jnp.full((DEC_BATCH,), PAST_LEN, dtype=jnp.int32))

    n_keys = DIL_GROUPS[0][0] // DIL_GROUPS[0][1]
    caches = tuple(c.reshape(DEPTH, DEC_BATCH, n_keys, dil, 2, N_C_HEADS, HEAD_DIM)
                   for c, (_, dil) in zip((cache_kv_w128, cache_kv_w512, cache_kv_w2048), DIL_GROUPS))
    st_all = state_conv.reshape(DEPTH, DEC_BATCH, (CONV_W - 1) * W_A)
    bs_t = jnp.swapaxes(b_s, 1, 2)
    wsc = jnp.repeat(w_s[:, :, 0, 0], B_GROUP_W, axis=1)
    bsc = jnp.repeat(b_s[:, :, 0], B_GROUP_W, axis=1)

    xp = x_prompt.reshape(BATCH * SEQ, D_MODEL)
    xs = x_sample.reshape(DEC_BATCH, D_MODEL)
    conv_p, conv_s, vchunk_s = [], [], []
    kv_p = None
    kv_s = [[] for _ in DIL_GROUPS]

    def row(a, l):
        return a[l][None, :]

    for l in range(DEPTH):
        proj = _inproj(xp, g_pre_mix, w_in_b, l, tabs_p, TM_INPROJ, SEQ // TM_INPROJ)
        ya, yb, ctail = _mix_ab(proj, conv_w[l], row(ln_g, l), row(ln_b, l), w_s[l], bs_t[l], TS_MIX)
        yc = _attn(proj)
        kv_p = _kv_pack(proj, l, kv_p)
        conv_p.append(ctail[:, 8 - (CONV_W - 1):])
        xp = _merge(proj, ya, yb, yc, xp, wa_b, wb_b, wc_b, wo_b, g_post_mix, l, TM_MERGE)
        xp = _ffn(xp, g_pre_ffn, wfi_b, wfo_b, g_post_ffn, l, TM_FFN, TF_FFN)

        proj_s = _inproj(xs, g_pre_mix, w_in_b, l, tabs_s, DEC_BATCH, 1)
        ya_s, yb_s, yc_s, cst, vn = _mix_sample(proj_s, st_all, conv_w[l], row(ln_g, l), row(ln_b, l),
                                                row(wsc, l), row(bsc, l), caches, l)
        for g in range(N_DIL):
            k = proj_s[:, OFF_K + g * C_WIDTH:OFF_K + (g + 1) * C_WIDTH]
            v = proj_s[:, OFF_V + g * C_WIDTH:OFF_V + (g + 1) * C_WIDTH]
            kv_s[g].append(jnp.stack([k.reshape(DEC_BATCH, 1, N_C_HEADS, HEAD_DIM),
                                      v.reshape(DEC_BATCH, 1, N_C_HEADS, HEAD_DIM)], axis=2))
        conv_s.append(cst.reshape(DEC_BATCH, CONV_W - 1, W_A))
        vchunk_s.append(vn.reshape(DEC_BATCH, 1, W_B))
        xs = _merge(proj_s, ya_s, yb_s, yc_s.reshape(DEC_BATCH, C_WIDTH), xs, wa_b, wb_b, wc_b, wo_b,
                    g_post_mix, l, DEC_BATCH)
        xs = _ffn(xs, g_pre_ffn, wfi_b, wfo_b, g_post_ffn, l, DEC_BATCH, TF_FFN)

    return (xp.reshape(BATCH, SEQ, D_MODEL), xs.reshape(DEC_BATCH, 1, D_MODEL),
            jnp.stack(conv_p, axis=0),
            kv_p[0], kv_p[1], kv_p[2],
            jnp.stack(conv_s, axis=0),
            jnp.stack(kv_s[0], axis=0), jnp.stack(kv_s[1], axis=0), jnp.stack(kv_s[2], axis=0),
            jnp.stack(vchunk_s, axis=0))
```

```python
import functools

import jax
import jax.numpy as jnp
from jax import lax
from jax.experimental import pallas as pl
from jax.experimental.pallas import tpu as pltpu

D_MODEL = 2048
BATCH = 4
SEQ = 2048
DEPTH = 4
DEC_BATCH = 8
PAST_LEN = 16384
W_A = 1024
CONV_W = 3
W_B = 1024
CHUNK = 128
N_B_GROUPS = 4
B_GROUP_W = W_B // N_B_GROUPS
N_C_HEADS = 4
HEAD_DIM = 128
ROT_DIM = HEAD_DIM // 4
ROPE_THETA = 500000.0
DIL_GROUPS = ((128, 1), (512, 4), (2048, 16))
N_DIL = len(DIL_GROUPS)
QB = 128
C_WIDTH = N_C_HEADS * HEAD_DIM
QKV_W = N_DIL * C_WIDTH
D_FF = ((-(-8 * D_MODEL // 3) + 255) // 256) * 256
IN_WIDTH = 3 * W_A + 2 * W_B + 3 * QKV_W + 3 * D_MODEL
EPS = 1e-6

GATE_W = 3 * D_MODEL
OFF_AB = GATE_W
MAIN_W = GATE_W + 3 * W_A + 2 * W_B
OFF_Q, OFF_K, OFF_V = 0, QKV_W, 2 * QKV_W
ORIG_GATE_OFF = IN_WIDTH - GATE_W

ROPE_ROWS = 256
V7X_VMEM_LIMIT = 60 * 1024 * 1024
NEG_BIG = -1e30

F32 = jnp.float32
BF16 = jnp.bfloat16


def _params(*sem):
    return pltpu.CompilerParams(dimension_semantics=sem, vmem_limit_bytes=V7X_VMEM_LIMIT)


def _rms(x, g):
    return x * lax.rsqrt(jnp.mean(x * x, axis=-1, keepdims=True) + EPS) * g


def _inproj_body(x_ref, g_ref, w0_ref, w1_ref, c_ref, s1_ref, s2_ref, main_ref, qkv_ref, h_ref,
                 *, main_steps, rope_steps):
    j = pl.program_id(1)
    tn = w0_ref.shape[1]
    halves = ((w0_ref, slice(0, tn)), (w1_ref, slice(tn, 2 * tn)))

    @pl.when(j == 0)
    def _():
        h_ref[...] = _rms(x_ref[...], g_ref[...]).astype(BF16)

    @pl.when(j < main_steps)
    def _():
        for w_ref, cols in halves:
            main_ref[:, cols] = jnp.dot(h_ref[...], w_ref[...], preferred_element_type=F32).astype(main_ref.dtype)

    @pl.when(jnp.logical_and(j >= main_steps, j < main_steps + rope_steps))
    def _():
        tm = x_ref.shape[0]
        rc = min(tm, ROPE_ROWS)
        for w_ref, cols in halves:
            for r in range(tm // rc):
                rs = slice(r * rc, (r + 1) * rc)
                acc = jnp.dot(h_ref[rs, :], w_ref[...], preferred_element_type=F32)
                for h in range(tn // HEAD_DIM):
                    a = acc[:, h * HEAD_DIM:(h + 1) * HEAD_DIM]
                    sl = slice(cols.start + h * HEAD_DIM, cols.start + (h + 1) * HEAD_DIM)
                    qkv_ref[rs, sl] = (a * c_ref[rs, :]
                                       + pltpu.roll(a, HEAD_DIM - ROT_DIM // 2, 1) * s1_ref[rs, :]
                                       + pltpu.roll(a, ROT_DIM // 2, 1) * s2_ref[rs, :])

    @pl.when(j >= main_steps + rope_steps)
    def _():
        for w_ref, cols in halves:
            qkv_ref[:, cols] = jnp.dot(h_ref[...], w_ref[...], preferred_element_type=F32)


def _inproj(x, g, w, layer, tabs, tm, pos_blocks, main_dtype):
    m = x.shape[0]
    tn = C_WIDTH
    n_tiles = IN_WIDTH // tn
    rot = ORIG_GATE_OFF // tn
    main_steps = MAIN_W // (2 * tn)
    rope_steps = 2 * QKV_W // (2 * tn)
    steps = main_steps + pl.cdiv(3 * QKV_W, 2 * tn)
    tab_spec = pl.BlockSpec((tm, HEAD_DIM), lambda i, j: (i % pos_blocks, 0))

    def w_spec(half):
        return pl.BlockSpec((None, D_MODEL, tn), lambda i, j: (layer, 0, (2 * j + half + rot) % n_tiles))

    return pl.pallas_call(
        functools.partial(_inproj_body, main_steps=main_steps, rope_steps=rope_steps),
        grid=(m // tm, steps),
        in_specs=[
            pl.BlockSpec((tm, D_MODEL), lambda i, j: (i, 0)),
            pl.BlockSpec((None, 1, D_MODEL), lambda i, j: (layer, 0, 0)),
            w_spec(0), w_spec(1),
            tab_spec, tab_spec, tab_spec,
        ],
        out_specs=[pl.BlockSpec((tm, 2 * tn), lambda i, j: (i, jnp.minimum(j, main_steps - 1))),
                   pl.BlockSpec((tm, 2 * tn), lambda i, j: (i, jnp.maximum(j - main_steps, 0)))],
        out_shape=[jax.ShapeDtypeStruct((m, MAIN_W), main_dtype),
                   jax.ShapeDtypeStruct((m, 3 * QKV_W), F32)],
        scratch_shapes=[pltpu.VMEM((tm, D_MODEL), BF16)],
        compiler_params=_params("arbitrary", "arbitrary"),
        name="inproj",
    )(x, g, w, w, *tabs)


def _rope_tables(pos):
    inv_freq = ROPE_THETA ** (-jnp.arange(0, ROT_DIM, 2, dtype=jnp.float32) / ROT_DIM)
    ang = pos.astype(jnp.float32)[:, None] * inv_freq[None, :]
    cos, sin = jnp.cos(ang), jnp.sin(ang)
    n = pos.shape[0]
    half = ROT_DIM // 2
    rest = HEAD_DIM - ROT_DIM
    c = jnp.concatenate([cos, cos, jnp.ones((n, rest), F32)], axis=1)
    s1 = jnp.concatenate([-sin, jnp.zeros((n, half + rest), F32)], axis=1)
    s2 = jnp.concatenate([jnp.zeros((n, half), F32), sin, jnp.zeros((n, rest), F32)], axis=1)
    return c, s1, s2


def _mix_ab_body(ab_ref, ac_ref, ax_ref, bu_ref, bv_ref, cw_ref, lng_ref, lnb_ref, ws_ref, bs_ref,
                 ya_ref, yb_ref, cs_ref, zs_ref):
    ts = ab_ref.shape[0]

    @pl.when(pl.program_id(1) == 0)
    def _():
        zs_ref[0:8, :] = jnp.zeros((8, W_A), F32)

    z = ac_ref[...].astype(F32) * ax_ref[...].astype(F32)
    zs_ref[8:8 + ts, :] = z
    z1 = zs_ref[7:7 + ts, :]
    z2 = zs_ref[6:6 + ts, :]
    conv = cw_ref[0:1, :] * z2 + cw_ref[1:2, :] * z1 + cw_ref[2:3, :] * z
    ya_ref[...] = (ab_ref[...].astype(F32) * conv).astype(BF16)
    tail = zs_ref[ts:ts + 8, :]
    zs_ref[0:8, :] = tail
    cs_ref[0] = tail

    v = bv_ref[...].astype(F32)
    xc = v - jnp.mean(v, axis=-1, keepdims=True)
    var = jnp.mean(xc * xc, axis=-1, keepdims=True)
    vn = (xc * lax.rsqrt(var + EPS) * lng_ref[...] + lnb_ref[...]).astype(BF16)
    row = lax.broadcasted_iota(jnp.int32, (CHUNK, CHUNK), 0)
    col = lax.broadcasted_iota(jnp.int32, (CHUNK, CHUNK), 1)
    for g in range(N_B_GROUPS):
        wg = jnp.where(row >= col, ws_ref[g], 0.0).astype(BF16)
        bcol = bs_ref[:, g:g + 1]
        gs = slice(g * B_GROUP_W, (g + 1) * B_GROUP_W)
        for c in range(ts // CHUNK):
            rs = slice(c * CHUNK, (c + 1) * CHUNK)
            sg = jnp.dot(wg, vn[rs, gs], preferred_element_type=F32) + bcol
            yb_ref[rs, gs] = (bu_ref[rs, gs].astype(F32) * sg).astype(BF16)


def _mix_ab(proj, cw, lng, lnb, ws, bs_t, ts):
    nb = SEQ // ts
    blk = OFF_AB // W_A

    def seg(k):
        return pl.BlockSpec((ts, W_A), lambda b, s: (b * nb + s, blk + k))

    def full(shape):
        return pl.BlockSpec(shape, lambda b, s: (0,) * len(shape))

    rows = pl.BlockSpec((ts, W_A), lambda b, s: (b * nb + s, 0))
    return pl.pallas_call(
        _mix_ab_body,
        grid=(BATCH, nb),
        in_specs=[seg(0), seg(1), seg(2), seg(3), seg(4),
                  full((CONV_W, W_A)), full((1, W_B)), full((1, W_B)),
                  full((N_B_GROUPS, CHUNK, CHUNK)), full((CHUNK, N_B_GROUPS))],
        out_specs=[rows, rows, pl.BlockSpec((1, 8, W_A), lambda b, s: (b, 0, 0))],
        out_shape=[jax.ShapeDtypeStruct((BATCH * SEQ, W_A), BF16),
                   jax.ShapeDtypeStruct((BATCH * SEQ, W_B), BF16),
                   jax.ShapeDtypeStruct((BATCH, 8, W_A), F32)],
        scratch_shapes=[pltpu.VMEM((ts + 8, W_A), F32)],
        compiler_params=_params("arbitrary", "arbitrary"),
        name="mix_ab",
    )(proj, proj, proj, proj, proj, cw, lng, lnb, ws, bs_t)


def _attn_body(q0, q1, q2, k0, k1, k2, v0, v1, v2, y_ref,
               qc, kc, vc, s_scr, p_scr, inv_scr, o0, o1, o2, l0, l1, l2):
    row = lax.broadcasted_iota(jnp.int32, (QB, 2 * QB), 0)
    col = lax.broadcasted_iota(jnp.int32, (QB, 2 * QB), 1)
    band = jnp.logical_and(col >= row, col <= row + QB)
    band_first = jnp.logical_and(band, col >= QB)
    scale = HEAD_DIM ** -0.5
    nt = (((1,), (1,)), ((), ()))
    zero_blk = jnp.zeros((QB, HEAD_DIM), BF16)
    for (_, dil), q_ref, k_ref, v_ref, o_ref, l_ref in zip(
            DIL_GROUPS, (q0, q1, q2), (k0, k1, k2), (v0, v1, v2), (o0, o1, o2), (l0, l1, l2)):
        L = SEQ // dil
        nblk = L // QB

        def tok_rows(r, c):
            if dil == 1:
                return pl.ds(c * QB, QB)
            return pl.ds(r + c * QB * dil, QB, stride=dil)

        for r in range(dil):
            src = pl.ds(0, L) if dil == 1 else pl.ds(r, L, stride=dil)
            base = r * (L + QB)
            qc[r * L:(r + 1) * L, :] = q_ref[src, :].astype(BF16)
            kc[base:base + QB, :] = zero_blk
            vc[base:base + QB, :] = zero_blk
            kc[base + QB:base + QB + L, :] = k_ref[src, :].astype(BF16)
            vc[base + QB:base + QB + L, :] = v_ref[src, :].astype(BF16)

        for r in range(dil):
            for c in range(nblk):
                t = r * nblk + c
                kk = kc[r * (L + QB) + c * QB:r * (L + QB) + (c + 2) * QB, :]
                s = lax.dot_general(qc[t * QB:(t + 1) * QB, :], kk, nt, preferred_element_type=F32) * scale
                s_scr[t * QB:(t + 1) * QB, :] = jnp.where(band_first if c == 0 else band, s, NEG_BIG)

        for r in range(dil):
            for c in range(nblk):
                t = r * nblk + c
                s = s_scr[t * QB:(t + 1) * QB, :]
                m = jnp.max(s, axis=-1, keepdims=True)
                p = jnp.exp(s - m)
                den = jnp.sum(p, axis=-1, keepdims=True)
                p_scr[t * QB:(t + 1) * QB, :] = p.astype(BF16)
                inv_scr[t * QB:(t + 1) * QB, :] = jnp.broadcast_to(1.0 / den, (QB, HEAD_DIM))
                l_ref[tok_rows(r, c), :] = jnp.broadcast_to(m + jnp.log(den), (QB, HEAD_DIM))

        for r in range(dil):
            for c in range(nblk):
                t = r * nblk + c
                vv = vc[r * (L + QB) + c * QB:r * (L + QB) + (c + 2) * QB, :]
                o = jnp.dot(p_scr[t * QB:(t + 1) * QB, :], vv, preferred_element_type=F32)
                o_ref[tok_rows(r, c), :] = o * inv_scr[t * QB:(t + 1) * QB, :]
    y_ref[...] = _combine_groups(o0[...], o1[...], o2[...], l0[...], l1[...], l2[...]).astype(BF16)


def _attn(proj):
    def seg(off, g):
        base = (off + g * C_WIDTH) // HEAD_DIM
        return pl.BlockSpec((SEQ, HEAD_DIM), lambda b, h: (b, base + h))

    specs = [seg(off, g) for off in (OFF_Q, OFF_K, OFF_V) for g in range(N_DIL)]
    return pl.pallas_call(
        _attn_body,
        grid=(BATCH, N_C_HEADS),
        in_specs=specs,
        out_specs=pl.BlockSpec((SEQ, HEAD_DIM), lambda b, h: (b, h)),
        out_shape=jax.ShapeDtypeStruct((BATCH * SEQ, C_WIDTH), BF16),
        scratch_shapes=[pltpu.VMEM((SEQ, HEAD_DIM), BF16),
                        pltpu.VMEM((2 * SEQ, HEAD_DIM), BF16),
                        pltpu.VMEM((2 * SEQ, HEAD_DIM), BF16),
                        pltpu.VMEM((SEQ, 2 * QB), F32),
                        pltpu.VMEM((SEQ, 2 * QB), BF16),
                        pltpu.VMEM((SEQ, HEAD_DIM), F32)]
                       + [pltpu.VMEM((SEQ, HEAD_DIM), F32)] * (2 * N_DIL),
        compiler_params=_params("arbitrary", "arbitrary"),
        name="attn",
    )(*([proj] * (3 * N_DIL)))


def _kv_pack_body(*refs, fixed):
    k_refs, v_refs, out_refs = refs[0:N_DIL], refs[N_DIL:2 * N_DIL], refs[-N_DIL:]
    for k_ref, v_ref, o_ref, is_fixed in zip(k_refs, v_refs, out_refs, fixed):
        def pack(k_ref=k_ref, v_ref=v_ref, o_ref=o_ref):
            for h in range(N_C_HEADS):
                sl = slice(h * HEAD_DIM, (h + 1) * HEAD_DIM)
                o_ref[:, 0, h, :] = k_ref[:, sl]
                o_ref[:, 1, h, :] = v_ref[:, sl]

        if is_fixed:
            pl.when(pl.program_id(1) == 0)(pack)
        else:
            pack()


def _kv_pack(proj, layer, prev):
    ts = TS_KV
    keeps = [min(win, SEQ) for win, _ in DIL_GROUPS]

    def src(off, g):
        keep, col = keeps[g], (off + g * C_WIDTH) // C_WIDTH
        if keep <= ts:
            nb = SEQ // keep
            return pl.BlockSpec((keep, C_WIDTH), lambda b, s: (b * nb + nb - 1, col))
        nb, first = SEQ // ts, (SEQ - keep) // ts
        return pl.BlockSpec((ts, C_WIDTH), lambda b, s: (b * nb + jnp.maximum(s, first), col))

    def dst(keep):
        if keep <= ts:
            return pl.BlockSpec((None, None, keep, 2, N_C_HEADS, HEAD_DIM), lambda b, s: (layer, b, 0, 0, 0, 0))
        first = (SEQ - keep) // ts
        return pl.BlockSpec((None, None, ts, 2, N_C_HEADS, HEAD_DIM),
                            lambda b, s: (layer, b, jnp.maximum(s - first, 0), 0, 0, 0))

    in_specs = [src(OFF_K, g) for g in range(N_DIL)] + [src(OFF_V, g) for g in range(N_DIL)]
    args = [proj] * (2 * N_DIL)
    aliases = {}
    if prev is not None:
        in_specs += [pl.BlockSpec(memory_space=pl.ANY)] * N_DIL
        aliases = {2 * N_DIL + g: g for g in range(N_DIL)}
        args += list(prev)
    return pl.pallas_call(
        functools.partial(_kv_pack_body, fixed=tuple(keep <= ts for keep in keeps)),
        grid=(BATCH, SEQ // ts),
        in_specs=in_specs,
        out_specs=[dst(keep) for keep in keeps],
        out_shape=[jax.ShapeDtypeStruct((DEPTH, BATCH, keep, 2, N_C_HEADS, HEAD_DIM), F32) for keep in keeps],
        input_output_aliases=aliases,
        compiler_params=_params("arbitrary", "arbitrary"),
        name="kv_pack",
    )(*args)


def _mix_sample_body(p_ref, qkv_ref, st_ref, cw_ref, lng_ref, lnb_ref, wsc_ref, bsc_ref, c0_ref, c1_ref, c2_ref,
                     ya_ref, yb_ref, yc_ref, cs_ref, vn_ref):
    @pl.when(pl.program_id(0) == 0)
    def _():
        def seg(off, w):
            return p_ref[:, off:off + w]

        z = seg(OFF_AB + W_A, W_A) * seg(OFF_AB + 2 * W_A, W_A)
        st0 = st_ref[0, :, 0:W_A]
        st1 = st_ref[0, :, W_A:2 * W_A]
        conv = cw_ref[0:1, :] * st0 + cw_ref[1:2, :] * st1 + cw_ref[2:3, :] * z
        ya_ref[...] = (seg(OFF_AB, W_A) * conv).astype(BF16)
        cs_ref[:, 0:W_A] = st1
        cs_ref[:, W_A:2 * W_A] = z

        v = seg(OFF_AB + 3 * W_A + W_B, W_B)
        xc = v - jnp.mean(v, axis=-1, keepdims=True)
        var = jnp.mean(xc * xc, axis=-1, keepdims=True)
        vn = xc * lax.rsqrt(var + EPS) * lng_ref[...] + lnb_ref[...]
        vn_ref[...] = vn
        yb_ref[...] = (seg(OFF_AB + 3 * W_A, W_B) * (wsc_ref[...] * vn + bsc_ref[...])).astype(BF16)

    scale = HEAD_DIM ** -0.5
    outs, lses = [], []
    for g, c_ref in enumerate((c0_ref, c1_ref, c2_ref)):
        q = qkv_ref[g]
        kn = qkv_ref[N_DIL + g]
        vnew = qkv_ref[2 * N_DIL + g]
        kc = c_ref[:, 0]
        vc = c_ref[:, 1]
        s_c = jnp.sum(q[None] * kc, axis=-1, keepdims=True) * scale
        s_n = jnp.sum(q * kn, axis=-1, keepdims=True) * scale
        m = jnp.maximum(jnp.max(s_c, axis=0), s_n)
        p_c = jnp.exp(s_c - m[None])
        p_n = jnp.exp(s_n - m)
        den = jnp.sum(p_c, axis=0) + p_n
        outs.append((jnp.sum(p_c * vc, axis=0) + p_n * vnew) / den)
        lses.append(m + jnp.log(den))
    yc_ref[...] = _combine_groups(*outs, *lses)


def _mix_sample(proj_s, qkv_s, st, cw, lng, lnb, wsc, bsc, caches, layer):
    def full(shape):
        return pl.BlockSpec(shape, lambda b: (0,) * len(shape))

    n_keys = DIL_GROUPS[0][0] // DIL_GROUPS[0][1]
    cache_specs = [pl.BlockSpec((None, None, n_keys, None, 2, N_C_HEADS, HEAD_DIM),
                                lambda b: (layer, b, 0, 0, 0, 0, 0)) for _ in caches]
    qkv = qkv_s.reshape(DEC_BATCH, 3 * N_DIL, N_C_HEADS, HEAD_DIM)
    return pl.pallas_call(
        _mix_sample_body,
        grid=(DEC_BATCH,),
        in_specs=[full((DEC_BATCH, MAIN_W)),
                  pl.BlockSpec((None, 3 * N_DIL, N_C_HEADS, HEAD_DIM), lambda b: (b, 0, 0, 0)),
                  pl.BlockSpec((1, DEC_BATCH, 2 * W_A), lambda b: (layer, 0, 0)),
                  full((CONV_W, W_A)), full((1, W_B)), full((1, W_B)), full((1, W_B)), full((1, W_B))]
                 + cache_specs,
        out_specs=[full((DEC_BATCH, W_A)), full((DEC_BATCH, W_B)),
                   pl.BlockSpec((None, N_C_HEADS, HEAD_DIM), lambda b: (b, 0, 0)),
                   full((DEC_BATCH, 2 * W_A)), full((DEC_BATCH, W_B))],
        out_shape=[jax.ShapeDtypeStruct((DEC_BATCH, W_A), BF16),
                   jax.ShapeDtypeStruct((DEC_BATCH, W_B), BF16),
                   jax.ShapeDtypeStruct((DEC_BATCH, N_C_HEADS, HEAD_DIM), F32),
                   jax.ShapeDtypeStruct((DEC_BATCH, 2 * W_A), F32),
                   jax.ShapeDtypeStruct((DEC_BATCH, W_B), F32)],
        compiler_params=_params("arbitrary"),
        name="mix_sample",
    )(proj_s, qkv, st, cw, lng, lnb, wsc, bsc, *caches)


def _combine_groups(o0, o1, o2, l0, l1, l2):
    m = jnp.maximum(jnp.maximum(l0, l1), l2)
    e0, e1, e2 = jnp.exp(l0 - m), jnp.exp(l1 - m), jnp.exp(l2 - m)
    return (e0 * o0 + e1 * o1 + e2 * o2) / (e0 + e1 + e2)


def _merge_tail(x, ga, gb, gc, ya, yb, yc, wa_ref, wb_ref, wc_ref, wo_ref, gp_ref):
    mm = (jax.nn.sigmoid(ga) * jnp.dot(ya, wa_ref[...], preferred_element_type=F32)
          + jax.nn.sigmoid(gb) * jnp.dot(yb, wb_ref[...], preferred_element_type=F32)
          + jax.nn.sigmoid(gc) * jnp.dot(yc, wc_ref[...], preferred_element_type=F32))
    r = jnp.dot(mm.astype(BF16), wo_ref[...], preferred_element_type=F32)
    return x + _rms(r, gp_ref[...])


def _merge_body(ga_ref, gb_ref, gc_ref, ya_ref, yb_ref, yc_ref,
                x_ref, wa_ref, wb_ref, wc_ref, wo_ref, gp_ref, out_ref):
    out_ref[...] = _merge_tail(x_ref[...], ga_ref[...].astype(F32), gb_ref[...].astype(F32),
                               gc_ref[...].astype(F32), ya_ref[...], yb_ref[...],
                               yc_ref[...].astype(BF16), wa_ref, wb_ref, wc_ref, wo_ref, gp_ref)


def _merge(proj, ya, yb, yc, x, wa, wb, wc, wo, gp, layer, tm):
    m = x.shape[0]

    def rows(w, k=0):
        return pl.BlockSpec((tm, w), lambda i: (i, k))

    def resident(k):
        return pl.BlockSpec((None, k, D_MODEL), lambda i: (layer, 0, 0), pipeline_mode=pl.Buffered(1))

    return pl.pallas_call(
        _merge_body,
        grid=(m // tm,),
        in_specs=[rows(D_MODEL, 0), rows(D_MODEL, 1), rows(D_MODEL, 2), rows(W_A), rows(W_B), rows(C_WIDTH),
                  rows(D_MODEL),
                  resident(W_A), resident(W_B), resident(C_WIDTH), resident(D_MODEL), resident(1)],
        out_specs=rows(D_MODEL),
        out_shape=jax.ShapeDtypeStruct((m, D_MODEL), F32),
        compiler_params=_params("arbitrary"),
        name="merge",
    )(proj, proj, proj, ya, yb, yc, x, wa, wb, wc, wo, gp)


def _ffn_body(x_ref, g1_ref, wg_ref, wu_ref, wo_ref, g2_ref, out_ref, h_ref):
    j = pl.program_id(1)

    @pl.when(j == 0)
    def _():
        h_ref[...] = _rms(x_ref[...], g1_ref[...]).astype(BF16)
        out_ref[...] = jnp.zeros_like(out_ref)

    h = h_ref[...]
    gate = jnp.dot(h, wg_ref[...], preferred_element_type=F32)
    up = jnp.dot(h, wu_ref[...], preferred_element_type=F32)
    act = (gate * jax.nn.sigmoid(gate) * up).astype(BF16)
    out_ref[...] += jnp.dot(act, wo_ref[...], preferred_element_type=F32)

    @pl.when(j == pl.num_programs(1) - 1)
    def _():
        out_ref[...] = x_ref[...] + _rms(out_ref[...], g2_ref[...])


def _ffn(x, g1, w_in, w_out, g2, layer, tm, tf):
    m = x.shape[0]
    nj = D_FF // tf
    return pl.pallas_call(
        _ffn_body,
        grid=(m // tm, nj),
        in_specs=[
            pl.BlockSpec((tm, D_MODEL), lambda i, j: (i, 0)),
            pl.BlockSpec((None, 1, D_MODEL), lambda i, j: (layer, 0, 0)),
            pl.BlockSpec((None, D_MODEL, tf), lambda i, j: (layer, 0, j)),
            pl.BlockSpec((None, D_MODEL, tf), lambda i, j: (layer, 0, nj + j)),
            pl.BlockSpec((None, tf, D_MODEL), lambda i, j: (layer, j, 0)),
            pl.BlockSpec((None, 1, D_MODEL), lambda i, j: (layer, 0, 0)),
        ],
        out_specs=pl.BlockSpec((tm, D_MODEL), lambda i, j: (i, 0)),
        out_shape=jax.ShapeDtypeStruct((m, D_MODEL), F32),
        scratch_shapes=[pltpu.VMEM((tm, D_MODEL), BF16)],
        compiler_params=_params("arbitrary", "arbitrary"),
        name="ffn",
    )(x, g1, w_in, w_in, w_out, g2)


TM_INPROJ = 1024
TS_MIX = 512
TM_MERGE = 256
TM_FFN = 512
TF_FFN = 512
TS_KV = 512


def kernel(x_prompt, x_sample, state_conv, cache_kv_w128, cache_kv_w512, cache_kv_w2048, g_pre_mix, w_in, conv_w, ln_g, ln_b, w_s, b_s, w_a_out, w_b_out, w_c_out, w_o, g_post_mix, g_pre_ffn, w_ffn_in, w_ffn_out, g_post_ffn):
    w_in_b, wa_b, wb_b, wc_b, wo_b, wfi_b, wfo_b = (
        w.astype(BF16) for w in (w_in, w_a_out, w_b_out, w_c_out, w_o, w_ffn_in, w_ffn_out))
    g_pre_mix, g_post_mix, g_pre_ffn, g_post_ffn = (
        g.reshape(DEPTH, 1, D_MODEL) for g in (g_pre_mix, g_post_mix, g_pre_ffn, g_post_ffn))

    tabs_p = _rope_tables(jnp.arange(SEQ, dtype=jnp.int32))
    tabs_s = _rope_tables(jnp.full((DEC_BATCH,), PAST_LEN, dtype=jnp.int32))

    n_keys = DIL_GROUPS[0][0] // DIL_GROUPS[0][1]
    caches = tuple(c.reshape(DEPTH, DEC_BATCH, n_keys, dil, 2, N_C_HEADS, HEAD_DIM)
                   for c, (_, dil) in zip((cache_kv_w128, cache_kv_w512, cache_kv_w2048), DIL_GROUPS))
    st_all = state_conv.reshape(DEPTH, DEC_BATCH, (CONV_W - 1) * W_A)
    bs_t = jnp.swapaxes(b_s, 1, 2)
    wsc = jnp.repeat(w_s[:, :, 0, 0], B_GROUP_W, axis=1)
    bsc = jnp.repeat(b_s[:, :, 0], B_GROUP_W, axis=1)

    xp = x_prompt.reshape(BATCH * SEQ, D_MODEL)
    xs = x_sample.reshape(DEC_BATCH, D_MODEL)
    conv_p, conv_s, vchunk_s = [], [], []
    kv_p = None
    kv_s = [[] for _ in DIL_GROUPS]

    def row(a, l):
        return a[l][None, :]

    for l in range(DEPTH):
        proj, qkv = _inproj(xp, g_pre_mix, w_in_b, l, tabs_p, TM_INPROJ, SEQ // TM_INPROJ, BF16)
        ya, yb, ctail = _mix_ab(proj, conv_w[l], row(ln_g, l), row(ln_b, l), w_s[l], bs_t[l], TS_MIX)
        yc = _attn(qkv)
        kv_p = _kv_pack(qkv, l, kv_p)
        conv_p.append(ctail[:, 8 - (CONV_W - 1):])
        xp = _merge(proj, ya, yb, yc, xp, wa_b, wb_b, wc_b, wo_b, g_post_mix, l, TM_MERGE)
        xp = _ffn(xp, g_pre_ffn, wfi_b, wfo_b, g_post_ffn, l, TM_FFN, TF_FFN)

        proj_s, qkv_s = _inproj(xs, g_pre_mix, w_in_b, l, tabs_s, DEC_BATCH, 1, F32)
        ya_s, yb_s, yc_s, cst, vn = _mix_sample(proj_s, qkv_s, st_all, conv_w[l], row(ln_g, l), row(ln_b, l),
                                                row(wsc, l), row(bsc, l), caches, l)
        for g in range(N_DIL):
            k = qkv_s[:, OFF_K + g * C_WIDTH:OFF_K + (g + 1) * C_WIDTH]
            v = qkv_s[:, OFF_V + g * C_WIDTH:OFF_V + (g + 1) * C_WIDTH]
            kv_s[g].append(jnp.stack([k.reshape(DEC_BATCH, 1, N_C_HEADS, HEAD_DIM),
                                      v.reshape(DEC_BATCH, 1, N_C_HEADS, HEAD_DIM)], axis=2))
        conv_s.append(cst.reshape(DEC_BATCH, CONV_W - 1, W_A))
        vchunk_s.append(vn.reshape(DEC_BATCH, 1, W_B))
        xs = _merge(proj_s, ya_s, yb_s, yc_s.reshape(DEC_BATCH, C_WIDTH), xs, wa_b, wb_b, wc_b, wo_b,
                    g_post_mix, l, DEC_BATCH)
        xs = _ffn(xs, g_pre_ffn, wfi_b, wfo_b, g_post_ffn, l, DEC_BATCH, TF_FFN)

    return (xp.reshape(BATCH, SEQ, D_MODEL), xs.reshape(DEC_BATCH, 1, D_MODEL),
            jnp.stack(conv_p, axis=0),
            kv_p[0], kv_p[1], kv_p[2],
            jnp.stack(conv_s, axis=0),
            jnp.stack(kv_s[0], axis=0), jnp.stack(kv_s[1], axis=0), jnp.stack(kv_s[2], axis=0),
            jnp.stack(vchunk_s, axis=0))
```

```python
import functools

import jax
import jax.numpy as jnp
from jax import lax
from jax.experimental import pallas as pl
from jax.experimental.pallas import tpu as pltpu

D_MODEL = 2048
BATCH = 4
SEQ = 2048
DEPTH = 4
DEC_BATCH = 8
PAST_LEN = 16384
W_A = 1024
CONV_W = 3
W_B = 1024
CHUNK = 128
N_B_GROUPS = 4
B_GROUP_W = W_B // N_B_GROUPS
N_C_HEADS = 4
HEAD_DIM = 128
ROT_DIM = HEAD_DIM // 4
ROPE_THETA = 500000.0
DIL_GROUPS = ((128, 1), (512, 4), (2048, 16))
N_DIL = len(DIL_GROUPS)
QB = 128
C_WIDTH = N_C_HEADS * HEAD_DIM
QKV_W = N_DIL * C_WIDTH
D_FF = ((-(-8 * D_MODEL // 3) + 255) // 256) * 256
IN_WIDTH = 3 * W_A + 2 * W_B + 3 * QKV_W + 3 * D_MODEL
EPS = 1e-6

GATE_W = 3 * D_MODEL
OFF_AB = GATE_W
MAIN_W = GATE_W + 3 * W_A + 2 * W_B
OFF_Q, OFF_K, OFF_V = 0, QKV_W, 2 * QKV_W
ORIG_GATE_OFF = IN_WIDTH - GATE_W

ROPE_ROWS = 256
V7X_VMEM_LIMIT = 60 * 1024 * 1024
NEG_BIG = -1e30

F32 = jnp.float32
BF16 = jnp.bfloat16


def _params(*sem):
    return pltpu.CompilerParams(dimension_semantics=sem, vmem_limit_bytes=V7X_VMEM_LIMIT)


def _rms(x, g):
    return x * lax.rsqrt(jnp.mean(x * x, axis=-1, keepdims=True) + EPS) * g


def _cast_specs(src, layer, rows, steps_per_row_tile, n_steps):
    _, r, c = src.shape
    nblk = r // rows
    assert r % rows == 0 and nblk <= n_steps

    def blk(i, j):
        return jnp.minimum(i * steps_per_row_tile + j, nblk - 1)

    return (pl.BlockSpec((None, rows, c), lambda i, j: (layer, blk(i, j), 0)),
            pl.BlockSpec((rows, c), lambda i, j: (blk(i, j), 0)),
            jax.ShapeDtypeStruct((r, c), BF16))


def _cast_blocks(src_refs, dst_refs):
    for s_ref, d_ref in zip(src_refs, dst_refs):
        d_ref[...] = s_ref[...].astype(BF16)


def _inproj_body(x_ref, g_ref, w0_ref, w1_ref, c_ref, s1_ref, s2_ref, *rest, main_steps, rope_steps, n_cast):
    cast_src, (main_ref, qkv_ref), cast_dst, h_ref = (
        rest[:n_cast], rest[n_cast:n_cast + 2], rest[n_cast + 2:2 * n_cast + 2], rest[-1])
    j = pl.program_id(1)
    tn = w0_ref.shape[1]
    halves = ((w0_ref, slice(0, tn)), (w1_ref, slice(tn, 2 * tn)))

    @pl.when(j == 0)
    def _():
        h_ref[...] = _rms(x_ref[...], g_ref[...]).astype(BF16)

    @pl.when(j < main_steps)
    def _():
        _cast_blocks(cast_src, cast_dst)
        for w_ref, cols in halves:
            main_ref[:, cols] = jnp.dot(h_ref[...], w_ref[...], preferred_element_type=F32).astype(main_ref.dtype)

    @pl.when(jnp.logical_and(j >= main_steps, j < main_steps + rope_steps))
    def _():
        _cast_blocks(cast_src, cast_dst)
        tm = x_ref.shape[0]
        rc = min(tm, ROPE_ROWS)
        for w_ref, cols in halves:
            for r in range(tm // rc):
                rs = slice(r * rc, (r + 1) * rc)
                acc = jnp.dot(h_ref[rs, :], w_ref[...], preferred_element_type=F32)
                for h in range(tn // HEAD_DIM):
                    a = acc[:, h * HEAD_DIM:(h + 1) * HEAD_DIM]
                    sl = slice(cols.start + h * HEAD_DIM, cols.start + (h + 1) * HEAD_DIM)
                    qkv_ref[rs, sl] = (a * c_ref[rs, :]
                                       + pltpu.roll(a, HEAD_DIM - ROT_DIM // 2, 1) * s1_ref[rs, :]
                                       + pltpu.roll(a, ROT_DIM // 2, 1) * s2_ref[rs, :])

    @pl.when(j >= main_steps + rope_steps)
    def _():
        _cast_blocks(cast_src, cast_dst)
        for w_ref, cols in halves:
            qkv_ref[:, cols] = jnp.dot(h_ref[...], w_ref[...], preferred_element_type=F32)


def _inproj(x, g, w, layer, tabs, tm, pos_blocks, main_dtype, casts=()):
    m = x.shape[0]
    tn = C_WIDTH
    n_tiles = IN_WIDTH // tn
    rot = ORIG_GATE_OFF // tn
    main_steps = MAIN_W // (2 * tn)
    rope_steps = 2 * QKV_W // (2 * tn)
    steps = main_steps + pl.cdiv(3 * QKV_W, 2 * tn)
    tab_spec = pl.BlockSpec((tm, HEAD_DIM), lambda i, j: (i % pos_blocks, 0))

    def w_spec(half):
        return pl.BlockSpec((D_MODEL, tn), lambda i, j: (0, (2 * j + half + rot) % n_tiles))

    cast_specs = [_cast_specs(src, layer, rows, steps, (m // tm) * steps) for src, rows in casts]
    return pl.pallas_call(
        functools.partial(_inproj_body, main_steps=main_steps, rope_steps=rope_steps, n_cast=len(casts)),
        grid=(m // tm, steps),
        in_specs=[
            pl.BlockSpec((tm, D_MODEL), lambda i, j: (i, 0)),
            pl.BlockSpec((None, 1, D_MODEL), lambda i, j: (layer, 0, 0)),
            w_spec(0), w_spec(1),
            tab_spec, tab_spec, tab_spec,
        ] + [c[0] for c in cast_specs],
        out_specs=[pl.BlockSpec((tm, 2 * tn), lambda i, j: (i, jnp.minimum(j, main_steps - 1))),
                   pl.BlockSpec((tm, 2 * tn), lambda i, j: (i, jnp.maximum(j - main_steps, 0)))]
                  + [c[1] for c in cast_specs],
        out_shape=[jax.ShapeDtypeStruct((m, MAIN_W), main_dtype),
                   jax.ShapeDtypeStruct((m, 3 * QKV_W), F32)] + [c[2] for c in cast_specs],
        scratch_shapes=[pltpu.VMEM((tm, D_MODEL), BF16)],
        compiler_params=_params("arbitrary", "arbitrary"),
        name="inproj",
    )(x, g, w, w, *tabs, *[src for src, _ in casts])


def _rope_tables(pos):
    inv_freq = ROPE_THETA ** (-jnp.arange(0, ROT_DIM, 2, dtype=jnp.float32) / ROT_DIM)
    ang = pos.astype(jnp.float32)[:, None] * inv_freq[None, :]
    cos, sin = jnp.cos(ang), jnp.sin(ang)
    n = pos.shape[0]
    half = ROT_DIM // 2
    rest = HEAD_DIM - ROT_DIM
    c = jnp.concatenate([cos, cos, jnp.ones((n, rest), F32)], axis=1)
    s1 = jnp.concatenate([-sin, jnp.zeros((n, half + rest), F32)], axis=1)
    s2 = jnp.concatenate([jnp.zeros((n, half), F32), sin, jnp.zeros((n, rest), F32)], axis=1)
    return c, s1, s2


def _mix_ab_body(ab_ref, ac_ref, ax_ref, bu_ref, bv_ref, cw_ref, lng_ref, lnb_ref, ws_ref, bs_ref,
                 ya_ref, yb_ref, cs_ref, zs_ref):
    ts = ab_ref.shape[0]

    @pl.when(pl.program_id(1) == 0)
    def _():
        zs_ref[0:8, :] = jnp.zeros((8, W_A), F32)

    z = ac_ref[...].astype(F32) * ax_ref[...].astype(F32)
    zs_ref[8:8 + ts, :] = z
    z1 = zs_ref[7:7 + ts, :]
    z2 = zs_ref[6:6 + ts, :]
    conv = cw_ref[0:1, :] * z2 + cw_ref[1:2, :] * z1 + cw_ref[2:3, :] * z
    ya_ref[...] = (ab_ref[...].astype(F32) * conv).astype(BF16)
    tail = zs_ref[ts:ts + 8, :]
    zs_ref[0:8, :] = tail
    cs_ref[0] = tail

    v = bv_ref[...].astype(F32)
    xc = v - jnp.mean(v, axis=-1, keepdims=True)
    var = jnp.mean(xc * xc, axis=-1, keepdims=True)
    vn = (xc * lax.rsqrt(var + EPS) * lng_ref[...] + lnb_ref[...]).astype(BF16)
    row = lax.broadcasted_iota(jnp.int32, (CHUNK, CHUNK), 0)
    col = lax.broadcasted_iota(jnp.int32, (CHUNK, CHUNK), 1)
    for g in range(N_B_GROUPS):
        wg = jnp.where(row >= col, ws_ref[g], 0.0).astype(BF16)
        bcol = bs_ref[:, g:g + 1]
        gs = slice(g * B_GROUP_W, (g + 1) * B_GROUP_W)
        for c in range(ts // CHUNK):
            rs = slice(c * CHUNK, (c + 1) * CHUNK)
            sg = jnp.dot(wg, vn[rs, gs], preferred_element_type=F32) + bcol
            yb_ref[rs, gs] = (bu_ref[rs, gs].astype(F32) * sg).astype(BF16)


def _mix_ab(proj, cw, lng, lnb, ws, bs_t, ts):
    nb = SEQ // ts
    blk = OFF_AB // W_A

    def seg(k):
        return pl.BlockSpec((ts, W_A), lambda b, s: (b * nb + s, blk + k))

    def full(shape):
        return pl.BlockSpec(shape, lambda b, s: (0,) * len(shape))

    rows = pl.BlockSpec((ts, W_A), lambda b, s: (b * nb + s, 0))
    return pl.pallas_call(
        _mix_ab_body,
        grid=(BATCH, nb),
        in_specs=[seg(0), seg(1), seg(2), seg(3), seg(4),
                  full((CONV_W, W_A)), full((1, W_B)), full((1, W_B)),
                  full((N_B_GROUPS, CHUNK, CHUNK)), full((CHUNK, N_B_GROUPS))],
        out_specs=[rows, rows, pl.BlockSpec((1, 8, W_A), lambda b, s: (b, 0, 0))],
        out_shape=[jax.ShapeDtypeStruct((BATCH * SEQ, W_A), BF16),
                   jax.ShapeDtypeStruct((BATCH * SEQ, W_B), BF16),
                   jax.ShapeDtypeStruct((BATCH, 8, W_A), F32)],
        scratch_shapes=[pltpu.VMEM((ts + 8, W_A), F32)],
        compiler_params=_params("arbitrary", "arbitrary"),
        name="mix_ab",
    )(proj, proj, proj, proj, proj, cw, lng, lnb, ws, bs_t)


def _attn_body(q0, q1, q2, k0, k1, k2, v0, v1, v2, y_ref,
               qc, kc, vc, s_scr, p_scr, inv_scr, o0, o1, o2, l0, l1, l2):
    row = lax.broadcasted_iota(jnp.int32, (QB, 2 * QB), 0)
    col = lax.broadcasted_iota(jnp.int32, (QB, 2 * QB), 1)
    band = jnp.logical_and(col >= row, col <= row + QB)
    band_first = jnp.logical_and(band, col >= QB)
    scale = HEAD_DIM ** -0.5
    nt = (((1,), (1,)), ((), ()))
    zero_blk = jnp.zeros((QB, HEAD_DIM), BF16)
    for (_, dil), q_ref, k_ref, v_ref, o_ref, l_ref in zip(
            DIL_GROUPS, (q0, q1, q2), (k0, k1, k2), (v0, v1, v2), (o0, o1, o2), (l0, l1, l2)):
        L = SEQ // dil
        nblk = L // QB

        def tok_rows(r, c):
            if dil == 1:
                return pl.ds(c * QB, QB)
            return pl.ds(r + c * QB * dil, QB, stride=dil)

        for r in range(dil):
            src = pl.ds(0, L) if dil == 1 else pl.ds(r, L, stride=dil)
            base = r * (L + QB)
            qc[r * L:(r + 1) * L, :] = q_ref[src, :].astype(BF16)
            kc[base:base + QB, :] = zero_blk
            vc[base:base + QB, :] = zero_blk
            kc[base + QB:base + QB + L, :] = k_ref[src, :].astype(BF16)
            vc[base + QB:base + QB + L, :] = v_ref[src, :].astype(BF16)

        for r in range(dil):
            for c in range(nblk):
                t = r * nblk + c
                kk = kc[r * (L + QB) + c * QB:r * (L + QB) + (c + 2) * QB, :]
                s = lax.dot_general(qc[t * QB:(t + 1) * QB, :], kk, nt, preferred_element_type=F32) * scale
                s_scr[t * QB:(t + 1) * QB, :] = jnp.where(band_first if c == 0 else band, s, NEG_BIG)

        for r in range(dil):
            for c in range(nblk):
                t = r * nblk + c
                s = s_scr[t * QB:(t + 1) * QB, :]
                m = jnp.max(s, axis=-1, keepdims=True)
                p = jnp.exp(s - m)
                den = jnp.sum(p, axis=-1, keepdims=True)
                p_scr[t * QB:(t + 1) * QB, :] = p.astype(BF16)
                inv_scr[t * QB:(t + 1) * QB, :] = jnp.broadcast_to(1.0 / den, (QB, HEAD_DIM))
                l_ref[tok_rows(r, c), :] = jnp.broadcast_to(m + jnp.log(den), (QB, HEAD_DIM))

        for r in range(dil):
            for c in range(nblk):
                t = r * nblk + c
                vv = vc[r * (L + QB) + c * QB:r * (L + QB) + (c + 2) * QB, :]
                o = jnp.dot(p_scr[t * QB:(t + 1) * QB, :], vv, preferred_element_type=F32)
                o_ref[tok_rows(r, c), :] = o * inv_scr[t * QB:(t + 1) * QB, :]
    y_ref[...] = _combine_groups(o0[...], o1[...], o2[...], l0[...], l1[...], l2[...]).astype(BF16)


def _attn(proj):
    def seg(off, g):
        base = (off + g * C_WIDTH) // HEAD_DIM
        return pl.BlockSpec((SEQ, HEAD_DIM), lambda b, h: (b, base + h))

    specs = [seg(off, g) for off in (OFF_Q, OFF_K, OFF_V) for g in range(N_DIL)]
    return pl.pallas_call(
        _attn_body,
        grid=(BATCH, N_C_HEADS),
        in_specs=specs,
        out_specs=pl.BlockSpec((SEQ, HEAD_DIM), lambda b, h: (b, h)),
        out_shape=jax.ShapeDtypeStruct((BATCH * SEQ, C_WIDTH), BF16),
        scratch_shapes=[pltpu.VMEM((SEQ, HEAD_DIM), BF16),
                        pltpu.VMEM((2 * SEQ, HEAD_DIM), BF16),
                        pltpu.VMEM((2 * SEQ, HEAD_DIM), BF16),
                        pltpu.VMEM((SEQ, 2 * QB), F32),
                        pltpu.VMEM((SEQ, 2 * QB), BF16),
                        pltpu.VMEM((SEQ, HEAD_DIM), F32)]
                       + [pltpu.VMEM((SEQ, HEAD_DIM), F32)] * (2 * N_DIL),
        compiler_params=_params("arbitrary", "arbitrary"),
        name="attn",
    )(*([proj] * (3 * N_DIL)))


def _kv_pack_body(*refs, fixed):
    k_refs, v_refs, out_refs = refs[0:N_DIL], refs[N_DIL:2 * N_DIL], refs[-N_DIL:]
    for k_ref, v_ref, o_ref, is_fixed in zip(k_refs, v_refs, out_refs, fixed):
        def pack(k_ref=k_ref, v_ref=v_ref, o_ref=o_ref):
            for h in range(N_C_HEADS):
                sl = slice(h * HEAD_DIM, (h + 1) * HEAD_DIM)
                o_ref[:, 0, h, :] = k_ref[:, sl]
                o_ref[:, 1, h, :] = v_ref[:, sl]

        if is_fixed:
            pl.when(pl.program_id(1) == 0)(pack)
        else:
            pack()


def _kv_pack(proj, layer, prev):
    ts = TS_KV
    keeps = [min(win, SEQ) for win, _ in DIL_GROUPS]

    def src(off, g):
        keep, col = keeps[g], (off + g * C_WIDTH) // C_WIDTH
        if keep <= ts:
            nb = SEQ // keep
            return pl.BlockSpec((keep, C_WIDTH), lambda b, s: (b * nb + nb - 1, col))
        nb, first = SEQ // ts, (SEQ - keep) // ts
        return pl.BlockSpec((ts, C_WIDTH), lambda b, s: (b * nb + jnp.maximum(s, first), col))

    def dst(keep):
        if keep <= ts:
            return pl.BlockSpec((None, None, keep, 2, N_C_HEADS, HEAD_DIM), lambda b, s: (layer, b, 0, 0, 0, 0))
        first = (SEQ - keep) // ts
        return pl.BlockSpec((None, None, ts, 2, N_C_HEADS, HEAD_DIM),
                            lambda b, s: (layer, b, jnp.maximum(s - first, 0), 0, 0, 0))

    in_specs = [src(OFF_K, g) for g in range(N_DIL)] + [src(OFF_V, g) for g in range(N_DIL)]
    args = [proj] * (2 * N_DIL)
    aliases = {}
    if prev is not None:
        in_specs += [pl.BlockSpec(memory_space=pl.ANY)] * N_DIL
        aliases = {2 * N_DIL + g: g for g in range(N_DIL)}
        args += list(prev)
    return pl.pallas_call(
        functools.partial(_kv_pack_body, fixed=tuple(keep <= ts for keep in keeps)),
        grid=(BATCH, SEQ // ts),
        in_specs=in_specs,
        out_specs=[dst(keep) for keep in keeps],
        out_shape=[jax.ShapeDtypeStruct((DEPTH, BATCH, keep, 2, N_C_HEADS, HEAD_DIM), F32) for keep in keeps],
        input_output_aliases=aliases,
        compiler_params=_params("arbitrary", "arbitrary"),
        name="kv_pack",
    )(*args)


def _mix_sample_body(p_ref, qkv_ref, st_ref, cw_ref, lng_ref, lnb_ref, wsc_ref, bsc_ref, c0_ref, c1_ref, c2_ref,
                     ya_ref, yb_ref, yc_ref, cs_ref, vn_ref):
    @pl.when(pl.program_id(0) == 0)
    def _():
        def seg(off, w):
            return p_ref[:, off:off + w]

        z = seg(OFF_AB + W_A, W_A) * seg(OFF_AB + 2 * W_A, W_A)
        st0 = st_ref[0, :, 0:W_A]
        st1 = st_ref[0, :, W_A:2 * W_A]
        conv = cw_ref[0:1, :] * st0 + cw_ref[1:2, :] * st1 + cw_ref[2:3, :] * z
        ya_ref[...] = (seg(OFF_AB, W_A) * conv).astype(BF16)
        cs_ref[:, 0:W_A] = st1
        cs_ref[:, W_A:2 * W_A] = z

        v = seg(OFF_AB + 3 * W_A + W_B, W_B)
        xc = v - jnp.mean(v, axis=-1, keepdims=True)
        var = jnp.mean(xc * xc, axis=-1, keepdims=True)
        vn = xc * lax.rsqrt(var + EPS) * lng_ref[...] + lnb_ref[...]
        vn_ref[...] = vn
        yb_ref[...] = (seg(OFF_AB + 3 * W_A, W_B) * (wsc_ref[...] * vn + bsc_ref[...])).astype(BF16)

    scale = HEAD_DIM ** -0.5
    outs, lses = [], []
    for g, c_ref in enumerate((c0_ref, c1_ref, c2_ref)):
        q = qkv_ref[g]
        kn = qkv_ref[N_DIL + g]
        vnew = qkv_ref[2 * N_DIL + g]
        kc = c_ref[:, 0]
        vc = c_ref[:, 1]
        s_c = jnp.sum(q[None] * kc, axis=-1, keepdims=True) * scale
        s_n = jnp.sum(q * kn, axis=-1, keepdims=True) * scale
        m = jnp.maximum(jnp.max(s_c, axis=0), s_n)
        p_c = jnp.exp(s_c - m[None])
        p_n = jnp.exp(s_n - m)
        den = jnp.sum(p_c, axis=0) + p_n
        outs.append((jnp.sum(p_c * vc, axis=0) + p_n * vnew) / den)
        lses.append(m + jnp.log(den))
    yc_ref[...] = _combine_groups(*outs, *lses)


def _mix_sample(proj_s, qkv_s, st, cw, lng, lnb, wsc, bsc, caches, layer):
    def full(shape):
        return pl.BlockSpec(shape, lambda b: (0,) * len(shape))

    n_keys = DIL_GROUPS[0][0] // DIL_GROUPS[0][1]
    cache_specs = [pl.BlockSpec((None, None, n_keys, None, 2, N_C_HEADS, HEAD_DIM),
                                lambda b: (layer, b, 0, 0, 0, 0, 0)) for _ in caches]
    qkv = qkv_s.reshape(DEC_BATCH, 3 * N_DIL, N_C_HEADS, HEAD_DIM)
    return pl.pallas_call(
        _mix_sample_body,
        grid=(DEC_BATCH,),
        in_specs=[full((DEC_BATCH, MAIN_W)),
                  pl.BlockSpec((None, 3 * N_DIL, N_C_HEADS, HEAD_DIM), lambda b: (b, 0, 0, 0)),
                  pl.BlockSpec((1, DEC_BATCH, 2 * W_A), lambda b: (layer, 0, 0)),
                  full((CONV_W, W_A)), full((1, W_B)), full((1, W_B)), full((1, W_B)), full((1, W_B))]
                 + cache_specs,
        out_specs=[full((DEC_BATCH, W_A)), full((DEC_BATCH, W_B)),
                   pl.BlockSpec((None, N_C_HEADS, HEAD_DIM), lambda b: (b, 0, 0)),
                   full((DEC_BATCH, 2 * W_A)), full((DEC_BATCH, W_B))],
        out_shape=[jax.ShapeDtypeStruct((DEC_BATCH, W_A), BF16),
                   jax.ShapeDtypeStruct((DEC_BATCH, W_B), BF16),
                   jax.ShapeDtypeStruct((DEC_BATCH, N_C_HEADS, HEAD_DIM), F32),
                   jax.ShapeDtypeStruct((DEC_BATCH, 2 * W_A), F32),
                   jax.ShapeDtypeStruct((DEC_BATCH, W_B), F32)],
        compiler_params=_params("arbitrary"),
        name="mix_sample",
    )(proj_s, qkv, st, cw, lng, lnb, wsc, bsc, *caches)


def _combine_groups(o0, o1, o2, l0, l1, l2):
    m = jnp.maximum(jnp.maximum(l0, l1), l2)
    e0, e1, e2 = jnp.exp(l0 - m), jnp.exp(l1 - m), jnp.exp(l2 - m)
    return (e0 * o0 + e1 * o1 + e2 * o2) / (e0 + e1 + e2)


def _merge_tail(x, ga, gb, gc, ya, yb, yc, wa_ref, wb_ref, wc_ref, wo_ref, gp_ref):
    mm = (jax.nn.sigmoid(ga) * jnp.dot(ya, wa_ref[...], preferred_element_type=F32)
          + jax.nn.sigmoid(gb) * jnp.dot(yb, wb_ref[...], preferred_element_type=F32)
          + jax.nn.sigmoid(gc) * jnp.dot(yc, wc_ref[...], preferred_element_type=F32))
    r = jnp.dot(mm.astype(BF16), wo_ref[...], preferred_element_type=F32)
    return x + _rms(r, gp_ref[...])


def _merge_body(ga_ref, gb_ref, gc_ref, ya_ref, yb_ref, yc_ref,
                x_ref, wa_ref, wb_ref, wc_ref, wo_ref, gp_ref, out_ref):
    out_ref[...] = _merge_tail(x_ref[...], ga_ref[...].astype(F32), gb_ref[...].astype(F32),
                               gc_ref[...].astype(F32), ya_ref[...], yb_ref[...],
                               yc_ref[...].astype(BF16), wa_ref, wb_ref, wc_ref, wo_ref, gp_ref)


def _merge(proj, ya, yb, yc, x, wa, wb, wc, wo, gp, layer, tm):
    m = x.shape[0]

    def rows(w, k=0):
        return pl.BlockSpec((tm, w), lambda i: (i, k))

    def resident(k):
        return pl.BlockSpec((None, k, D_MODEL), lambda i: (layer, 0, 0), pipeline_mode=pl.Buffered(1))

    return pl.pallas_call(
        _merge_body,
        grid=(m // tm,),
        in_specs=[rows(D_MODEL, 0), rows(D_MODEL, 1), rows(D_MODEL, 2), rows(W_A), rows(W_B), rows(C_WIDTH),
                  rows(D_MODEL),
                  resident(W_A), resident(W_B), resident(C_WIDTH), resident(D_MODEL), resident(1)],
        out_specs=rows(D_MODEL),
        out_shape=jax.ShapeDtypeStruct((m, D_MODEL), F32),
        compiler_params=_params("arbitrary"),
        name="merge",
    )(proj, proj, proj, ya, yb, yc, x, wa, wb, wc, wo, gp)


def _ffn_body(x_ref, g1_ref, wg0_ref, wu0_ref, wo0_ref, wg1_ref, wu1_ref, wo1_ref, g2_ref, *rest,
              n_cast, full_steps):
    cast_src, out_ref, cast_dst, h_ref = rest[:n_cast], rest[n_cast], rest[n_cast + 1:2 * n_cast + 1], rest[-1]
    j = pl.program_id(1)
    last = pl.num_programs(1) - 1

    @pl.when(j == 0)
    def _():
        h_ref[...] = _rms(x_ref[...], g1_ref[...]).astype(BF16)
        out_ref[...] = jnp.zeros_like(out_ref)

    def tile(wg_ref, wu_ref, wo_ref):
        h = h_ref[...]
        gate = jnp.dot(h, wg_ref[...], preferred_element_type=F32)
        up = jnp.dot(h, wu_ref[...], preferred_element_type=F32)
        act = (gate * jax.nn.sigmoid(gate) * up).astype(BF16)
        return jnp.dot(act, wo_ref[...], preferred_element_type=F32)

    @pl.when(j < full_steps)
    def _():
        _cast_blocks(cast_src, cast_dst)
        out_ref[...] += tile(wg0_ref, wu0_ref, wo0_ref) + tile(wg1_ref, wu1_ref, wo1_ref)

    @pl.when(j >= full_steps)
    def _():
        _cast_blocks(cast_src, cast_dst)
        out_ref[...] += tile(wg0_ref, wu0_ref, wo0_ref)

    @pl.when(j == last)
    def _():
        out_ref[...] = x_ref[...] + _rms(out_ref[...], g2_ref[...])


def _ffn(x, g1, w_in, w_out, g2, layer, tm, tf, casts=()):
    m = x.shape[0]
    nt = D_FF // tf
    steps = pl.cdiv(nt, 2)
    cast_specs = [_cast_specs(src, layer + 1, rows, steps, (m // tm) * steps) for src, rows in casts]

    def tile_specs(half):
        def t(j):
            return jnp.minimum(2 * j + half, nt - 1)
        return [pl.BlockSpec((D_MODEL, tf), lambda i, j: (0, t(j))),
                pl.BlockSpec((D_MODEL, tf), lambda i, j: (0, nt + t(j))),
                pl.BlockSpec((tf, D_MODEL), lambda i, j: (t(j), 0))]

    res = pl.pallas_call(
        functools.partial(_ffn_body, n_cast=len(casts), full_steps=nt // 2),
        grid=(m // tm, steps),
        in_specs=[
            pl.BlockSpec((tm, D_MODEL), lambda i, j: (i, 0)),
            pl.BlockSpec((None, 1, D_MODEL), lambda i, j: (layer, 0, 0)),
        ] + tile_specs(0) + tile_specs(1) + [
            pl.BlockSpec((None, 1, D_MODEL), lambda i, j: (layer, 0, 0)),
        ] + [c[0] for c in cast_specs],
        out_specs=[pl.BlockSpec((tm, D_MODEL), lambda i, j: (i, 0))] + [c[1] for c in cast_specs],
        out_shape=[jax.ShapeDtypeStruct((m, D_MODEL), F32)] + [c[2] for c in cast_specs],
        scratch_shapes=[pltpu.VMEM((tm, D_MODEL), BF16)],
        compiler_params=_params("arbitrary", "arbitrary"),
        name="ffn",
    )(x, g1, w_in, w_in, w_out, w_in, w_in, w_out, g2, *[src for src, _ in casts])
    return res[0], res[1:]


TM_INPROJ = 1024
TS_MIX = 512
TM_MERGE = 256
TM_FFN = 512
TF_FFN = 512
TS_KV = 512
CAST_ROWS_FFN_IN = 16
CAST_ROWS_FFN_OUT = 64
CAST_ROWS_W_IN = 32


def kernel(x_prompt, x_sample, state_conv, cache_kv_w128, cache_kv_w512, cache_kv_w2048, g_pre_mix, w_in, conv_w, ln_g, ln_b, w_s, b_s, w_a_out, w_b_out, w_c_out, w_o, g_post_mix, g_pre_ffn, w_ffn_in, w_ffn_out, g_post_ffn):
    wa_b, wb_b, wc_b, wo_b = (w.astype(BF16) for w in (w_a_out, w_b_out, w_c_out, w_o))
    w_in_l = w_in[0].astype(BF16)
    g_pre_mix, g_post_mix, g_pre_ffn, g_post_ffn = (
        g.reshape(DEPTH, 1, D_MODEL) for g in (g_pre_mix, g_post_mix, g_pre_ffn, g_post_ffn))

    tabs_p = _rope_tables(jnp.arange(SEQ, dtype=jnp.int32))
    tabs_s = _rope_tables(jnp.full((DEC_BATCH,), PAST_LEN, dtype=jnp.int32))

    n_keys = DIL_GROUPS[0][0] // DIL_GROUPS[0][1]
    caches = tuple(c.reshape(DEPTH, DEC_BATCH, n_keys, dil, 2, N_C_HEADS, HEAD_DIM)
                   for c, (_, dil) in zip((cache_kv_w128, cache_kv_w512, cache_kv_w2048), DIL_GROUPS))
    st_all = state_conv.reshape(DEPTH, DEC_BATCH, (CONV_W - 1) * W_A)
    bs_t = jnp.swapaxes(b_s, 1, 2)
    wsc = jnp.repeat(w_s[:, :, 0, 0], B_GROUP_W, axis=1)
    bsc = jnp.repeat(b_s[:, :, 0], B_GROUP_W, axis=1)

    xp = x_prompt.reshape(BATCH * SEQ, D_MODEL)
    xs = x_sample.reshape(DEC_BATCH, D_MODEL)
    conv_p, conv_s, vchunk_s = [], [], []
    kv_p = None
    kv_s = [[] for _ in DIL_GROUPS]

    def row(a, l):
        return a[l][None, :]

    for l in range(DEPTH):
        proj, qkv, wfi_l, wfo_l = _inproj(xp, g_pre_mix, w_in_l, l, tabs_p, TM_INPROJ, SEQ // TM_INPROJ, BF16,
                                          casts=((w_ffn_in, CAST_ROWS_FFN_IN), (w_ffn_out, CAST_ROWS_FFN_OUT)))
        ya, yb, ctail = _mix_ab(proj, conv_w[l], row(ln_g, l), row(ln_b, l), w_s[l], bs_t[l], TS_MIX)
        yc = _attn(qkv)
        kv_p = _kv_pack(qkv, l, kv_p)
        conv_p.append(ctail[:, 8 - (CONV_W - 1):])
        xp = _merge(proj, ya, yb, yc, xp, wa_b, wb_b, wc_b, wo_b, g_post_mix, l, TM_MERGE)
        next_casts = ((w_in, CAST_ROWS_W_IN),) if l + 1 < DEPTH else ()
        xp, w_in_next = _ffn(xp, g_pre_ffn, wfi_l, wfo_l, g_post_ffn, l, TM_FFN, TF_FFN, casts=next_casts)

        proj_s, qkv_s = _inproj(xs, g_pre_mix, w_in_l, l, tabs_s, DEC_BATCH, 1, F32)
        ya_s, yb_s, yc_s, cst, vn = _mix_sample(proj_s, qkv_s, st_all, conv_w[l], row(ln_g, l), row(ln_b, l),
                                                row(wsc, l), row(bsc, l), caches, l)
        for g in range(N_DIL):
            k = qkv_s[:, OFF_K + g * C_WIDTH:OFF_K + (g + 1) * C_WIDTH]
            v = qkv_s[:, OFF_V + g * C_WIDTH:OFF_V + (g + 1) * C_WIDTH]
            kv_s[g].append(jnp.stack([k.reshape(DEC_BATCH, 1, N_C_HEADS, HEAD_DIM),
                                      v.reshape(DEC_BATCH, 1, N_C_HEADS, HEAD_DIM)], axis=2))
        conv_s.append(cst.reshape(DEC_BATCH, CONV_W - 1, W_A))
        vchunk_s.append(vn.reshape(DEC_BATCH, 1, W_B))
        xs = _merge(proj_s, ya_s, yb_s, yc_s.reshape(DEC_BATCH, C_WIDTH), xs, wa_b, wb_b, wc_b, wo_b,
                    g_post_mix, l, DEC_BATCH)
        xs, _ = _ffn(xs, g_pre_ffn, wfi_l, wfo_l, g_post_ffn, l, DEC_BATCH, TF_FFN)
        if w_in_next:
            w_in_l = w_in_next[0]

    return (xp.reshape(BATCH, SEQ, D_MODEL), xs.reshape(DEC_BATCH, 1, D_MODEL),
            jnp.stack(conv_p, axis=0),
            kv_p[0], kv_p[1], kv_p[2],
            jnp.stack(conv_s, axis=0),
            jnp.stack(kv_s[0], axis=0), jnp.stack(kv_s[1], axis=0), jnp.stack(kv_s[2], axis=0),
            jnp.stack(vchunk_s, axis=0))
```

```python
import functools

import jax
import jax.numpy as jnp
from jax import lax
from jax.experimental import pallas as pl
from jax.experimental.pallas import tpu as pltpu

D_MODEL = 2048
BATCH = 4
SEQ = 2048
DEPTH = 4
DEC_BATCH = 8
PAST_LEN = 16384
W_A = 1024
CONV_W = 3
W_B = 1024
CHUNK = 128
N_B_GROUPS = 4
B_GROUP_W = W_B // N_B_GROUPS
N_C_HEADS = 4
HEAD_DIM = 128
ROT_DIM = HEAD_DIM // 4
ROPE_THETA = 500000.0
DIL_GROUPS = ((128, 1), (512, 4), (2048, 16))
N_DIL = len(DIL_GROUPS)
QB = 128
C_WIDTH = N_C_HEADS * HEAD_DIM
QKV_W = N_DIL * C_WIDTH
D_FF = ((-(-8 * D_MODEL // 3) + 255) // 256) * 256
IN_WIDTH = 3 * W_A + 2 * W_B + 3 * QKV_W + 3 * D_MODEL
EPS = 1e-6

GATE_W = 3 * D_MODEL
OFF_AB = GATE_W
MAIN_W = GATE_W + 3 * W_A + 2 * W_B
OFF_Q, OFF_K, OFF_V = 0, QKV_W, 2 * QKV_W
ORIG_GATE_OFF = IN_WIDTH - GATE_W

ROPE_ROWS = 256
V7X_VMEM_LIMIT = 60 * 1024 * 1024
NEG_BIG = -1e30

F32 = jnp.float32
BF16 = jnp.bfloat16


def _params(*sem):
    return pltpu.CompilerParams(dimension_semantics=sem, vmem_limit_bytes=V7X_VMEM_LIMIT)


def _rms(x, g):
    return x * lax.rsqrt(jnp.mean(x * x, axis=-1, keepdims=True) + EPS) * g


def _cast_specs(src, layer, rows, steps_per_row_tile, n_steps):
    _, r, c = src.shape
    nblk = r // rows
    assert r % rows == 0 and nblk <= n_steps

    def blk(i, j):
        return jnp.minimum(i * steps_per_row_tile + j, nblk - 1)

    return (pl.BlockSpec((None, rows, c), lambda i, j: (layer, blk(i, j), 0)),
            pl.BlockSpec((rows, c), lambda i, j: (blk(i, j), 0)),
            jax.ShapeDtypeStruct((r, c), BF16))


def _cast_blocks(src_refs, dst_refs):
    for s_ref, d_ref in zip(src_refs, dst_refs):
        d_ref[...] = s_ref[...].astype(BF16)


def _inproj_body(x_ref, g_ref, w0_ref, w1_ref, c_ref, s1_ref, s2_ref, *rest, main_steps, rope_steps, n_cast):
    cast_src, (main_ref, qkv_ref), cast_dst, h_ref = (
        rest[:n_cast], rest[n_cast:n_cast + 2], rest[n_cast + 2:2 * n_cast + 2], rest[-1])
    j = pl.program_id(1)
    tn = w0_ref.shape[1]
    halves = ((w0_ref, slice(0, tn)), (w1_ref, slice(tn, 2 * tn)))

    @pl.when(j == 0)
    def _():
        h_ref[...] = _rms(x_ref[...], g_ref[...]).astype(BF16)

    @pl.when(j < main_steps)
    def _():
        _cast_blocks(cast_src, cast_dst)
        for w_ref, cols in halves:
            main_ref[:, cols] = jnp.dot(h_ref[...], w_ref[...], preferred_element_type=F32).astype(main_ref.dtype)

    @pl.when(jnp.logical_and(j >= main_steps, j < main_steps + rope_steps))
    def _():
        _cast_blocks(cast_src, cast_dst)
        tm = x_ref.shape[0]
        rc = min(tm, ROPE_ROWS)
        for w_ref, cols in halves:
            for r in range(tm // rc):
                rs = slice(r * rc, (r + 1) * rc)
                acc = jnp.dot(h_ref[rs, :], w_ref[...], preferred_element_type=F32)
                for h in range(tn // HEAD_DIM):
                    a = acc[:, h * HEAD_DIM:(h + 1) * HEAD_DIM]
                    sl = slice(cols.start + h * HEAD_DIM, cols.start + (h + 1) * HEAD_DIM)
                    qkv_ref[rs, sl] = (a * c_ref[rs, :]
                                       + pltpu.roll(a, HEAD_DIM - ROT_DIM // 2, 1) * s1_ref[rs, :]
                                       + pltpu.roll(a, ROT_DIM // 2, 1) * s2_ref[rs, :])

    @pl.when(j >= main_steps + rope_steps)
    def _():
        _cast_blocks(cast_src, cast_dst)
        for w_ref, cols in halves:
            qkv_ref[:, cols] = jnp.dot(h_ref[...], w_ref[...], preferred_element_type=F32)


def _inproj(x, g, w, layer, tabs, tm, pos_blocks, main_dtype, casts=()):
    m = x.shape[0]
    tn = C_WIDTH
    n_tiles = IN_WIDTH // tn
    rot = ORIG_GATE_OFF // tn
    main_steps = MAIN_W // (2 * tn)
    rope_steps = 2 * QKV_W // (2 * tn)
    steps = main_steps + pl.cdiv(3 * QKV_W, 2 * tn)
    tab_spec = pl.BlockSpec((tm, HEAD_DIM), lambda i, j: (i % pos_blocks, 0))

    def w_spec(half):
        return pl.BlockSpec((D_MODEL, tn), lambda i, j: (0, (2 * j + half + rot) % n_tiles))

    cast_specs = [_cast_specs(src, layer, rows, steps, (m // tm) * steps) for src, rows in casts]
    return pl.pallas_call(
        functools.partial(_inproj_body, main_steps=main_steps, rope_steps=rope_steps, n_cast=len(casts)),
        grid=(m // tm, steps),
        in_specs=[
            pl.BlockSpec((tm, D_MODEL), lambda i, j: (i, 0)),
            pl.BlockSpec((None, 1, D_MODEL), lambda i, j: (layer, 0, 0)),
            w_spec(0), w_spec(1),
            tab_spec, tab_spec, tab_spec,
        ] + [c[0] for c in cast_specs],
        out_specs=[pl.BlockSpec((tm, 2 * tn), lambda i, j: (i, jnp.minimum(j, main_steps - 1))),
                   pl.BlockSpec((tm, 2 * tn), lambda i, j: (i, jnp.maximum(j - main_steps, 0)))]
                  + [c[1] for c in cast_specs],
        out_shape=[jax.ShapeDtypeStruct((m, MAIN_W), main_dtype),
                   jax.ShapeDtypeStruct((m, 3 * QKV_W), F32)] + [c[2] for c in cast_specs],
        scratch_shapes=[pltpu.VMEM((tm, D_MODEL), BF16)],
        compiler_params=_params("arbitrary", "arbitrary"),
        name="inproj",
    )(x, g, w, w, *tabs, *[src for src, _ in casts])


def _rope_tables(pos):
    inv_freq = ROPE_THETA ** (-jnp.arange(0, ROT_DIM, 2, dtype=jnp.float32) / ROT_DIM)
    ang = pos.astype(jnp.float32)[:, None] * inv_freq[None, :]
    cos, sin = jnp.cos(ang), jnp.sin(ang)
    n = pos.shape[0]
    half = ROT_DIM // 2
    rest = HEAD_DIM - ROT_DIM
    c = jnp.concatenate([cos, cos, jnp.ones((n, rest), F32)], axis=1)
    s1 = jnp.concatenate([-sin, jnp.zeros((n, half + rest), F32)], axis=1)
    s2 = jnp.concatenate([jnp.zeros((n, half), F32), sin, jnp.zeros((n, rest), F32)], axis=1)
    return c, s1, s2


def _mix_ab_body(ab_ref, ac_ref, ax_ref, bu_ref, bv_ref, cw_ref, lng_ref, lnb_ref, ws_ref, bs_ref,
                 ya_ref, yb_ref, cs_ref, zs_ref):
    ts = ab_ref.shape[0]

    @pl.when(pl.program_id(1) == 0)
    def _():
        zs_ref[0:8, :] = jnp.zeros((8, W_A), F32)

    z = ac_ref[...].astype(F32) * ax_ref[...].astype(F32)
    zs_ref[8:8 + ts, :] = z
    z1 = zs_ref[7:7 + ts, :]
    z2 = zs_ref[6:6 + ts, :]
    conv = cw_ref[0:1, :] * z2 + cw_ref[1:2, :] * z1 + cw_ref[2:3, :] * z
    ya_ref[...] = (ab_ref[...].astype(F32) * conv).astype(BF16)
    tail = zs_ref[ts:ts + 8, :]
    zs_ref[0:8, :] = tail
    cs_ref[0] = tail

    v = bv_ref[...].astype(F32)
    xc = v - jnp.mean(v, axis=-1, keepdims=True)
    var = jnp.mean(xc * xc, axis=-1, keepdims=True)
    vn = (xc * lax.rsqrt(var + EPS) * lng_ref[...] + lnb_ref[...]).astype(BF16)
    row = lax.broadcasted_iota(jnp.int32, (CHUNK, CHUNK), 0)
    col = lax.broadcasted_iota(jnp.int32, (CHUNK, CHUNK), 1)
    for g in range(N_B_GROUPS):
        wg = jnp.where(row >= col, ws_ref[g], 0.0).astype(BF16)
        bcol = bs_ref[:, g:g + 1]
        gs = slice(g * B_GROUP_W, (g + 1) * B_GROUP_W)
        for c in range(ts // CHUNK):
            rs = slice(c * CHUNK, (c + 1) * CHUNK)
            sg = jnp.dot(wg, vn[rs, gs], preferred_element_type=F32) + bcol
            yb_ref[rs, gs] = (bu_ref[rs, gs].astype(F32) * sg).astype(BF16)


def _mix_ab(proj, cw, lng, lnb, ws, bs_t, ts):
    nb = SEQ // ts
    blk = OFF_AB // W_A

    def seg(k):
        return pl.BlockSpec((ts, W_A), lambda b, s: (b * nb + s, blk + k))

    def full(shape):
        return pl.BlockSpec(shape, lambda b, s: (0,) * len(shape))

    rows = pl.BlockSpec((ts, W_A), lambda b, s: (b * nb + s, 0))
    return pl.pallas_call(
        _mix_ab_body,
        grid=(BATCH, nb),
        in_specs=[seg(0), seg(1), seg(2), seg(3), seg(4),
                  full((CONV_W, W_A)), full((1, W_B)), full((1, W_B)),
                  full((N_B_GROUPS, CHUNK, CHUNK)), full((CHUNK, N_B_GROUPS))],
        out_specs=[rows, rows, pl.BlockSpec((1, 8, W_A), lambda b, s: (b, 0, 0))],
        out_shape=[jax.ShapeDtypeStruct((BATCH * SEQ, W_A), BF16),
                   jax.ShapeDtypeStruct((BATCH * SEQ, W_B), BF16),
                   jax.ShapeDtypeStruct((BATCH, 8, W_A), F32)],
        scratch_shapes=[pltpu.VMEM((ts + 8, W_A), F32)],
        compiler_params=_params("arbitrary", "arbitrary"),
        name="mix_ab",
    )(proj, proj, proj, proj, proj, cw, lng, lnb, ws, bs_t)


def _attn_body(q0, q1, q2, k0, k1, k2, v0, v1, v2, y_ref,
               qc, kc, vc, s_scr, p_scr, inv_scr, o0, o1, o2, l0, l1, l2):
    row = lax.broadcasted_iota(jnp.int32, (QB, 2 * QB), 0)
    col = lax.broadcasted_iota(jnp.int32, (QB, 2 * QB), 1)
    band = jnp.logical_and(col >= row, col <= row + QB)
    band_first = jnp.logical_and(band, col >= QB)
    scale = HEAD_DIM ** -0.5
    nt = (((1,), (1,)), ((), ()))
    zero_blk = jnp.zeros((QB, HEAD_DIM), BF16)
    for (_, dil), q_ref, k_ref, v_ref, o_ref, l_ref in zip(
            DIL_GROUPS, (q0, q1, q2), (k0, k1, k2), (v0, v1, v2), (o0, o1, o2), (l0, l1, l2)):
        L = SEQ // dil
        nblk = L // QB

        def tok_rows(r, c):
            if dil == 1:
                return pl.ds(c * QB, QB)
            return pl.ds(r + c * QB * dil, QB, stride=dil)

        for r in range(dil):
            src = pl.ds(0, L) if dil == 1 else pl.ds(r, L, stride=dil)
            base = r * (L + QB)
            qc[r * L:(r + 1) * L, :] = q_ref[src, :].astype(BF16)
            kc[base:base + QB, :] = zero_blk
            vc[base:base + QB, :] = zero_blk
            kc[base + QB:base + QB + L, :] = k_ref[src, :].astype(BF16)
            vc[base + QB:base + QB + L, :] = v_ref[src, :].astype(BF16)

        for r in range(dil):
            for c in range(nblk):
                t = r * nblk + c
                kk = kc[r * (L + QB) + c * QB:r * (L + QB) + (c + 2) * QB, :]
                s = lax.dot_general(qc[t * QB:(t + 1) * QB, :], kk, nt, preferred_element_type=F32) * scale
                s_scr[t * QB:(t + 1) * QB, :] = jnp.where(band_first if c == 0 else band, s, NEG_BIG)

        for r in range(dil):
            for c in range(nblk):
                t = r * nblk + c
                s = s_scr[t * QB:(t + 1) * QB, :]
                m = jnp.max(s, axis=-1, keepdims=True)
                p = jnp.exp(s - m)
                den = jnp.sum(p, axis=-1, keepdims=True)
                p_scr[t * QB:(t + 1) * QB, :] = p.astype(BF16)
                inv_scr[t * QB:(t + 1) * QB, :] = jnp.broadcast_to(1.0 / den, (QB, HEAD_DIM))
                l_ref[tok_rows(r, c), :] = jnp.broadcast_to(m + jnp.log(den), (QB, HEAD_DIM))

        for r in range(dil):
            for c in range(nblk):
                t = r * nblk + c
                vv = vc[r * (L + QB) + c * QB:r * (L + QB) + (c + 2) * QB, :]
                o = jnp.dot(p_scr[t * QB:(t + 1) * QB, :], vv, preferred_element_type=F32)
                o_ref[tok_rows(r, c), :] = o * inv_scr[t * QB:(t + 1) * QB, :]
    y_ref[...] = _combine_groups(o0[...], o1[...], o2[...], l0[...], l1[...], l2[...]).astype(BF16)


def _attn(proj):
    def seg(off, g):
        base = (off + g * C_WIDTH) // HEAD_DIM
        return pl.BlockSpec((SEQ, HEAD_DIM), lambda b, h: (b, base + h))

    specs = [seg(off, g) for off in (OFF_Q, OFF_K, OFF_V) for g in range(N_DIL)]
    return pl.pallas_call(
        _attn_body,
        grid=(BATCH, N_C_HEADS),
        in_specs=specs,
        out_specs=pl.BlockSpec((SEQ, HEAD_DIM), lambda b, h: (b, h)),
        out_shape=jax.ShapeDtypeStruct((BATCH * SEQ, C_WIDTH), BF16),
        scratch_shapes=[pltpu.VMEM((SEQ, HEAD_DIM), BF16),
                        pltpu.VMEM((2 * SEQ, HEAD_DIM), BF16),
                        pltpu.VMEM((2 * SEQ, HEAD_DIM), BF16),
                        pltpu.VMEM((SEQ, 2 * QB), F32),
                        pltpu.VMEM((SEQ, 2 * QB), BF16),
                        pltpu.VMEM((SEQ, HEAD_DIM), F32)]
                       + [pltpu.VMEM((SEQ, HEAD_DIM), F32)] * (2 * N_DIL),
        compiler_params=_params("arbitrary", "arbitrary"),
        name="attn",
    )(*([proj] * (3 * N_DIL)))


def _kv_pack_body(*refs, fixed):
    k_refs, v_refs, out_refs = refs[0:N_DIL], refs[N_DIL:2 * N_DIL], refs[-N_DIL:]
    for k_ref, v_ref, o_ref, is_fixed in zip(k_refs, v_refs, out_refs, fixed):
        def pack(k_ref=k_ref, v_ref=v_ref, o_ref=o_ref):
            for h in range(N_C_HEADS):
                sl = slice(h * HEAD_DIM, (h + 1) * HEAD_DIM)
                o_ref[:, 0, h, :] = k_ref[:, sl]
                o_ref[:, 1, h, :] = v_ref[:, sl]

        if is_fixed:
            pl.when(pl.program_id(1) == 0)(pack)
        else:
            pack()


def _kv_pack(proj, layer, prev):
    ts = TS_KV
    keeps = [min(win, SEQ) for win, _ in DIL_GROUPS]

    def src(off, g):
        keep, col = keeps[g], (off + g * C_WIDTH) // C_WIDTH
        if keep <= ts:
            nb = SEQ // keep
            return pl.BlockSpec((keep, C_WIDTH), lambda b, s: (b * nb + nb - 1, col))
        nb, first = SEQ // ts, (SEQ - keep) // ts
        return pl.BlockSpec((ts, C_WIDTH), lambda b, s: (b * nb + jnp.maximum(s, first), col))

    def dst(keep):
        if keep <= ts:
            return pl.BlockSpec((None, None, keep, 2, N_C_HEADS, HEAD_DIM), lambda b, s: (layer, b, 0, 0, 0, 0))
        first = (SEQ - keep) // ts
        return pl.BlockSpec((None, None, ts, 2, N_C_HEADS, HEAD_DIM),
                            lambda b, s: (layer, b, jnp.maximum(s - first, 0), 0, 0, 0))

    in_specs = [src(OFF_K, g) for g in range(N_DIL)] + [src(OFF_V, g) for g in range(N_DIL)]
    args = [proj] * (2 * N_DIL)
    aliases = {}
    if prev is not None:
        in_specs += [pl.BlockSpec(memory_space=pl.ANY)] * N_DIL
        aliases = {2 * N_DIL + g: g for g in range(N_DIL)}
        args += list(prev)
    return pl.pallas_call(
        functools.partial(_kv_pack_body, fixed=tuple(keep <= ts for keep in keeps)),
        grid=(BATCH, SEQ // ts),
        in_specs=in_specs,
        out_specs=[dst(keep) for keep in keeps],
        out_shape=[jax.ShapeDtypeStruct((DEPTH, BATCH, keep, 2, N_C_HEADS, HEAD_DIM), F32) for keep in keeps],
        input_output_aliases=aliases,
        compiler_params=_params("arbitrary", "arbitrary"),
        name="kv_pack",
    )(*args)


def _mix_sample_body(p_ref, qkv_ref, st_ref, cw_ref, lng_ref, lnb_ref, wsc_ref, bsc_ref, c0_ref, c1_ref, c2_ref,
                     ya_ref, yb_ref, yc_ref, cs_ref, vn_ref):
    @pl.when(pl.program_id(0) == 0)
    def _():
        def seg(off, w):
            return p_ref[:, off:off + w]

        z = seg(OFF_AB + W_A, W_A) * seg(OFF_AB + 2 * W_A, W_A)
        st0 = st_ref[0, :, 0:W_A]
        st1 = st_ref[0, :, W_A:2 * W_A]
        conv = cw_ref[0:1, :] * st0 + cw_ref[1:2, :] * st1 + cw_ref[2:3, :] * z
        ya_ref[...] = (seg(OFF_AB, W_A) * conv).astype(BF16)
        cs_ref[:, 0:W_A] = st1
        cs_ref[:, W_A:2 * W_A] = z

        v = seg(OFF_AB + 3 * W_A + W_B, W_B)
        xc = v - jnp.mean(v, axis=-1, keepdims=True)
        var = jnp.mean(xc * xc, axis=-1, keepdims=True)
        vn = xc * lax.rsqrt(var + EPS) * lng_ref[...] + lnb_ref[...]
        vn_ref[...] = vn
        yb_ref[...] = (seg(OFF_AB + 3 * W_A, W_B) * (wsc_ref[...] * vn + bsc_ref[...])).astype(BF16)

    scale = HEAD_DIM ** -0.5
    outs, lses = [], []
    for g, c_ref in enumerate((c0_ref, c1_ref, c2_ref)):
        q = qkv_ref[g]
        kn = qkv_ref[N_DIL + g]
        vnew = qkv_ref[2 * N_DIL + g]
        kc = c_ref[:, 0]
        vc = c_ref[:, 1]
        s_c = jnp.sum(q[None] * kc, axis=-1, keepdims=True) * scale
        s_n = jnp.sum(q * kn, axis=-1, keepdims=True) * scale
        m = jnp.maximum(jnp.max(s_c, axis=0), s_n)
        p_c = jnp.exp(s_c - m[None])
        p_n = jnp.exp(s_n - m)
        den = jnp.sum(p_c, axis=0) + p_n
        outs.append((jnp.sum(p_c * vc, axis=0) + p_n * vnew) / den)
        lses.append(m + jnp.log(den))
    yc_ref[...] = _combine_groups(*outs, *lses)


def _mix_sample(proj_s, qkv_s, st, cw, lng, lnb, wsc, bsc, caches, layer):
    def full(shape):
        return pl.BlockSpec(shape, lambda b: (0,) * len(shape))

    n_keys = DIL_GROUPS[0][0] // DIL_GROUPS[0][1]
    cache_specs = [pl.BlockSpec((None, None, n_keys, None, 2, N_C_HEADS, HEAD_DIM),
                                lambda b: (layer, b, 0, 0, 0, 0, 0)) for _ in caches]
    qkv = qkv_s.reshape(DEC_BATCH, 3 * N_DIL, N_C_HEADS, HEAD_DIM)
    return pl.pallas_call(
        _mix_sample_body,
        grid=(DEC_BATCH,),
        in_specs=[full((DEC_BATCH, MAIN_W)),
                  pl.BlockSpec((None, 3 * N_DIL, N_C_HEADS, HEAD_DIM), lambda b: (b, 0, 0, 0)),
                  pl.BlockSpec((1, DEC_BATCH, 2 * W_A), lambda b: (layer, 0, 0)),
                  full((CONV_W, W_A)), full((1, W_B)), full((1, W_B)), full((1, W_B)), full((1, W_B))]
                 + cache_specs,
        out_specs=[full((DEC_BATCH, W_A)), full((DEC_BATCH, W_B)),
                   pl.BlockSpec((None, N_C_HEADS, HEAD_DIM), lambda b: (b, 0, 0)),
                   full((DEC_BATCH, 2 * W_A)), full((DEC_BATCH, W_B))],
        out_shape=[jax.ShapeDtypeStruct((DEC_BATCH, W_A), BF16),
                   jax.ShapeDtypeStruct((DEC_BATCH, W_B), BF16),
                   jax.ShapeDtypeStruct((DEC_BATCH, N_C_HEADS, HEAD_DIM), F32),
                   jax.ShapeDtypeStruct((DEC_BATCH, 2 * W_A), F32),
                   jax.ShapeDtypeStruct((DEC_BATCH, W_B), F32)],
        compiler_params=_params("arbitrary"),
        name="mix_sample",
    )(proj_s, qkv, st, cw, lng, lnb, wsc, bsc, *caches)


def _combine_groups(o0, o1, o2, l0, l1, l2):
    m = jnp.maximum(jnp.maximum(l0, l1), l2)
    e0, e1, e2 = jnp.exp(l0 - m), jnp.exp(l1 - m), jnp.exp(l2 - m)
    return (e0 * o0 + e1 * o1 + e2 * o2) / (e0 + e1 + e2)


def _merge_body(ga_ref, gb_ref, gc_ref, ya_ref, yb_ref, yc_ref,
                x_ref, wa_ref, wb_ref, wc_ref, wo_ref, gp_ref, out_ref, mm_ref):
    s = pl.program_id(0)
    n_tiles = pl.num_programs(0) - 1

    def gated():
        def branch(g_ref, y_ref, w_ref):
            return (jax.nn.sigmoid(g_ref[...].astype(F32))
                    * jnp.dot(y_ref[...].astype(BF16), w_ref[...], preferred_element_type=F32))
        return (branch(ga_ref, ya_ref, wa_ref) + branch(gb_ref, yb_ref, wb_ref)
                + branch(gc_ref, yc_ref, wc_ref)).astype(BF16)

    def finish():
        r = jnp.dot(mm_ref[...], wo_ref[...], preferred_element_type=F32)
        return x_ref[...] + _rms(r, gp_ref[...])

    @pl.when(s == 0)
    def _():
        mm_ref[...] = gated()

    @pl.when(jnp.logical_and(s > 0, s < n_tiles))
    def _():
        out_ref[...] = finish()
        mm_ref[...] = gated()

    @pl.when(s == n_tiles)
    def _():
        out_ref[...] = finish()


def _merge(proj, ya, yb, yc, x, wa, wb, wc, wo, gp, layer, tm):
    m = x.shape[0]
    n_tiles = m // tm

    def ahead(w, k=0):
        return pl.BlockSpec((tm, w), lambda s: (jnp.minimum(s, n_tiles - 1), k))

    def behind(w):
        return pl.BlockSpec((tm, w), lambda s: (jnp.maximum(s - 1, 0), 0))

    def resident(k):
        return pl.BlockSpec((None, k, D_MODEL), lambda s: (layer, 0, 0), pipeline_mode=pl.Buffered(1))

    return pl.pallas_call(
        _merge_body,
        grid=(n_tiles + 1,),
        in_specs=[ahead(D_MODEL, 0), ahead(D_MODEL, 1), ahead(D_MODEL, 2), ahead(W_A), ahead(W_B), ahead(C_WIDTH),
                  behind(D_MODEL),
                  resident(W_A), resident(W_B), resident(C_WIDTH), resident(D_MODEL), resident(1)],
        out_specs=behind(D_MODEL),
        out_shape=jax.ShapeDtypeStruct((m, D_MODEL), F32),
        scratch_shapes=[pltpu.VMEM((tm, D_MODEL), BF16)],
        compiler_params=_params("arbitrary"),
        name="merge",
    )(proj, proj, proj, ya, yb, yc, x, wa, wb, wc, wo, gp)


def _ffn_body(x_ref, g1_ref, wg_ref, wu_ref, wo_ref, g2_ref, *rest, n_cast):
    cast_src, out_ref, cast_dst = rest[:n_cast], rest[n_cast], rest[n_cast + 1:2 * n_cast + 1]
    h_ref, act_ref = rest[-2:]
    j = pl.program_id(1)
    n_tiles = pl.num_programs(1) - 1

    def activation():
        h = h_ref[...]
        gate = jnp.dot(h, wg_ref[...], preferred_element_type=F32)
        up = jnp.dot(h, wu_ref[...], preferred_element_type=F32)
        return (gate * jax.nn.sigmoid(gate) * up).astype(BF16)

    def out_part():
        return jnp.dot(act_ref[...], wo_ref[...], preferred_element_type=F32)

    @pl.when(j == 0)
    def _():
        _cast_blocks(cast_src, cast_dst)
        h_ref[...] = _rms(x_ref[...], g1_ref[...]).astype(BF16)
        out_ref[...] = jnp.zeros_like(out_ref)
        act_ref[...] = activation()

    @pl.when(jnp.logical_and(j > 0, j < n_tiles))
    def _():
        _cast_blocks(cast_src, cast_dst)
        out_ref[...] += out_part()
        act_ref[...] = activation()

    @pl.when(j == n_tiles)
    def _():
        _cast_blocks(cast_src, cast_dst)
        out_ref[...] = x_ref[...] + _rms(out_ref[...] + out_part(), g2_ref[...])


def _ffn(x, g1, w_in, w_out, g2, layer, tm, tf, casts=()):
    m = x.shape[0]
    nt = D_FF // tf
    steps = nt + 1

    def in_tile(j):
        return jnp.minimum(j, nt - 1)

    def out_tile(j):
        return jnp.maximum(j - 1, 0)

    cast_specs = [_cast_specs(src, layer + 1, rows, steps, (m // tm) * steps) for src, rows in casts]
    res = pl.pallas_call(
        functools.partial(_ffn_body, n_cast=len(casts)),
        grid=(m // tm, steps),
        in_specs=[
            pl.BlockSpec((tm, D_MODEL), lambda i, j: (i, 0)),
            pl.BlockSpec((None, 1, D_MODEL), lambda i, j: (layer, 0, 0)),
            pl.BlockSpec((D_MODEL, tf), lambda i, j: (0, in_tile(j))),
            pl.BlockSpec((D_MODEL, tf), lambda i, j: (0, nt + in_tile(j))),
            pl.BlockSpec((tf, D_MODEL), lambda i, j: (out_tile(j), 0)),
            pl.BlockSpec((None, 1, D_MODEL), lambda i, j: (layer, 0, 0)),
        ] + [c[0] for c in cast_specs],
        out_specs=[pl.BlockSpec((tm, D_MODEL), lambda i, j: (i, 0))] + [c[1] for c in cast_specs],
        out_shape=[jax.ShapeDtypeStruct((m, D_MODEL), F32)] + [c[2] for c in cast_specs],
        scratch_shapes=[pltpu.VMEM((tm, D_MODEL), BF16), pltpu.VMEM((tm, tf), BF16)],
        compiler_params=_params("arbitrary", "arbitrary"),
        name="ffn",
    )(x, g1, w_in, w_in, w_out, g2, *[src for src, _ in casts])
    return res[0], res[1:]


TM_INPROJ = 1024
TS_MIX = 512
TM_MERGE = 256
TM_FFN = 512
TF_FFN = 512
TS_KV = 512
CAST_ROWS_FFN_IN = 16
CAST_ROWS_FFN_OUT = 64
CAST_ROWS_W_IN = 16


def kernel(x_prompt, x_sample, state_conv, cache_kv_w128, cache_kv_w512, cache_kv_w2048, g_pre_mix, w_in, conv_w, ln_g, ln_b, w_s, b_s, w_a_out, w_b_out, w_c_out, w_o, g_post_mix, g_pre_ffn, w_ffn_in, w_ffn_out, g_post_ffn):
    wa_b, wb_b, wc_b, wo_b = (w.astype(BF16) for w in (w_a_out, w_b_out, w_c_out, w_o))
    w_in_l = w_in[0].astype(BF16)
    g_pre_mix, g_post_mix, g_pre_ffn, g_post_ffn = (
        g.reshape(DEPTH, 1, D_MODEL) for g in (g_pre_mix, g_post_mix, g_pre_ffn, g_post_ffn))

    tabs_p = _rope_tables(jnp.arange(SEQ, dtype=jnp.int32))
    tabs_s = _rope_tables(jnp.full((DEC_BATCH,), PAST_LEN, dtype=jnp.int32))

    n_keys = DIL_GROUPS[0][0] // DIL_GROUPS[0][1]
    caches = tuple(c.reshape(DEPTH, DEC_BATCH, n_keys, dil, 2, N_C_HEADS, HEAD_DIM)
                   for c, (_, dil) in zip((cache_kv_w128, cache_kv_w512, cache_kv_w2048), DIL_GROUPS))
    st_all = state_conv.reshape(DEPTH, DEC_BATCH, (CONV_W - 1) * W_A)
    bs_t = jnp.swapaxes(b_s, 1, 2)
    wsc = jnp.repeat(w_s[:, :, 0, 0], B_GROUP_W, axis=1)
    bsc = jnp.repeat(b_s[:, :, 0], B_GROUP_W, axis=1)

    xp = x_prompt.reshape(BATCH * SEQ, D_MODEL)
    xs = x_sample.reshape(DEC_BATCH, D_MODEL)
    conv_p, conv_s, vchunk_s = [], [], []
    kv_p = None
    kv_s = [[] for _ in DIL_GROUPS]

    def row(a, l):
        return a[l][None, :]

    for l in range(DEPTH):
        proj, qkv, wfi_l, wfo_l = _inproj(xp, g_pre_mix, w_in_l, l, tabs_p, TM_INPROJ, SEQ // TM_INPROJ, BF16,
                                          casts=((w_ffn_in, CAST_ROWS_FFN_IN), (w_ffn_out, CAST_ROWS_FFN_OUT)))
        ya, yb, ctail = _mix_ab(proj, conv_w[l], row(ln_g, l), row(ln_b, l), w_s[l], bs_t[l], TS_MIX)
        yc = _attn(qkv)
        kv_p = _kv_pack(qkv, l, kv_p)
        conv_p.append(ctail[:, 8 - (CONV_W - 1):])
        xp = _merge(proj, ya, yb, yc, xp, wa_b, wb_b, wc_b, wo_b, g_post_mix, l, TM_MERGE)
        next_casts = ((w_in, CAST_ROWS_W_IN),) if l + 1 < DEPTH else ()
        xp, w_in_next = _ffn(xp, g_pre_ffn, wfi_l, wfo_l, g_post_ffn, l, TM_FFN, TF_FFN, casts=next_casts)

        proj_s, qkv_s = _inproj(xs, g_pre_mix, w_in_l, l, tabs_s, DEC_BATCH, 1, F32)
        ya_s, yb_s, yc_s, cst, vn = _mix_sample(proj_s, qkv_s, st_all, conv_w[l], row(ln_g, l), row(ln_b, l),
                                                row(wsc, l), row(bsc, l), caches, l)
        for g in range(N_DIL):
            k = qkv_s[:, OFF_K + g * C_WIDTH:OFF_K + (g + 1) * C_WIDTH]
            v = qkv_s[:, OFF_V + g * C_WIDTH:OFF_V + (g + 1) * C_WIDTH]
            kv_s[g].append(jnp.stack([k.reshape(DEC_BATCH, 1, N_C_HEADS, HEAD_DIM),
                                      v.reshape(DEC_BATCH, 1, N_C_HEADS, HEAD_DIM)], axis=2))
        conv_s.append(cst.reshape(DEC_BATCH, CONV_W - 1, W_A))
        vchunk_s.append(vn.reshape(DEC_BATCH, 1, W_B))
        xs = _merge(proj_s, ya_s, yb_s, yc_s.reshape(DEC_BATCH, C_WIDTH), xs, wa_b, wb_b, wc_b, wo_b,
                    g_post_mix, l, DEC_BATCH)
        xs, _ = _ffn(xs, g_pre_ffn, wfi_l, wfo_l, g_post_ffn, l, DEC_BATCH, TF_FFN)
        if w_in_next:
            w_in_l = w_in_next[0]

    return (xp.reshape(BATCH, SEQ, D_MODEL), xs.reshape(DEC_BATCH, 1, D_MODEL),
            jnp.stack(conv_p, axis=0),
            kv_p[0], kv_p[1], kv_p[2],
            jnp.stack(conv_s, axis=0),
            jnp.stack(kv_s[0], axis=0), jnp.stack(kv_s[1], axis=0), jnp.stack(kv_s[2], axis=0),
            jnp.stack(vchunk_s, axis=0))
```

```python
import functools

import jax
import jax.numpy as jnp
from jax import lax
from jax.experimental import pallas as pl
from jax.experimental.pallas import tpu as pltpu

D_MODEL = 2048
BATCH = 4
SEQ = 2048
DEPTH = 4
DEC_BATCH = 8
PAST_LEN = 16384
W_A = 1024
CONV_W = 3
W_B = 1024
CHUNK = 128
N_B_GROUPS = 4
B_GROUP_W = W_B // N_B_GROUPS
N_C_HEADS = 4
HEAD_DIM = 128
ROT_DIM = HEAD_DIM // 4
ROPE_THETA = 500000.0
DIL_GROUPS = ((128, 1), (512, 4), (2048, 16))
N_DIL = len(DIL_GROUPS)
QB = 128
C_WIDTH = N_C_HEADS * HEAD_DIM
QKV_W = N_DIL * C_WIDTH
D_FF = ((-(-8 * D_MODEL // 3) + 255) // 256) * 256
IN_WIDTH = 3 * W_A + 2 * W_B + 3 * QKV_W + 3 * D_MODEL
EPS = 1e-6

GATE_W = 3 * D_MODEL
OFF_AB = GATE_W
MAIN_W = GATE_W + 3 * W_A + 2 * W_B
OFF_Q, OFF_K, OFF_V = 0, QKV_W, 2 * QKV_W
ORIG_GATE_OFF = IN_WIDTH - GATE_W

ROPE_ROWS = 256
V7X_VMEM_LIMIT = 60 * 1024 * 1024
NEG_BIG = -1e30

F32 = jnp.float32
BF16 = jnp.bfloat16


def _params(*sem):
    return pltpu.CompilerParams(dimension_semantics=sem, vmem_limit_bytes=V7X_VMEM_LIMIT)


def _rms(x, g):
    return x * lax.rsqrt(jnp.mean(x * x, axis=-1, keepdims=True) + EPS) * g


def _cast_specs(src, layer, rows, steps_per_row_tile, n_steps):
    _, r, c = src.shape
    nblk = r // rows
    assert r % rows == 0 and nblk <= n_steps

    def blk(i, j):
        return jnp.minimum(i * steps_per_row_tile + j, nblk - 1)

    return (pl.BlockSpec((None, rows, c), lambda i, j: (layer, blk(i, j), 0)),
            pl.BlockSpec((rows, c), lambda i, j: (blk(i, j), 0)),
            jax.ShapeDtypeStruct((r, c), BF16))


def _cast_blocks(src_refs, dst_refs):
    for s_ref, d_ref in zip(src_refs, dst_refs):
        d_ref[...] = s_ref[...].astype(BF16)


def _rope(a, c, s1, s2):
    return a * c + pltpu.roll(a, HEAD_DIM - ROT_DIM // 2, 1) * s1 + pltpu.roll(a, ROT_DIM // 2, 1) * s2


def _inproj_body(x_ref, g_ref, w0_ref, w1_ref, c_ref, s1_ref, s2_ref, xs_ref, cs_ref, s1s_ref, s2s_ref, *rest,
                 main_steps, rope_steps, n_cast):
    cast_src = rest[:n_cast]
    main_ref, qkv_ref, mains_ref, qkvs_ref = rest[n_cast:n_cast + 4]
    cast_dst = rest[n_cast + 4:2 * n_cast + 4]
    h_ref, hs_ref = rest[-2:]
    i = pl.program_id(0)
    j = pl.program_id(1)
    tn = w0_ref.shape[1]
    heads = tn // HEAD_DIM
    halves = ((w0_ref, slice(0, tn)), (w1_ref, slice(tn, 2 * tn)))
    with_samples = i == 0

    @pl.when(j == 0)
    def _():
        h_ref[...] = _rms(x_ref[...], g_ref[...]).astype(BF16)

    @pl.when(jnp.logical_and(with_samples, j == 0))
    def _():
        hs_ref[...] = _rms(xs_ref[...], g_ref[...]).astype(BF16)

    @pl.when(j < main_steps)
    def _():
        _cast_blocks(cast_src, cast_dst)
        for w_ref, cols in halves:
            main_ref[:, cols] = jnp.dot(h_ref[...], w_ref[...], preferred_element_type=F32).astype(main_ref.dtype)

        @pl.when(with_samples)
        def _():
            for w_ref, cols in halves:
                mains_ref[j, :, cols] = jnp.dot(hs_ref[...], w_ref[...], preferred_element_type=F32)

    @pl.when(jnp.logical_and(j >= main_steps, j < main_steps + rope_steps))
    def _():
        _cast_blocks(cast_src, cast_dst)
        tm = x_ref.shape[0]
        rc = min(tm, ROPE_ROWS)
        for w_ref, cols in halves:
            for r in range(tm // rc):
                rs = slice(r * rc, (r + 1) * rc)
                acc = jnp.dot(h_ref[rs, :], w_ref[...], preferred_element_type=F32)
                for h in range(heads):
                    sl = slice(cols.start + h * HEAD_DIM, cols.start + (h + 1) * HEAD_DIM)
                    qkv_ref[rs, sl] = _rope(acc[:, h * HEAD_DIM:(h + 1) * HEAD_DIM],
                                            c_ref[rs, :], s1_ref[rs, :], s2_ref[rs, :])

        @pl.when(with_samples)
        def _():
            for w_ref, cols in halves:
                acc = jnp.dot(hs_ref[...], w_ref[...], preferred_element_type=F32)
                for h in range(heads):
                    sl = slice(cols.start + h * HEAD_DIM, cols.start + (h + 1) * HEAD_DIM)
                    qkvs_ref[j - main_steps, :, sl] = _rope(acc[:, h * HEAD_DIM:(h + 1) * HEAD_DIM],
                                                            cs_ref[...], s1s_ref[...], s2s_ref[...])

    @pl.when(j >= main_steps + rope_steps)
    def _():
        _cast_blocks(cast_src, cast_dst)
        for w_ref, cols in halves:
            qkv_ref[:, cols] = jnp.dot(h_ref[...], w_ref[...], preferred_element_type=F32)

        @pl.when(with_samples)
        def _():
            for w_ref, cols in halves:
                qkvs_ref[j - main_steps, :, cols] = jnp.dot(hs_ref[...], w_ref[...], preferred_element_type=F32)


def _inproj(x, xs, g, w, layer, tabs, tabs_s, tm, casts=()):
    m = x.shape[0]
    ns = xs.shape[0]
    tn = C_WIDTH
    n_tiles = IN_WIDTH // tn
    rot = ORIG_GATE_OFF // tn
    main_steps = MAIN_W // (2 * tn)
    rope_steps = 2 * QKV_W // (2 * tn)
    qkv_steps = pl.cdiv(3 * QKV_W, 2 * tn)
    steps = main_steps + qkv_steps
    tab_spec = pl.BlockSpec((tm, HEAD_DIM), lambda i, j: (i % (SEQ // tm), 0))
    tab_s_spec = pl.BlockSpec((ns, HEAD_DIM), lambda i, j: (0, 0))

    def w_spec(half):
        return pl.BlockSpec((D_MODEL, tn), lambda i, j: (0, (2 * j + half + rot) % n_tiles))

    cast_specs = [_cast_specs(src, layer, rows, steps, (m // tm) * steps) for src, rows in casts]
    res = pl.pallas_call(
        functools.partial(_inproj_body, main_steps=main_steps, rope_steps=rope_steps, n_cast=len(casts)),
        grid=(m // tm, steps),
        in_specs=[
            pl.BlockSpec((tm, D_MODEL), lambda i, j: (i, 0)),
            pl.BlockSpec((None, 1, D_MODEL), lambda i, j: (layer, 0, 0)),
            w_spec(0), w_spec(1),
            tab_spec, tab_spec, tab_spec,
            pl.BlockSpec((ns, D_MODEL), lambda i, j: (0, 0)),
            tab_s_spec, tab_s_spec, tab_s_spec,
        ] + [c[0] for c in cast_specs],
        out_specs=[pl.BlockSpec((tm, 2 * tn), lambda i, j: (i, jnp.minimum(j, main_steps - 1))),
                   pl.BlockSpec((tm, 2 * tn), lambda i, j: (i, jnp.maximum(j - main_steps, 0))),
                   pl.BlockSpec((main_steps, ns, 2 * tn), lambda i, j: (0, 0, 0)),
                   pl.BlockSpec((qkv_steps, ns, 2 * tn), lambda i, j: (0, 0, 0))]
                  + [c[1] for c in cast_specs],
        out_shape=[jax.ShapeDtypeStruct((m, MAIN_W), BF16),
                   jax.ShapeDtypeStruct((m, 3 * QKV_W), F32),
                   jax.ShapeDtypeStruct((main_steps, ns, 2 * tn), F32),
                   jax.ShapeDtypeStruct((qkv_steps, ns, 2 * tn), F32)] + [c[2] for c in cast_specs],
        scratch_shapes=[pltpu.VMEM((tm, D_MODEL), BF16), pltpu.VMEM((ns, D_MODEL), BF16)],
        compiler_params=_params("arbitrary", "arbitrary"),
        name="inproj",
    )(x, g, w, w, *tabs, xs, *tabs_s, *[src for src, _ in casts])
    main_s = jnp.swapaxes(res[2], 0, 1).reshape(ns, MAIN_W)
    qkv_s = jnp.swapaxes(res[3], 0, 1).reshape(ns, qkv_steps * 2 * tn)[:, :3 * QKV_W]
    return (res[0], res[1], main_s, qkv_s, *res[4:])


def _rope_tables(pos):
    inv_freq = ROPE_THETA ** (-jnp.arange(0, ROT_DIM, 2, dtype=jnp.float32) / ROT_DIM)
    ang = pos.astype(jnp.float32)[:, None] * inv_freq[None, :]
    cos, sin = jnp.cos(ang), jnp.sin(ang)
    n = pos.shape[0]
    half = ROT_DIM // 2
    rest = HEAD_DIM - ROT_DIM
    c = jnp.concatenate([cos, cos, jnp.ones((n, rest), F32)], axis=1)
    s1 = jnp.concatenate([-sin, jnp.zeros((n, half + rest), F32)], axis=1)
    s2 = jnp.concatenate([jnp.zeros((n, half), F32), sin, jnp.zeros((n, rest), F32)], axis=1)
    return c, s1, s2


def _mix_ab_body(ab_ref, ac_ref, ax_ref, bu_ref, bv_ref, cw_ref, lng_ref, lnb_ref, ws_ref, bs_ref,
                 ya_ref, yb_ref, cs_ref, zs_ref):
    ts = ab_ref.shape[0]

    @pl.when(pl.program_id(1) == 0)
    def _():
        zs_ref[0:8, :] = jnp.zeros((8, W_A), F32)

    z = ac_ref[...].astype(F32) * ax_ref[...].astype(F32)
    zs_ref[8:8 + ts, :] = z
    z1 = zs_ref[7:7 + ts, :]
    z2 = zs_ref[6:6 + ts, :]
    conv = cw_ref[0:1, :] * z2 + cw_ref[1:2, :] * z1 + cw_ref[2:3, :] * z
    ya_ref[...] = (ab_ref[...].astype(F32) * conv).astype(BF16)
    tail = zs_ref[ts:ts + 8, :]
    zs_ref[0:8, :] = tail
    cs_ref[0] = tail

    v = bv_ref[...].astype(F32)
    xc = v - jnp.mean(v, axis=-1, keepdims=True)
    var = jnp.mean(xc * xc, axis=-1, keepdims=True)
    vn = (xc * lax.rsqrt(var + EPS) * lng_ref[...] + lnb_ref[...]).astype(BF16)
    row = lax.broadcasted_iota(jnp.int32, (CHUNK, CHUNK), 0)
    col = lax.broadcasted_iota(jnp.int32, (CHUNK, CHUNK), 1)
    for g in range(N_B_GROUPS):
        wg = jnp.where(row >= col, ws_ref[g], 0.0).astype(BF16)
        bcol = bs_ref[:, g:g + 1]
        gs = slice(g * B_GROUP_W, (g + 1) * B_GROUP_W)
        for c in range(ts // CHUNK):
            rs = slice(c * CHUNK, (c + 1) * CHUNK)
            sg = jnp.dot(wg, vn[rs, gs], preferred_element_type=F32) + bcol
            yb_ref[rs, gs] = (bu_ref[rs, gs].astype(F32) * sg).astype(BF16)


def _mix_ab(proj, cw, lng, lnb, ws, bs_t, ts):
    nb = SEQ // ts
    blk = OFF_AB // W_A

    def seg(k):
        return pl.BlockSpec((ts, W_A), lambda b, s: (b * nb + s, blk + k))

    def full(shape):
        return pl.BlockSpec(shape, lambda b, s: (0,) * len(shape))

    rows = pl.BlockSpec((ts, W_A), lambda b, s: (b * nb + s, 0))
    return pl.pallas_call(
        _mix_ab_body,
        grid=(BATCH, nb),
        in_specs=[seg(0), seg(1), seg(2), seg(3), seg(4),
                  full((CONV_W, W_A)), full((1, W_B)), full((1, W_B)),
                  full((N_B_GROUPS, CHUNK, CHUNK)), full((CHUNK, N_B_GROUPS))],
        out_specs=[rows, rows, pl.BlockSpec((1, 8, W_A), lambda b, s: (b, 0, 0))],
        out_shape=[jax.ShapeDtypeStruct((BATCH * SEQ, W_A), BF16),
                   jax.ShapeDtypeStruct((BATCH * SEQ, W_B), BF16),
                   jax.ShapeDtypeStruct((BATCH, 8, W_A), F32)],
        scratch_shapes=[pltpu.VMEM((ts + 8, W_A), F32)],
        compiler_params=_params("arbitrary", "arbitrary"),
        name="mix_ab",
    )(proj, proj, proj, proj, proj, cw, lng, lnb, ws, bs_t)


def _attn_body(q0, q1, q2, k0, k1, k2, v0, v1, v2, y_ref,
               qc, kc, vc, s_scr, p_scr, inv_scr, o0, o1, o2, l0, l1, l2):
    row = lax.broadcasted_iota(jnp.int32, (QB, 2 * QB), 0)
    col = lax.broadcasted_iota(jnp.int32, (QB, 2 * QB), 1)
    band = jnp.logical_and(col >= row, col <= row + QB)
    band_first = jnp.logical_and(band, col >= QB)
    scale = HEAD_DIM ** -0.5
    nt = (((1,), (1,)), ((), ()))
    zero_blk = jnp.zeros((QB, HEAD_DIM), BF16)
    for (_, dil), q_ref, k_ref, v_ref, o_ref, l_ref in zip(
            DIL_GROUPS, (q0, q1, q2), (k0, k1, k2), (v0, v1, v2), (o0, o1, o2), (l0, l1, l2)):
        L = SEQ // dil
        nblk = L // QB

        def tok_rows(r, c):
            if dil == 1:
                return pl.ds(c * QB, QB)
            return pl.ds(r + c * QB * dil, QB, stride=dil)

        for r in range(dil):
            src = pl.ds(0, L) if dil == 1 else pl.ds(r, L, stride=dil)
            base = r * (L + QB)
            qc[r * L:(r + 1) * L, :] = q_ref[src, :].astype(BF16)
            kc[base:base + QB, :] = zero_blk
            vc[base:base + QB, :] = zero_blk
            kc[base + QB:base + QB + L, :] = k_ref[src, :].astype(BF16)
            vc[base + QB:base + QB + L, :] = v_ref[src, :].astype(BF16)

        for r in range(dil):
            for c in range(nblk):
                t = r * nblk + c
                kk = kc[r * (L + QB) + c * QB:r * (L + QB) + (c + 2) * QB, :]
                s = lax.dot_general(qc[t * QB:(t + 1) * QB, :], kk, nt, preferred_element_type=F32) * scale
                s_scr[t * QB:(t + 1) * QB, :] = jnp.where(band_first if c == 0 else band, s, NEG_BIG)

        for r in range(dil):
            for c in range(nblk):
                t = r * nblk + c
                s = s_scr[t * QB:(t + 1) * QB, :]
                m = jnp.max(s, axis=-1, keepdims=True)
                p = jnp.exp(s - m)
                den = jnp.sum(p, axis=-1, keepdims=True)
                p_scr[t * QB:(t + 1) * QB, :] = p.astype(BF16)
                inv_scr[t * QB:(t + 1) * QB, :] = jnp.broadcast_to(1.0 / den, (QB, HEAD_DIM))
                l_ref[tok_rows(r, c), :] = jnp.broadcast_to(m + jnp.log(den), (QB, HEAD_DIM))

        for r in range(dil):
            for c in range(nblk):
                t = r * nblk + c
                vv = vc[r * (L + QB) + c * QB:r * (L + QB) + (c + 2) * QB, :]
                o = jnp.dot(p_scr[t * QB:(t + 1) * QB, :], vv, preferred_element_type=F32)
                o_ref[tok_rows(r, c), :] = o * inv_scr[t * QB:(t + 1) * QB, :]
    y_ref[...] = _combine_groups(o0[...], o1[...], o2[...], l0[...], l1[...], l2[...]).astype(BF16)


def _attn(proj):
    def seg(off, g):
        base = (off + g * C_WIDTH) // HEAD_DIM
        return pl.BlockSpec((SEQ, HEAD_DIM), lambda b, h: (b, base + h))

    specs = [seg(off, g) for off in (OFF_Q, OFF_K, OFF_V) for g in range(N_DIL)]
    return pl.pallas_call(
        _attn_body,
        grid=(BATCH, N_C_HEADS),
        in_specs=specs,
        out_specs=pl.BlockSpec((SEQ, HEAD_DIM), lambda b, h: (b, h)),
        out_shape=jax.ShapeDtypeStruct((BATCH * SEQ, C_WIDTH), BF16),
        scratch_shapes=[pltpu.VMEM((SEQ, HEAD_DIM), BF16),
                        pltpu.VMEM((2 * SEQ, HEAD_DIM), BF16),
                        pltpu.VMEM((2 * SEQ, HEAD_DIM), BF16),
                        pltpu.VMEM((SEQ, 2 * QB), F32),
                        pltpu.VMEM((SEQ, 2 * QB), BF16),
                        pltpu.VMEM((SEQ, HEAD_DIM), F32)]
                       + [pltpu.VMEM((SEQ, HEAD_DIM), F32)] * (2 * N_DIL),
        compiler_params=_params("arbitrary", "arbitrary"),
        name="attn",
    )(*([proj] * (3 * N_DIL)))


def _kv_pack_body(*refs, fixed):
    k_refs, v_refs, out_refs = refs[0:N_DIL], refs[N_DIL:2 * N_DIL], refs[-N_DIL:]
    for k_ref, v_ref, o_ref, is_fixed in zip(k_refs, v_refs, out_refs, fixed):
        def pack(k_ref=k_ref, v_ref=v_ref, o_ref=o_ref):
            for h in range(N_C_HEADS):
                sl = slice(h * HEAD_DIM, (h + 1) * HEAD_DIM)
                o_ref[:, 0, h, :] = k_ref[:, sl]
                o_ref[:, 1, h, :] = v_ref[:, sl]

        if is_fixed:
            pl.when(pl.program_id(1) == 0)(pack)
        else:
            pack()


def _kv_pack(proj, layer, prev):
    ts = TS_KV
    keeps = [min(win, SEQ) for win, _ in DIL_GROUPS]

    def src(off, g):
        keep, col = keeps[g], (off + g * C_WIDTH) // C_WIDTH
        if keep <= ts:
            nb = SEQ // keep
            return pl.BlockSpec((keep, C_WIDTH), lambda b, s: (b * nb + nb - 1, col))
        nb, first = SEQ // ts, (SEQ - keep) // ts
        return pl.BlockSpec((ts, C_WIDTH), lambda b, s: (b * nb + jnp.maximum(s, first), col))

    def dst(keep):
        if keep <= ts:
            return pl.BlockSpec((None, None, keep, 2, N_C_HEADS, HEAD_DIM), lambda b, s: (layer, b, 0, 0, 0, 0))
        first = (SEQ - keep) // ts
        return pl.BlockSpec((None, None, ts, 2, N_C_HEADS, HEAD_DIM),
                            lambda b, s: (layer, b, jnp.maximum(s - first, 0), 0, 0, 0))

    in_specs = [src(OFF_K, g) for g in range(N_DIL)] + [src(OFF_V, g) for g in range(N_DIL)]
    args = [proj] * (2 * N_DIL)
    aliases = {}
    if prev is not None:
        in_specs += [pl.BlockSpec(memory_space=pl.ANY)] * N_DIL
        aliases = {2 * N_DIL + g: g for g in range(N_DIL)}
        args += list(prev)
    return pl.pallas_call(
        functools.partial(_kv_pack_body, fixed=tuple(keep <= ts for keep in keeps)),
        grid=(BATCH, SEQ // ts),
        in_specs=in_specs,
        out_specs=[dst(keep) for keep in keeps],
        out_shape=[jax.ShapeDtypeStruct((DEPTH, BATCH, keep, 2, N_C_HEADS, HEAD_DIM), F32) for keep in keeps],
        input_output_aliases=aliases,
        compiler_params=_params("arbitrary", "arbitrary"),
        name="kv_pack",
    )(*args)


def _mix_sample_body(p_ref, qkv_ref, st_ref, cw_ref, lng_ref, lnb_ref, wsc_ref, bsc_ref, c0_ref, c1_ref, c2_ref,
                     ya_ref, yb_ref, yc_ref, cs_ref, vn_ref):
    @pl.when(pl.program_id(0) == 0)
    def _():
        def seg(off, w):
            return p_ref[:, off:off + w]

        z = seg(OFF_AB + W_A, W_A) * seg(OFF_AB + 2 * W_A, W_A)
        st0 = st_ref[0, :, 0:W_A]
        st1 = st_ref[0, :, W_A:2 * W_A]
        conv = cw_ref[0:1, :] * st0 + cw_ref[1:2, :] * st1 + cw_ref[2:3, :] * z
        ya_ref[...] = (seg(OFF_AB, W_A) * conv).astype(BF16)
        cs_ref[:, 0:W_A] = st1
        cs_ref[:, W_A:2 * W_A] = z

        v = seg(OFF_AB + 3 * W_A + W_B, W_B)
        xc = v - jnp.mean(v, axis=-1, keepdims=True)
        var = jnp.mean(xc * xc, axis=-1, keepdims=True)
        vn = xc * lax.rsqrt(var + EPS) * lng_ref[...] + lnb_ref[...]
        vn_ref[...] = vn
        yb_ref[...] = (seg(OFF_AB + 3 * W_A, W_B) * (wsc_ref[...] * vn + bsc_ref[...])).astype(BF16)

    scale = HEAD_DIM ** -0.5
    outs, lses = [], []
    for g, c_ref in enumerate((c0_ref, c1_ref, c2_ref)):
        q = qkv_ref[g]
        kn = qkv_ref[N_DIL + g]
        vnew = qkv_ref[2 * N_DIL + g]
        kc = c_ref[:, 0]
        vc = c_ref[:, 1]
        s_c = jnp.sum(q[None] * kc, axis=-1, keepdims=True) * scale
        s_n = jnp.sum(q * kn, axis=-1, keepdims=True) * scale
        m = jnp.maximum(jnp.max(s_c, axis=0), s_n)
        p_c = jnp.exp(s_c - m[None])
        p_n = jnp.exp(s_n - m)
        den = jnp.sum(p_c, axis=0) + p_n
        outs.append((jnp.sum(p_c * vc, axis=0) + p_n * vnew) / den)
        lses.append(m + jnp.log(den))
    yc_ref[...] = _combine_groups(*outs, *lses)


def _mix_sample(proj_s, qkv_s, st, cw, lng, lnb, wsc, bsc, caches, layer):
    def full(shape):
        return pl.BlockSpec(shape, lambda b: (0,) * len(shape))

    n_keys = DIL_GROUPS[0][0] // DIL_GROUPS[0][1]
    cache_specs = [pl.BlockSpec((None, None, n_keys, None, 2, N_C_HEADS, HEAD_DIM),
                                lambda b: (layer, b, 0, 0, 0, 0, 0)) for _ in caches]
    qkv = qkv_s.reshape(DEC_BATCH, 3 * N_DIL, N_C_HEADS, HEAD_DIM)
    return pl.pallas_call(
        _mix_sample_body,
        grid=(DEC_BATCH,),
        in_specs=[full((DEC_BATCH, MAIN_W)),
                  pl.BlockSpec((None, 3 * N_DIL, N_C_HEADS, HEAD_DIM), lambda b: (b, 0, 0, 0)),
                  pl.BlockSpec((1, DEC_BATCH, 2 * W_A), lambda b: (layer, 0, 0)),
                  full((CONV_W, W_A)), full((1, W_B)), full((1, W_B)), full((1, W_B)), full((1, W_B))]
                 + cache_specs,
        out_specs=[full((DEC_BATCH, W_A)), full((DEC_BATCH, W_B)),
                   pl.BlockSpec((None, N_C_HEADS, HEAD_DIM), lambda b: (b, 0, 0)),
                   full((DEC_BATCH, 2 * W_A)), full((DEC_BATCH, W_B))],
        out_shape=[jax.ShapeDtypeStruct((DEC_BATCH, W_A), BF16),
                   jax.ShapeDtypeStruct((DEC_BATCH, W_B), BF16),
                   jax.ShapeDtypeStruct((DEC_BATCH, N_C_HEADS, HEAD_DIM), F32),
                   jax.ShapeDtypeStruct((DEC_BATCH, 2 * W_A), F32),
                   jax.ShapeDtypeStruct((DEC_BATCH, W_B), F32)],
        compiler_params=_params("arbitrary"),
        name="mix_sample",
    )(proj_s, qkv, st, cw, lng, lnb, wsc, bsc, *caches)


def _combine_groups(o0, o1, o2, l0, l1, l2):
    m = jnp.maximum(jnp.maximum(l0, l1), l2)
    e0, e1, e2 = jnp.exp(l0 - m), jnp.exp(l1 - m), jnp.exp(l2 - m)
    return (e0 * o0 + e1 * o1 + e2 * o2) / (e0 + e1 + e2)


def _merge_tail(x, ga, gb, gc, ya, yb, yc, wa_ref, wb_ref, wc_ref, wo_ref, gp_ref):
    mm = (jax.nn.sigmoid(ga) * jnp.dot(ya, wa_ref[...], preferred_element_type=F32)
          + jax.nn.sigmoid(gb) * jnp.dot(yb, wb_ref[...], preferred_element_type=F32)
          + jax.nn.sigmoid(gc) * jnp.dot(yc, wc_ref[...], preferred_element_type=F32))
    r = jnp.dot(mm.astype(BF16), wo_ref[...], preferred_element_type=F32)
    return x + _rms(r, gp_ref[...])


def _merge_body(ga_ref, gb_ref, gc_ref, ya_ref, yb_ref, yc_ref, x_ref,
                ms_ref, yas_ref, ybs_ref, ycs_ref, xs_ref,
                wa_ref, wb_ref, wc_ref, wo_ref, gp_ref, out_ref, outs_ref):
    weights = (wa_ref, wb_ref, wc_ref, wo_ref, gp_ref)
    out_ref[...] = _merge_tail(x_ref[...], ga_ref[...].astype(F32), gb_ref[...].astype(F32),
                               gc_ref[...].astype(F32), ya_ref[...], yb_ref[...], yc_ref[...], *weights)

    @pl.when(pl.program_id(0) == 0)
    def _():
        gates = [ms_ref[:, k * D_MODEL:(k + 1) * D_MODEL] for k in range(3)]
        outs_ref[...] = _merge_tail(xs_ref[...], *gates, yas_ref[...], ybs_ref[...],
                                    ycs_ref[...].astype(BF16), *weights)


def _merge(proj, ya, yb, yc, x, main_s, ya_s, yb_s, yc_s, xs, wa, wb, wc, wo, gp, layer, tm):
    m = x.shape[0]

    def rows(w, k=0):
        return pl.BlockSpec((tm, w), lambda i: (i, k))

    def whole(a):
        return pl.BlockSpec(a.shape, lambda i: (0, 0))

    def resident(k):
        return pl.BlockSpec((None, k, D_MODEL), lambda i: (layer, 0, 0), pipeline_mode=pl.Buffered(1))

    return pl.pallas_call(
        _merge_body,
        grid=(m // tm,),
        in_specs=[rows(D_MODEL, 0), rows(D_MODEL, 1), rows(D_MODEL, 2), rows(W_A), rows(W_B), rows(C_WIDTH),
                  rows(D_MODEL),
                  whole(main_s), whole(ya_s), whole(yb_s), whole(yc_s), whole(xs),
                  resident(W_A), resident(W_B), resident(C_WIDTH), resident(D_MODEL), resident(1)],
        out_specs=[rows(D_MODEL), whole(xs)],
        out_shape=[jax.ShapeDtypeStruct((m, D_MODEL), F32), jax.ShapeDtypeStruct(xs.shape, F32)],
        compiler_params=_params("arbitrary"),
        name="merge",
    )(proj, proj, proj, ya, yb, yc, x, main_s, ya_s, yb_s, yc_s, xs, wa, wb, wc, wo, gp)


def _swiglu_part(h, wg_ref, wu_ref, wo_ref):
    gate = jnp.dot(h, wg_ref[...], preferred_element_type=F32)
    up = jnp.dot(h, wu_ref[...], preferred_element_type=F32)
    act = (gate * jax.nn.sigmoid(gate) * up).astype(BF16)
    return jnp.dot(act, wo_ref[...], preferred_element_type=F32)


def _ffn_body(x_ref, g1_ref, wg_ref, wu_ref, wo_ref, g2_ref, xs_ref, *rest, n_cast):
    cast_src = rest[:n_cast]
    out_ref, outs_ref = rest[n_cast:n_cast + 2]
    cast_dst = rest[n_cast + 2:2 * n_cast + 2]
    h_ref, hs_ref = rest[-2:]
    i = pl.program_id(0)
    j = pl.program_id(1)
    last = pl.num_programs(1) - 1

    @pl.when(j == 0)
    def _():
        h_ref[...] = _rms(x_ref[...], g1_ref[...]).astype(BF16)
        out_ref[...] = jnp.zeros_like(out_ref)

    _cast_blocks(cast_src, cast_dst)
    out_ref[...] += _swiglu_part(h_ref[...], wg_ref, wu_ref, wo_ref)

    @pl.when(j == last)
    def _():
        out_ref[...] = x_ref[...] + _rms(out_ref[...], g2_ref[...])

    @pl.when(i == 0)
    def _():
        @pl.when(j == 0)
        def _():
            hs_ref[...] = _rms(xs_ref[...], g1_ref[...]).astype(BF16)
            outs_ref[...] = jnp.zeros_like(outs_ref)

        outs_ref[...] += _swiglu_part(hs_ref[...], wg_ref, wu_ref, wo_ref)

        @pl.when(j == last)
        def _():
            outs_ref[...] = xs_ref[...] + _rms(outs_ref[...], g2_ref[...])


def _ffn(x, xs, g1, w_in, w_out, g2, layer, tm, tf, casts=()):
    m = x.shape[0]
    ns = xs.shape[0]
    nj = D_FF // tf
    cast_specs = [_cast_specs(src, layer + 1, rows, nj, (m // tm) * nj) for src, rows in casts]
    res = pl.pallas_call(
        functools.partial(_ffn_body, n_cast=len(casts)),
        grid=(m // tm, nj),
        in_specs=[
            pl.BlockSpec((tm, D_MODEL), lambda i, j: (i, 0)),
            pl.BlockSpec((None, 1, D_MODEL), lambda i, j: (layer, 0, 0)),
            pl.BlockSpec((D_MODEL, tf), lambda i, j: (0, j)),
            pl.BlockSpec((D_MODEL, tf), lambda i, j: (0, nj + j)),
            pl.BlockSpec((tf, D_MODEL), lambda i, j: (j, 0)),
            pl.BlockSpec((None, 1, D_MODEL), lambda i, j: (layer, 0, 0)),
            pl.BlockSpec((ns, D_MODEL), lambda i, j: (0, 0)),
        ] + [c[0] for c in cast_specs],
        out_specs=[pl.BlockSpec((tm, D_MODEL), lambda i, j: (i, 0)),
                   pl.BlockSpec((ns, D_MODEL), lambda i, j: (0, 0))] + [c[1] for c in cast_specs],
        out_shape=[jax.ShapeDtypeStruct((m, D_MODEL), F32),
                   jax.ShapeDtypeStruct((ns, D_MODEL), F32)] + [c[2] for c in cast_specs],
        scratch_shapes=[pltpu.VMEM((tm, D_MODEL), BF16), pltpu.VMEM((ns, D_MODEL), BF16)],
        compiler_params=_params("arbitrary", "arbitrary"),
        name="ffn",
    )(x, g1, w_in, w_in, w_out, g2, xs, *[src for src, _ in casts])
    return res[0], res[1], res[2:]


TM_INPROJ = 1024
TS_MIX = 512
TM_MERGE = 256
TM_FFN = 512
TF_FFN = 512
TS_KV = 512
CAST_ROWS_FFN_IN = 16
CAST_ROWS_FFN_OUT = 64
CAST_ROWS_W_IN = 16


def kernel(x_prompt, x_sample, state_conv, cache_kv_w128, cache_kv_w512, cache_kv_w2048, g_pre_mix, w_in, conv_w, ln_g, ln_b, w_s, b_s, w_a_out, w_b_out, w_c_out, w_o, g_post_mix, g_pre_ffn, w_ffn_in, w_ffn_out, g_post_ffn):
    wa_b, wb_b, wc_b, wo_b = (w.astype(BF16) for w in (w_a_out, w_b_out, w_c_out, w_o))
    w_in_l = w_in[0].astype(BF16)
    g_pre_mix, g_post_mix, g_pre_ffn, g_post_ffn = (
        g.reshape(DEPTH, 1, D_MODEL) for g in (g_pre_mix, g_post_mix, g_pre_ffn, g_post_ffn))

    tabs_p = _rope_tables(jnp.arange(SEQ, dtype=jnp.int32))
    tabs_s = _rope_tables(jnp.full((DEC_BATCH,), PAST_LEN, dtype=jnp.int32))

    n_keys = DIL_GROUPS[0][0] // DIL_GROUPS[0][1]
    caches = tuple(c.reshape(DEPTH, DEC_BATCH, n_keys, dil, 2, N_C_HEADS, HEAD_DIM)
                   for c, (_, dil) in zip((cache_kv_w128, cache_kv_w512, cache_kv_w2048), DIL_GROUPS))
    st_all = state_conv.reshape(DEPTH, DEC_BATCH, (CONV_W - 1) * W_A)
    bs_t = jnp.swapaxes(b_s, 1, 2)
    wsc = jnp.repeat(w_s[:, :, 0, 0], B_GROUP_W, axis=1)
    bsc = jnp.repeat(b_s[:, :, 0], B_GROUP_W, axis=1)

    xp = x_prompt.reshape(BATCH * SEQ, D_MODEL)
    xs = x_sample.reshape(DEC_BATCH, D_MODEL)
    conv_p, conv_s, vchunk_s = [], [], []
    kv_p = None
    kv_s = [[] for _ in DIL_GROUPS]

    def row(a, l):
        return a[l][None, :]

    for l in range(DEPTH):
        proj, qkv, proj_s, qkv_s, wfi_l, wfo_l = _inproj(
            xp, xs, g_pre_mix, w_in_l, l, tabs_p, tabs_s, TM_INPROJ,
            casts=((w_ffn_in, CAST_ROWS_FFN_IN), (w_ffn_out, CAST_ROWS_FFN_OUT)))
        ya, yb, ctail = _mix_ab(proj, conv_w[l], row(ln_g, l), row(ln_b, l), w_s[l], bs_t[l], TS_MIX)
        yc = _attn(qkv)
        kv_p = _kv_pack(qkv, l, kv_p)
        conv_p.append(ctail[:, 8 - (CONV_W - 1):])
        ya_s, yb_s, yc_s, cst, vn = _mix_sample(proj_s, qkv_s, st_all, conv_w[l], row(ln_g, l), row(ln_b, l),
                                                row(wsc, l), row(bsc, l), caches, l)
        for g in range(N_DIL):
            k = qkv_s[:, OFF_K + g * C_WIDTH:OFF_K + (g + 1) * C_WIDTH]
            v = qkv_s[:, OFF_V + g * C_WIDTH:OFF_V + (g + 1) * C_WIDTH]
            kv_s[g].append(jnp.stack([k.reshape(DEC_BATCH, 1, N_C_HEADS, HEAD_DIM),
                                      v.reshape(DEC_BATCH, 1, N_C_HEADS, HEAD_DIM)], axis=2))
        conv_s.append(cst.reshape(DEC_BATCH, CONV_W - 1, W_A))
        vchunk_s.append(vn.reshape(DEC_BATCH, 1, W_B))
        xp, xs = _merge(proj, ya, yb, yc, xp, proj_s, ya_s, yb_s, yc_s.reshape(DEC_BATCH, C_WIDTH), xs,
                        wa_b, wb_b, wc_b, wo_b, g_post_mix, l, TM_MERGE)
        next_casts = ((w_in, CAST_ROWS_W_IN),) if l + 1 < DEPTH else ()
        xp, xs, w_in_next = _ffn(xp, xs, g_pre_ffn, wfi_l, wfo_l, g_post_ffn, l, TM_FFN, TF_FFN, casts=next_casts)
        if w_in_next:
            w_in_l = w_in_next[0]

    return (xp.reshape(BATCH, SEQ, D_MODEL), xs.reshape(DEC_BATCH, 1, D_MODEL),
            jnp.stack(conv_p, axis=0),
            kv_p[0], kv_p[1], kv_p[2],
            jnp.stack(conv_s, axis=0),
            jnp.stack(kv_s[0], axis=0), jnp.stack(kv_s[1], axis=0), jnp.stack(kv_s[2], axis=0),
            jnp.stack(vchunk_s, axis=0))
```

```python
import functools

import jax
import jax.numpy as jnp
from jax import lax
from jax.experimental import pallas as pl
from jax.experimental.pallas import tpu as pltpu

D_MODEL = 2048
BATCH = 4
SEQ = 2048
DEPTH = 4
DEC_BATCH = 8
PAST_LEN = 16384
W_A = 1024
CONV_W = 3
W_B = 1024
CHUNK = 128
N_B_GROUPS = 4
B_GROUP_W = W_B // N_B_GROUPS
N_C_HEADS = 4
HEAD_DIM = 128
ROT_DIM = HEAD_DIM // 4
ROPE_THETA = 500000.0
DIL_GROUPS = ((128, 1), (512, 4), (2048, 16))
N_DIL = len(DIL_GROUPS)
QB = 128
C_WIDTH = N_C_HEADS * HEAD_DIM
QKV_W = N_DIL * C_WIDTH
D_FF = ((-(-8 * D_MODEL // 3) + 255) // 256) * 256
IN_WIDTH = 3 * W_A + 2 * W_B + 3 * QKV_W + 3 * D_MODEL
EPS = 1e-6

GATE_W = 3 * D_MODEL
OFF_AB = GATE_W
MAIN_W = GATE_W + 3 * W_A + 2 * W_B
OFF_Q, OFF_K, OFF_V = 0, QKV_W, 2 * QKV_W
ORIG_GATE_OFF = IN_WIDTH - GATE_W

ROPE_ROWS = 256
SOFTMAX_ROWS = 32
MERGE_ROWS = 64
V7X_VMEM_LIMIT = 60 * 1024 * 1024
NEG_BIG = -1e30

F32 = jnp.float32
BF16 = jnp.bfloat16


def _params(*sem):
    return pltpu.CompilerParams(dimension_semantics=sem, vmem_limit_bytes=V7X_VMEM_LIMIT)


def _rms(x, g):
    return x * lax.rsqrt(jnp.mean(x * x, axis=-1, keepdims=True) + EPS) * g


def _cast_specs(src, layer, rows, steps_per_row_tile, n_steps):
    _, r, c = src.shape
    nblk = r // rows
    assert r % rows == 0 and nblk <= n_steps

    def blk(i, j):
        return jnp.minimum(i * steps_per_row_tile + j, nblk - 1)

    return (pl.BlockSpec((None, rows, c), lambda i, j: (layer, blk(i, j), 0)),
            pl.BlockSpec((rows, c), lambda i, j: (blk(i, j), 0)),
            jax.ShapeDtypeStruct((r, c), BF16))


def _cast_blocks(src_refs, dst_refs):
    for s_ref, d_ref in zip(src_refs, dst_refs):
        d_ref[...] = s_ref[...].astype(BF16)


def _rope(a, c, s1, s2):
    return a * c + pltpu.roll(a, HEAD_DIM - ROT_DIM // 2, 1) * s1 + pltpu.roll(a, ROT_DIM // 2, 1) * s2


def _inproj_body(x_ref, g_ref, w0_ref, w1_ref, c_ref, s1_ref, s2_ref, xs_ref, cs_ref, s1s_ref, s2s_ref, *rest,
                 main_steps, rope_steps, n_cast):
    cast_src = rest[:n_cast]
    main_ref, qkv_ref, mains_ref, qkvs_ref = rest[n_cast:n_cast + 4]
    cast_dst = rest[n_cast + 4:2 * n_cast + 4]
    h_ref, hs_ref = rest[-2:]
    i = pl.program_id(0)
    j = pl.program_id(1)
    tn = w0_ref.shape[1]
    heads = tn // HEAD_DIM
    halves = ((w0_ref, slice(0, tn)), (w1_ref, slice(tn, 2 * tn)))
    with_samples = i == 0

    @pl.when(j == 0)
    def _():
        h_ref[...] = _rms(x_ref[...], g_ref[...]).astype(BF16)

    @pl.when(jnp.logical_and(with_samples, j == 0))
    def _():
        hs_ref[...] = _rms(xs_ref[...], g_ref[...]).astype(BF16)

    @pl.when(j < main_steps)
    def _():
        _cast_blocks(cast_src, cast_dst)
        for w_ref, cols in halves:
            main_ref[:, cols] = jnp.dot(h_ref[...], w_ref[...], preferred_element_type=F32).astype(main_ref.dtype)

        @pl.when(with_samples)
        def _():
            for w_ref, cols in halves:
                mains_ref[j, :, cols] = jnp.dot(hs_ref[...], w_ref[...], preferred_element_type=F32)

    @pl.when(jnp.logical_and(j >= main_steps, j < main_steps + rope_steps))
    def _():
        _cast_blocks(cast_src, cast_dst)
        tm = x_ref.shape[0]
        rc = min(tm, ROPE_ROWS)
        for w_ref, cols in halves:
            for r in range(tm // rc):
                rs = slice(r * rc, (r + 1) * rc)
                acc = jnp.dot(h_ref[rs, :], w_ref[...], preferred_element_type=F32)
                for h in range(heads):
                    sl = slice(cols.start + h * HEAD_DIM, cols.start + (h + 1) * HEAD_DIM)
                    qkv_ref[rs, sl] = _rope(acc[:, h * HEAD_DIM:(h + 1) * HEAD_DIM],
                                            c_ref[rs, :], s1_ref[rs, :], s2_ref[rs, :])

        @pl.when(with_samples)
        def _():
            for w_ref, cols in halves:
                acc = jnp.dot(hs_ref[...], w_ref[...], preferred_element_type=F32)
                for h in range(heads):
                    sl = slice(cols.start + h * HEAD_DIM, cols.start + (h + 1) * HEAD_DIM)
                    qkvs_ref[j - main_steps, :, sl] = _rope(acc[:, h * HEAD_DIM:(h + 1) * HEAD_DIM],
                                                            cs_ref[...], s1s_ref[...], s2s_ref[...])

    @pl.when(j >= main_steps + rope_steps)
    def _():
        _cast_blocks(cast_src, cast_dst)
        for w_ref, cols in halves:
            qkv_ref[:, cols] = jnp.dot(h_ref[...], w_ref[...], preferred_element_type=F32)

        @pl.when(with_samples)
        def _():
            for w_ref, cols in halves:
                qkvs_ref[j - main_steps, :, cols] = jnp.dot(hs_ref[...], w_ref[...], preferred_element_type=F32)


def _inproj(x, xs, g, w, layer, tabs, tabs_s, tm, casts=()):
    m = x.shape[0]
    ns = xs.shape[0]
    tn = C_WIDTH
    n_tiles = IN_WIDTH // tn
    rot = ORIG_GATE_OFF // tn
    main_steps = MAIN_W // (2 * tn)
    rope_steps = 2 * QKV_W // (2 * tn)
    qkv_steps = pl.cdiv(3 * QKV_W, 2 * tn)
    steps = main_steps + qkv_steps
    tab_spec = pl.BlockSpec((tm, HEAD_DIM), lambda i, j: (i % (SEQ // tm), 0))
    tab_s_spec = pl.BlockSpec((ns, HEAD_DIM), lambda i, j: (0, 0))

    def w_spec(half):
        return pl.BlockSpec((D_MODEL, tn), lambda i, j: (0, (2 * j + half + rot) % n_tiles))

    cast_specs = [_cast_specs(src, layer, rows, steps, (m // tm) * steps) for src, rows in casts]
    res = pl.pallas_call(
        functools.partial(_inproj_body, main_steps=main_steps, rope_steps=rope_steps, n_cast=len(casts)),
        grid=(m // tm, steps),
        in_specs=[
            pl.BlockSpec((tm, D_MODEL), lambda i, j: (i, 0)),
            pl.BlockSpec((None, 1, D_MODEL), lambda i, j: (layer, 0, 0)),
            w_spec(0), w_spec(1),
            tab_spec, tab_spec, tab_spec,
            pl.BlockSpec((ns, D_MODEL), lambda i, j: (0, 0)),
            tab_s_spec, tab_s_spec, tab_s_spec,
        ] + [c[0] for c in cast_specs],
        out_specs=[pl.BlockSpec((tm, 2 * tn), lambda i, j: (i, jnp.minimum(j, main_steps - 1))),
                   pl.BlockSpec((tm, 2 * tn), lambda i, j: (i, jnp.maximum(j - main_steps, 0))),
                   pl.BlockSpec((main_steps, ns, 2 * tn), lambda i, j: (0, 0, 0)),
                   pl.BlockSpec((qkv_steps, ns, 2 * tn), lambda i, j: (0, 0, 0))]
                  + [c[1] for c in cast_specs],
        out_shape=[jax.ShapeDtypeStruct((m, MAIN_W), BF16),
                   jax.ShapeDtypeStruct((m, 3 * QKV_W), F32),
                   jax.ShapeDtypeStruct((main_steps, ns, 2 * tn), F32),
                   jax.ShapeDtypeStruct((qkv_steps, ns, 2 * tn), F32)] + [c[2] for c in cast_specs],
        scratch_shapes=[pltpu.VMEM((tm, D_MODEL), BF16), pltpu.VMEM((ns, D_MODEL), BF16)],
        compiler_params=_params("arbitrary", "arbitrary"),
        name="inproj",
    )(x, g, w, w, *tabs, xs, *tabs_s, *[src for src, _ in casts])
    main_s = jnp.swapaxes(res[2], 0, 1).reshape(ns, MAIN_W)
    qkv_s = jnp.swapaxes(res[3], 0, 1).reshape(ns, qkv_steps * 2 * tn)[:, :3 * QKV_W]
    return (res[0], res[1], main_s, qkv_s, *res[4:])


def _rope_tables(pos):
    inv_freq = ROPE_THETA ** (-jnp.arange(0, ROT_DIM, 2, dtype=jnp.float32) / ROT_DIM)
    ang = pos.astype(jnp.float32)[:, None] * inv_freq[None, :]
    cos, sin = jnp.cos(ang), jnp.sin(ang)
    n = pos.shape[0]
    half = ROT_DIM // 2
    rest = HEAD_DIM - ROT_DIM
    c = jnp.concatenate([cos, cos, jnp.ones((n, rest), F32)], axis=1)
    s1 = jnp.concatenate([-sin, jnp.zeros((n, half + rest), F32)], axis=1)
    s2 = jnp.concatenate([jnp.zeros((n, half), F32), sin, jnp.zeros((n, rest), F32)], axis=1)
    return c, s1, s2


def _mix_ab_body(ab_ref, ac_ref, ax_ref, bu_ref, bv_ref, cw_ref, lng_ref, lnb_ref, ws_ref, bs_ref,
                 ya_ref, yb_ref, cs_ref, zs_ref):
    ts = ab_ref.shape[0]

    @pl.when(pl.program_id(1) == 0)
    def _():
        zs_ref[0:8, :] = jnp.zeros((8, W_A), F32)

    z = ac_ref[...].astype(F32) * ax_ref[...].astype(F32)
    zs_ref[8:8 + ts, :] = z
    z1 = zs_ref[7:7 + ts, :]
    z2 = zs_ref[6:6 + ts, :]
    conv = cw_ref[0:1, :] * z2 + cw_ref[1:2, :] * z1 + cw_ref[2:3, :] * z
    ya_ref[...] = (ab_ref[...].astype(F32) * conv).astype(BF16)
    tail = zs_ref[ts:ts + 8, :]
    zs_ref[0:8, :] = tail
    cs_ref[0] = tail

    v = bv_ref[...].astype(F32)
    xc = v - jnp.mean(v, axis=-1, keepdims=True)
    var = jnp.mean(xc * xc, axis=-1, keepdims=True)
    vn = (xc * lax.rsqrt(var + EPS) * lng_ref[...] + lnb_ref[...]).astype(BF16)
    row = lax.broadcasted_iota(jnp.int32, (CHUNK, CHUNK), 0)
    col = lax.broadcasted_iota(jnp.int32, (CHUNK, CHUNK), 1)
    for g in range(N_B_GROUPS):
        wg = jnp.where(row >= col, ws_ref[g], 0.0).astype(BF16)
        bcol = bs_ref[:, g:g + 1]
        gs = slice(g * B_GROUP_W, (g + 1) * B_GROUP_W)
        for c in range(ts // CHUNK):
            rs = slice(c * CHUNK, (c + 1) * CHUNK)
            sg = jnp.dot(wg, vn[rs, gs], preferred_element_type=F32) + bcol
            yb_ref[rs, gs] = (bu_ref[rs, gs].astype(F32) * sg).astype(BF16)


def _mix_ab(proj, cw, lng, lnb, ws, bs_t, ts):
    nb = SEQ // ts
    blk = OFF_AB // W_A

    def seg(k):
        return pl.BlockSpec((ts, W_A), lambda b, s: (b * nb + s, blk + k))

    def full(shape):
        return pl.BlockSpec(shape, lambda b, s: (0,) * len(shape))

    rows = pl.BlockSpec((ts, W_A), lambda b, s: (b * nb + s, 0))
    return pl.pallas_call(
        _mix_ab_body,
        grid=(BATCH, nb),
        in_specs=[seg(0), seg(1), seg(2), seg(3), seg(4),
                  full((CONV_W, W_A)), full((1, W_B)), full((1, W_B)),
                  full((N_B_GROUPS, CHUNK, CHUNK)), full((CHUNK, N_B_GROUPS))],
        out_specs=[rows, rows, pl.BlockSpec((1, 8, W_A), lambda b, s: (b, 0, 0))],
        out_shape=[jax.ShapeDtypeStruct((BATCH * SEQ, W_A), BF16),
                   jax.ShapeDtypeStruct((BATCH * SEQ, W_B), BF16),
                   jax.ShapeDtypeStruct((BATCH, 8, W_A), F32)],
        scratch_shapes=[pltpu.VMEM((ts + 8, W_A), F32)],
        compiler_params=_params("arbitrary", "arbitrary"),
        name="mix_ab",
    )(proj, proj, proj, proj, proj, cw, lng, lnb, ws, bs_t)


def _attn_body(q0, q1, q2, k0, k1, k2, v0, v1, v2, y_ref,
               qc, kc, vc, s_scr, p_scr, inv_scr, o0, o1, o2, l0, l1, l2):
    row = lax.broadcasted_iota(jnp.int32, (QB, 2 * QB), 0)
    col = lax.broadcasted_iota(jnp.int32, (QB, 2 * QB), 1)
    band = jnp.logical_and(col >= row, col <= row + QB)
    band_first = jnp.logical_and(band, col >= QB)
    scale = HEAD_DIM ** -0.5
    nt = (((1,), (1,)), ((), ()))
    zero_blk = jnp.zeros((QB, HEAD_DIM), BF16)
    for (_, dil), q_ref, k_ref, v_ref, o_ref, l_ref in zip(
            DIL_GROUPS, (q0, q1, q2), (k0, k1, k2), (v0, v1, v2), (o0, o1, o2), (l0, l1, l2)):
        L = SEQ // dil
        nblk = L // QB

        def tok_rows(r, c):
            if dil == 1:
                return pl.ds(c * QB, QB)
            return pl.ds(r + c * QB * dil, QB, stride=dil)

        for r in range(dil):
            src = pl.ds(0, L) if dil == 1 else pl.ds(r, L, stride=dil)
            base = r * (L + QB)
            qc[r * L:(r + 1) * L, :] = q_ref[src, :].astype(BF16)
            kc[base:base + QB, :] = zero_blk
            vc[base:base + QB, :] = zero_blk
            kc[base + QB:base + QB + L, :] = k_ref[src, :].astype(BF16)
            vc[base + QB:base + QB + L, :] = v_ref[src, :].astype(BF16)

        for r in range(dil):
            for c in range(nblk):
                t = r * nblk + c
                kk = kc[r * (L + QB) + c * QB:r * (L + QB) + (c + 2) * QB, :]
                s = lax.dot_general(qc[t * QB:(t + 1) * QB, :], kk, nt, preferred_element_type=F32) * scale
                s_scr[t * QB:(t + 1) * QB, :] = jnp.where(band_first if c == 0 else band, s, NEG_BIG)

        for r in range(dil):
            for u in range(L // SOFTMAX_ROWS):
                rows = slice(r * L + u * SOFTMAX_ROWS, r * L + (u + 1) * SOFTMAX_ROWS)
                s = s_scr[rows, :]
                m = jnp.max(s, axis=-1, keepdims=True)
                p = jnp.exp(s - m)
                den = jnp.sum(p, axis=-1, keepdims=True)
                p_scr[rows, :] = p.astype(BF16)
                inv_scr[rows, :] = jnp.broadcast_to(1.0 / den, (SOFTMAX_ROWS, HEAD_DIM))
                tok = (pl.ds(u * SOFTMAX_ROWS, SOFTMAX_ROWS) if dil == 1
                       else pl.ds(r + u * SOFTMAX_ROWS * dil, SOFTMAX_ROWS, stride=dil))
                l_ref[tok, :] = jnp.broadcast_to(m + jnp.log(den), (SOFTMAX_ROWS, HEAD_DIM))

        for r in range(dil):
            for c in range(nblk):
                t = r * nblk + c
                vv = vc[r * (L + QB) + c * QB:r * (L + QB) + (c + 2) * QB, :]
                o = jnp.dot(p_scr[t * QB:(t + 1) * QB, :], vv, preferred_element_type=F32)
                o_ref[tok_rows(r, c), :] = o * inv_scr[t * QB:(t + 1) * QB, :]
    for u in range(SEQ // MERGE_ROWS):
        rows = slice(u * MERGE_ROWS, (u + 1) * MERGE_ROWS)
        y_ref[rows, :] = _combine_groups(o0[rows, :], o1[rows, :], o2[rows, :],
                                         l0[rows, :], l1[rows, :], l2[rows, :]).astype(BF16)


def _attn(proj):
    def seg(off, g):
        base = (off + g * C_WIDTH) // HEAD_DIM
        return pl.BlockSpec((SEQ, HEAD_DIM), lambda b, h: (b, base + h))

    specs = [seg(off, g) for off in (OFF_Q, OFF_K, OFF_V) for g in range(N_DIL)]
    return pl.pallas_call(
        _attn_body,
        grid=(BATCH, N_C_HEADS),
        in_specs=specs,
        out_specs=pl.BlockSpec((SEQ, HEAD_DIM), lambda b, h: (b, h)),
        out_shape=jax.ShapeDtypeStruct((BATCH * SEQ, C_WIDTH), BF16),
        scratch_shapes=[pltpu.VMEM((SEQ, HEAD_DIM), BF16),
                        pltpu.VMEM((2 * SEQ, HEAD_DIM), BF16),
                        pltpu.VMEM((2 * SEQ, HEAD_DIM), BF16),
                        pltpu.VMEM((SEQ, 2 * QB), F32),
                        pltpu.VMEM((SEQ, 2 * QB), BF16),
                        pltpu.VMEM((SEQ, HEAD_DIM), F32)]
                       + [pltpu.VMEM((SEQ, HEAD_DIM), F32)] * (2 * N_DIL),
        compiler_params=_params("arbitrary", "arbitrary"),
        name="attn",
    )(*([proj] * (3 * N_DIL)))


def _kv_pack_body(*refs, fixed):
    k_refs, v_refs, out_refs = refs[0:N_DIL], refs[N_DIL:2 * N_DIL], refs[-N_DIL:]
    for k_ref, v_ref, o_ref, is_fixed in zip(k_refs, v_refs, out_refs, fixed):
        def pack(k_ref=k_ref, v_ref=v_ref, o_ref=o_ref):
            for h in range(N_C_HEADS):
                sl = slice(h * HEAD_DIM, (h + 1) * HEAD_DIM)
                o_ref[:, 0, h, :] = k_ref[:, sl]
                o_ref[:, 1, h, :] = v_ref[:, sl]

        if is_fixed:
            pl.when(pl.program_id(1) == 0)(pack)
        else:
            pack()


def _kv_pack(proj, layer, prev):
    ts = TS_KV
    keeps = [min(win, SEQ) for win, _ in DIL_GROUPS]

    def src(off, g):
        keep, col = keeps[g], (off + g * C_WIDTH) // C_WIDTH
        if keep <= ts:
            nb = SEQ // keep
            return pl.BlockSpec((keep, C_WIDTH), lambda b, s: (b * nb + nb - 1, col))
        nb, first = SEQ // ts, (SEQ - keep) // ts
        return pl.BlockSpec((ts, C_WIDTH), lambda b, s: (b * nb + jnp.maximum(s, first), col))

    def dst(keep):
        if keep <= ts:
            return pl.BlockSpec((None, None, keep, 2, N_C_HEADS, HEAD_DIM), lambda b, s: (layer, b, 0, 0, 0, 0))
        first = (SEQ - keep) // ts
        return pl.BlockSpec((None, None, ts, 2, N_C_HEADS, HEAD_DIM),
                            lambda b, s: (layer, b, jnp.maximum(s - first, 0), 0, 0, 0))

    in_specs = [src(OFF_K, g) for g in range(N_DIL)] + [src(OFF_V, g) for g in range(N_DIL)]
    args = [proj] * (2 * N_DIL)
    aliases = {}
    if prev is not None:
        in_specs += [pl.BlockSpec(memory_space=pl.ANY)] * N_DIL
        aliases = {2 * N_DIL + g: g for g in range(N_DIL)}
        args += list(prev)
    return pl.pallas_call(
        functools.partial(_kv_pack_body, fixed=tuple(keep <= ts for keep in keeps)),
        grid=(BATCH, SEQ // ts),
        in_specs=in_specs,
        out_specs=[dst(keep) for keep in keeps],
        out_shape=[jax.ShapeDtypeStruct((DEPTH, BATCH, keep, 2, N_C_HEADS, HEAD_DIM), F32) for keep in keeps],
        input_output_aliases=aliases,
        compiler_params=_params("arbitrary", "arbitrary"),
        name="kv_pack",
    )(*args)


def _mix_sample_body(p_ref, qkv_ref, st_ref, cw_ref, lng_ref, lnb_ref, wsc_ref, bsc_ref, c0_ref, c1_ref, c2_ref,
                     ya_ref, yb_ref, yc_ref, cs_ref, vn_ref):
    @pl.when(pl.program_id(0) == 0)
    def _():
        def seg(off, w):
            return p_ref[:, off:off + w]

        z = seg(OFF_AB + W_A, W_A) * seg(OFF_AB + 2 * W_A, W_A)
        st0 = st_ref[0, :, 0:W_A]
        st1 = st_ref[0, :, W_A:2 * W_A]
        conv = cw_ref[0:1, :] * st0 + cw_ref[1:2, :] * st1 + cw_ref[2:3, :] * z
        ya_ref[...] = (seg(OFF_AB, W_A) * conv).astype(BF16)
        cs_ref[:, 0:W_A] = st1
        cs_ref[:, W_A:2 * W_A] = z

        v = seg(OFF_AB + 3 * W_A + W_B, W_B)
        xc = v - jnp.mean(v, axis=-1, keepdims=True)
        var = jnp.mean(xc * xc, axis=-1, keepdims=True)
        vn = xc * lax.rsqrt(var + EPS) * lng_ref[...] + lnb_ref[...]
        vn_ref[...] = vn
        yb_ref[...] = (seg(OFF_AB + 3 * W_A, W_B) * (wsc_ref[...] * vn + bsc_ref[...])).astype(BF16)

    scale = HEAD_DIM ** -0.5
    outs, lses = [], []
    for g, c_ref in enumerate((c0_ref, c1_ref, c2_ref)):
        q = qkv_ref[g]
        kn = qkv_ref[N_DIL + g]
        vnew = qkv_ref[2 * N_DIL + g]
        kc = c_ref[:, 0]
        vc = c_ref[:, 1]
        s_c = jnp.sum(q[None] * kc, axis=-1, keepdims=True) * scale
        s_n = jnp.sum(q * kn, axis=-1, keepdims=True) * scale
        m = jnp.maximum(jnp.max(s_c, axis=0), s_n)
        p_c = jnp.exp(s_c - m[None])
        p_n = jnp.exp(s_n - m)
        den = jnp.sum(p_c, axis=0) + p_n
        outs.append((jnp.sum(p_c * vc, axis=0) + p_n * vnew) / den)
        lses.append(m + jnp.log(den))
    yc_ref[...] = _combine_groups(*outs, *lses)


def _mix_sample(proj_s, qkv_s, st, cw, lng, lnb, wsc, bsc, caches, layer):
    def full(shape):
        return pl.BlockSpec(shape, lambda b: (0,) * len(shape))

    n_keys = DIL_GROUPS[0][0] // DIL_GROUPS[0][1]
    cache_specs = [pl.BlockSpec((None, None, n_keys, None, 2, N_C_HEADS, HEAD_DIM),
                                lambda b: (layer, b, 0, 0, 0, 0, 0)) for _ in caches]
    qkv = qkv_s.reshape(DEC_BATCH, 3 * N_DIL, N_C_HEADS, HEAD_DIM)
    return pl.pallas_call(
        _mix_sample_body,
        grid=(DEC_BATCH,),
        in_specs=[full((DEC_BATCH, MAIN_W)),
                  pl.BlockSpec((None, 3 * N_DIL, N_C_HEADS, HEAD_DIM), lambda b: (b, 0, 0, 0)),
                  pl.BlockSpec((1, DEC_BATCH, 2 * W_A), lambda b: (layer, 0, 0)),
                  full((CONV_W, W_A)), full((1, W_B)), full((1, W_B)), full((1, W_B)), full((1, W_B))]
                 + cache_specs,
        out_specs=[full((DEC_BATCH, W_A)), full((DEC_BATCH, W_B)),
                   pl.BlockSpec((None, N_C_HEADS, HEAD_DIM), lambda b: (b, 0, 0)),
                   full((DEC_BATCH, 2 * W_A)), full((DEC_BATCH, W_B))],
        out_shape=[jax.ShapeDtypeStruct((DEC_BATCH, W_A), BF16),
                   jax.ShapeDtypeStruct((DEC_BATCH, W_B), BF16),
                   jax.ShapeDtypeStruct((DEC_BATCH, N_C_HEADS, HEAD_DIM), F32),
                   jax.ShapeDtypeStruct((DEC_BATCH, 2 * W_A), F32),
                   jax.ShapeDtypeStruct((DEC_BATCH, W_B), F32)],
        compiler_params=_params("arbitrary"),
        name="mix_sample",
    )(proj_s, qkv, st, cw, lng, lnb, wsc, bsc, *caches)


def _combine_groups(o0, o1, o2, l0, l1, l2):
    m = jnp.maximum(jnp.maximum(l0, l1), l2)
    e0, e1, e2 = jnp.exp(l0 - m), jnp.exp(l1 - m), jnp.exp(l2 - m)
    return (e0 * o0 + e1 * o1 + e2 * o2) / (e0 + e1 + e2)


def _merge_tail(x, ga, gb, gc, ya, yb, yc, wa_ref, wb_ref, wc_ref, wo_ref, gp_ref):
    mm = (jax.nn.sigmoid(ga) * jnp.dot(ya, wa_ref[...], preferred_element_type=F32)
          + jax.nn.sigmoid(gb) * jnp.dot(yb, wb_ref[...], preferred_element_type=F32)
          + jax.nn.sigmoid(gc) * jnp.dot(yc, wc_ref[...], preferred_element_type=F32))
    r = jnp.dot(mm.astype(BF16), wo_ref[...], preferred_element_type=F32)
    return x + _rms(r, gp_ref[...])


def _merge_body(ga_ref, gb_ref, gc_ref, ya_ref, yb_ref, yc_ref, x_ref,
                ms_ref, yas_ref, ybs_ref, ycs_ref, xs_ref,
                wa_ref, wb_ref, wc_ref, wo_ref, gp_ref, out_ref, outs_ref):
    weights = (wa_ref, wb_ref, wc_ref, wo_ref, gp_ref)
    out_ref[...] = _merge_tail(x_ref[...], ga_ref[...].astype(F32), gb_ref[...].astype(F32),
                               gc_ref[...].astype(F32), ya_ref[...], yb_ref[...], yc_ref[...], *weights)

    @pl.when(pl.program_id(0) == 0)
    def _():
        gates = [ms_ref[:, k * D_MODEL:(k + 1) * D_MODEL] for k in range(3)]
        outs_ref[...] = _merge_tail(xs_ref[...], *gates, yas_ref[...], ybs_ref[...],
                                    ycs_ref[...].astype(BF16), *weights)


def _merge(proj, ya, yb, yc, x, main_s, ya_s, yb_s, yc_s, xs, wa, wb, wc, wo, gp, layer, tm):
    m = x.shape[0]

    def rows(w, k=0):
        return pl.BlockSpec((tm, w), lambda i: (i, k))

    def whole(a):
        return pl.BlockSpec(a.shape, lambda i: (0, 0))

    def resident(k):
        return pl.BlockSpec((k, D_MODEL), lambda i: (0, 0), pipeline_mode=pl.Buffered(1))

    return pl.pallas_call(
        _merge_body,
        grid=(m // tm,),
        in_specs=[rows(D_MODEL, 0), rows(D_MODEL, 1), rows(D_MODEL, 2), rows(W_A), rows(W_B), rows(C_WIDTH),
                  rows(D_MODEL),
                  whole(main_s), whole(ya_s), whole(yb_s), whole(yc_s), whole(xs),
                  resident(W_A), resident(W_B), resident(C_WIDTH), resident(D_MODEL),
                  pl.BlockSpec((None, 1, D_MODEL), lambda i: (layer, 0, 0), pipeline_mode=pl.Buffered(1))],
        out_specs=[rows(D_MODEL), whole(xs)],
        out_shape=[jax.ShapeDtypeStruct((m, D_MODEL), F32), jax.ShapeDtypeStruct(xs.shape, F32)],
        compiler_params=_params("arbitrary"),
        name="merge",
    )(proj, proj, proj, ya, yb, yc, x, main_s, ya_s, yb_s, yc_s, xs, wa, wb, wc, wo, gp)


def _swiglu_part(h, wg_ref, wu_ref, wo_ref):
    gate = jnp.dot(h, wg_ref[...], preferred_element_type=F32)
    up = jnp.dot(h, wu_ref[...], preferred_element_type=F32)
    act = (gate * jax.nn.sigmoid(gate) * up).astype(BF16)
    return jnp.dot(act, wo_ref[...], preferred_element_type=F32)


def _ffn_body(x_ref, g1_ref, wg_ref, wu_ref, wo_ref, g2_ref, xs_ref, *rest, n_cast):
    cast_src = rest[:n_cast]
    out_ref, outs_ref = rest[n_cast:n_cast + 2]
    cast_dst = rest[n_cast + 2:2 * n_cast + 2]
    h_ref, hs_ref = rest[-2:]
    i = pl.program_id(0)
    j = pl.program_id(1)
    last = pl.num_programs(1) - 1

    @pl.when(j == 0)
    def _():
        h_ref[...] = _rms(x_ref[...], g1_ref[...]).astype(BF16)
        out_ref[...] = jnp.zeros_like(out_ref)

    _cast_blocks(cast_src, cast_dst)
    out_ref[...] += _swiglu_part(h_ref[...], wg_ref, wu_ref, wo_ref)

    @pl.when(j == last)
    def _():
        out_ref[...] = x_ref[...] + _rms(out_ref[...], g2_ref[...])

    @pl.when(i == 0)
    def _():
        @pl.when(j == 0)
        def _():
            hs_ref[...] = _rms(xs_ref[...], g1_ref[...]).astype(BF16)
            outs_ref[...] = jnp.zeros_like(outs_ref)

        outs_ref[...] += _swiglu_part(hs_ref[...], wg_ref, wu_ref, wo_ref)

        @pl.when(j == last)
        def _():
            outs_ref[...] = xs_ref[...] + _rms(outs_ref[...], g2_ref[...])


def _ffn(x, xs, g1, w_in, w_out, g2, layer, tm, tf, casts=()):
    m = x.shape[0]
    ns = xs.shape[0]
    nj = D_FF // tf
    cast_specs = [_cast_specs(src, layer + 1, rows, nj, (m // tm) * nj) for src, rows in casts]
    res = pl.pallas_call(
        functools.partial(_ffn_body, n_cast=len(casts)),
        grid=(m // tm, nj),
        in_specs=[
            pl.BlockSpec((tm, D_MODEL), lambda i, j: (i, 0)),
            pl.BlockSpec((None, 1, D_MODEL), lambda i, j: (layer, 0, 0)),
            pl.BlockSpec((D_MODEL, tf), lambda i, j: (0, j)),
            pl.BlockSpec((D_MODEL, tf), lambda i, j: (0, nj + j)),
            pl.BlockSpec((tf, D_MODEL), lambda i, j: (j, 0)),
            pl.BlockSpec((None, 1, D_MODEL), lambda i, j: (layer, 0, 0)),
            pl.BlockSpec((ns, D_MODEL), lambda i, j: (0, 0)),
        ] + [c[0] for c in cast_specs],
        out_specs=[pl.BlockSpec((tm, D_MODEL), lambda i, j: (i, 0)),
                   pl.BlockSpec((ns, D_MODEL), lambda i, j: (0, 0))] + [c[1] for c in cast_specs],
        out_shape=[jax.ShapeDtypeStruct((m, D_MODEL), F32),
                   jax.ShapeDtypeStruct((ns, D_MODEL), F32)] + [c[2] for c in cast_specs],
        scratch_shapes=[pltpu.VMEM((tm, D_MODEL), BF16), pltpu.VMEM((ns, D_MODEL), BF16)],
        compiler_params=_params("arbitrary", "arbitrary"),
        name="ffn",
    )(x, g1, w_in, w_in, w_out, g2, xs, *[src for src, _ in casts])
    return res[0], res[1], res[2:]


TM_INPROJ = 1024
TS_MIX = 512
TM_MERGE = 256
TM_FFN = 512
TF_FFN = 512
TS_KV = 512
CAST_ROWS_FFN_IN = 16
CAST_ROWS_FFN_OUT = 64
CAST_ROWS_W_IN = 16
CAST_ROWS_MERGE = 16


def kernel(x_prompt, x_sample, state_conv, cache_kv_w128, cache_kv_w512, cache_kv_w2048, g_pre_mix, w_in, conv_w, ln_g, ln_b, w_s, b_s, w_a_out, w_b_out, w_c_out, w_o, g_post_mix, g_pre_ffn, w_ffn_in, w_ffn_out, g_post_ffn):
    w_in_l = w_in[0].astype(BF16)
    inproj_casts = ((w_ffn_in, CAST_ROWS_FFN_IN), (w_ffn_out, CAST_ROWS_FFN_OUT),
                    (w_a_out, CAST_ROWS_MERGE), (w_b_out, CAST_ROWS_MERGE), (w_c_out, CAST_ROWS_MERGE),
                    (w_o, CAST_ROWS_MERGE))
    g_pre_mix, g_post_mix, g_pre_ffn, g_post_ffn = (
        g.reshape(DEPTH, 1, D_MODEL) for g in (g_pre_mix, g_post_mix, g_pre_ffn, g_post_ffn))

    tabs_p = _rope_tables(jnp.arange(SEQ, dtype=jnp.int32))
    tabs_s = _rope_tables(jnp.full((DEC_BATCH,), PAST_LEN, dtype=jnp.int32))

    n_keys = DIL_GROUPS[0][0] // DIL_GROUPS[0][1]
    caches = tuple(c.reshape(DEPTH, DEC_BATCH, n_keys, dil, 2, N_C_HEADS, HEAD_DIM)
                   for c, (_, dil) in zip((cache_kv_w128, cache_kv_w512, cache_kv_w2048), DIL_GROUPS))
    st_all = state_conv.reshape(DEPTH, DEC_BATCH, (CONV_W - 1) * W_A)
    bs_t = jnp.swapaxes(b_s, 1, 2)
    wsc = jnp.repeat(w_s[:, :, 0, 0], B_GROUP_W, axis=1)
    bsc = jnp.repeat(b_s[:, :, 0], B_GROUP_W, axis=1)

    xp = x_prompt.reshape(BATCH * SEQ, D_MODEL)
    xs = x_sample.reshape(DEC_BATCH, D_MODEL)
    conv_p, conv_s, vchunk_s = [], [], []
    kv_p = None
    kv_s = [[] for _ in DIL_GROUPS]

    def row(a, l):
        return a[l][None, :]

    for l in range(DEPTH):
        proj, qkv, proj_s, qkv_s, wfi_l, wfo_l, wa_l, wb_l, wc_l, wo_l = _inproj(
            xp, xs, g_pre_mix, w_in_l, l, tabs_p, tabs_s, TM_INPROJ, casts=inproj_casts)
        ya, yb, ctail = _mix_ab(proj, conv_w[l], row(ln_g, l), row(ln_b, l), w_s[l], bs_t[l], TS_MIX)
        yc = _attn(qkv)
        kv_p = _kv_pack(qkv, l, kv_p)
        conv_p.append(ctail[:, 8 - (CONV_W - 1):])
        ya_s, yb_s, yc_s, cst, vn = _mix_sample(proj_s, qkv_s, st_all, conv_w[l], row(ln_g, l), row(ln_b, l),
                                                row(wsc, l), row(bsc, l), caches, l)
        for g in range(N_DIL):
            k = qkv_s[:, OFF_K + g * C_WIDTH:OFF_K + (g + 1) * C_WIDTH]
            v = qkv_s[:, OFF_V + g * C_WIDTH:OFF_V + (g + 1) * C_WIDTH]
            kv_s[g].append(jnp.stack([k.reshape(DEC_BATCH, 1, N_C_HEADS, HEAD_DIM),
                                      v.reshape(DEC_BATCH, 1, N_C_HEADS, HEAD_DIM)], axis=2))
        conv_s.append(cst.reshape(DEC_BATCH, CONV_W - 1, W_A))
        vchunk_s.append(vn.reshape(DEC_BATCH, 1, W_B))
        xp, xs = _merge(proj, ya, yb, yc, xp, proj_s, ya_s, yb_s, yc_s.reshape(DEC_BATCH, C_WIDTH), xs,
                        wa_l, wb_l, wc_l, wo_l, g_post_mix, l, TM_MERGE)
        next_casts = ((w_in, CAST_ROWS_W_IN),) if l + 1 < DEPTH else ()
        xp, xs, w_in_next = _ffn(xp, xs, g_pre_ffn, wfi_l, wfo_l, g_post_ffn, l, TM_FFN, TF_FFN, casts=next_casts)
        if w_in_next:
            w_in_l = w_in_next[0]

    return (xp.reshape(BATCH, SEQ, D_MODEL), xs.reshape(DEC_BATCH, 1, D_MODEL),
            jnp.stack(conv_p, axis=0),
            kv_p[0], kv_p[1], kv_p[2],
            jnp.stack(conv_s, axis=0),
            jnp.stack(kv_s[0], axis=0), jnp.stack(kv_s[1], axis=0), jnp.stack(kv_s[2], axis=0),
            jnp.stack(vchunk_s, axis=0))
```

```python
import functools

import jax
import jax.numpy as jnp
from jax import lax
from jax.experimental import pallas as pl
from jax.experimental.pallas import tpu as pltpu

D_MODEL = 2048
BATCH = 4
SEQ = 2048
DEPTH = 4
DEC_BATCH = 8
PAST_LEN = 16384
W_A = 1024
CONV_W = 3
W_B = 1024
CHUNK = 128
N_B_GROUPS = 4
B_GROUP_W = W_B // N_B_GROUPS
N_C_HEADS = 4
HEAD_DIM = 128
ROT_DIM = HEAD_DIM // 4
ROPE_THETA = 500000.0
DIL_GROUPS = ((128, 1), (512, 4), (2048, 16))
N_DIL = len(DIL_GROUPS)
QB = 128
C_WIDTH = N_C_HEADS * HEAD_DIM
QKV_W = N_DIL * C_WIDTH
D_FF = ((-(-8 * D_MODEL // 3) + 255) // 256) * 256
IN_WIDTH = 3 * W_A + 2 * W_B + 3 * QKV_W + 3 * D_MODEL
EPS = 1e-6

GATE_W = 3 * D_MODEL
OFF_AB = GATE_W
MAIN_W = GATE_W + 3 * W_A + 2 * W_B
OFF_Q, OFF_K, OFF_V = 0, QKV_W, 2 * QKV_W
ORIG_GATE_OFF = IN_WIDTH - GATE_W

ROPE_ROWS = 256
SOFTMAX_ROWS = 32
MERGE_ROWS = 64
V7X_VMEM_LIMIT = 60 * 1024 * 1024
NEG_BIG = -1e30

F32 = jnp.float32
BF16 = jnp.bfloat16


def _params(*sem):
    return pltpu.CompilerParams(dimension_semantics=sem, vmem_limit_bytes=V7X_VMEM_LIMIT)


def _rms(x, g):
    return x * lax.rsqrt(jnp.mean(x * x, axis=-1, keepdims=True) + EPS) * g


def _cast_specs(src, layer, rows, steps_per_row_tile, n_steps):
    _, r, c = src.shape
    nblk = r // rows
    assert r % rows == 0 and nblk <= n_steps

    def blk(i, j):
        return jnp.minimum(i * steps_per_row_tile + j, nblk - 1)

    return (pl.BlockSpec((None, rows, c), lambda i, j: (layer, blk(i, j), 0)),
            pl.BlockSpec((rows, c), lambda i, j: (blk(i, j), 0)),
            jax.ShapeDtypeStruct((r, c), BF16))


def _cast_blocks(src_refs, dst_refs):
    for s_ref, d_ref in zip(src_refs, dst_refs):
        d_ref[...] = s_ref[...].astype(BF16)


def _rope(a, c, s1, s2):
    return a * c + pltpu.roll(a, HEAD_DIM - ROT_DIM // 2, 1) * s1 + pltpu.roll(a, ROT_DIM // 2, 1) * s2


def _inproj_body(x_ref, g_ref, w0_ref, w1_ref, c_ref, s1_ref, s2_ref, xs_ref, cs_ref, s1s_ref, s2s_ref, *rest,
                 main_steps, rope_steps, n_cast):
    cast_src = rest[:n_cast]
    main_ref, qkv_ref, mains_ref, qkvs_ref = rest[n_cast:n_cast + 4]
    cast_dst = rest[n_cast + 4:2 * n_cast + 4]
    h_ref, hs_ref = rest[-2:]
    i = pl.program_id(0)
    j = pl.program_id(1)
    tn = w0_ref.shape[1]
    heads = tn // HEAD_DIM
    halves = ((w0_ref, slice(0, tn)), (w1_ref, slice(tn, 2 * tn)))
    with_samples = i == 0

    @pl.when(j == 0)
    def _():
        h_ref[...] = _rms(x_ref[...], g_ref[...]).astype(BF16)

    @pl.when(jnp.logical_and(with_samples, j == 0))
    def _():
        hs_ref[...] = _rms(xs_ref[...], g_ref[...]).astype(BF16)

    @pl.when(j < main_steps)
    def _():
        _cast_blocks(cast_src, cast_dst)
        for w_ref, cols in halves:
            main_ref[:, cols] = jnp.dot(h_ref[...], w_ref[...], preferred_element_type=F32).astype(main_ref.dtype)

        @pl.when(with_samples)
        def _():
            for w_ref, cols in halves:
                mains_ref[j, :, cols] = jnp.dot(hs_ref[...], w_ref[...], preferred_element_type=F32)

    @pl.when(jnp.logical_and(j >= main_steps, j < main_steps + rope_steps))
    def _():
        _cast_blocks(cast_src, cast_dst)
        tm = x_ref.shape[0]
        rc = min(tm, ROPE_ROWS)
        for w_ref, cols in halves:
            for r in range(tm // rc):
                rs = slice(r * rc, (r + 1) * rc)
                acc = jnp.dot(h_ref[rs, :], w_ref[...], preferred_element_type=F32)
                for h in range(heads):
                    sl = slice(cols.start + h * HEAD_DIM, cols.start + (h + 1) * HEAD_DIM)
                    qkv_ref[rs, sl] = _rope(acc[:, h * HEAD_DIM:(h + 1) * HEAD_DIM],
                                            c_ref[rs, :], s1_ref[rs, :], s2_ref[rs, :])

        @pl.when(with_samples)
        def _():
            for w_ref, cols in halves:
                acc = jnp.dot(hs_ref[...], w_ref[...], preferred_element_type=F32)
                for h in range(heads):
                    sl = slice(cols.start + h * HEAD_DIM, cols.start + (h + 1) * HEAD_DIM)
                    qkvs_ref[j - main_steps, :, sl] = _rope(acc[:, h * HEAD_DIM:(h + 1) * HEAD_DIM],
                                                            cs_ref[...], s1s_ref[...], s2s_ref[...])

    @pl.when(j >= main_steps + rope_steps)
    def _():
        _cast_blocks(cast_src, cast_dst)
        for w_ref, cols in halves:
            qkv_ref[:, cols] = jnp.dot(h_ref[...], w_ref[...], preferred_element_type=F32)

        @pl.when(with_samples)
        def _():
            for w_ref, cols in halves:
                qkvs_ref[j - main_steps, :, cols] = jnp.dot(hs_ref[...], w_ref[...], preferred_element_type=F32)


def _inproj(x, xs, g, w, layer, tabs, tabs_s, tm, casts=()):
    m = x.shape[0]
    ns = xs.shape[0]
    tn = C_WIDTH
    n_tiles = IN_WIDTH // tn
    rot = ORIG_GATE_OFF // tn
    main_steps = MAIN_W // (2 * tn)
    rope_steps = 2 * QKV_W // (2 * tn)
    qkv_steps = pl.cdiv(3 * QKV_W, 2 * tn)
    steps = main_steps + qkv_steps
    tab_spec = pl.BlockSpec((tm, HEAD_DIM), lambda i, j: (i % (SEQ // tm), 0))
    tab_s_spec = pl.BlockSpec((ns, HEAD_DIM), lambda i, j: (0, 0))

    def w_spec(half):
        return pl.BlockSpec((D_MODEL, tn), lambda i, j: (0, (2 * j + half + rot) % n_tiles))

    cast_specs = [_cast_specs(src, layer, rows, steps, (m // tm) * steps) for src, rows in casts]
    res = pl.pallas_call(
        functools.partial(_inproj_body, main_steps=main_steps, rope_steps=rope_steps, n_cast=len(casts)),
        grid=(m // tm, steps),
        in_specs=[
            pl.BlockSpec((tm, D_MODEL), lambda i, j: (i, 0)),
            pl.BlockSpec((None, 1, D_MODEL), lambda i, j: (layer, 0, 0)),
            w_spec(0), w_spec(1),
            tab_spec, tab_spec, tab_spec,
            pl.BlockSpec((ns, D_MODEL), lambda i, j: (0, 0)),
            tab_s_spec, tab_s_spec, tab_s_spec,
        ] + [c[0] for c in cast_specs],
        out_specs=[pl.BlockSpec((tm, 2 * tn), lambda i, j: (i, jnp.minimum(j, main_steps - 1))),
                   pl.BlockSpec((tm, 2 * tn), lambda i, j: (i, jnp.maximum(j - main_steps, 0))),
                   pl.BlockSpec((main_steps, ns, 2 * tn), lambda i, j: (0, 0, 0)),
                   pl.BlockSpec((qkv_steps, ns, 2 * tn), lambda i, j: (0, 0, 0))]
                  + [c[1] for c in cast_specs],
        out_shape=[jax.ShapeDtypeStruct((m, MAIN_W), BF16),
                   jax.ShapeDtypeStruct((m, 3 * QKV_W), F32),
                   jax.ShapeDtypeStruct((main_steps, ns, 2 * tn), F32),
                   jax.ShapeDtypeStruct((qkv_steps, ns, 2 * tn), F32)] + [c[2] for c in cast_specs],
        scratch_shapes=[pltpu.VMEM((tm, D_MODEL), BF16), pltpu.VMEM((ns, D_MODEL), BF16)],
        compiler_params=_params("arbitrary", "arbitrary"),
        name="inproj",
    )(x, g, w, w, *tabs, xs, *tabs_s, *[src for src, _ in casts])
    main_s = jnp.swapaxes(res[2], 0, 1).reshape(ns, MAIN_W)
    qkv_s = jnp.swapaxes(res[3], 0, 1).reshape(ns, qkv_steps * 2 * tn)[:, :3 * QKV_W]
    return (res[0], res[1], main_s, qkv_s, *res[4:])


def _rope_tables(pos):
    inv_freq = ROPE_THETA ** (-jnp.arange(0, ROT_DIM, 2, dtype=jnp.float32) / ROT_DIM)
    ang = pos.astype(jnp.float32)[:, None] * inv_freq[None, :]
    cos, sin = jnp.cos(ang), jnp.sin(ang)
    n = pos.shape[0]
    half = ROT_DIM // 2
    rest = HEAD_DIM - ROT_DIM
    c = jnp.concatenate([cos, cos, jnp.ones((n, rest), F32)], axis=1)
    s1 = jnp.concatenate([-sin, jnp.zeros((n, half + rest), F32)], axis=1)
    s2 = jnp.concatenate([jnp.zeros((n, half), F32), sin, jnp.zeros((n, rest), F32)], axis=1)
    return c, s1, s2


def _mix_ab_body(ab_ref, ac_ref, ax_ref, bu_ref, bv_ref, cw_ref, lng_ref, lnb_ref, ws_ref, bs_ref,
                 ya_ref, yb_ref, cs_ref, zs_ref):
    ts = ab_ref.shape[0]

    @pl.when(pl.program_id(1) == 0)
    def _():
        zs_ref[0:8, :] = jnp.zeros((8, W_A), F32)

    z = ac_ref[...].astype(F32) * ax_ref[...].astype(F32)
    zs_ref[8:8 + ts, :] = z
    z1 = zs_ref[7:7 + ts, :]
    z2 = zs_ref[6:6 + ts, :]
    conv = cw_ref[0:1, :] * z2 + cw_ref[1:2, :] * z1 + cw_ref[2:3, :] * z
    ya_ref[...] = (ab_ref[...].astype(F32) * conv).astype(BF16)
    tail = zs_ref[ts:ts + 8, :]
    zs_ref[0:8, :] = tail
    cs_ref[0] = tail

    v = bv_ref[...].astype(F32)
    xc = v - jnp.mean(v, axis=-1, keepdims=True)
    var = jnp.mean(xc * xc, axis=-1, keepdims=True)
    vn = (xc * lax.rsqrt(var + EPS) * lng_ref[...] + lnb_ref[...]).astype(BF16)
    row = lax.broadcasted_iota(jnp.int32, (CHUNK, CHUNK), 0)
    col = lax.broadcasted_iota(jnp.int32, (CHUNK, CHUNK), 1)
    for g in range(N_B_GROUPS):
        wg = jnp.where(row >= col, ws_ref[g], 0.0).astype(BF16)
        bcol = bs_ref[:, g:g + 1]
        gs = slice(g * B_GROUP_W, (g + 1) * B_GROUP_W)
        for c in range(ts // CHUNK):
            rs = slice(c * CHUNK, (c + 1) * CHUNK)
            sg = jnp.dot(wg, vn[rs, gs], preferred_element_type=F32) + bcol
            yb_ref[rs, gs] = (bu_ref[rs, gs].astype(F32) * sg).astype(BF16)


def _attn_body(q0, q1, q2, k0, k1, k2, v0, v1, v2, y_ref,
               qc, kc, vc, s_scr, p_scr, inv_scr, o0, o1, o2, l0, l1, l2):
    row = lax.broadcasted_iota(jnp.int32, (QB, 2 * QB), 0)
    col = lax.broadcasted_iota(jnp.int32, (QB, 2 * QB), 1)
    band = jnp.logical_and(col >= row, col <= row + QB)
    band_first = jnp.logical_and(band, col >= QB)
    scale = HEAD_DIM ** -0.5
    nt = (((1,), (1,)), ((), ()))
    zero_blk = jnp.zeros((QB, HEAD_DIM), BF16)
    for (_, dil), q_ref, k_ref, v_ref, o_ref, l_ref in zip(
            DIL_GROUPS, (q0, q1, q2), (k0, k1, k2), (v0, v1, v2), (o0, o1, o2), (l0, l1, l2)):
        L = SEQ // dil
        nblk = L // QB

        def tok_rows(r, c):
            if dil == 1:
                return pl.ds(c * QB, QB)
            return pl.ds(r + c * QB * dil, QB, stride=dil)

        for r in range(dil):
            src = pl.ds(0, L) if dil == 1 else pl.ds(r, L, stride=dil)
            base = r * (L + QB)
            qc[r * L:(r + 1) * L, :] = q_ref[src, :].astype(BF16)
            kc[base:base + QB, :] = zero_blk
            vc[base:base + QB, :] = zero_blk
            kc[base + QB:base + QB + L, :] = k_ref[src, :].astype(BF16)
            vc[base + QB:base + QB + L, :] = v_ref[src, :].astype(BF16)

        for r in range(dil):
            for c in range(nblk):
                t = r * nblk + c
                kk = kc[r * (L + QB) + c * QB:r * (L + QB) + (c + 2) * QB, :]
                s = lax.dot_general(qc[t * QB:(t + 1) * QB, :], kk, nt, preferred_element_type=F32) * scale
                s_scr[t * QB:(t + 1) * QB, :] = jnp.where(band_first if c == 0 else band, s, NEG_BIG)

        for r in range(dil):
            for u in range(L // SOFTMAX_ROWS):
                rows = slice(r * L + u * SOFTMAX_ROWS, r * L + (u + 1) * SOFTMAX_ROWS)
                s = s_scr[rows, :]
                m = jnp.max(s, axis=-1, keepdims=True)
                p = jnp.exp(s - m)
                den = jnp.sum(p, axis=-1, keepdims=True)
                p_scr[rows, :] = p.astype(BF16)
                inv_scr[rows, :] = jnp.broadcast_to(1.0 / den, (SOFTMAX_ROWS, HEAD_DIM))
                tok = (pl.ds(u * SOFTMAX_ROWS, SOFTMAX_ROWS) if dil == 1
                       else pl.ds(r + u * SOFTMAX_ROWS * dil, SOFTMAX_ROWS, stride=dil))
                l_ref[tok, :] = jnp.broadcast_to(m + jnp.log(den), (SOFTMAX_ROWS, HEAD_DIM))

        for r in range(dil):
            for c in range(nblk):
                t = r * nblk + c
                vv = vc[r * (L + QB) + c * QB:r * (L + QB) + (c + 2) * QB, :]
                o = jnp.dot(p_scr[t * QB:(t + 1) * QB, :], vv, preferred_element_type=F32)
                o_ref[tok_rows(r, c), :] = o * inv_scr[t * QB:(t + 1) * QB, :]
    for u in range(SEQ // MERGE_ROWS):
        rows = slice(u * MERGE_ROWS, (u + 1) * MERGE_ROWS)
        y_ref[rows, :] = _combine_groups(o0[rows, :], o1[rows, :], o2[rows, :],
                                         l0[rows, :], l1[rows, :], l2[rows, :]).astype(BF16)


def _attn(proj):
    def seg(off, g):
        base = (off + g * C_WIDTH) // HEAD_DIM
        return pl.BlockSpec((SEQ, HEAD_DIM), lambda b, h: (b, base + h))

    specs = [seg(off, g) for off in (OFF_Q, OFF_K, OFF_V) for g in range(N_DIL)]
    return pl.pallas_call(
        _attn_body,
        grid=(BATCH, N_C_HEADS),
        in_specs=specs,
        out_specs=pl.BlockSpec((SEQ, HEAD_DIM), lambda b, h: (b, h)),
        out_shape=jax.ShapeDtypeStruct((BATCH * SEQ, C_WIDTH), BF16),
        scratch_shapes=[pltpu.VMEM((SEQ, HEAD_DIM), BF16),
                        pltpu.VMEM((2 * SEQ, HEAD_DIM), BF16),
                        pltpu.VMEM((2 * SEQ, HEAD_DIM), BF16),
                        pltpu.VMEM((SEQ, 2 * QB), F32),
                        pltpu.VMEM((SEQ, 2 * QB), BF16),
                        pltpu.VMEM((SEQ, HEAD_DIM), F32)]
                       + [pltpu.VMEM((SEQ, HEAD_DIM), F32)] * (2 * N_DIL),
        compiler_params=_params("arbitrary", "arbitrary"),
        name="attn",
    )(*([proj] * (3 * N_DIL)))


def _kv_pack_body(*refs, fixed):
    k_refs, v_refs, out_refs = refs[0:N_DIL], refs[N_DIL:2 * N_DIL], refs[-N_DIL:]
    for k_ref, v_ref, o_ref, is_fixed in zip(k_refs, v_refs, out_refs, fixed):
        def pack(k_ref=k_ref, v_ref=v_ref, o_ref=o_ref):
            for h in range(N_C_HEADS):
                sl = slice(h * HEAD_DIM, (h + 1) * HEAD_DIM)
                o_ref[:, 0, h, :] = k_ref[:, sl]
                o_ref[:, 1, h, :] = v_ref[:, sl]

        if is_fixed:
            pl.when(pl.program_id(1) == 0)(pack)
        else:
            pack()


N_MIX_IN, N_MIX_OUT = 10, 3


def _mix_ab_kv_body(*refs, fixed):
    n_kv_in = len(refs) - N_MIX_IN - N_MIX_OUT - N_DIL - 1
    mix_in, kv_in = refs[:N_MIX_IN], refs[N_MIX_IN:N_MIX_IN + 2 * N_DIL]
    outs = refs[N_MIX_IN + n_kv_in:-1]
    _mix_ab_body(*mix_in, *outs[:N_MIX_OUT], refs[-1])
    _kv_pack_body(*kv_in, *outs[N_MIX_OUT:], fixed=fixed)


def _mix_ab_kv(main, qkv, cw, lng, lnb, ws, bs_t, layer, prev):
    ts = TS_MIX
    nb = SEQ // ts
    blk = OFF_AB // W_A
    keeps = [min(win, SEQ) for win, _ in DIL_GROUPS]

    def seg(k):
        return pl.BlockSpec((ts, W_A), lambda b, s: (b * nb + s, blk + k))

    def full(shape):
        return pl.BlockSpec(shape, lambda b, s: (0,) * len(shape))

    rows = pl.BlockSpec((ts, W_A), lambda b, s: (b * nb + s, 0))
    mix_in = [seg(0), seg(1), seg(2), seg(3), seg(4),
              full((CONV_W, W_A)), full((1, W_B)), full((1, W_B)),
              full((N_B_GROUPS, CHUNK, CHUNK)), full((CHUNK, N_B_GROUPS))]
    mix_out = [rows, rows, pl.BlockSpec((1, 8, W_A), lambda b, s: (b, 0, 0))]
    mix_shapes = [jax.ShapeDtypeStruct((BATCH * SEQ, W_A), BF16),
                  jax.ShapeDtypeStruct((BATCH * SEQ, W_B), BF16),
                  jax.ShapeDtypeStruct((BATCH, 8, W_A), F32)]
    assert len(mix_in) == N_MIX_IN and len(mix_out) == N_MIX_OUT

    def src(off, g):
        keep, col = keeps[g], (off + g * C_WIDTH) // C_WIDTH
        if keep <= ts:
            nb = SEQ // keep
            return pl.BlockSpec((keep, C_WIDTH), lambda b, s: (b * nb + nb - 1, col))
        nb, first = SEQ // ts, (SEQ - keep) // ts
        return pl.BlockSpec((ts, C_WIDTH), lambda b, s: (b * nb + jnp.maximum(s, first), col))

    def dst(keep):
        if keep <= ts:
            return pl.BlockSpec((None, None, keep, 2, N_C_HEADS, HEAD_DIM), lambda b, s: (layer, b, 0, 0, 0, 0))
        first = (SEQ - keep) // ts
        return pl.BlockSpec((None, None, ts, 2, N_C_HEADS, HEAD_DIM),
                            lambda b, s: (layer, b, jnp.maximum(s - first, 0), 0, 0, 0))

    in_specs = mix_in + [src(OFF_K, g) for g in range(N_DIL)] + [src(OFF_V, g) for g in range(N_DIL)]
    args = [main] * 5 + [cw, lng, lnb, ws, bs_t] + [qkv] * (2 * N_DIL)
    aliases = {}
    if prev is not None:
        aliases = {len(in_specs) + g: N_MIX_OUT + g for g in range(N_DIL)}
        in_specs += [pl.BlockSpec(memory_space=pl.ANY)] * N_DIL
        args += list(prev)
    res = pl.pallas_call(
        functools.partial(_mix_ab_kv_body, fixed=tuple(keep <= ts for keep in keeps)),
        grid=(BATCH, nb),
        in_specs=in_specs,
        out_specs=mix_out + [dst(keep) for keep in keeps],
        out_shape=mix_shapes + [jax.ShapeDtypeStruct((DEPTH, BATCH, keep, 2, N_C_HEADS, HEAD_DIM), F32)
                                for keep in keeps],
        input_output_aliases=aliases,
        scratch_shapes=[pltpu.VMEM((ts + 8, W_A), F32)],
        compiler_params=_params("arbitrary", "arbitrary"),
        name="mix_ab_kv",
    )(*args)
    return res[0], res[1], res[2], res[N_MIX_OUT:]


def _mix_sample_body(p_ref, qkv_ref, st_ref, cw_ref, lng_ref, lnb_ref, wsc_ref, bsc_ref, c0_ref, c1_ref, c2_ref,
                     ya_ref, yb_ref, yc_ref, cs_ref, vn_ref):
    @pl.when(pl.program_id(0) == 0)
    def _():
        def seg(off, w):
            return p_ref[:, off:off + w]

        z = seg(OFF_AB + W_A, W_A) * seg(OFF_AB + 2 * W_A, W_A)
        st0 = st_ref[0, :, 0:W_A]
        st1 = st_ref[0, :, W_A:2 * W_A]
        conv = cw_ref[0:1, :] * st0 + cw_ref[1:2, :] * st1 + cw_ref[2:3, :] * z
        ya_ref[...] = (seg(OFF_AB, W_A) * conv).astype(BF16)
        cs_ref[:, 0:W_A] = st1
        cs_ref[:, W_A:2 * W_A] = z

        v = seg(OFF_AB + 3 * W_A + W_B, W_B)
        xc = v - jnp.mean(v, axis=-1, keepdims=True)
        var = jnp.mean(xc * xc, axis=-1, keepdims=True)
        vn = xc * lax.rsqrt(var + EPS) * lng_ref[...] + lnb_ref[...]
        vn_ref[...] = vn
        yb_ref[...] = (seg(OFF_AB + 3 * W_A, W_B) * (wsc_ref[...] * vn + bsc_ref[...])).astype(BF16)

    scale = HEAD_DIM ** -0.5
    outs, lses = [], []
    for g, c_ref in enumerate((c0_ref, c1_ref, c2_ref)):
        q = qkv_ref[g]
        kn = qkv_ref[N_DIL + g]
        vnew = qkv_ref[2 * N_DIL + g]
        kc = c_ref[:, 0]
        vc = c_ref[:, 1]
        s_c = jnp.sum(q[None] * kc, axis=-1, keepdims=True) * scale
        s_n = jnp.sum(q * kn, axis=-1, keepdims=True) * scale
        m = jnp.maximum(jnp.max(s_c, axis=0), s_n)
        p_c = jnp.exp(s_c - m[None])
        p_n = jnp.exp(s_n - m)
        den = jnp.sum(p_c, axis=0) + p_n
        outs.append((jnp.sum(p_c * vc, axis=0) + p_n * vnew) / den)
        lses.append(m + jnp.log(den))
    yc_ref[...] = _combine_groups(*outs, *lses)


def _mix_sample(proj_s, qkv_s, st, cw, lng, lnb, wsc, bsc, caches, layer):
    def full(shape):
        return pl.BlockSpec(shape, lambda b: (0,) * len(shape))

    n_keys = DIL_GROUPS[0][0] // DIL_GROUPS[0][1]
    cache_specs = [pl.BlockSpec((None, None, n_keys, None, 2, N_C_HEADS, HEAD_DIM),
                                lambda b: (layer, b, 0, 0, 0, 0, 0)) for _ in caches]
    qkv = qkv_s.reshape(DEC_BATCH, 3 * N_DIL, N_C_HEADS, HEAD_DIM)
    return pl.pallas_call(
        _mix_sample_body,
        grid=(DEC_BATCH,),
        in_specs=[full((DEC_BATCH, MAIN_W)),
                  pl.BlockSpec((None, 3 * N_DIL, N_C_HEADS, HEAD_DIM), lambda b: (b, 0, 0, 0)),
                  pl.BlockSpec((1, DEC_BATCH, 2 * W_A), lambda b: (layer, 0, 0)),
                  full((CONV_W, W_A)), full((1, W_B)), full((1, W_B)), full((1, W_B)), full((1, W_B))]
                 + cache_specs,
        out_specs=[full((DEC_BATCH, W_A)), full((DEC_BATCH, W_B)),
                   pl.BlockSpec((None, N_C_HEADS, HEAD_DIM), lambda b: (b, 0, 0)),
                   full((DEC_BATCH, 2 * W_A)), full((DEC_BATCH, W_B))],
        out_shape=[jax.ShapeDtypeStruct((DEC_BATCH, W_A), BF16),
                   jax.ShapeDtypeStruct((DEC_BATCH, W_B), BF16),
                   jax.ShapeDtypeStruct((DEC_BATCH, N_C_HEADS, HEAD_DIM), F32),
                   jax.ShapeDtypeStruct((DEC_BATCH, 2 * W_A), F32),
                   jax.ShapeDtypeStruct((DEC_BATCH, W_B), F32)],
        compiler_params=_params("arbitrary"),
        name="mix_sample",
    )(proj_s, qkv, st, cw, lng, lnb, wsc, bsc, *caches)


def _combine_groups(o0, o1, o2, l0, l1, l2):
    m = jnp.maximum(jnp.maximum(l0, l1), l2)
    e0, e1, e2 = jnp.exp(l0 - m), jnp.exp(l1 - m), jnp.exp(l2 - m)
    return (e0 * o0 + e1 * o1 + e2 * o2) / (e0 + e1 + e2)


def _merge_tail(x, ga, gb, gc, ya, yb, yc, wa_ref, wb_ref, wc_ref, wo_ref, gp_ref):
    mm = (jax.nn.sigmoid(ga) * jnp.dot(ya, wa_ref[...], preferred_element_type=F32)
          + jax.nn.sigmoid(gb) * jnp.dot(yb, wb_ref[...], preferred_element_type=F32)
          + jax.nn.sigmoid(gc) * jnp.dot(yc, wc_ref[...], preferred_element_type=F32))
    r = jnp.dot(mm.astype(BF16), wo_ref[...], preferred_element_type=F32)
    return x + _rms(r, gp_ref[...])


def _merge_body(ga_ref, gb_ref, gc_ref, ya_ref, yb_ref, yc_ref, x_ref,
                ms_ref, yas_ref, ybs_ref, ycs_ref, xs_ref,
                wa_ref, wb_ref, wc_ref, wo_ref, gp_ref, out_ref, outs_ref):
    weights = (wa_ref, wb_ref, wc_ref, wo_ref, gp_ref)
    out_ref[...] = _merge_tail(x_ref[...], ga_ref[...].astype(F32), gb_ref[...].astype(F32),
                               gc_ref[...].astype(F32), ya_ref[...], yb_ref[...], yc_ref[...], *weights)

    @pl.when(pl.program_id(0) == 0)
    def _():
        gates = [ms_ref[:, k * D_MODEL:(k + 1) * D_MODEL] for k in range(3)]
        outs_ref[...] = _merge_tail(xs_ref[...], *gates, yas_ref[...], ybs_ref[...],
                                    ycs_ref[...].astype(BF16), *weights)


def _merge(proj, ya, yb, yc, x, main_s, ya_s, yb_s, yc_s, xs, wa, wb, wc, wo, gp, layer, tm):
    m = x.shape[0]

    def rows(w, k=0):
        return pl.BlockSpec((tm, w), lambda i: (i, k))

    def whole(a):
        return pl.BlockSpec(a.shape, lambda i: (0, 0))

    def resident(k):
        return pl.BlockSpec((k, D_MODEL), lambda i: (0, 0), pipeline_mode=pl.Buffered(1))

    return pl.pallas_call(
        _merge_body,
        grid=(m // tm,),
        in_specs=[rows(D_MODEL, 0), rows(D_MODEL, 1), rows(D_MODEL, 2), rows(W_A), rows(W_B), rows(C_WIDTH),
                  rows(D_MODEL),
                  whole(main_s), whole(ya_s), whole(yb_s), whole(yc_s), whole(xs),
                  resident(W_A), resident(W_B), resident(C_WIDTH), resident(D_MODEL),
                  pl.BlockSpec((None, 1, D_MODEL), lambda i: (layer, 0, 0), pipeline_mode=pl.Buffered(1))],
        out_specs=[rows(D_MODEL), whole(xs)],
        out_shape=[jax.ShapeDtypeStruct((m, D_MODEL), F32), jax.ShapeDtypeStruct(xs.shape, F32)],
        compiler_params=_params("arbitrary"),
        name="merge",
    )(proj, proj, proj, ya, yb, yc, x, main_s, ya_s, yb_s, yc_s, xs, wa, wb, wc, wo, gp)


def _swiglu_part(h, wg_ref, wu_ref, wo_ref):
    gate = jnp.dot(h, wg_ref[...], preferred_element_type=F32)
    up = jnp.dot(h, wu_ref[...], preferred_element_type=F32)
    act = (gate * jax.nn.sigmoid(gate) * up).astype(BF16)
    return jnp.dot(act, wo_ref[...], preferred_element_type=F32)


def _ffn_body(x_ref, g1_ref, wg_ref, wu_ref, wo_ref, g2_ref, xs_ref, *rest, n_cast):
    cast_src = rest[:n_cast]
    out_ref, outs_ref = rest[n_cast:n_cast + 2]
    cast_dst = rest[n_cast + 2:2 * n_cast + 2]
    h_ref, hs_ref = rest[-2:]
    i = pl.program_id(0)
    j = pl.program_id(1)
    last = pl.num_programs(1) - 1

    @pl.when(j == 0)
    def _():
        h_ref[...] = _rms(x_ref[...], g1_ref[...]).astype(BF16)
        out_ref[...] = jnp.zeros_like(out_ref)

    _cast_blocks(cast_src, cast_dst)
    out_ref[...] += _swiglu_part(h_ref[...], wg_ref, wu_ref, wo_ref)

    @pl.when(j == last)
    def _():
        out_ref[...] = x_ref[...] + _rms(out_ref[...], g2_ref[...])

    @pl.when(i == 0)
    def _():
        @pl.when(j == 0)
        def _():
            hs_ref[...] = _rms(xs_ref[...], g1_ref[...]).astype(BF16)
            outs_ref[...] = jnp.zeros_like(outs_ref)

        outs_ref[...] += _swiglu_part(hs_ref[...], wg_ref, wu_ref, wo_ref)

        @pl.when(j == last)
        def _():
            outs_ref[...] = xs_ref[...] + _rms(outs_ref[...], g2_ref[...])


def _ffn(x, xs, g1, w_in, w_out, g2, layer, tm, tf, casts=()):
    m = x.shape[0]
    ns = xs.shape[0]
    nj = D_FF // tf
    cast_specs = [_cast_specs(src, layer + 1, rows, nj, (m // tm) * nj) for src, rows in casts]
    res = pl.pallas_call(
        functools.partial(_ffn_body, n_cast=len(casts)),
        grid=(m // tm, nj),
        in_specs=[
            pl.BlockSpec((tm, D_MODEL), lambda i, j: (i, 0)),
            pl.BlockSpec((None, 1, D_MODEL), lambda i, j: (layer, 0, 0)),
            pl.BlockSpec((D_MODEL, tf), lambda i, j: (0, j)),
            pl.BlockSpec((D_MODEL, tf), lambda i, j: (0, nj + j)),
            pl.BlockSpec((tf, D_MODEL), lambda i, j: (j, 0)),
            pl.BlockSpec((None, 1, D_MODEL), lambda i, j: (layer, 0, 0)),
            pl.BlockSpec((ns, D_MODEL), lambda i, j: (0, 0)),
        ] + [c[0] for c in cast_specs],
        out_specs=[pl.BlockSpec((tm, D_MODEL), lambda i, j: (i, 0)),
                   pl.BlockSpec((ns, D_MODEL), lambda i, j: (0, 0))] + [c[1] for c in cast_specs],
        out_shape=[jax.ShapeDtypeStruct((m, D_MODEL), F32),
                   jax.ShapeDtypeStruct((ns, D_MODEL), F32)] + [c[2] for c in cast_specs],
        scratch_shapes=[pltpu.VMEM((tm, D_MODEL), BF16), pltpu.VMEM((ns, D_MODEL), BF16)],
        compiler_params=_params("arbitrary", "arbitrary"),
        name="ffn",
    )(x, g1, w_in, w_in, w_out, g2, xs, *[src for src, _ in casts])
    return res[0], res[1], res[2:]


TM_INPROJ = 1024
TS_MIX = 512
TM_MERGE = 256
TM_FFN = 512
TF_FFN = 512
CAST_ROWS_FFN_IN = 16
CAST_ROWS_FFN_OUT = 64
CAST_ROWS_W_IN = 16
CAST_ROWS_MERGE = 16


def kernel(x_prompt, x_sample, state_conv, cache_kv_w128, cache_kv_w512, cache_kv_w2048, g_pre_mix, w_in, conv_w, ln_g, ln_b, w_s, b_s, w_a_out, w_b_out, w_c_out, w_o, g_post_mix, g_pre_ffn, w_ffn_in, w_ffn_out, g_post_ffn):
    w_in_l = w_in[0].astype(BF16)
    inproj_casts = ((w_ffn_in, CAST_ROWS_FFN_IN), (w_ffn_out, CAST_ROWS_FFN_OUT),
                    (w_a_out, CAST_ROWS_MERGE), (w_b_out, CAST_ROWS_MERGE), (w_c_out, CAST_ROWS_MERGE),
                    (w_o, CAST_ROWS_MERGE))
    g_pre_mix, g_post_mix, g_pre_ffn, g_post_ffn = (
        g.reshape(DEPTH, 1, D_MODEL) for g in (g_pre_mix, g_post_mix, g_pre_ffn, g_post_ffn))

    tabs_p = _rope_tables(jnp.arange(SEQ, dtype=jnp.int32))
    tabs_s = _rope_tables(jnp.full((DEC_BATCH,), PAST_LEN, dtype=jnp.int32))

    n_keys = DIL_GROUPS[0][0] // DIL_GROUPS[0][1]
    caches = tuple(c.reshape(DEPTH, DEC_BATCH, n_keys, dil, 2, N_C_HEADS, HEAD_DIM)
                   for c, (_, dil) in zip((cache_kv_w128, cache_kv_w512, cache_kv_w2048), DIL_GROUPS))
    st_all = state_conv.reshape(DEPTH, DEC_BATCH, (CONV_W - 1) * W_A)
    bs_t = jnp.swapaxes(b_s, 1, 2)
    wsc = jnp.repeat(w_s[:, :, 0, 0], B_GROUP_W, axis=1)
    bsc = jnp.repeat(b_s[:, :, 0], B_GROUP_W, axis=1)

    xp = x_prompt.reshape(BATCH * SEQ, D_MODEL)
    xs = x_sample.reshape(DEC_BATCH, D_MODEL)
    conv_p, conv_s, vchunk_s = [], [], []
    kv_p = None
    kv_s = [[] for _ in DIL_GROUPS]

    def row(a, l):
        return a[l][None, :]

    for l in range(DEPTH):
        proj, qkv, proj_s, qkv_s, wfi_l, wfo_l, wa_l, wb_l, wc_l, wo_l = _inproj(
            xp, xs, g_pre_mix, w_in_l, l, tabs_p, tabs_s, TM_INPROJ, casts=inproj_casts)
        ya, yb, ctail, kv_p = _mix_ab_kv(proj, qkv, conv_w[l], row(ln_g, l), row(ln_b, l), w_s[l], bs_t[l], l, kv_p)
        yc = _attn(qkv)
        conv_p.append(ctail[:, 8 - (CONV_W - 1):])
        ya_s, yb_s, yc_s, cst, vn = _mix_sample(proj_s, qkv_s, st_all, conv_w[l], row(ln_g, l), row(ln_b, l),
                                                row(wsc, l), row(bsc, l), caches, l)
        for g in range(N_DIL):
            k = qkv_s[:, OFF_K + g * C_WIDTH:OFF_K + (g + 1) * C_WIDTH]
            v = qkv_s[:, OFF_V + g * C_WIDTH:OFF_V + (g + 1) * C_WIDTH]
            kv_s[g].append(jnp.stack([k.reshape(DEC_BATCH, 1, N_C_HEADS, HEAD_DIM),
                                      v.reshape(DEC_BATCH, 1, N_C_HEADS, HEAD_DIM)], axis=2))
        conv_s.append(cst.reshape(DEC_BATCH, CONV_W - 1, W_A))
        vchunk_s.append(vn.reshape(DEC_BATCH, 1, W_B))
        xp, xs = _merge(proj, ya, yb, yc, xp, proj_s, ya_s, yb_s, yc_s.reshape(DEC_BATCH, C_WIDTH), xs,
                        wa_l, wb_l, wc_l, wo_l, g_post_mix, l, TM_MERGE)
        next_casts = ((w_in, CAST_ROWS_W_IN),) if l + 1 < DEPTH else ()
        xp, xs, w_in_next = _ffn(xp, xs, g_pre_ffn, wfi_l, wfo_l, g_post_ffn, l, TM_FFN, TF_FFN, casts=next_casts)
        if w_in_next:
            w_in_l = w_in_next[0]

    return (xp.reshape(BATCH, SEQ, D_MODEL), xs.reshape(DEC_BATCH, 1, D_MODEL),
            jnp.stack(conv_p, axis=0),
            kv_p[0], kv_p[1], kv_p[2],
            jnp.stack(conv_s, axis=0),
            jnp.stack(kv_s[0], axis=0), jnp.stack(kv_s[1], axis=0), jnp.stack(kv_s[2], axis=0),
            jnp.stack(vchunk_s, axis=0))
```

```python
import functools

import jax
import jax.numpy as jnp
from jax import lax
from jax.experimental import pallas as pl
from jax.experimental.pallas import tpu as pltpu

D_MODEL = 2048
BATCH = 4
SEQ = 2048
DEPTH = 4
DEC_BATCH = 8
PAST_LEN = 16384
W_A = 1024
CONV_W = 3
W_B = 1024
CHUNK = 128
N_B_GROUPS = 4
B_GROUP_W = W_B // N_B_GROUPS
N_C_HEADS = 4
HEAD_DIM = 128
ROT_DIM = HEAD_DIM // 4
ROPE_THETA = 500000.0
DIL_GROUPS = ((128, 1), (512, 4), (2048, 16))
N_DIL = len(DIL_GROUPS)
QB = 128
C_WIDTH = N_C_HEADS * HEAD_DIM
QKV_W = N_DIL * C_WIDTH
D_FF = ((-(-8 * D_MODEL // 3) + 255) // 256) * 256
IN_WIDTH = 3 * W_A + 2 * W_B + 3 * QKV_W + 3 * D_MODEL
EPS = 1e-6

GATE_W = 3 * D_MODEL
OFF_AB = GATE_W
MAIN_W = GATE_W + 3 * W_A + 2 * W_B
OFF_Q, OFF_K, OFF_V = 0, QKV_W, 2 * QKV_W
ORIG_GATE_OFF = IN_WIDTH - GATE_W

ROPE_ROWS = 256
SOFTMAX_ROWS = 32
MERGE_ROWS = 64
V7X_VMEM_LIMIT = 60 * 1024 * 1024
NEG_BIG = -1e30

F32 = jnp.float32
BF16 = jnp.bfloat16


def _params(*sem):
    return pltpu.CompilerParams(dimension_semantics=sem, vmem_limit_bytes=V7X_VMEM_LIMIT)


def _rms(x, g):
    return x * lax.rsqrt(jnp.mean(x * x, axis=-1, keepdims=True) + EPS) * g


def _cast_specs(src, layer, rows, steps_per_row_tile, n_steps):
    _, r, c = src.shape
    nblk = r // rows
    assert r % rows == 0 and nblk <= n_steps

    def blk(i, j):
        return jnp.minimum(i * steps_per_row_tile + j, nblk - 1)

    return (pl.BlockSpec((None, rows, c), lambda i, j: (layer, blk(i, j), 0)),
            pl.BlockSpec((rows, c), lambda i, j: (blk(i, j), 0)),
            jax.ShapeDtypeStruct((r, c), BF16))


def _cast_blocks(src_refs, dst_refs):
    for s_ref, d_ref in zip(src_refs, dst_refs):
        d_ref[...] = s_ref[...].astype(BF16)


def _rope(a, c, s1, s2):
    return a * c + pltpu.roll(a, HEAD_DIM - ROT_DIM // 2, 1) * s1 + pltpu.roll(a, ROT_DIM // 2, 1) * s2


def _inproj_body(x_ref, g_ref, w0_ref, w1_ref, c_ref, s1_ref, s2_ref, xs_ref, cs_ref, s1s_ref, s2s_ref, *rest,
                 main_steps, rope_steps, odd_tiles, n_cast):
    cast_src = rest[:n_cast]
    main_ref, qkv_ref, mains_ref, qkvs_ref = rest[n_cast:n_cast + 4]
    cast_dst = rest[n_cast + 4:2 * n_cast + 4]
    h_ref, hs_ref = rest[-2:]
    i = pl.program_id(0)
    j = pl.program_id(1)
    tn = w0_ref.shape[1]
    heads = tn // HEAD_DIM
    halves = ((w0_ref, slice(0, tn)), (w1_ref, slice(tn, 2 * tn)))
    with_samples = i == 0

    @pl.when(j == 0)
    def _():
        h_ref[...] = _rms(x_ref[...], g_ref[...]).astype(BF16)

    @pl.when(jnp.logical_and(with_samples, j == 0))
    def _():
        hs_ref[...] = _rms(xs_ref[...], g_ref[...]).astype(BF16)

    @pl.when(j < main_steps)
    def _():
        _cast_blocks(cast_src, cast_dst)
        for w_ref, cols in halves:
            main_ref[:, cols] = jnp.dot(h_ref[...], w_ref[...], preferred_element_type=F32).astype(main_ref.dtype)

        @pl.when(with_samples)
        def _():
            for w_ref, cols in halves:
                mains_ref[j, :, cols] = jnp.dot(hs_ref[...], w_ref[...], preferred_element_type=F32)

    @pl.when(jnp.logical_and(j >= main_steps, j < main_steps + rope_steps))
    def _():
        _cast_blocks(cast_src, cast_dst)
        tm = x_ref.shape[0]
        rc = min(tm, ROPE_ROWS)
        for w_ref, cols in halves:
            for r in range(tm // rc):
                rs = slice(r * rc, (r + 1) * rc)
                acc = jnp.dot(h_ref[rs, :], w_ref[...], preferred_element_type=F32)
                for h in range(heads):
                    sl = slice(cols.start + h * HEAD_DIM, cols.start + (h + 1) * HEAD_DIM)
                    qkv_ref[rs, sl] = _rope(acc[:, h * HEAD_DIM:(h + 1) * HEAD_DIM],
                                            c_ref[rs, :], s1_ref[rs, :], s2_ref[rs, :])

        @pl.when(with_samples)
        def _():
            for w_ref, cols in halves:
                acc = jnp.dot(hs_ref[...], w_ref[...], preferred_element_type=F32)
                for h in range(heads):
                    sl = slice(cols.start + h * HEAD_DIM, cols.start + (h + 1) * HEAD_DIM)
                    qkvs_ref[j - main_steps, :, sl] = _rope(acc[:, h * HEAD_DIM:(h + 1) * HEAD_DIM],
                                                            cs_ref[...], s1s_ref[...], s2s_ref[...])

    def plain(tiles):
        _cast_blocks(cast_src, cast_dst)
        for w_ref, cols in tiles:
            qkv_ref[:, cols] = jnp.dot(h_ref[...], w_ref[...], preferred_element_type=F32)

        @pl.when(with_samples)
        def _():
            for w_ref, cols in tiles:
                qkvs_ref[j - main_steps, :, cols] = jnp.dot(hs_ref[...], w_ref[...], preferred_element_type=F32)
            for _, cols in halves[len(tiles):]:
                qkvs_ref[j - main_steps, :, cols] = jnp.zeros((xs_ref.shape[0], tn), F32)

    last = pl.num_programs(1) - 1
    pl.when(jnp.logical_and(j >= main_steps + rope_steps, j < last))(lambda: plain(halves))
    pl.when(j == last)(lambda: plain(halves[:1] if odd_tiles else halves))


def _inproj(x, xs, g, w, layer, tabs, tabs_s, tm, casts=()):
    m = x.shape[0]
    ns = xs.shape[0]
    tn = C_WIDTH
    n_tiles = IN_WIDTH // tn
    rot = ORIG_GATE_OFF // tn
    main_steps = MAIN_W // (2 * tn)
    rope_steps = 2 * QKV_W // (2 * tn)
    qkv_steps = pl.cdiv(3 * QKV_W, 2 * tn)
    steps = main_steps + qkv_steps
    tab_spec = pl.BlockSpec((tm, HEAD_DIM), lambda i, j: (i % (SEQ // tm), 0))
    tab_s_spec = pl.BlockSpec((ns, HEAD_DIM), lambda i, j: (0, 0))

    def w_spec(half):
        return pl.BlockSpec((D_MODEL, tn), lambda i, j: (0, (2 * j + half + rot) % n_tiles))

    cast_specs = [_cast_specs(src, layer, rows, steps, (m // tm) * steps) for src, rows in casts]
    res = pl.pallas_call(
        functools.partial(_inproj_body, main_steps=main_steps, rope_steps=rope_steps,
                          odd_tiles=(3 * QKV_W // tn) % 2 == 1, n_cast=len(casts)),
        grid=(m // tm, steps),
        in_specs=[
            pl.BlockSpec((tm, D_MODEL), lambda i, j: (i, 0)),
            pl.BlockSpec((None, 1, D_MODEL), lambda i, j: (layer, 0, 0)),
            w_spec(0), w_spec(1),
            tab_spec, tab_spec, tab_spec,
            pl.BlockSpec((ns, D_MODEL), lambda i, j: (0, 0)),
            tab_s_spec, tab_s_spec, tab_s_spec,
        ] + [c[0] for c in cast_specs],
        out_specs=[pl.BlockSpec((tm, 2 * tn), lambda i, j: (i, jnp.minimum(j, main_steps - 1))),
                   pl.BlockSpec((tm, 2 * tn), lambda i, j: (i, jnp.maximum(j - main_steps, 0))),
                   pl.BlockSpec((main_steps, ns, 2 * tn), lambda i, j: (0, 0, 0)),
                   pl.BlockSpec((qkv_steps, ns, 2 * tn), lambda i, j: (0, 0, 0))]
                  + [c[1] for c in cast_specs],
        out_shape=[jax.ShapeDtypeStruct((m, MAIN_W), BF16),
                   jax.ShapeDtypeStruct((m, 3 * QKV_W), F32),
                   jax.ShapeDtypeStruct((main_steps, ns, 2 * tn), F32),
                   jax.ShapeDtypeStruct((qkv_steps, ns, 2 * tn), F32)] + [c[2] for c in cast_specs],
        scratch_shapes=[pltpu.VMEM((tm, D_MODEL), BF16), pltpu.VMEM((ns, D_MODEL), BF16)],
        compiler_params=_params("arbitrary", "arbitrary"),
        name="inproj",
    )(x, g, w, w, *tabs, xs, *tabs_s, *[src for src, _ in casts])
    main_s = jnp.swapaxes(res[2], 0, 1).reshape(ns, MAIN_W)
    qkv_s = jnp.swapaxes(res[3], 0, 1).reshape(ns, qkv_steps * 2 * tn)[:, :3 * QKV_W]
    return (res[0], res[1], main_s, qkv_s, *res[4:])


def _rope_tables(pos):
    inv_freq = ROPE_THETA ** (-jnp.arange(0, ROT_DIM, 2, dtype=jnp.float32) / ROT_DIM)
    ang = pos.astype(jnp.float32)[:, None] * inv_freq[None, :]
    cos, sin = jnp.cos(ang), jnp.sin(ang)
    n = pos.shape[0]
    half = ROT_DIM // 2
    rest = HEAD_DIM - ROT_DIM
    c = jnp.concatenate([cos, cos, jnp.ones((n, rest), F32)], axis=1)
    s1 = jnp.concatenate([-sin, jnp.zeros((n, half + rest), F32)], axis=1)
    s2 = jnp.concatenate([jnp.zeros((n, half), F32), sin, jnp.zeros((n, rest), F32)], axis=1)
    return c, s1, s2


def _mix_ab_body(ab_ref, ac_ref, ax_ref, bu_ref, bv_ref, cw_ref, lng_ref, lnb_ref, ws_ref, bs_ref,
                 ya_ref, yb_ref, cs_ref, zs_ref):
    ts = ab_ref.shape[0]

    @pl.when(pl.program_id(1) == 0)
    def _():
        zs_ref[0:8, :] = jnp.zeros((8, W_A), F32)

    z = ac_ref[...].astype(F32) * ax_ref[...].astype(F32)
    zs_ref[8:8 + ts, :] = z
    z1 = zs_ref[7:7 + ts, :]
    z2 = zs_ref[6:6 + ts, :]
    conv = cw_ref[0:1, :] * z2 + cw_ref[1:2, :] * z1 + cw_ref[2:3, :] * z
    ya_ref[...] = (ab_ref[...].astype(F32) * conv).astype(BF16)
    tail = zs_ref[ts:ts + 8, :]
    zs_ref[0:8, :] = tail
    cs_ref[0] = tail

    v = bv_ref[...].astype(F32)
    xc = v - jnp.mean(v, axis=-1, keepdims=True)
    var = jnp.mean(xc * xc, axis=-1, keepdims=True)
    vn = (xc * lax.rsqrt(var + EPS) * lng_ref[...] + lnb_ref[...]).astype(BF16)
    row = lax.broadcasted_iota(jnp.int32, (CHUNK, CHUNK), 0)
    col = lax.broadcasted_iota(jnp.int32, (CHUNK, CHUNK), 1)
    for g in range(N_B_GROUPS):
        wg = jnp.where(row >= col, ws_ref[g], 0.0).astype(BF16)
        bcol = bs_ref[:, g:g + 1]
        gs = slice(g * B_GROUP_W, (g + 1) * B_GROUP_W)
        for c in range(ts // CHUNK):
            rs = slice(c * CHUNK, (c + 1) * CHUNK)
            sg = jnp.dot(wg, vn[rs, gs], preferred_element_type=F32) + bcol
            yb_ref[rs, gs] = (bu_ref[rs, gs].astype(F32) * sg).astype(BF16)


def _attn_body(q0, q1, q2, k0, k1, k2, v0, v1, v2, y_ref,
               qc, kc, vc, s_scr, p_scr, inv_scr, o0, o1, o2, l0, l1, l2):
    row = lax.broadcasted_iota(jnp.int32, (QB, 2 * QB), 0)
    col = lax.broadcasted_iota(jnp.int32, (QB, 2 * QB), 1)
    band = jnp.logical_and(col >= row, col <= row + QB)
    band_first = jnp.logical_and(band, col >= QB)
    scale = HEAD_DIM ** -0.5
    nt = (((1,), (1,)), ((), ()))
    zero_blk = jnp.zeros((QB, HEAD_DIM), BF16)
    for (_, dil), q_ref, k_ref, v_ref, o_ref, l_ref in zip(
            DIL_GROUPS, (q0, q1, q2), (k0, k1, k2), (v0, v1, v2), (o0, o1, o2), (l0, l1, l2)):
        L = SEQ // dil
        nblk = L // QB

        def tok_rows(r, c):
            if dil == 1:
                return pl.ds(c * QB, QB)
            return pl.ds(r + c * QB * dil, QB, stride=dil)

        for r in range(dil):
            src = pl.ds(0, L) if dil == 1 else pl.ds(r, L, stride=dil)
            base = r * (L + QB)
            qc[r * L:(r + 1) * L, :] = q_ref[src, :].astype(BF16)
            kc[base:base + QB, :] = zero_blk
            vc[base:base + QB, :] = zero_blk
            kc[base + QB:base + QB + L, :] = k_ref[src, :].astype(BF16)
            vc[base + QB:base + QB + L, :] = v_ref[src, :].astype(BF16)

        for r in range(dil):
            for c in range(nblk):
                t = r * nblk + c
                kk = kc[r * (L + QB) + c * QB:r * (L + QB) + (c + 2) * QB, :]
                s = lax.dot_general(qc[t * QB:(t + 1) * QB, :], kk, nt, preferred_element_type=F32) * scale
                s_scr[t * QB:(t + 1) * QB, :] = jnp.where(band_first if c == 0 else band, s, NEG_BIG)

        for r in range(dil):
            for u in range(L // SOFTMAX_ROWS):
                rows = slice(r * L + u * SOFTMAX_ROWS, r * L + (u + 1) * SOFTMAX_ROWS)
                s = s_scr[rows, :]
                m = jnp.max(s, axis=-1, keepdims=True)
                p = jnp.exp(s - m)
                den = jnp.sum(p, axis=-1, keepdims=True)
                p_scr[rows, :] = p.astype(BF16)
                inv_scr[rows, :] = jnp.broadcast_to(1.0 / den, (SOFTMAX_ROWS, HEAD_DIM))
                tok = (pl.ds(u * SOFTMAX_ROWS, SOFTMAX_ROWS) if dil == 1
                       else pl.ds(r + u * SOFTMAX_ROWS * dil, SOFTMAX_ROWS, stride=dil))
                l_ref[tok, :] = jnp.broadcast_to(m + jnp.log(den), (SOFTMAX_ROWS, HEAD_DIM))

        for r in range(dil):
            for c in range(nblk):
                t = r * nblk + c
                vv = vc[r * (L + QB) + c * QB:r * (L + QB) + (c + 2) * QB, :]
                o = jnp.dot(p_scr[t * QB:(t + 1) * QB, :], vv, preferred_element_type=F32)
                o_ref[tok_rows(r, c), :] = o * inv_scr[t * QB:(t + 1) * QB, :]
    for u in range(SEQ // MERGE_ROWS):
        rows = slice(u * MERGE_ROWS, (u + 1) * MERGE_ROWS)
        y_ref[rows, :] = _combine_groups(o0[rows, :], o1[rows, :], o2[rows, :],
                                         l0[rows, :], l1[rows, :], l2[rows, :]).astype(BF16)


def _attn(proj):
    def seg(off, g):
        base = (off + g * C_WIDTH) // HEAD_DIM
        return pl.BlockSpec((SEQ, HEAD_DIM), lambda b, h: (b, base + h))

    specs = [seg(off, g) for off in (OFF_Q, OFF_K, OFF_V) for g in range(N_DIL)]
    return pl.pallas_call(
        _attn_body,
        grid=(BATCH, N_C_HEADS),
        in_specs=specs,
        out_specs=pl.BlockSpec((SEQ, HEAD_DIM), lambda b, h: (b, h)),
        out_shape=jax.ShapeDtypeStruct((BATCH * SEQ, C_WIDTH), BF16),
        scratch_shapes=[pltpu.VMEM((SEQ, HEAD_DIM), BF16),
                        pltpu.VMEM((2 * SEQ, HEAD_DIM), BF16),
                        pltpu.VMEM((2 * SEQ, HEAD_DIM), BF16),
                        pltpu.VMEM((SEQ, 2 * QB), F32),
                        pltpu.VMEM((SEQ, 2 * QB), BF16),
                        pltpu.VMEM((SEQ, HEAD_DIM), F32)]
                       + [pltpu.VMEM((SEQ, HEAD_DIM), F32)] * (2 * N_DIL),
        compiler_params=_params("arbitrary", "arbitrary"),
        name="attn",
    )(*([proj] * (3 * N_DIL)))


def _kv_pack_body(*refs, fixed):
    k_refs, v_refs, out_refs = refs[0:N_DIL], refs[N_DIL:2 * N_DIL], refs[-N_DIL:]
    for k_ref, v_ref, o_ref, is_fixed in zip(k_refs, v_refs, out_refs, fixed):
        def pack(k_ref=k_ref, v_ref=v_ref, o_ref=o_ref):
            for h in range(N_C_HEADS):
                sl = slice(h * HEAD_DIM, (h + 1) * HEAD_DIM)
                o_ref[:, 0, h, :] = k_ref[:, sl]
                o_ref[:, 1, h, :] = v_ref[:, sl]

        if is_fixed:
            pl.when(pl.program_id(1) == 0)(pack)
        else:
            pack()


N_MIX_IN, N_MIX_OUT = 10, 3


def _mix_ab_kv_body(*refs, fixed):
    n_kv_in = len(refs) - N_MIX_IN - N_MIX_OUT - N_DIL - 1
    mix_in, kv_in = refs[:N_MIX_IN], refs[N_MIX_IN:N_MIX_IN + 2 * N_DIL]
    outs = refs[N_MIX_IN + n_kv_in:-1]
    _mix_ab_body(*mix_in, *outs[:N_MIX_OUT], refs[-1])
    _kv_pack_body(*kv_in, *outs[N_MIX_OUT:], fixed=fixed)


def _mix_ab_kv(main, qkv, cw, lng, lnb, ws, bs_t, layer, prev):
    ts = TS_MIX
    nb = SEQ // ts
    blk = OFF_AB // W_A
    keeps = [min(win, SEQ) for win, _ in DIL_GROUPS]

    def seg(k):
        return pl.BlockSpec((ts, W_A), lambda b, s: (b * nb + s, blk + k))

    def full(shape):
        return pl.BlockSpec(shape, lambda b, s: (0,) * len(shape))

    rows = pl.BlockSpec((ts, W_A), lambda b, s: (b * nb + s, 0))
    mix_in = [seg(0), seg(1), seg(2), seg(3), seg(4),
              full((CONV_W, W_A)), full((1, W_B)), full((1, W_B)),
              full((N_B_GROUPS, CHUNK, CHUNK)), full((CHUNK, N_B_GROUPS))]
    mix_out = [rows, rows, pl.BlockSpec((1, 8, W_A), lambda b, s: (b, 0, 0))]
    mix_shapes = [jax.ShapeDtypeStruct((BATCH * SEQ, W_A), BF16),
                  jax.ShapeDtypeStruct((BATCH * SEQ, W_B), BF16),
                  jax.ShapeDtypeStruct((BATCH, 8, W_A), F32)]
    assert len(mix_in) == N_MIX_IN and len(mix_out) == N_MIX_OUT

    def src(off, g):
        keep, col = keeps[g], (off + g * C_WIDTH) // C_WIDTH
        if keep <= ts:
            nb = SEQ // keep
            return pl.BlockSpec((keep, C_WIDTH), lambda b, s: (b * nb + nb - 1, col))
        nb, first = SEQ // ts, (SEQ - keep) // ts
        return pl.BlockSpec((ts, C_WIDTH), lambda b, s: (b * nb + jnp.maximum(s, first), col))

    def dst(keep):
        if keep <= ts:
            return pl.BlockSpec((None, None, keep, 2, N_C_HEADS, HEAD_DIM), lambda b, s: (layer, b, 0, 0, 0, 0))
        first = (SEQ - keep) // ts
        return pl.BlockSpec((None, None, ts, 2, N_C_HEADS, HEAD_DIM),
                            lambda b, s: (layer, b, jnp.maximum(s - first, 0), 0, 0, 0))

    in_specs = mix_in + [src(OFF_K, g) for g in range(N_DIL)] + [src(OFF_V, g) for g in range(N_DIL)]
    args = [main] * 5 + [cw, lng, lnb, ws, bs_t] + [qkv] * (2 * N_DIL)
    aliases = {}
    if prev is not None:
        aliases = {len(in_specs) + g: N_MIX_OUT + g for g in range(N_DIL)}
        in_specs += [pl.BlockSpec(memory_space=pl.ANY)] * N_DIL
        args += list(prev)
    res = pl.pallas_call(
        functools.partial(_mix_ab_kv_body, fixed=tuple(keep <= ts for keep in keeps)),
        grid=(BATCH, nb),
        in_specs=in_specs,
        out_specs=mix_out + [dst(keep) for keep in keeps],
        out_shape=mix_shapes + [jax.ShapeDtypeStruct((DEPTH, BATCH, keep, 2, N_C_HEADS, HEAD_DIM), F32)
                                for keep in keeps],
        input_output_aliases=aliases,
        scratch_shapes=[pltpu.VMEM((ts + 8, W_A), F32)],
        compiler_params=_params("arbitrary", "arbitrary"),
        name="mix_ab_kv",
    )(*args)
    return res[0], res[1], res[2], res[N_MIX_OUT:]


def _mix_sample_body(p_ref, qkv_ref, st_ref, cw_ref, lng_ref, lnb_ref, wsc_ref, bsc_ref, c0_ref, c1_ref, c2_ref,
                     ya_ref, yb_ref, yc_ref, cs_ref, vn_ref):
    @pl.when(pl.program_id(0) == 0)
    def _():
        def seg(off, w):
            return p_ref[:, off:off + w]

        z = seg(OFF_AB + W_A, W_A) * seg(OFF_AB + 2 * W_A, W_A)
        st0 = st_ref[0, :, 0:W_A]
        st1 = st_ref[0, :, W_A:2 * W_A]
        conv = cw_ref[0:1, :] * st0 + cw_ref[1:2, :] * st1 + cw_ref[2:3, :] * z
        ya_ref[...] = (seg(OFF_AB, W_A) * conv).astype(BF16)
        cs_ref[:, 0:W_A] = st1
        cs_ref[:, W_A:2 * W_A] = z

        v = seg(OFF_AB + 3 * W_A + W_B, W_B)
        xc = v - jnp.mean(v, axis=-1, keepdims=True)
        var = jnp.mean(xc * xc, axis=-1, keepdims=True)
        vn = xc * lax.rsqrt(var + EPS) * lng_ref[...] + lnb_ref[...]
        vn_ref[...] = vn
        yb_ref[...] = (seg(OFF_AB + 3 * W_A, W_B) * (wsc_ref[...] * vn + bsc_ref[...])).astype(BF16)

    scale = HEAD_DIM ** -0.5
    outs, lses = [], []
    for g, c_ref in enumerate((c0_ref, c1_ref, c2_ref)):
        q = qkv_ref[g]
        kn = qkv_ref[N_DIL + g]
        vnew = qkv_ref[2 * N_DIL + g]
        kc = c_ref[:, 0]
        vc = c_ref[:, 1]
        s_c = jnp.sum(q[None] * kc, axis=-1, keepdims=True) * scale
        s_n = jnp.sum(q * kn, axis=-1, keepdims=True) * scale
        m = jnp.maximum(jnp.max(s_c, axis=0), s_n)
        p_c = jnp.exp(s_c - m[None])
        p_n = jnp.exp(s_n - m)
        den = jnp.sum(p_c, axis=0) + p_n
        outs.append((jnp.sum(p_c * vc, axis=0) + p_n * vnew) / den)
        lses.append(m + jnp.log(den))
    yc_ref[...] = _combine_groups(*outs, *lses)


def _mix_sample(proj_s, qkv_s, st, cw, lng, lnb, wsc, bsc, caches, layer):
    def full(shape):
        return pl.BlockSpec(shape, lambda b: (0,) * len(shape))

    n_keys = DIL_GROUPS[0][0] // DIL_GROUPS[0][1]
    cache_specs = [pl.BlockSpec((None, None, n_keys, None, 2, N_C_HEADS, HEAD_DIM),
                                lambda b: (layer, b, 0, 0, 0, 0, 0)) for _ in caches]
    qkv = qkv_s.reshape(DEC_BATCH, 3 * N_DIL, N_C_HEADS, HEAD_DIM)
    return pl.pallas_call(
        _mix_sample_body,
        grid=(DEC_BATCH,),
        in_specs=[full((DEC_BATCH, MAIN_W)),
                  pl.BlockSpec((None, 3 * N_DIL, N_C_HEADS, HEAD_DIM), lambda b: (b, 0, 0, 0)),
                  pl.BlockSpec((1, DEC_BATCH, 2 * W_A), lambda b: (layer, 0, 0)),
                  full((CONV_W, W_A)), full((1, W_B)), full((1, W_B)), full((1, W_B)), full((1, W_B))]
                 + cache_specs,
        out_specs=[full((DEC_BATCH, W_A)), full((DEC_BATCH, W_B)),
                   pl.BlockSpec((None, N_C_HEADS, HEAD_DIM), lambda b: (b, 0, 0)),
                   full((DEC_BATCH, 2 * W_A)), full((DEC_BATCH, W_B))],
        out_shape=[jax.ShapeDtypeStruct((DEC_BATCH, W_A), BF16),
                   jax.ShapeDtypeStruct((DEC_BATCH, W_B), BF16),
                   jax.ShapeDtypeStruct((DEC_BATCH, N_C_HEADS, HEAD_DIM), F32),
                   jax.ShapeDtypeStruct((DEC_BATCH, 2 * W_A), F32),
                   jax.ShapeDtypeStruct((DEC_BATCH, W_B), F32)],
        compiler_params=_params("arbitrary"),
        name="mix_sample",
    )(proj_s, qkv, st, cw, lng, lnb, wsc, bsc, *caches)


def _combine_groups(o0, o1, o2, l0, l1, l2):
    m = jnp.maximum(jnp.maximum(l0, l1), l2)
    e0, e1, e2 = jnp.exp(l0 - m), jnp.exp(l1 - m), jnp.exp(l2 - m)
    return (e0 * o0 + e1 * o1 + e2 * o2) / (e0 + e1 + e2)


def _merge_tail(x, ga, gb, gc, ya, yb, yc, wa_ref, wb_ref, wc_ref, wo_ref, gp_ref):
    mm = (jax.nn.sigmoid(ga) * jnp.dot(ya, wa_ref[...], preferred_element_type=F32)
          + jax.nn.sigmoid(gb) * jnp.dot(yb, wb_ref[...], preferred_element_type=F32)
          + jax.nn.sigmoid(gc) * jnp.dot(yc, wc_ref[...], preferred_element_type=F32))
    r = jnp.dot(mm.astype(BF16), wo_ref[...], preferred_element_type=F32)
    return x + _rms(r, gp_ref[...])


def _merge_body(ga_ref, gb_ref, gc_ref, ya_ref, yb_ref, yc_ref, x_ref,
                ms_ref, yas_ref, ybs_ref, ycs_ref, xs_ref,
                wa_ref, wb_ref, wc_ref, wo_ref, gp_ref, out_ref, outs_ref):
    weights = (wa_ref, wb_ref, wc_ref, wo_ref, gp_ref)
    out_ref[...] = _merge_tail(x_ref[...], ga_ref[...].astype(F32), gb_ref[...].astype(F32),
                               gc_ref[...].astype(F32), ya_ref[...], yb_ref[...], yc_ref[...], *weights)

    @pl.when(pl.program_id(0) == 0)
    def _():
        gates = [ms_ref[:, k * D_MODEL:(k + 1) * D_MODEL] for k in range(3)]
        outs_ref[...] = _merge_tail(xs_ref[...], *gates, yas_ref[...], ybs_ref[...],
                                    ycs_ref[...].astype(BF16), *weights)


def _merge(proj, ya, yb, yc, x, main_s, ya_s, yb_s, yc_s, xs, wa, wb, wc, wo, gp, layer, tm):
    m = x.shape[0]

    def rows(w, k=0):
        return pl.BlockSpec((tm, w), lambda i: (i, k))

    def whole(a):
        return pl.BlockSpec(a.shape, lambda i: (0, 0))

    def resident(k):
        return pl.BlockSpec((k, D_MODEL), lambda i: (0, 0), pipeline_mode=pl.Buffered(1))

    return pl.pallas_call(
        _merge_body,
        grid=(m // tm,),
        in_specs=[rows(D_MODEL, 0), rows(D_MODEL, 1), rows(D_MODEL, 2), rows(W_A), rows(W_B), rows(C_WIDTH),
                  rows(D_MODEL),
                  whole(main_s), whole(ya_s), whole(yb_s), whole(yc_s), whole(xs),
                  resident(W_A), resident(W_B), resident(C_WIDTH), resident(D_MODEL),
                  pl.BlockSpec((None, 1, D_MODEL), lambda i: (layer, 0, 0), pipeline_mode=pl.Buffered(1))],
        out_specs=[rows(D_MODEL), whole(xs)],
        out_shape=[jax.ShapeDtypeStruct((m, D_MODEL), F32), jax.ShapeDtypeStruct(xs.shape, F32)],
        compiler_params=_params("arbitrary"),
        name="merge",
    )(proj, proj, proj, ya, yb, yc, x, main_s, ya_s, yb_s, yc_s, xs, wa, wb, wc, wo, gp)


def _swiglu_part(h, wg_ref, wu_ref, wo_ref):
    gate = jnp.dot(h, wg_ref[...], preferred_element_type=F32)
    up = jnp.dot(h, wu_ref[...], preferred_element_type=F32)
    act = (gate * jax.nn.sigmoid(gate) * up).astype(BF16)
    return jnp.dot(act, wo_ref[...], preferred_element_type=F32)


def _ffn_body(x_ref, g1_ref, wg_ref, wu_ref, wo_ref, g2_ref, xs_ref, *rest, n_cast):
    cast_src = rest[:n_cast]
    out_ref, outs_ref = rest[n_cast:n_cast + 2]
    cast_dst = rest[n_cast + 2:2 * n_cast + 2]
    h_ref, hs_ref = rest[-2:]
    i = pl.program_id(0)
    j = pl.program_id(1)
    last = pl.num_programs(1) - 1

    @pl.when(j == 0)
    def _():
        h_ref[...] = _rms(x_ref[...], g1_ref[...]).astype(BF16)
        out_ref[...] = jnp.zeros_like(out_ref)

    _cast_blocks(cast_src, cast_dst)
    out_ref[...] += _swiglu_part(h_ref[...], wg_ref, wu_ref, wo_ref)

    @pl.when(j == last)
    def _():
        out_ref[...] = x_ref[...] + _rms(out_ref[...], g2_ref[...])

    @pl.when(i == 0)
    def _():
        @pl.when(j == 0)
        def _():
            hs_ref[...] = _rms(xs_ref[...], g1_ref[...]).astype(BF16)
            outs_ref[...] = jnp.zeros_like(outs_ref)

        outs_ref[...] += _swiglu_part(hs_ref[...], wg_ref, wu_ref, wo_ref)

        @pl.when(j == last)
        def _():
            outs_ref[...] = xs_ref[...] + _rms(outs_ref[...], g2_ref[...])


def _ffn(x, xs, g1, w_in, w_out, g2, layer, tm, tf, casts=()):
    m = x.shape[0]
    ns = xs.shape[0]
    nj = D_FF // tf
    cast_specs = [_cast_specs(src, layer + 1, rows, nj, (m // tm) * nj) for src, rows in casts]
    res = pl.pallas_call(
        functools.partial(_ffn_body, n_cast=len(casts)),
        grid=(m // tm, nj),
        in_specs=[
            pl.BlockSpec((tm, D_MODEL), lambda i, j: (i, 0)),
            pl.BlockSpec((None, 1, D_MODEL), lambda i, j: (layer, 0, 0)),
            pl.BlockSpec((D_MODEL, tf), lambda i, j: (0, j)),
            pl.BlockSpec((D_MODEL, tf), lambda i, j: (0, nj + j)),
            pl.BlockSpec((tf, D_MODEL), lambda i, j: (j, 0)),
            pl.BlockSpec((None, 1, D_MODEL), lambda i, j: (layer, 0, 0)),
            pl.BlockSpec((ns, D_MODEL), lambda i, j: (0, 0)),
        ] + [c[0] for c in cast_specs],
        out_specs=[pl.BlockSpec((tm, D_MODEL), lambda i, j: (i, 0)),
                   pl.BlockSpec((ns, D_MODEL), lambda i, j: (0, 0))] + [c[1] for c in cast_specs],
        out_shape=[jax.ShapeDtypeStruct((m, D_MODEL), F32),
                   jax.ShapeDtypeStruct((ns, D_MODEL), F32)] + [c[2] for c in cast_specs],
        scratch_shapes=[pltpu.VMEM((tm, D_MODEL), BF16), pltpu.VMEM((ns, D_MODEL), BF16)],
        compiler_params=_params("arbitrary", "arbitrary"),
        name="ffn",
    )(x, g1, w_in, w_in, w_out, g2, xs, *[src for src, _ in casts])
    return res[0], res[1], res[2:]


TM_INPROJ = 1024
TS_MIX = 512
TM_MERGE = 256
TM_FFN = 512
TF_FFN = 512
CAST_ROWS_FFN_IN = 16
CAST_ROWS_FFN_OUT = 64
CAST_ROWS_W_IN = 16
CAST_ROWS_MERGE = 16


def kernel(x_prompt, x_sample, state_conv, cache_kv_w128, cache_kv_w512, cache_kv_w2048, g_pre_mix, w_in, conv_w, ln_g, ln_b, w_s, b_s, w_a_out, w_b_out, w_c_out, w_o, g_post_mix, g_pre_ffn, w_ffn_in, w_ffn_out, g_post_ffn):
    w_in_l = w_in[0].astype(BF16)
    inproj_casts = ((w_ffn_in, CAST_ROWS_FFN_IN), (w_ffn_out, CAST_ROWS_FFN_OUT),
                    (w_a_out, CAST_ROWS_MERGE), (w_b_out, CAST_ROWS_MERGE), (w_c_out, CAST_ROWS_MERGE),
                    (w_o, CAST_ROWS_MERGE))
    g_pre_mix, g_post_mix, g_pre_ffn, g_post_ffn = (
        g.reshape(DEPTH, 1, D_MODEL) for g in (g_pre_mix, g_post_mix, g_pre_ffn, g_post_ffn))

    tabs_p = _rope_tables(jnp.arange(SEQ, dtype=jnp.int32))
    tabs_s = _rope_tables(jnp.full((DEC_BATCH,), PAST_LEN, dtype=jnp.int32))

    n_keys = DIL_GROUPS[0][0] // DIL_GROUPS[0][1]
    caches = tuple(c.reshape(DEPTH, DEC_BATCH, n_keys, dil, 2, N_C_HEADS, HEAD_DIM)
                   for c, (_, dil) in zip((cache_kv_w128, cache_kv_w512, cache_kv_w2048), DIL_GROUPS))
    st_all = state_conv.reshape(DEPTH, DEC_BATCH, (CONV_W - 1) * W_A)
    bs_t = jnp.swapaxes(b_s, 1, 2)
    wsc = jnp.repeat(w_s[:, :, 0, 0], B_GROUP_W, axis=1)
    bsc = jnp.repeat(b_s[:, :, 0], B_GROUP_W, axis=1)

    xp = x_prompt.reshape(BATCH * SEQ, D_MODEL)
    xs = x_sample.reshape(DEC_BATCH, D_MODEL)
    conv_p, conv_s, vchunk_s = [], [], []
    kv_p = None
    kv_s = [[] for _ in DIL_GROUPS]

    def row(a, l):
        return a[l][None, :]

    for l in range(DEPTH):
        proj, qkv, proj_s, qkv_s, wfi_l, wfo_l, wa_l, wb_l, wc_l, wo_l = _inproj(
            xp, xs, g_pre_mix, w_in_l, l, tabs_p, tabs_s, TM_INPROJ, casts=inproj_casts)
        ya, yb, ctail, kv_p = _mix_ab_kv(proj, qkv, conv_w[l], row(ln_g, l), row(ln_b, l), w_s[l], bs_t[l], l, kv_p)
        yc = _attn(qkv)
        conv_p.append(ctail[:, 8 - (CONV_W - 1):])
        ya_s, yb_s, yc_s, cst, vn = _mix_sample(proj_s, qkv_s, st_all, conv_w[l], row(ln_g, l), row(ln_b, l),
                                                row(wsc, l), row(bsc, l), caches, l)
        for g in range(N_DIL):
            k = qkv_s[:, OFF_K + g * C_WIDTH:OFF_K + (g + 1) * C_WIDTH]
            v = qkv_s[:, OFF_V + g * C_WIDTH:OFF_V + (g + 1) * C_WIDTH]
            kv_s[g].append(jnp.stack([k.reshape(DEC_BATCH, 1, N_C_HEADS, HEAD_DIM),
                                      v.reshape(DEC_BATCH, 1, N_C_HEADS, HEAD_DIM)], axis=2))
        conv_s.append(cst.reshape(DEC_BATCH, CONV_W - 1, W_A))
        vchunk_s.append(vn.reshape(DEC_BATCH, 1, W_B))
        xp, xs = _merge(proj, ya, yb, yc, xp, proj_s, ya_s, yb_s, yc_s.reshape(DEC_BATCH, C_WIDTH), xs,
                        wa_l, wb_l, wc_l, wo_l, g_post_mix, l, TM_MERGE)
        next_casts = ((w_in, CAST_ROWS_W_IN),) if l + 1 < DEPTH else ()
        xp, xs, w_in_next = _ffn(xp, xs, g_pre_ffn, wfi_l, wfo_l, g_post_ffn, l, TM_FFN, TF_FFN, casts=next_casts)
        if w_in_next:
            w_in_l = w_in_next[0]

    return (xp.reshape(BATCH, SEQ, D_MODEL), xs.reshape(DEC_BATCH, 1, D_MODEL),
            jnp.stack(conv_p, axis=0),
            kv_p[0], kv_p[1], kv_p[2],
            jnp.stack(conv_s, axis=0),
            jnp.stack(kv_s[0], axis=0), jnp.stack(kv_s[1], axis=0), jnp.stack(kv_s[2], axis=0),
            jnp.stack(vchunk_s, axis=0))
```

```python
import functools

import jax
import jax.numpy as jnp
from jax import lax
from jax.experimental import pallas as pl
from jax.experimental.pallas import tpu as pltpu

D_MODEL = 2048
BATCH = 4
SEQ = 2048
DEPTH = 4
DEC_BATCH = 8
PAST_LEN = 16384
W_A = 1024
CONV_W = 3
W_B = 1024
CHUNK = 128
N_B_GROUPS = 4
B_GROUP_W = W_B // N_B_GROUPS
N_C_HEADS = 4
HEAD_DIM = 128
ROT_DIM = HEAD_DIM // 4
ROPE_THETA = 500000.0
DIL_GROUPS = ((128, 1), (512, 4), (2048, 16))
N_DIL = len(DIL_GROUPS)
QB = 128
C_WIDTH = N_C_HEADS * HEAD_DIM
QKV_W = N_DIL * C_WIDTH
D_FF = ((-(-8 * D_MODEL // 3) + 255) // 256) * 256
IN_WIDTH = 3 * W_A + 2 * W_B + 3 * QKV_W + 3 * D_MODEL
EPS = 1e-6

GATE_W = 3 * D_MODEL
OFF_AB = GATE_W
MAIN_W = GATE_W + 3 * W_A + 2 * W_B
OFF_Q, OFF_K, OFF_V = 0, QKV_W, 2 * QKV_W
ORIG_GATE_OFF = IN_WIDTH - GATE_W

ROPE_ROWS = 256
SOFTMAX_ROWS = 32
MERGE_ROWS = 64
V7X_VMEM_LIMIT = 60 * 1024 * 1024
NEG_BIG = -1e30

F32 = jnp.float32
BF16 = jnp.bfloat16


def _params(*sem):
    return pltpu.CompilerParams(dimension_semantics=sem, vmem_limit_bytes=V7X_VMEM_LIMIT)


def _rms(x, g):
    return x * lax.rsqrt(jnp.mean(x * x, axis=-1, keepdims=True) + EPS) * g


def _cast_specs(src, layer, rows, steps_per_row_tile, n_steps):
    _, r, c = src.shape
    nblk = r // rows
    assert r % rows == 0 and nblk <= n_steps

    def blk(i, j):
        return jnp.minimum(i * steps_per_row_tile + j, nblk - 1)

    return (pl.BlockSpec((None, rows, c), lambda i, j: (layer, blk(i, j), 0)),
            pl.BlockSpec((rows, c), lambda i, j: (blk(i, j), 0)),
            jax.ShapeDtypeStruct((r, c), BF16))


def _cast_blocks(src_refs, dst_refs):
    for s_ref, d_ref in zip(src_refs, dst_refs):
        d_ref[...] = s_ref[...].astype(BF16)


def _rope(a, c, s1, s2):
    return a * c + pltpu.roll(a, HEAD_DIM - ROT_DIM // 2, 1) * s1 + pltpu.roll(a, ROT_DIM // 2, 1) * s2


def _inproj_body(x_ref, g_ref, w0_ref, w1_ref, c_ref, s1_ref, s2_ref, xs_ref, cs_ref, s1s_ref, s2s_ref, *rest,
                 main_steps, rope_steps, n_cast):
    cast_src = rest[:n_cast]
    main_ref, qkv_ref, mains_ref, qkvs_ref = rest[n_cast:n_cast + 4]
    cast_dst = rest[n_cast + 4:2 * n_cast + 4]
    h_ref, hs_ref = rest[-2:]
    i = pl.program_id(0)
    j = pl.program_id(1)
    tn = w0_ref.shape[1]
    heads = tn // HEAD_DIM
    halves = ((w0_ref, slice(0, tn)), (w1_ref, slice(tn, 2 * tn)))
    with_samples = i == 0

    @pl.when(j == 0)
    def _():
        h_ref[...] = _rms(x_ref[...], g_ref[...]).astype(BF16)

    @pl.when(jnp.logical_and(with_samples, j == 0))
    def _():
        hs_ref[...] = _rms(xs_ref[...], g_ref[...]).astype(BF16)

    @pl.when(j < main_steps)
    def _():
        _cast_blocks(cast_src, cast_dst)
        for w_ref, cols in halves:
            main_ref[:, cols] = jnp.dot(h_ref[...], w_ref[...], preferred_element_type=F32).astype(main_ref.dtype)

        @pl.when(with_samples)
        def _():
            for w_ref, cols in halves:
                mains_ref[j, :, cols] = jnp.dot(hs_ref[...], w_ref[...], preferred_element_type=F32)

    @pl.when(jnp.logical_and(j >= main_steps, j < main_steps + rope_steps))
    def _():
        _cast_blocks(cast_src, cast_dst)
        tm = x_ref.shape[0]
        rc = min(tm, ROPE_ROWS)
        for w_ref, cols in halves:
            for r in range(tm // rc):
                rs = slice(r * rc, (r + 1) * rc)
                acc = jnp.dot(h_ref[rs, :], w_ref[...], preferred_element_type=F32)
                for h in range(heads):
                    sl = slice(cols.start + h * HEAD_DIM, cols.start + (h + 1) * HEAD_DIM)
                    qkv_ref[rs, sl] = _rope(acc[:, h * HEAD_DIM:(h + 1) * HEAD_DIM],
                                            c_ref[rs, :], s1_ref[rs, :], s2_ref[rs, :])

        @pl.when(with_samples)
        def _():
            for w_ref, cols in halves:
                acc = jnp.dot(hs_ref[...], w_ref[...], preferred_element_type=F32)
                for h in range(heads):
                    sl = slice(cols.start + h * HEAD_DIM, cols.start + (h + 1) * HEAD_DIM)
                    qkvs_ref[j - main_steps, :, sl] = _rope(acc[:, h * HEAD_DIM:(h + 1) * HEAD_DIM],
                                                            cs_ref[...], s1s_ref[...], s2s_ref[...])

    @pl.when(j >= main_steps + rope_steps)
    def _():
        _cast_blocks(cast_src, cast_dst)
        for w_ref, cols in halves:
            qkv_ref[:, cols] = jnp.dot(h_ref[...], w_ref[...], preferred_element_type=F32)

        @pl.when(with_samples)
        def _():
            for w_ref, cols in halves:
                qkvs_ref[j - main_steps, :, cols] = jnp.dot(hs_ref[...], w_ref[...], preferred_element_type=F32)


def _inproj(x, xs, g, w, layer, tabs, tabs_s, tm, casts=()):
    m = x.shape[0]
    ns = xs.shape[0]
    tn = C_WIDTH
    n_tiles = IN_WIDTH // tn
    rot = ORIG_GATE_OFF // tn
    main_steps = MAIN_W // (2 * tn)
    rope_steps = 2 * QKV_W // (2 * tn)
    qkv_steps = pl.cdiv(3 * QKV_W, 2 * tn)
    steps = main_steps + qkv_steps
    tab_spec = pl.BlockSpec((tm, HEAD_DIM), lambda i, j: (i % (SEQ // tm), 0))
    tab_s_spec = pl.BlockSpec((ns, HEAD_DIM), lambda i, j: (0, 0))

    def w_spec(half):
        return pl.BlockSpec((D_MODEL, tn), lambda i, j: (0, (2 * j + half + rot) % n_tiles))

    def x_tile(i, j):
        return jnp.minimum(i + jnp.where(j >= steps // 2, 1, 0), m // tm - 1)

    cast_specs = [_cast_specs(src, layer, rows, steps, (m // tm) * steps) for src, rows in casts]
    res = pl.pallas_call(
        functools.partial(_inproj_body, main_steps=main_steps, rope_steps=rope_steps, n_cast=len(casts)),
        grid=(m // tm, steps),
        in_specs=[
            pl.BlockSpec((tm, D_MODEL), lambda i, j: (x_tile(i, j), 0)),
            pl.BlockSpec((None, 1, D_MODEL), lambda i, j: (layer, 0, 0)),
            w_spec(0), w_spec(1),
            tab_spec, tab_spec, tab_spec,
            pl.BlockSpec((ns, D_MODEL), lambda i, j: (0, 0)),
            tab_s_spec, tab_s_spec, tab_s_spec,
        ] + [c[0] for c in cast_specs],
        out_specs=[pl.BlockSpec((tm, 2 * tn), lambda i, j: (i, jnp.minimum(j, main_steps - 1))),
                   pl.BlockSpec((tm, 2 * tn), lambda i, j: (i, jnp.maximum(j - main_steps, 0))),
                   pl.BlockSpec((main_steps, ns, 2 * tn), lambda i, j: (0, 0, 0)),
                   pl.BlockSpec((qkv_steps, ns, 2 * tn), lambda i, j: (0, 0, 0))]
                  + [c[1] for c in cast_specs],
        out_shape=[jax.ShapeDtypeStruct((m, MAIN_W), BF16),
                   jax.ShapeDtypeStruct((m, 3 * QKV_W), F32),
                   jax.ShapeDtypeStruct((main_steps, ns, 2 * tn), F32),
                   jax.ShapeDtypeStruct((qkv_steps, ns, 2 * tn), F32)] + [c[2] for c in cast_specs],
        scratch_shapes=[pltpu.VMEM((tm, D_MODEL), BF16), pltpu.VMEM((ns, D_MODEL), BF16)],
        compiler_params=_params("arbitrary", "arbitrary"),
        name="inproj",
    )(x, g, w, w, *tabs, xs, *tabs_s, *[src for src, _ in casts])
    main_s = jnp.swapaxes(res[2], 0, 1).reshape(ns, MAIN_W)
    qkv_s = jnp.swapaxes(res[3], 0, 1).reshape(ns, qkv_steps * 2 * tn)[:, :3 * QKV_W]
    return (res[0], res[1], main_s, qkv_s, *res[4:])


def _rope_tables(pos):
    inv_freq = ROPE_THETA ** (-jnp.arange(0, ROT_DIM, 2, dtype=jnp.float32) / ROT_DIM)
    ang = pos.astype(jnp.float32)[:, None] * inv_freq[None, :]
    cos, sin = jnp.cos(ang), jnp.sin(ang)
    n = pos.shape[0]
    half = ROT_DIM // 2
    rest = HEAD_DIM - ROT_DIM
    c = jnp.concatenate([cos, cos, jnp.ones((n, rest), F32)], axis=1)
    s1 = jnp.concatenate([-sin, jnp.zeros((n, half + rest), F32)], axis=1)
    s2 = jnp.concatenate([jnp.zeros((n, half), F32), sin, jnp.zeros((n, rest), F32)], axis=1)
    return c, s1, s2


def _mix_ab_body(ab_ref, ac_ref, ax_ref, bu_ref, bv_ref, cw_ref, lng_ref, lnb_ref, ws_ref, bs_ref,
                 ya_ref, yb_ref, cs_ref, zs_ref):
    ts = ab_ref.shape[0]

    @pl.when(pl.program_id(1) == 0)
    def _():
        zs_ref[0:8, :] = jnp.zeros((8, W_A), F32)

    z = ac_ref[...].astype(F32) * ax_ref[...].astype(F32)
    zs_ref[8:8 + ts, :] = z
    z1 = zs_ref[7:7 + ts, :]
    z2 = zs_ref[6:6 + ts, :]
    conv = cw_ref[0:1, :] * z2 + cw_ref[1:2, :] * z1 + cw_ref[2:3, :] * z
    ya_ref[...] = (ab_ref[...].astype(F32) * conv).astype(BF16)
    tail = zs_ref[ts:ts + 8, :]
    zs_ref[0:8, :] = tail
    cs_ref[0] = tail

    v = bv_ref[...].astype(F32)
    xc = v - jnp.mean(v, axis=-1, keepdims=True)
    var = jnp.mean(xc * xc, axis=-1, keepdims=True)
    vn = (xc * lax.rsqrt(var + EPS) * lng_ref[...] + lnb_ref[...]).astype(BF16)
    row = lax.broadcasted_iota(jnp.int32, (CHUNK, CHUNK), 0)
    col = lax.broadcasted_iota(jnp.int32, (CHUNK, CHUNK), 1)
    for g in range(N_B_GROUPS):
        wg = jnp.where(row >= col, ws_ref[g], 0.0).astype(BF16)
        bcol = bs_ref[:, g:g + 1]
        gs = slice(g * B_GROUP_W, (g + 1) * B_GROUP_W)
        for c in range(ts // CHUNK):
            rs = slice(c * CHUNK, (c + 1) * CHUNK)
            sg = jnp.dot(wg, vn[rs, gs], preferred_element_type=F32) + bcol
            yb_ref[rs, gs] = (bu_ref[rs, gs].astype(F32) * sg).astype(BF16)


def _attn_body(q0, q1, q2, k0, k1, k2, v0, v1, v2, y_ref,
               qc, kc, vc, s_scr, p_scr, inv_scr, o0, o1, o2, l0, l1, l2):
    row = lax.broadcasted_iota(jnp.int32, (QB, 2 * QB), 0)
    col = lax.broadcasted_iota(jnp.int32, (QB, 2 * QB), 1)
    band = jnp.logical_and(col >= row, col <= row + QB)
    band_first = jnp.logical_and(band, col >= QB)
    scale = HEAD_DIM ** -0.5
    nt = (((1,), (1,)), ((), ()))
    zero_blk = jnp.zeros((QB, HEAD_DIM), BF16)
    for (_, dil), q_ref, k_ref, v_ref, o_ref, l_ref in zip(
            DIL_GROUPS, (q0, q1, q2), (k0, k1, k2), (v0, v1, v2), (o0, o1, o2), (l0, l1, l2)):
        L = SEQ // dil
        nblk = L // QB

        def tok_rows(r, c):
            if dil == 1:
                return pl.ds(c * QB, QB)
            return pl.ds(r + c * QB * dil, QB, stride=dil)

        for r in range(dil):
            src = pl.ds(0, L) if dil == 1 else pl.ds(r, L, stride=dil)
            base = r * (L + QB)
            qc[r * L:(r + 1) * L, :] = q_ref[src, :].astype(BF16)
            kc[base:base + QB, :] = zero_blk
            vc[base:base + QB, :] = zero_blk
            kc[base + QB:base + QB + L, :] = k_ref[src, :].astype(BF16)
            vc[base + QB:base + QB + L, :] = v_ref[src, :].astype(BF16)

        for r in range(dil):
            for c in range(nblk):
                t = r * nblk + c
                kk = kc[r * (L + QB) + c * QB:r * (L + QB) + (c + 2) * QB, :]
                s = lax.dot_general(qc[t * QB:(t + 1) * QB, :], kk, nt, preferred_element_type=F32) * scale
                s_scr[t * QB:(t + 1) * QB, :] = jnp.where(band_first if c == 0 else band, s, NEG_BIG)

        for r in range(dil):
            for u in range(L // SOFTMAX_ROWS):
                rows = slice(r * L + u * SOFTMAX_ROWS, r * L + (u + 1) * SOFTMAX_ROWS)
                s = s_scr[rows, :]
                m = jnp.max(s, axis=-1, keepdims=True)
                p = jnp.exp(s - m)
                den = jnp.sum(p, axis=-1, keepdims=True)
                p_scr[rows, :] = p.astype(BF16)
                inv_scr[rows, :] = jnp.broadcast_to(1.0 / den, (SOFTMAX_ROWS, HEAD_DIM))
                tok = (pl.ds(u * SOFTMAX_ROWS, SOFTMAX_ROWS) if dil == 1
                       else pl.ds(r + u * SOFTMAX_ROWS * dil, SOFTMAX_ROWS, stride=dil))
                l_ref[tok, :] = jnp.broadcast_to(m + jnp.log(den), (SOFTMAX_ROWS, HEAD_DIM))

        for r in range(dil):
            for c in range(nblk):
                t = r * nblk + c
                vv = vc[r * (L + QB) + c * QB:r * (L + QB) + (c + 2) * QB, :]
                o = jnp.dot(p_scr[t * QB:(t + 1) * QB, :], vv, preferred_element_type=F32)
                o_ref[tok_rows(r, c), :] = o * inv_scr[t * QB:(t + 1) * QB, :]
    for u in range(SEQ // MERGE_ROWS):
        rows = slice(u * MERGE_ROWS, (u + 1) * MERGE_ROWS)
        y_ref[rows, :] = _combine_groups(o0[rows, :], o1[rows, :], o2[rows, :],
                                         l0[rows, :], l1[rows, :], l2[rows, :]).astype(BF16)


def _attn(proj):
    def seg(off, g):
        base = (off + g * C_WIDTH) // HEAD_DIM
        return pl.BlockSpec((SEQ, HEAD_DIM), lambda b, h: (b, base + h))

    specs = [seg(off, g) for off in (OFF_Q, OFF_K, OFF_V) for g in range(N_DIL)]
    return pl.pallas_call(
        _attn_body,
        grid=(BATCH, N_C_HEADS),
        in_specs=specs,
        out_specs=pl.BlockSpec((SEQ, HEAD_DIM), lambda b, h: (b, h)),
        out_shape=jax.ShapeDtypeStruct((BATCH * SEQ, C_WIDTH), BF16),
        scratch_shapes=[pltpu.VMEM((SEQ, HEAD_DIM), BF16),
                        pltpu.VMEM((2 * SEQ, HEAD_DIM), BF16),
                        pltpu.VMEM((2 * SEQ, HEAD_DIM), BF16),
                        pltpu.VMEM((SEQ, 2 * QB), F32),
                        pltpu.VMEM((SEQ, 2 * QB), BF16),
                        pltpu.VMEM((SEQ, HEAD_DIM), F32)]
                       + [pltpu.VMEM((SEQ, HEAD_DIM), F32)] * (2 * N_DIL),
        compiler_params=_params("arbitrary", "arbitrary"),
        name="attn",
    )(*([proj] * (3 * N_DIL)))


def _kv_pack_body(*refs, fixed):
    k_refs, v_refs, out_refs = refs[0:N_DIL], refs[N_DIL:2 * N_DIL], refs[-N_DIL:]
    for k_ref, v_ref, o_ref, is_fixed in zip(k_refs, v_refs, out_refs, fixed):
        def pack(k_ref=k_ref, v_ref=v_ref, o_ref=o_ref):
            for h in range(N_C_HEADS):
                sl = slice(h * HEAD_DIM, (h + 1) * HEAD_DIM)
                o_ref[:, 0, h, :] = k_ref[:, sl]
                o_ref[:, 1, h, :] = v_ref[:, sl]

        if is_fixed:
            pl.when(pl.program_id(1) == 0)(pack)
        else:
            pack()


N_MIX_IN, N_MIX_OUT = 10, 3


def _mix_ab_kv_body(*refs, fixed):
    n_kv_in = len(refs) - N_MIX_IN - N_MIX_OUT - N_DIL - 1
    mix_in, kv_in = refs[:N_MIX_IN], refs[N_MIX_IN:N_MIX_IN + 2 * N_DIL]
    outs = refs[N_MIX_IN + n_kv_in:-1]
    _mix_ab_body(*mix_in, *outs[:N_MIX_OUT], refs[-1])
    _kv_pack_body(*kv_in, *outs[N_MIX_OUT:], fixed=fixed)


def _mix_ab_kv(main, qkv, cw, lng, lnb, ws, bs_t, layer, prev):
    ts = TS_MIX
    nb = SEQ // ts
    blk = OFF_AB // W_A
    keeps = [min(win, SEQ) for win, _ in DIL_GROUPS]

    def seg(k):
        return pl.BlockSpec((ts, W_A), lambda b, s: (b * nb + s, blk + k))

    def full(shape):
        return pl.BlockSpec(shape, lambda b, s: (0,) * len(shape))

    rows = pl.BlockSpec((ts, W_A), lambda b, s: (b * nb + s, 0))
    mix_in = [seg(0), seg(1), seg(2), seg(3), seg(4),
              full((CONV_W, W_A)), full((1, W_B)), full((1, W_B)),
              full((N_B_GROUPS, CHUNK, CHUNK)), full((CHUNK, N_B_GROUPS))]
    mix_out = [rows, rows, pl.BlockSpec((1, 8, W_A), lambda b, s: (b, 0, 0))]
    mix_shapes = [jax.ShapeDtypeStruct((BATCH * SEQ, W_A), BF16),
                  jax.ShapeDtypeStruct((BATCH * SEQ, W_B), BF16),
                  jax.ShapeDtypeStruct((BATCH, 8, W_A), F32)]
    assert len(mix_in) == N_MIX_IN and len(mix_out) == N_MIX_OUT

    def src(off, g):
        keep, col = keeps[g], (off + g * C_WIDTH) // C_WIDTH
        if keep <= ts:
            nb = SEQ // keep
            return pl.BlockSpec((keep, C_WIDTH), lambda b, s: (b * nb + nb - 1, col))
        nb, first = SEQ // ts, (SEQ - keep) // ts
        return pl.BlockSpec((ts, C_WIDTH), lambda b, s: (b * nb + jnp.maximum(s, first), col))

    def dst(keep):
        if keep <= ts:
            return pl.BlockSpec((None, None, keep, 2, N_C_HEADS, HEAD_DIM), lambda b, s: (layer, b, 0, 0, 0, 0))
        first = (SEQ - keep) // ts
        return pl.BlockSpec((None, None, ts, 2, N_C_HEADS, HEAD_DIM),
                            lambda b, s: (layer, b, jnp.maximum(s - first, 0), 0, 0, 0))

    in_specs = mix_in + [src(OFF_K, g) for g in range(N_DIL)] + [src(OFF_V, g) for g in range(N_DIL)]
    args = [main] * 5 + [cw, lng, lnb, ws, bs_t] + [qkv] * (2 * N_DIL)
    aliases = {}
    if prev is not None:
        aliases = {len(in_specs) + g: N_MIX_OUT + g for g in range(N_DIL)}
        in_specs += [pl.BlockSpec(memory_space=pl.ANY)] * N_DIL
        args += list(prev)
    res = pl.pallas_call(
        functools.partial(_mix_ab_kv_body, fixed=tuple(keep <= ts for keep in keeps)),
        grid=(BATCH, nb),
        in_specs=in_specs,
        out_specs=mix_out + [dst(keep) for keep in keeps],
        out_shape=mix_shapes + [jax.ShapeDtypeStruct((DEPTH, BATCH, keep, 2, N_C_HEADS, HEAD_DIM), F32)
                                for keep in keeps],
        input_output_aliases=aliases,
        scratch_shapes=[pltpu.VMEM((ts + 8, W_A), F32)],
        compiler_params=_params("arbitrary", "arbitrary"),
        name="mix_ab_kv",
    )(*args)
    return res[0], res[1], res[2], res[N_MIX_OUT:]


def _mix_sample_body(p_ref, qkv_ref, st_ref, cw_ref, lng_ref, lnb_ref, wsc_ref, bsc_ref, c0_ref, c1_ref, c2_ref,
                     ya_ref, yb_ref, yc_ref, cs_ref, vn_ref):
    @pl.when(pl.program_id(0) == 0)
    def _():
        def seg(off, w):
            return p_ref[:, off:off + w]

        z = seg(OFF_AB + W_A, W_A) * seg(OFF_AB + 2 * W_A, W_A)
        st0 = st_ref[0, :, 0:W_A]
        st1 = st_ref[0, :, W_A:2 * W_A]
        conv = cw_ref[0:1, :] * st0 + cw_ref[1:2, :] * st1 + cw_ref[2:3, :] * z
        ya_ref[...] = (seg(OFF_AB, W_A) * conv).astype(BF16)
        cs_ref[:, 0:W_A] = st1
        cs_ref[:, W_A:2 * W_A] = z

        v = seg(OFF_AB + 3 * W_A + W_B, W_B)
        xc = v - jnp.mean(v, axis=-1, keepdims=True)
        var = jnp.mean(xc * xc, axis=-1, keepdims=True)
        vn = xc * lax.rsqrt(var + EPS) * lng_ref[...] + lnb_ref[...]
        vn_ref[...] = vn
        yb_ref[...] = (seg(OFF_AB + 3 * W_A, W_B) * (wsc_ref[...] * vn + bsc_ref[...])).astype(BF16)

    scale = HEAD_DIM ** -0.5
    outs, lses = [], []
    for g, c_ref in enumerate((c0_ref, c1_ref, c2_ref)):
        q = qkv_ref[g]
        kn = qkv_ref[N_DIL + g]
        vnew = qkv_ref[2 * N_DIL + g]
        kc = c_ref[:, 0]
        vc = c_ref[:, 1]
        s_c = jnp.sum(q[None] * kc, axis=-1, keepdims=True) * scale
        s_n = jnp.sum(q * kn, axis=-1, keepdims=True) * scale
        m = jnp.maximum(jnp.max(s_c, axis=0), s_n)
        p_c = jnp.exp(s_c - m[None])
        p_n = jnp.exp(s_n - m)
        den = jnp.sum(p_c, axis=0) + p_n
        outs.append((jnp.sum(p_c * vc, axis=0) + p_n * vnew) / den)
        lses.append(m + jnp.log(den))
    yc_ref[...] = _combine_groups(*outs, *lses)


def _mix_sample(proj_s, qkv_s, st, cw, lng, lnb, wsc, bsc, caches, layer):
    def full(shape):
        return pl.BlockSpec(shape, lambda b: (0,) * len(shape))

    n_keys = DIL_GROUPS[0][0] // DIL_GROUPS[0][1]
    cache_specs = [pl.BlockSpec((None, None, n_keys, None, 2, N_C_HEADS, HEAD_DIM),
                                lambda b: (layer, b, 0, 0, 0, 0, 0)) for _ in caches]
    qkv = qkv_s.reshape(DEC_BATCH, 3 * N_DIL, N_C_HEADS, HEAD_DIM)
    return pl.pallas_call(
        _mix_sample_body,
        grid=(DEC_BATCH,),
        in_specs=[full((DEC_BATCH, MAIN_W)),
                  pl.BlockSpec((None, 3 * N_DIL, N_C_HEADS, HEAD_DIM), lambda b: (b, 0, 0, 0)),
                  pl.BlockSpec((1, DEC_BATCH, 2 * W_A), lambda b: (layer, 0, 0)),
                  full((CONV_W, W_A)), full((1, W_B)), full((1, W_B)), full((1, W_B)), full((1, W_B))]
                 + cache_specs,
        out_specs=[full((DEC_BATCH, W_A)), full((DEC_BATCH, W_B)),
                   pl.BlockSpec((None, N_C_HEADS, HEAD_DIM), lambda b: (b, 0, 0)),
                   full((DEC_BATCH, 2 * W_A)), full((DEC_BATCH, W_B))],
        out_shape=[jax.ShapeDtypeStruct((DEC_BATCH, W_A), BF16),
                   jax.ShapeDtypeStruct((DEC_BATCH, W_B), BF16),
                   jax.ShapeDtypeStruct((DEC_BATCH, N_C_HEADS, HEAD_DIM), F32),
                   jax.ShapeDtypeStruct((DEC_BATCH, 2 * W_A), F32),
                   jax.ShapeDtypeStruct((DEC_BATCH, W_B), F32)],
        compiler_params=_params("arbitrary"),
        name="mix_sample",
    )(proj_s, qkv, st, cw, lng, lnb, wsc, bsc, *caches)


def _combine_groups(o0, o1, o2, l0, l1, l2):
    m = jnp.maximum(jnp.maximum(l0, l1), l2)
    e0, e1, e2 = jnp.exp(l0 - m), jnp.exp(l1 - m), jnp.exp(l2 - m)
    return (e0 * o0 + e1 * o1 + e2 * o2) / (e0 + e1 + e2)


def _merge_tail(x, ga, gb, gc, ya, yb, yc, wa_ref, wb_ref, wc_ref, wo_ref, gp_ref):
    mm = (jax.nn.sigmoid(ga) * jnp.dot(ya, wa_ref[...], preferred_element_type=F32)
          + jax.nn.sigmoid(gb) * jnp.dot(yb, wb_ref[...], preferred_element_type=F32)
          + jax.nn.sigmoid(gc) * jnp.dot(yc, wc_ref[...], preferred_element_type=F32))
    r = jnp.dot(mm.astype(BF16), wo_ref[...], preferred_element_type=F32)
    return x + _rms(r, gp_ref[...])


def _merge_body(ga_ref, gb_ref, gc_ref, ya_ref, yb_ref, yc_ref, x_ref,
                ms_ref, yas_ref, ybs_ref, ycs_ref, xs_ref,
                wa_ref, wb_ref, wc_ref, wo_ref, gp_ref, out_ref, outs_ref):
    weights = (wa_ref, wb_ref, wc_ref, wo_ref, gp_ref)
    out_ref[...] = _merge_tail(x_ref[...], ga_ref[...].astype(F32), gb_ref[...].astype(F32),
                               gc_ref[...].astype(F32), ya_ref[...], yb_ref[...], yc_ref[...], *weights)

    @pl.when(pl.program_id(0) == 0)
    def _():
        gates = [ms_ref[:, k * D_MODEL:(k + 1) * D_MODEL] for k in range(3)]
        outs_ref[...] = _merge_tail(xs_ref[...], *gates, yas_ref[...], ybs_ref[...],
                                    ycs_ref[...].astype(BF16), *weights)


def _merge(proj, ya, yb, yc, x, main_s, ya_s, yb_s, yc_s, xs, wa, wb, wc, wo, gp, layer, tm):
    m = x.shape[0]

    def rows(w, k=0):
        return pl.BlockSpec((tm, w), lambda i: (i, k))

    def whole(a):
        return pl.BlockSpec(a.shape, lambda i: (0, 0))

    def resident(k):
        return pl.BlockSpec((k, D_MODEL), lambda i: (0, 0), pipeline_mode=pl.Buffered(1))

    return pl.pallas_call(
        _merge_body,
        grid=(m // tm,),
        in_specs=[rows(D_MODEL, 0), rows(D_MODEL, 1), rows(D_MODEL, 2), rows(W_A), rows(W_B), rows(C_WIDTH),
                  rows(D_MODEL),
                  whole(main_s), whole(ya_s), whole(yb_s), whole(yc_s), whole(xs),
                  resident(W_A), resident(W_B), resident(C_WIDTH), resident(D_MODEL),
                  pl.BlockSpec((None, 1, D_MODEL), lambda i: (layer, 0, 0), pipeline_mode=pl.Buffered(1))],
        out_specs=[rows(D_MODEL), whole(xs)],
        out_shape=[jax.ShapeDtypeStruct((m, D_MODEL), F32), jax.ShapeDtypeStruct(xs.shape, F32)],
        compiler_params=_params("arbitrary"),
        name="merge",
    )(proj, proj, proj, ya, yb, yc, x, main_s, ya_s, yb_s, yc_s, xs, wa, wb, wc, wo, gp)


def _swiglu_part(h, wg_ref, wu_ref, wo_ref):
    gate = jnp.dot(h, wg_ref[...], preferred_element_type=F32)
    up = jnp.dot(h, wu_ref[...], preferred_element_type=F32)
    act = (gate * jax.nn.sigmoid(gate) * up).astype(BF16)
    return jnp.dot(act, wo_ref[...], preferred_element_type=F32)


def _ffn_body(x_ref, g1_ref, wg_ref, wu_ref, wo_ref, g2_ref, xs_ref, *rest, n_cast):
    cast_src = rest[:n_cast]
    out_ref, outs_ref = rest[n_cast:n_cast + 2]
    cast_dst = rest[n_cast + 2:2 * n_cast + 2]
    h_ref, hs_ref = rest[-2:]
    i = pl.program_id(0)
    j = pl.program_id(1)
    last = pl.num_programs(1) - 1

    @pl.when(j == 0)
    def _():
        h_ref[...] = _rms(x_ref[...], g1_ref[...]).astype(BF16)
        out_ref[...] = jnp.zeros_like(out_ref)

    _cast_blocks(cast_src, cast_dst)
    out_ref[...] += _swiglu_part(h_ref[...], wg_ref, wu_ref, wo_ref)

    @pl.when(j == last)
    def _():
        out_ref[...] = x_ref[...] + _rms(out_ref[...], g2_ref[...])

    @pl.when(i == 0)
    def _():
        @pl.when(j == 0)
        def _():
            hs_ref[...] = _rms(xs_ref[...], g1_ref[...]).astype(BF16)
            outs_ref[...] = jnp.zeros_like(outs_ref)

        outs_ref[...] += _swiglu_part(hs_ref[...], wg_ref, wu_ref, wo_ref)

        @pl.when(j == last)
        def _():
            outs_ref[...] = xs_ref[...] + _rms(outs_ref[...], g2_ref[...])


def _ffn(x, xs, g1, w_in, w_out, g2, layer, tm, tf, casts=()):
    m = x.shape[0]
    ns = xs.shape[0]
    nj = D_FF // tf
    cast_specs = [_cast_specs(src, layer + 1, rows, nj, (m // tm) * nj) for src, rows in casts]
    res = pl.pallas_call(
        functools.partial(_ffn_body, n_cast=len(casts)),
        grid=(m // tm, nj),
        in_specs=[
            pl.BlockSpec((tm, D_MODEL), lambda i, j: (i, 0)),
            pl.BlockSpec((None, 1, D_MODEL), lambda i, j: (layer, 0, 0)),
            pl.BlockSpec((D_MODEL, tf), lambda i, j: (0, j)),
            pl.BlockSpec((D_MODEL, tf), lambda i, j: (0, nj + j)),
            pl.BlockSpec((tf, D_MODEL), lambda i, j: (j, 0)),
            pl.BlockSpec((None, 1, D_MODEL), lambda i, j: (layer, 0, 0)),
            pl.BlockSpec((ns, D_MODEL), lambda i, j: (0, 0)),
        ] + [c[0] for c in cast_specs],
        out_specs=[pl.BlockSpec((tm, D_MODEL), lambda i, j: (i, 0)),
                   pl.BlockSpec((ns, D_MODEL), lambda i, j: (0, 0))] + [c[1] for c in cast_specs],
        out_shape=[jax.ShapeDtypeStruct((m, D_MODEL), F32),
                   jax.ShapeDtypeStruct((ns, D_MODEL), F32)] + [c[2] for c in cast_specs],
        scratch_shapes=[pltpu.VMEM((tm, D_MODEL), BF16), pltpu.VMEM((ns, D_MODEL), BF16)],
        compiler_params=_params("arbitrary", "arbitrary"),
        name="ffn",
    )(x, g1, w_in, w_in, w_out, g2, xs, *[src for src, _ in casts])
    return res[0], res[1], res[2:]


TM_INPROJ = 1024
TS_MIX = 512
TM_MERGE = 256
TM_FFN = 512
TF_FFN = 512
CAST_ROWS_FFN_IN = 16
CAST_ROWS_FFN_OUT = 64
CAST_ROWS_W_IN = 16
CAST_ROWS_MERGE = 16


def kernel(x_prompt, x_sample, state_conv, cache_kv_w128, cache_kv_w512, cache_kv_w2048, g_pre_mix, w_in, conv_w, ln_g, ln_b, w_s, b_s, w_a_out, w_b_out, w_c_out, w_o, g_post_mix, g_pre_ffn, w_ffn_in, w_ffn_out, g_post_ffn):
    w_in_l = w_in[0].astype(BF16)
    inproj_casts = ((w_ffn_in, CAST_ROWS_FFN_IN), (w_ffn_out, CAST_ROWS_FFN_OUT),
                    (w_a_out, CAST_ROWS_MERGE), (w_b_out, CAST_ROWS_MERGE), (w_c_out, CAST_ROWS_MERGE),
                    (w_o, CAST_ROWS_MERGE))
    g_pre_mix, g_post_mix, g_pre_ffn, g_post_ffn = (
        g.reshape(DEPTH, 1, D_MODEL) for g in (g_pre_mix, g_post_mix, g_pre_ffn, g_post_ffn))

    tabs_p = _rope_tables(jnp.arange(SEQ, dtype=jnp.int32))
    tabs_s = _rope_tables(jnp.full((DEC_BATCH,), PAST_LEN, dtype=jnp.int32))

    n_keys = DIL_GROUPS[0][0] // DIL_GROUPS[0][1]
    caches = tuple(c.reshape(DEPTH, DEC_BATCH, n_keys, dil, 2, N_C_HEADS, HEAD_DIM)
                   for c, (_, dil) in zip((cache_kv_w128, cache_kv_w512, cache_kv_w2048), DIL_GROUPS))
    st_all = state_conv.reshape(DEPTH, DEC_BATCH, (CONV_W - 1) * W_A)
    bs_t = jnp.swapaxes(b_s, 1, 2)
    wsc = jnp.repeat(w_s[:, :, 0, 0], B_GROUP_W, axis=1)
    bsc = jnp.repeat(b_s[:, :, 0], B_GROUP_W, axis=1)

    xp = x_prompt.reshape(BATCH * SEQ, D_MODEL)
    xs = x_sample.reshape(DEC_BATCH, D_MODEL)
    conv_p, conv_s, vchunk_s = [], [], []
    kv_p = None
    kv_s = [[] for _ in DIL_GROUPS]

    def row(a, l):
        return a[l][None, :]

    for l in range(DEPTH):
        proj, qkv, proj_s, qkv_s, wfi_l, wfo_l, wa_l, wb_l, wc_l, wo_l = _inproj(
            xp, xs, g_pre_mix, w_in_l, l, tabs_p, tabs_s, TM_INPROJ, casts=inproj_casts)
        ya, yb, ctail, kv_p = _mix_ab_kv(proj, qkv, conv_w[l], row(ln_g, l), row(ln_b, l), w_s[l], bs_t[l], l, kv_p)
        yc = _attn(qkv)
        conv_p.append(ctail[:, 8 - (CONV_W - 1):])
        ya_s, yb_s, yc_s, cst, vn = _mix_sample(proj_s, qkv_s, st_all, conv_w[l], row(ln_g, l), row(ln_b, l),
                                                row(wsc, l), row(bsc, l), caches, l)
        for g in range(N_DIL):
            k = qkv_s[:, OFF_K + g * C_WIDTH:OFF_K + (g + 1) * C_WIDTH]
            v = qkv_s[:, OFF_V + g * C_WIDTH:OFF_V + (g + 1) * C_WIDTH]
            kv_s[g].append(jnp.stack([k.reshape(DEC_BATCH, 1, N_C_HEADS, HEAD_DIM),
                                      v.reshape(DEC_BATCH, 1, N_C_HEADS, HEAD_DIM)], axis=2))
        conv_s.append(cst.reshape(DEC_BATCH, CONV_W - 1, W_A))
        vchunk_s.append(vn.reshape(DEC_BATCH, 1, W_B))
        xp, xs = _merge(proj, ya, yb, yc, xp, proj_s, ya_s, yb_s, yc_s.reshape(DEC_BATCH, C_WIDTH), xs,
                        wa_l, wb_l, wc_l, wo_l, g_post_mix, l, TM_MERGE)
        next_casts = ((w_in, CAST_ROWS_W_IN),) if l + 1 < DEPTH else ()
        xp, xs, w_in_next = _ffn(xp, xs, g_pre_ffn, wfi_l, wfo_l, g_post_ffn, l, TM_FFN, TF_FFN, casts=next_casts)
        if w_in_next:
            w_in_l = w_in_next[0]

    return (xp.reshape(BATCH, SEQ, D_MODEL), xs.reshape(DEC_BATCH, 1, D_MODEL),
            jnp.stack(conv_p, axis=0),
            kv_p[0], kv_p[1], kv_p[2],
            jnp.stack(conv_s, axis=0),
            jnp.stack(kv_s[0], axis=0), jnp.stack(kv_s[1], axis=0), jnp.stack(kv_s[2], axis=0),
            jnp.stack(vchunk_s, axis=0))
```

```python
import functools

import jax
import jax.numpy as jnp
from jax import lax
from jax.experimental import pallas as pl
from jax.experimental.pallas import tpu as pltpu

D_MODEL = 2048
BATCH = 4
SEQ = 2048
DEPTH = 4
DEC_BATCH = 8
PAST_LEN = 16384
W_A = 1024
CONV_W = 3
W_B = 1024
CHUNK = 128
N_B_GROUPS = 4
B_GROUP_W = W_B // N_B_GROUPS
N_C_HEADS = 4
HEAD_DIM = 128
ROT_DIM = HEAD_DIM // 4
ROPE_THETA = 500000.0
DIL_GROUPS = ((128, 1), (512, 4), (2048, 16))
N_DIL = len(DIL_GROUPS)
QB = 128
C_WIDTH = N_C_HEADS * HEAD_DIM
QKV_W = N_DIL * C_WIDTH
D_FF = ((-(-8 * D_MODEL // 3) + 255) // 256) * 256
IN_WIDTH = 3 * W_A + 2 * W_B + 3 * QKV_W + 3 * D_MODEL
EPS = 1e-6

GATE_W = 3 * D_MODEL
OFF_AB = GATE_W
MAIN_W = GATE_W + 3 * W_A + 2 * W_B
OFF_Q, OFF_K, OFF_V = 0, QKV_W, 2 * QKV_W
ORIG_GATE_OFF = IN_WIDTH - GATE_W

ROPE_ROWS = 256
SOFTMAX_ROWS = 32
MERGE_ROWS = 64
V7X_VMEM_LIMIT = 60 * 1024 * 1024
NEG_BIG = -1e30

F32 = jnp.float32
BF16 = jnp.bfloat16


def _params(*sem):
    return pltpu.CompilerParams(dimension_semantics=sem, vmem_limit_bytes=V7X_VMEM_LIMIT)


def _rms(x, g):
    return x * lax.rsqrt(jnp.mean(x * x, axis=-1, keepdims=True) + EPS) * g


def _cast_specs(src, layer, rows, steps_per_row_tile, n_steps):
    _, r, c = src.shape
    nblk = r // rows
    assert r % rows == 0 and nblk <= n_steps

    def blk(i, j):
        return jnp.minimum(i * steps_per_row_tile + j, nblk - 1)

    return (pl.BlockSpec((None, rows, c), lambda i, j: (layer, blk(i, j), 0)),
            pl.BlockSpec((rows, c), lambda i, j: (blk(i, j), 0)),
            jax.ShapeDtypeStruct((r, c), BF16))


def _cast_blocks(src_refs, dst_refs):
    for s_ref, d_ref in zip(src_refs, dst_refs):
        d_ref[...] = s_ref[...].astype(BF16)


def _rope(a, c, s1, s2):
    return a * c + pltpu.roll(a, HEAD_DIM - ROT_DIM // 2, 1) * s1 + pltpu.roll(a, ROT_DIM // 2, 1) * s2


def _inproj_body(x_ref, g_ref, w0_ref, w1_ref, c_ref, s1_ref, s2_ref, xs_ref, cs_ref, s1s_ref, s2s_ref, *rest,
                 main_steps, rope_steps, odd_tiles, n_cast):
    cast_src = rest[:n_cast]
    main_ref, qkv_ref, mains_ref, qkvs_ref = rest[n_cast:n_cast + 4]
    cast_dst = rest[n_cast + 4:2 * n_cast + 4]
    h_ref, hs_ref = rest[-2:]
    i = pl.program_id(0)
    j = pl.program_id(1)
    tn = w0_ref.shape[1]
    heads = tn // HEAD_DIM
    halves = ((w0_ref, slice(0, tn)), (w1_ref, slice(tn, 2 * tn)))
    with_samples = i == 0

    @pl.when(j == 0)
    def _():
        h_ref[...] = _rms(x_ref[...], g_ref[...]).astype(BF16)

    @pl.when(jnp.logical_and(with_samples, j == 0))
    def _():
        hs_ref[...] = _rms(xs_ref[...], g_ref[...]).astype(BF16)

    @pl.when(j < main_steps)
    def _():
        _cast_blocks(cast_src, cast_dst)
        for w_ref, cols in halves:
            main_ref[:, cols] = jnp.dot(h_ref[...], w_ref[...], preferred_element_type=F32).astype(main_ref.dtype)

        @pl.when(with_samples)
        def _():
            for w_ref, cols in halves:
                mains_ref[j, :, cols] = jnp.dot(hs_ref[...], w_ref[...], preferred_element_type=F32)

    @pl.when(jnp.logical_and(j >= main_steps, j < main_steps + rope_steps))
    def _():
        _cast_blocks(cast_src, cast_dst)
        tm = x_ref.shape[0]
        rc = min(tm, ROPE_ROWS)
        for w_ref, cols in halves:
            for r in range(tm // rc):
                rs = slice(r * rc, (r + 1) * rc)
                acc = jnp.dot(h_ref[rs, :], w_ref[...], preferred_element_type=F32)
                for h in range(heads):
                    sl = slice(cols.start + h * HEAD_DIM, cols.start + (h + 1) * HEAD_DIM)
                    qkv_ref[rs, sl] = _rope(acc[:, h * HEAD_DIM:(h + 1) * HEAD_DIM],
                                            c_ref[rs, :], s1_ref[rs, :], s2_ref[rs, :])

        @pl.when(with_samples)
        def _():
            for w_ref, cols in halves:
                acc = jnp.dot(hs_ref[...], w_ref[...], preferred_element_type=F32)
                for h in range(heads):
                    sl = slice(cols.start + h * HEAD_DIM, cols.start + (h + 1) * HEAD_DIM)
                    qkvs_ref[j - main_steps, :, sl] = _rope(acc[:, h * HEAD_DIM:(h + 1) * HEAD_DIM],
                                                            cs_ref[...], s1s_ref[...], s2s_ref[...])

    def plain(tiles):
        _cast_blocks(cast_src, cast_dst)
        for w_ref, cols in tiles:
            qkv_ref[:, cols] = jnp.dot(h_ref[...], w_ref[...], preferred_element_type=F32)

        @pl.when(with_samples)
        def _():
            for w_ref, cols in tiles:
                qkvs_ref[j - main_steps, :, cols] = jnp.dot(hs_ref[...], w_ref[...], preferred_element_type=F32)
            for _, cols in halves[len(tiles):]:
                qkvs_ref[j - main_steps, :, cols] = jnp.zeros((xs_ref.shape[0], tn), F32)

    last = pl.num_programs(1) - 1
    pl.when(jnp.logical_and(j >= main_steps + rope_steps, j < last))(lambda: plain(halves))
    pl.when(j == last)(lambda: plain(halves[:1] if odd_tiles else halves))


def _inproj(x, xs, g, w, layer, tabs, tabs_s, tm, casts=()):
    m = x.shape[0]
    ns = xs.shape[0]
    tn = C_WIDTH
    n_tiles = IN_WIDTH // tn
    rot = ORIG_GATE_OFF // tn
    main_steps = MAIN_W // (2 * tn)
    rope_steps = 2 * QKV_W // (2 * tn)
    qkv_steps = pl.cdiv(3 * QKV_W, 2 * tn)
    steps = main_steps + qkv_steps
    tab_spec = pl.BlockSpec((tm, HEAD_DIM), lambda i, j: (i % (SEQ // tm), 0))
    tab_s_spec = pl.BlockSpec((ns, HEAD_DIM), lambda i, j: (0, 0))

    def w_spec(half):
        return pl.BlockSpec((D_MODEL, tn), lambda i, j: (0, (2 * j + half + rot) % n_tiles))

    def x_tile(i, j):
        return jnp.minimum(i + jnp.where(j >= steps // 2, 1, 0), m // tm - 1)

    cast_specs = [_cast_specs(src, layer, rows, steps, (m // tm) * steps) for src, rows in casts]
    res = pl.pallas_call(
        functools.partial(_inproj_body, main_steps=main_steps, rope_steps=rope_steps,
                          odd_tiles=(3 * QKV_W // tn) % 2 == 1, n_cast=len(casts)),
        grid=(m // tm, steps),
        in_specs=[
            pl.BlockSpec((tm, D_MODEL), lambda i, j: (x_tile(i, j), 0)),
            pl.BlockSpec((None, 1, D_MODEL), lambda i, j: (layer, 0, 0)),
            w_spec(0), w_spec(1),
            tab_spec, tab_spec, tab_spec,
            pl.BlockSpec((ns, D_MODEL), lambda i, j: (0, 0)),
            tab_s_spec, tab_s_spec, tab_s_spec,
        ] + [c[0] for c in cast_specs],
        out_specs=[pl.BlockSpec((tm, 2 * tn), lambda i, j: (i, jnp.minimum(j, main_steps - 1))),
                   pl.BlockSpec((tm, 2 * tn), lambda i, j: (i, jnp.maximum(j - main_steps, 0))),
                   pl.BlockSpec((main_steps, ns, 2 * tn), lambda i, j: (0, 0, 0)),
                   pl.BlockSpec((qkv_steps, ns, 2 * tn), lambda i, j: (0, 0, 0))]
                  + [c[1] for c in cast_specs],
        out_shape=[jax.ShapeDtypeStruct((m, MAIN_W), BF16),
                   jax.ShapeDtypeStruct((m, 3 * QKV_W), F32),
                   jax.ShapeDtypeStruct((main_steps, ns, 2 * tn), F32),
                   jax.ShapeDtypeStruct((qkv_steps, ns, 2 * tn), F32)] + [c[2] for c in cast_specs],
        scratch_shapes=[pltpu.VMEM((tm, D_MODEL), BF16), pltpu.VMEM((ns, D_MODEL), BF16)],
        compiler_params=_params("arbitrary", "arbitrary"),
        name="inproj",
    )(x, g, w, w, *tabs, xs, *tabs_s, *[src for src, _ in casts])
    main_s = jnp.swapaxes(res[2], 0, 1).reshape(ns, MAIN_W)
    qkv_s = jnp.swapaxes(res[3], 0, 1).reshape(ns, qkv_steps * 2 * tn)[:, :3 * QKV_W]
    return (res[0], res[1], main_s, qkv_s, *res[4:])


def _rope_tables(pos):
    inv_freq = ROPE_THETA ** (-jnp.arange(0, ROT_DIM, 2, dtype=jnp.float32) / ROT_DIM)
    ang = pos.astype(jnp.float32)[:, None] * inv_freq[None, :]
    cos, sin = jnp.cos(ang), jnp.sin(ang)
    n = pos.shape[0]
    half = ROT_DIM // 2
    rest = HEAD_DIM - ROT_DIM
    c = jnp.concatenate([cos, cos, jnp.ones((n, rest), F32)], axis=1)
    s1 = jnp.concatenate([-sin, jnp.zeros((n, half + rest), F32)], axis=1)
    s2 = jnp.concatenate([jnp.zeros((n, half), F32), sin, jnp.zeros((n, rest), F32)], axis=1)
    return c, s1, s2


def _mix_ab_body(ab_ref, ac_ref, ax_ref, bu_ref, bv_ref, cw_ref, lng_ref, lnb_ref, ws_ref, bs_ref,
                 ya_ref, yb_ref, cs_ref, zs_ref):
    ts = ab_ref.shape[0]

    @pl.when(pl.program_id(1) == 0)
    def _():
        zs_ref[0:8, :] = jnp.zeros((8, W_A), F32)

    z = ac_ref[...].astype(F32) * ax_ref[...].astype(F32)
    zs_ref[8:8 + ts, :] = z
    z1 = zs_ref[7:7 + ts, :]
    z2 = zs_ref[6:6 + ts, :]
    conv = cw_ref[0:1, :] * z2 + cw_ref[1:2, :] * z1 + cw_ref[2:3, :] * z
    ya_ref[...] = (ab_ref[...].astype(F32) * conv).astype(BF16)
    tail = zs_ref[ts:ts + 8, :]
    zs_ref[0:8, :] = tail
    cs_ref[0] = tail

    v = bv_ref[...].astype(F32)
    xc = v - jnp.mean(v, axis=-1, keepdims=True)
    var = jnp.mean(xc * xc, axis=-1, keepdims=True)
    vn = (xc * lax.rsqrt(var + EPS) * lng_ref[...] + lnb_ref[...]).astype(BF16)
    row = lax.broadcasted_iota(jnp.int32, (CHUNK, CHUNK), 0)
    col = lax.broadcasted_iota(jnp.int32, (CHUNK, CHUNK), 1)
    for g in range(N_B_GROUPS):
        wg = jnp.where(row >= col, ws_ref[g], 0.0).astype(BF16)
        bcol = bs_ref[:, g:g + 1]
        gs = slice(g * B_GROUP_W, (g + 1) * B_GROUP_W)
        for c in range(ts // CHUNK):
            rs = slice(c * CHUNK, (c + 1) * CHUNK)
            sg = jnp.dot(wg, vn[rs, gs], preferred_element_type=F32) + bcol
            yb_ref[rs, gs] = (bu_ref[rs, gs].astype(F32) * sg).astype(BF16)


def _attn_body(q0, q1, q2, k0, k1, k2, v0, v1, v2, y_ref,
               qc, kc, vc, s_scr, p_scr, inv_scr, o0, o1, o2, l0, l1, l2):
    row = lax.broadcasted_iota(jnp.int32, (QB, 2 * QB), 0)
    col = lax.broadcasted_iota(jnp.int32, (QB, 2 * QB), 1)
    band = jnp.logical_and(col >= row, col <= row + QB)
    band_first = jnp.logical_and(band, col >= QB)
    scale = HEAD_DIM ** -0.5
    nt = (((1,), (1,)), ((), ()))
    zero_blk = jnp.zeros((QB, HEAD_DIM), BF16)
    for (_, dil), q_ref, k_ref, v_ref, o_ref, l_ref in zip(
            DIL_GROUPS, (q0, q1, q2), (k0, k1, k2), (v0, v1, v2), (o0, o1, o2), (l0, l1, l2)):
        L = SEQ // dil
        nblk = L // QB

        def tok_rows(r, c):
            if dil == 1:
                return pl.ds(c * QB, QB)
            return pl.ds(r + c * QB * dil, QB, stride=dil)

        for r in range(dil):
            src = pl.ds(0, L) if dil == 1 else pl.ds(r, L, stride=dil)
            base = r * (L + QB)
            qc[r * L:(r + 1) * L, :] = q_ref[src, :].astype(BF16)
            kc[base:base + QB, :] = zero_blk
            vc[base:base + QB, :] = zero_blk
            kc[base + QB:base + QB + L, :] = k_ref[src, :].astype(BF16)
            vc[base + QB:base + QB + L, :] = v_ref[src, :].astype(BF16)

        for r in range(dil):
            for c in range(nblk):
                t = r * nblk + c
                kk = kc[r * (L + QB) + c * QB:r * (L + QB) + (c + 2) * QB, :]
                s = lax.dot_general(qc[t * QB:(t + 1) * QB, :], kk, nt, preferred_element_type=F32) * scale
                s_scr[t * QB:(t + 1) * QB, :] = jnp.where(band_first if c == 0 else band, s, NEG_BIG)

        for r in range(dil):
            for u in range(L // SOFTMAX_ROWS):
                rows = slice(r * L + u * SOFTMAX_ROWS, r * L + (u + 1) * SOFTMAX_ROWS)
                s = s_scr[rows, :]
                m = jnp.max(s, axis=-1, keepdims=True)
                p = jnp.exp(s - m)
                den = jnp.sum(p, axis=-1, keepdims=True)
                p_scr[rows, :] = p.astype(BF16)
                inv_scr[rows, :] = jnp.broadcast_to(1.0 / den, (SOFTMAX_ROWS, HEAD_DIM))
                tok = (pl.ds(u * SOFTMAX_ROWS, SOFTMAX_ROWS) if dil == 1
                       else pl.ds(r + u * SOFTMAX_ROWS * dil, SOFTMAX_ROWS, stride=dil))
                l_ref[tok, :] = jnp.broadcast_to(m + jnp.log(den), (SOFTMAX_ROWS, HEAD_DIM))

        for r in range(dil):
            for c in range(nblk):
                t = r * nblk + c
                vv = vc[r * (L + QB) + c * QB:r * (L + QB) + (c + 2) * QB, :]
                o = jnp.dot(p_scr[t * QB:(t + 1) * QB, :], vv, preferred_element_type=F32)
                o_ref[tok_rows(r, c), :] = o * inv_scr[t * QB:(t + 1) * QB, :]
    for u in range(SEQ // MERGE_ROWS):
        rows = slice(u * MERGE_ROWS, (u + 1) * MERGE_ROWS)
        y_ref[rows, :] = _combine_groups(o0[rows, :], o1[rows, :], o2[rows, :],
                                         l0[rows, :], l1[rows, :], l2[rows, :]).astype(BF16)


def _attn(proj):
    def seg(off, g):
        base = (off + g * C_WIDTH) // HEAD_DIM
        return pl.BlockSpec((SEQ, HEAD_DIM), lambda b, h: (b, base + h))

    specs = [seg(off, g) for off in (OFF_Q, OFF_K, OFF_V) for g in range(N_DIL)]
    return pl.pallas_call(
        _attn_body,
        grid=(BATCH, N_C_HEADS),
        in_specs=specs,
        out_specs=pl.BlockSpec((SEQ, HEAD_DIM), lambda b, h: (b, h)),
        out_shape=jax.ShapeDtypeStruct((BATCH * SEQ, C_WIDTH), BF16),
        scratch_shapes=[pltpu.VMEM((SEQ, HEAD_DIM), BF16),
                        pltpu.VMEM((2 * SEQ, HEAD_DIM), BF16),
                        pltpu.VMEM((2 * SEQ, HEAD_DIM), BF16),
                        pltpu.VMEM((SEQ, 2 * QB), F32),
                        pltpu.VMEM((SEQ, 2 * QB), BF16),
                        pltpu.VMEM((SEQ, HEAD_DIM), F32)]
                       + [pltpu.VMEM((SEQ, HEAD_DIM), F32)] * (2 * N_DIL),
        compiler_params=_params("arbitrary", "arbitrary"),
        name="attn",
    )(*([proj] * (3 * N_DIL)))


def _kv_pack_body(*refs, fixed):
    k_refs, v_refs, out_refs = refs[0:N_DIL], refs[N_DIL:2 * N_DIL], refs[-N_DIL:]
    for k_ref, v_ref, o_ref, is_fixed in zip(k_refs, v_refs, out_refs, fixed):
        def pack(k_ref=k_ref, v_ref=v_ref, o_ref=o_ref):
            for h in range(N_C_HEADS):
                sl = slice(h * HEAD_DIM, (h + 1) * HEAD_DIM)
                o_ref[:, 0, h, :] = k_ref[:, sl]
                o_ref[:, 1, h, :] = v_ref[:, sl]

        if is_fixed:
            pl.when(pl.program_id(1) == 0)(pack)
        else:
            pack()


N_MIX_IN, N_MIX_OUT = 10, 3


def _mix_ab_kv_body(*refs, fixed):
    n_kv_in = len(refs) - N_MIX_IN - N_MIX_OUT - N_DIL - 1
    mix_in, kv_in = refs[:N_MIX_IN], refs[N_MIX_IN:N_MIX_IN + 2 * N_DIL]
    outs = refs[N_MIX_IN + n_kv_in:-1]
    _mix_ab_body(*mix_in, *outs[:N_MIX_OUT], refs[-1])
    _kv_pack_body(*kv_in, *outs[N_MIX_OUT:], fixed=fixed)


def _mix_ab_kv(main, qkv, cw, lng, lnb, ws, bs_t, layer, prev):
    ts = TS_MIX
    nb = SEQ // ts
    blk = OFF_AB // W_A
    keeps = [min(win, SEQ) for win, _ in DIL_GROUPS]

    def seg(k):
        return pl.BlockSpec((ts, W_A), lambda b, s: (b * nb + s, blk + k))

    def full(shape):
        return pl.BlockSpec(shape, lambda b, s: (0,) * len(shape))

    rows = pl.BlockSpec((ts, W_A), lambda b, s: (b * nb + s, 0))
    mix_in = [seg(0), seg(1), seg(2), seg(3), seg(4),
              full((CONV_W, W_A)), full((1, W_B)), full((1, W_B)),
              full((N_B_GROUPS, CHUNK, CHUNK)), full((CHUNK, N_B_GROUPS))]
    mix_out = [rows, rows, pl.BlockSpec((1, 8, W_A), lambda b, s: (b, 0, 0))]
    mix_shapes = [jax.ShapeDtypeStruct((BATCH * SEQ, W_A), BF16),
                  jax.ShapeDtypeStruct((BATCH * SEQ, W_B), BF16),
                  jax.ShapeDtypeStruct((BATCH, 8, W_A), F32)]
    assert len(mix_in) == N_MIX_IN and len(mix_out) == N_MIX_OUT

    def src(off, g):
        keep, col = keeps[g], (off + g * C_WIDTH) // C_WIDTH
        if keep <= ts:
            nb = SEQ // keep
            return pl.BlockSpec((keep, C_WIDTH), lambda b, s: (b * nb + nb - 1, col))
        nb, first = SEQ // ts, (SEQ - keep) // ts
        return pl.BlockSpec((ts, C_WIDTH), lambda b, s: (b * nb + jnp.maximum(s, first), col))

    def dst(keep):
        if keep <= ts:
            return pl.BlockSpec((None, None, keep, 2, N_C_HEADS, HEAD_DIM), lambda b, s: (layer, b, 0, 0, 0, 0))
        first = (SEQ - keep) // ts
        return pl.BlockSpec((None, None, ts, 2, N_C_HEADS, HEAD_DIM),
                            lambda b, s: (layer, b, jnp.maximum(s - first, 0), 0, 0, 0))

    in_specs = mix_in + [src(OFF_K, g) for g in range(N_DIL)] + [src(OFF_V, g) for g in range(N_DIL)]
    args = [main] * 5 + [cw, lng, lnb, ws, bs_t] + [qkv] * (2 * N_DIL)
    aliases = {}
    if prev is not None:
        aliases = {len(in_specs) + g: N_MIX_OUT + g for g in range(N_DIL)}
        in_specs += [pl.BlockSpec(memory_space=pl.ANY)] * N_DIL
        args += list(prev)
    res = pl.pallas_call(
        functools.partial(_mix_ab_kv_body, fixed=tuple(keep <= ts for keep in keeps)),
        grid=(BATCH, nb),
        in_specs=in_specs,
        out_specs=mix_out + [dst(keep) for keep in keeps],
        out_shape=mix_shapes + [jax.ShapeDtypeStruct((DEPTH, BATCH, keep, 2, N_C_HEADS, HEAD_DIM), F32)
                                for keep in keeps],
        input_output_aliases=aliases,
        scratch_shapes=[pltpu.VMEM((ts + 8, W_A), F32)],
        compiler_params=_params("arbitrary", "arbitrary"),
        name="mix_ab_kv",
    )(*args)
    return res[0], res[1], res[2], res[N_MIX_OUT:]


def _mix_sample_body(p_ref, qkv_ref, st_ref, cw_ref, lng_ref, lnb_ref, wsc_ref, bsc_ref, c0_ref, c1_ref, c2_ref,
                     ya_ref, yb_ref, yc_ref, cs_ref, vn_ref):
    @pl.when(pl.program_id(0) == 0)
    def _():
        def seg(off, w):
            return p_ref[:, off:off + w]

        z = seg(OFF_AB + W_A, W_A) * seg(OFF_AB + 2 * W_A, W_A)
        st0 = st_ref[0, :, 0:W_A]
        st1 = st_ref[0, :, W_A:2 * W_A]
        conv = cw_ref[0:1, :] * st0 + cw_ref[1:2, :] * st1 + cw_ref[2:3, :] * z
        ya_ref[...] = (seg(OFF_AB, W_A) * conv).astype(BF16)
        cs_ref[:, 0:W_A] = st1
        cs_ref[:, W_A:2 * W_A] = z

        v = seg(OFF_AB + 3 * W_A + W_B, W_B)
        xc = v - jnp.mean(v, axis=-1, keepdims=True)
        var = jnp.mean(xc * xc, axis=-1, keepdims=True)
        vn = xc * lax.rsqrt(var + EPS) * lng_ref[...] + lnb_ref[...]
        vn_ref[...] = vn
        yb_ref[...] = (seg(OFF_AB + 3 * W_A, W_B) * (wsc_ref[...] * vn + bsc_ref[...])).astype(BF16)

    scale = HEAD_DIM ** -0.5
    outs, lses = [], []
    for g, c_ref in enumerate((c0_ref, c1_ref, c2_ref)):
        q = qkv_ref[g]
        kn = qkv_ref[N_DIL + g]
        vnew = qkv_ref[2 * N_DIL + g]
        kc = c_ref[:, 0]
        vc = c_ref[:, 1]
        s_c = jnp.sum(q[None] * kc, axis=-1, keepdims=True) * scale
        s_n = jnp.sum(q * kn, axis=-1, keepdims=True) * scale
        m = jnp.maximum(jnp.max(s_c, axis=0), s_n)
        p_c = jnp.exp(s_c - m[None])
        p_n = jnp.exp(s_n - m)
        den = jnp.sum(p_c, axis=0) + p_n
        outs.append((jnp.sum(p_c * vc, axis=0) + p_n * vnew) / den)
        lses.append(m + jnp.log(den))
    yc_ref[...] = _combine_groups(*outs, *lses)


def _mix_sample(proj_s, qkv_s, st, cw, lng, lnb, wsc, bsc, caches, layer):
    def full(shape):
        return pl.BlockSpec(shape, lambda b: (0,) * len(shape))

    n_keys = DIL_GROUPS[0][0] // DIL_GROUPS[0][1]
    cache_specs = [pl.BlockSpec((None, None, n_keys, None, 2, N_C_HEADS, HEAD_DIM),
                                lambda b: (layer, b, 0, 0, 0, 0, 0)) for _ in caches]
    qkv = qkv_s.reshape(DEC_BATCH, 3 * N_DIL, N_C_HEADS, HEAD_DIM)
    return pl.pallas_call(
        _mix_sample_body,
        grid=(DEC_BATCH,),
        in_specs=[full((DEC_BATCH, MAIN_W)),
                  pl.BlockSpec((None, 3 * N_DIL, N_C_HEADS, HEAD_DIM), lambda b: (b, 0, 0, 0)),
                  pl.BlockSpec((1, DEC_BATCH, 2 * W_A), lambda b: (layer, 0, 0)),
                  full((CONV_W, W_A)), full((1, W_B)), full((1, W_B)), full((1, W_B)), full((1, W_B))]
                 + cache_specs,
        out_specs=[full((DEC_BATCH, W_A)), full((DEC_BATCH, W_B)),
                   pl.BlockSpec((None, N_C_HEADS, HEAD_DIM), lambda b: (b, 0, 0)),
                   full((DEC_BATCH, 2 * W_A)), full((DEC_BATCH, W_B))],
        out_shape=[jax.ShapeDtypeStruct((DEC_BATCH, W_A), BF16),
                   jax.ShapeDtypeStruct((DEC_BATCH, W_B), BF16),
                   jax.ShapeDtypeStruct((DEC_BATCH, N_C_HEADS, HEAD_DIM), F32),
                   jax.ShapeDtypeStruct((DEC_BATCH, 2 * W_A), F32),
                   jax.ShapeDtypeStruct((DEC_BATCH, W_B), F32)],
        compiler_params=_params("arbitrary"),
        name="mix_sample",
    )(proj_s, qkv, st, cw, lng, lnb, wsc, bsc, *caches)


def _combine_groups(o0, o1, o2, l0, l1, l2):
    m = jnp.maximum(jnp.maximum(l0, l1), l2)
    e0, e1, e2 = jnp.exp(l0 - m), jnp.exp(l1 - m), jnp.exp(l2 - m)
    return (e0 * o0 + e1 * o1 + e2 * o2) / (e0 + e1 + e2)


def _merge_tail(x, ga, gb, gc, ya, yb, yc, wa_ref, wb_ref, wc_ref, wo_ref, gp_ref):
    mm = (jax.nn.sigmoid(ga) * jnp.dot(ya, wa_ref[...], preferred_element_type=F32)
          + jax.nn.sigmoid(gb) * jnp.dot(yb, wb_ref[...], preferred_element_type=F32)
          + jax.nn.sigmoid(gc) * jnp.dot(yc, wc_ref[...], preferred_element_type=F32))
    r = jnp.dot(mm.astype(BF16), wo_ref[...], preferred_element_type=F32)
    return x + _rms(r, gp_ref[...])


def _merge_body(ga_ref, gb_ref, gc_ref, ya_ref, yb_ref, yc_ref, x_ref,
                ms_ref, yas_ref, ybs_ref, ycs_ref, xs_ref,
                wa_ref, wb_ref, wc_ref, wo_ref, gp_ref, out_ref, outs_ref):
    weights = (wa_ref, wb_ref, wc_ref, wo_ref, gp_ref)
    out_ref[...] = _merge_tail(x_ref[...], ga_ref[...].astype(F32), gb_ref[...].astype(F32),
                               gc_ref[...].astype(F32), ya_ref[...], yb_ref[...], yc_ref[...], *weights)

    @pl.when(pl.program_id(0) == 0)
    def _():
        gates = [ms_ref[:, k * D_MODEL:(k + 1) * D_MODEL] for k in range(3)]
        outs_ref[...] = _merge_tail(xs_ref[...], *gates, yas_ref[...], ybs_ref[...],
                                    ycs_ref[...].astype(BF16), *weights)


def _merge(proj, ya, yb, yc, x, main_s, ya_s, yb_s, yc_s, xs, wa, wb, wc, wo, gp, layer, tm):
    m = x.shape[0]

    def rows(w, k=0):
        return pl.BlockSpec((tm, w), lambda i: (i, k))

    def whole(a):
        return pl.BlockSpec(a.shape, lambda i: (0, 0))

    def resident(k):
        return pl.BlockSpec((k, D_MODEL), lambda i: (0, 0), pipeline_mode=pl.Buffered(1))

    return pl.pallas_call(
        _merge_body,
        grid=(m // tm,),
        in_specs=[rows(D_MODEL, 0), rows(D_MODEL, 1), rows(D_MODEL, 2), rows(W_A), rows(W_B), rows(C_WIDTH),
                  rows(D_MODEL),
                  whole(main_s), whole(ya_s), whole(yb_s), whole(yc_s), whole(xs),
                  resident(W_A), resident(W_B), resident(C_WIDTH), resident(D_MODEL),
                  pl.BlockSpec((None, 1, D_MODEL), lambda i: (layer, 0, 0), pipeline_mode=pl.Buffered(1))],
        out_specs=[rows(D_MODEL), whole(xs)],
        out_shape=[jax.ShapeDtypeStruct((m, D_MODEL), F32), jax.ShapeDtypeStruct(xs.shape, F32)],
        compiler_params=_params("arbitrary"),
        name="merge",
    )(proj, proj, proj, ya, yb, yc, x, main_s, ya_s, yb_s, yc_s, xs, wa, wb, wc, wo, gp)


def _swiglu_part(h, wg_ref, wu_ref, wo_ref):
    gate = jnp.dot(h, wg_ref[...], preferred_element_type=F32)
    up = jnp.dot(h, wu_ref[...], preferred_element_type=F32)
    act = (gate * jax.nn.sigmoid(gate) * up).astype(BF16)
    return jnp.dot(act, wo_ref[...], preferred_element_type=F32)


def _ffn_body(x_ref, g1_ref, wg_ref, wu_ref, wo_ref, g2_ref, xs_ref, *rest, n_cast):
    cast_src = rest[:n_cast]
    out_ref, outs_ref = rest[n_cast:n_cast + 2]
    cast_dst = rest[n_cast + 2:2 * n_cast + 2]
    h_ref, hs_ref, xk_ref = rest[-3:]
    i = pl.program_id(0)
    j = pl.program_id(1)
    last = pl.num_programs(1) - 1

    @pl.when(j == 0)
    def _():
        x = x_ref[...]
        xk_ref[...] = x
        h_ref[...] = _rms(x, g1_ref[...]).astype(BF16)
        out_ref[...] = jnp.zeros_like(out_ref)

    _cast_blocks(cast_src, cast_dst)
    out_ref[...] += _swiglu_part(h_ref[...], wg_ref, wu_ref, wo_ref)

    @pl.when(j == last)
    def _():
        out_ref[...] = xk_ref[...] + _rms(out_ref[...], g2_ref[...])

    @pl.when(i == 0)
    def _():
        @pl.when(j == 0)
        def _():
            hs_ref[...] = _rms(xs_ref[...], g1_ref[...]).astype(BF16)
            outs_ref[...] = jnp.zeros_like(outs_ref)

        outs_ref[...] += _swiglu_part(hs_ref[...], wg_ref, wu_ref, wo_ref)

        @pl.when(j == last)
        def _():
            outs_ref[...] = xs_ref[...] + _rms(outs_ref[...], g2_ref[...])


def _ffn(x, xs, g1, w_in, w_out, g2, layer, tm, tf, casts=()):
    m = x.shape[0]
    ns = xs.shape[0]
    nj = D_FF // tf
    cast_specs = [_cast_specs(src, layer + 1, rows, nj, (m // tm) * nj) for src, rows in casts]

    def x_tile(i, j):
        return jnp.minimum(i + jnp.where(j >= nj // 2, 1, 0), m // tm - 1)

    res = pl.pallas_call(
        functools.partial(_ffn_body, n_cast=len(casts)),
        grid=(m // tm, nj),
        in_specs=[
            pl.BlockSpec((tm, D_MODEL), lambda i, j: (x_tile(i, j), 0)),
            pl.BlockSpec((None, 1, D_MODEL), lambda i, j: (layer, 0, 0)),
            pl.BlockSpec((D_MODEL, tf), lambda i, j: (0, j)),
            pl.BlockSpec((D_MODEL, tf), lambda i, j: (0, nj + j)),
            pl.BlockSpec((tf, D_MODEL), lambda i, j: (j, 0)),
            pl.BlockSpec((None, 1, D_MODEL), lambda i, j: (layer, 0, 0)),
            pl.BlockSpec((ns, D_MODEL), lambda i, j: (0, 0)),
        ] + [c[0] for c in cast_specs],
        out_specs=[pl.BlockSpec((tm, D_MODEL), lambda i, j: (i, 0)),
                   pl.BlockSpec((ns, D_MODEL), lambda i, j: (0, 0))] + [c[1] for c in cast_specs],
        out_shape=[jax.ShapeDtypeStruct((m, D_MODEL), F32),
                   jax.ShapeDtypeStruct((ns, D_MODEL), F32)] + [c[2] for c in cast_specs],
        scratch_shapes=[pltpu.VMEM((tm, D_MODEL), BF16), pltpu.VMEM((ns, D_MODEL), BF16),
                        pltpu.VMEM((tm, D_MODEL), F32)],
        compiler_params=_params("arbitrary", "arbitrary"),
        name="ffn",
    )(x, g1, w_in, w_in, w_out, g2, xs, *[src for src, _ in casts])
    return res[0], res[1], res[2:]


TM_INPROJ = 1024
TS_MIX = 512
TM_MERGE = 256
TM_FFN = 512
TF_FFN = 512
CAST_ROWS_FFN_IN = 16
CAST_ROWS_FFN_OUT = 64
CAST_ROWS_W_IN = 16
CAST_ROWS_MERGE = 16


def kernel(x_prompt, x_sample, state_conv, cache_kv_w128, cache_kv_w512, cache_kv_w2048, g_pre_mix, w_in, conv_w, ln_g, ln_b, w_s, b_s, w_a_out, w_b_out, w_c_out, w_o, g_post_mix, g_pre_ffn, w_ffn_in, w_ffn_out, g_post_ffn):
    w_in_l = w_in[0].astype(BF16)
    inproj_casts = ((w_ffn_in, CAST_ROWS_FFN_IN), (w_ffn_out, CAST_ROWS_FFN_OUT),
                    (w_a_out, CAST_ROWS_MERGE), (w_b_out, CAST_ROWS_MERGE), (w_c_out, CAST_ROWS_MERGE),
                    (w_o, CAST_ROWS_MERGE))
    g_pre_mix, g_post_mix, g_pre_ffn, g_post_ffn = (
        g.reshape(DEPTH, 1, D_MODEL) for g in (g_pre_mix, g_post_mix, g_pre_ffn, g_post_ffn))

    tabs_p = _rope_tables(jnp.arange(SEQ, dtype=jnp.int32))
    tabs_s = _rope_tables(jnp.full((DEC_BATCH,), PAST_LEN, dtype=jnp.int32))

    n_keys = DIL_GROUPS[0][0] // DIL_GROUPS[0][1]
    caches = tuple(c.reshape(DEPTH, DEC_BATCH, n_keys, dil, 2, N_C_HEADS, HEAD_DIM)
                   for c, (_, dil) in zip((cache_kv_w128, cache_kv_w512, cache_kv_w2048), DIL_GROUPS))
    st_all = state_conv.reshape(DEPTH, DEC_BATCH, (CONV_W - 1) * W_A)
    bs_t = jnp.swapaxes(b_s, 1, 2)
    wsc = jnp.repeat(w_s[:, :, 0, 0], B_GROUP_W, axis=1)
    bsc = jnp.repeat(b_s[:, :, 0], B_GROUP_W, axis=1)

    xp = x_prompt.reshape(BATCH * SEQ, D_MODEL)
    xs = x_sample.reshape(DEC_BATCH, D_MODEL)
    conv_p, conv_s, vchunk_s = [], [], []
    kv_p = None
    kv_s = [[] for _ in DIL_GROUPS]

    def row(a, l):
        return a[l][None, :]

    for l in range(DEPTH):
        proj, qkv, proj_s, qkv_s, wfi_l, wfo_l, wa_l, wb_l, wc_l, wo_l = _inproj(
            xp, xs, g_pre_mix, w_in_l, l, tabs_p, tabs_s, TM_INPROJ, casts=inproj_casts)
        ya, yb, ctail, kv_p = _mix_ab_kv(proj, qkv, conv_w[l], row(ln_g, l), row(ln_b, l), w_s[l], bs_t[l], l, kv_p)
        yc = _attn(qkv)
        conv_p.append(ctail[:, 8 - (CONV_W - 1):])
        ya_s, yb_s, yc_s, cst, vn = _mix_sample(proj_s, qkv_s, st_all, conv_w[l], row(ln_g, l), row(ln_b, l),
                                                row(wsc, l), row(bsc, l), caches, l)
        for g in range(N_DIL):
            k = qkv_s[:, OFF_K + g * C_WIDTH:OFF_K + (g + 1) * C_WIDTH]
            v = qkv_s[:, OFF_V + g * C_WIDTH:OFF_V + (g + 1) * C_WIDTH]
            kv_s[g].append(jnp.stack([k.reshape(DEC_BATCH, 1, N_C_HEADS, HEAD_DIM),
                                      v.reshape(DEC_BATCH, 1, N_C_HEADS, HEAD_DIM)], axis=2))
        conv_s.append(cst.reshape(DEC_BATCH, CONV_W - 1, W_A))
        vchunk_s.append(vn.reshape(DEC_BATCH, 1, W_B))
        xp, xs = _merge(proj, ya, yb, yc, xp, proj_s, ya_s, yb_s, yc_s.reshape(DEC_BATCH, C_WIDTH), xs,
                        wa_l, wb_l, wc_l, wo_l, g_post_mix, l, TM_MERGE)
        next_casts = ((w_in, CAST_ROWS_W_IN),) if l + 1 < DEPTH else ()
        xp, xs, w_in_next = _ffn(xp, xs, g_pre_ffn, wfi_l, wfo_l, g_post_ffn, l, TM_FFN, TF_FFN, casts=next_casts)
        if w_in_next:
            w_in_l = w_in_next[0]

    return (xp.reshape(BATCH, SEQ, D_MODEL), xs.reshape(DEC_BATCH, 1, D_MODEL),
            jnp.stack(conv_p, axis=0),
            kv_p[0], kv_p[1], kv_p[2],
            jnp.stack(conv_s, axis=0),
            jnp.stack(kv_s[0], axis=0), jnp.stack(kv_s[1], axis=0), jnp.stack(kv_s[2], axis=0),
            jnp.stack(vchunk_s, axis=0))
```

```python
import functools

import jax
import jax.numpy as jnp
from jax import lax
from jax.experimental import pallas as pl
from jax.experimental.pallas import tpu as pltpu

D_MODEL = 2048
BATCH = 4
SEQ = 2048
DEPTH = 4
DEC_BATCH = 8
PAST_LEN = 16384
W_A = 1024
CONV_W = 3
W_B = 1024
CHUNK = 128
N_B_GROUPS = 4
B_GROUP_W = W_B // N_B_GROUPS
N_C_HEADS = 4
HEAD_DIM = 128
ROT_DIM = HEAD_DIM // 4
ROPE_THETA = 500000.0
DIL_GROUPS = ((128, 1), (512, 4), (2048, 16))
N_DIL = len(DIL_GROUPS)
QB = 128
C_WIDTH = N_C_HEADS * HEAD_DIM
QKV_W = N_DIL * C_WIDTH
D_FF = ((-(-8 * D_MODEL // 3) + 255) // 256) * 256
IN_WIDTH = 3 * W_A + 2 * W_B + 3 * QKV_W + 3 * D_MODEL
EPS = 1e-6

GATE_W = 3 * D_MODEL
OFF_AB = GATE_W
MAIN_W = GATE_W + 3 * W_A + 2 * W_B
OFF_Q, OFF_K, OFF_V = 0, QKV_W, 2 * QKV_W
ORIG_GATE_OFF = IN_WIDTH - GATE_W

ROPE_ROWS = 256
SOFTMAX_ROWS = 32
MERGE_ROWS = 64
V7X_VMEM_LIMIT = 60 * 1024 * 1024
NEG_BIG = -1e30

F32 = jnp.float32
BF16 = jnp.bfloat16


def _params(*sem):
    return pltpu.CompilerParams(dimension_semantics=sem, vmem_limit_bytes=V7X_VMEM_LIMIT)


def _rms(x, g):
    return x * lax.rsqrt(jnp.mean(x * x, axis=-1, keepdims=True) + EPS) * g


def _cast_specs(src, layer, rows, active_steps, n_row_tiles):
    _, r, c = src.shape
    nblk = r // rows
    assert r % rows == 0 and nblk <= n_row_tiles * active_steps

    def blk(i, j):
        return jnp.minimum(i * active_steps + jnp.minimum(j, active_steps - 1), nblk - 1)

    return (pl.BlockSpec((None, rows, c), lambda i, j: (layer, blk(i, j), 0)),
            pl.BlockSpec((rows, c), lambda i, j: (blk(i, j), 0)),
            jax.ShapeDtypeStruct((r, c), BF16))


def _cast_blocks(src_refs, dst_refs):
    for s_ref, d_ref in zip(src_refs, dst_refs):
        d_ref[...] = s_ref[...].astype(BF16)


def _rope(a, c, s1, s2):
    return a * c + pltpu.roll(a, HEAD_DIM - ROT_DIM // 2, 1) * s1 + pltpu.roll(a, ROT_DIM // 2, 1) * s2


def _inproj_body(x_ref, g_ref, w0_ref, w1_ref, c_ref, s1_ref, s2_ref, xs_ref, cs_ref, s1s_ref, s2s_ref, *rest,
                 main_steps, rope_steps, n_cast):
    cast_src = rest[:n_cast]
    main_ref, qkv_ref, mains_ref, qkvs_ref = rest[n_cast:n_cast + 4]
    cast_dst = rest[n_cast + 4:2 * n_cast + 4]
    h_ref, hs_ref = rest[-2:]
    i = pl.program_id(0)
    j = pl.program_id(1)
    tn = w0_ref.shape[1]
    heads = tn // HEAD_DIM
    halves = ((w0_ref, slice(0, tn)), (w1_ref, slice(tn, 2 * tn)))
    with_samples = i == 0

    @pl.when(j == 0)
    def _():
        h_ref[...] = _rms(x_ref[...], g_ref[...]).astype(BF16)

    @pl.when(jnp.logical_and(with_samples, j == 0))
    def _():
        hs_ref[...] = _rms(xs_ref[...], g_ref[...]).astype(BF16)

    @pl.when(j < main_steps)
    def _():
        _cast_blocks(cast_src, cast_dst)
        for w_ref, cols in halves:
            main_ref[:, cols] = jnp.dot(h_ref[...], w_ref[...], preferred_element_type=F32).astype(main_ref.dtype)

        @pl.when(with_samples)
        def _():
            for w_ref, cols in halves:
                mains_ref[j, :, cols] = jnp.dot(hs_ref[...], w_ref[...], preferred_element_type=F32)

    @pl.when(jnp.logical_and(j >= main_steps, j < main_steps + rope_steps))
    def _():
        tm = x_ref.shape[0]
        rc = min(tm, ROPE_ROWS)
        for w_ref, cols in halves:
            for r in range(tm // rc):
                rs = slice(r * rc, (r + 1) * rc)
                acc = jnp.dot(h_ref[rs, :], w_ref[...], preferred_element_type=F32)
                for h in range(heads):
                    sl = slice(cols.start + h * HEAD_DIM, cols.start + (h + 1) * HEAD_DIM)
                    qkv_ref[rs, sl] = _rope(acc[:, h * HEAD_DIM:(h + 1) * HEAD_DIM],
                                            c_ref[rs, :], s1_ref[rs, :], s2_ref[rs, :])

        @pl.when(with_samples)
        def _():
            for w_ref, cols in halves:
                acc = jnp.dot(hs_ref[...], w_ref[...], preferred_element_type=F32)
                for h in range(heads):
                    sl = slice(cols.start + h * HEAD_DIM, cols.start + (h + 1) * HEAD_DIM)
                    qkvs_ref[j - main_steps, :, sl] = _rope(acc[:, h * HEAD_DIM:(h + 1) * HEAD_DIM],
                                                            cs_ref[...], s1s_ref[...], s2s_ref[...])

    @pl.when(j >= main_steps + rope_steps)
    def _():
        for w_ref, cols in halves:
            qkv_ref[:, cols] = jnp.dot(h_ref[...], w_ref[...], preferred_element_type=F32)

        @pl.when(with_samples)
        def _():
            for w_ref, cols in halves:
                qkvs_ref[j - main_steps, :, cols] = jnp.dot(hs_ref[...], w_ref[...], preferred_element_type=F32)


def _inproj(x, xs, g, w, layer, tabs, tabs_s, tm, casts=()):
    m = x.shape[0]
    ns = xs.shape[0]
    tn = C_WIDTH
    n_tiles = IN_WIDTH // tn
    rot = ORIG_GATE_OFF // tn
    main_steps = MAIN_W // (2 * tn)
    rope_steps = 2 * QKV_W // (2 * tn)
    qkv_steps = pl.cdiv(3 * QKV_W, 2 * tn)
    steps = main_steps + qkv_steps
    tab_spec = pl.BlockSpec((tm, HEAD_DIM), lambda i, j: (i % (SEQ // tm), 0))
    tab_s_spec = pl.BlockSpec((ns, HEAD_DIM), lambda i, j: (0, 0))

    def w_spec(half):
        return pl.BlockSpec((D_MODEL, tn), lambda i, j: (0, (2 * j + half + rot) % n_tiles))

    def x_tile(i, j):
        return jnp.minimum(i + jnp.where(j >= steps // 2, 1, 0), m // tm - 1)

    cast_specs = [_cast_specs(src, layer, rows, main_steps, m // tm) for src, rows in casts]
    res = pl.pallas_call(
        functools.partial(_inproj_body, main_steps=main_steps, rope_steps=rope_steps, n_cast=len(casts)),
        grid=(m // tm, steps),
        in_specs=[
            pl.BlockSpec((tm, D_MODEL), lambda i, j: (x_tile(i, j), 0)),
            pl.BlockSpec((None, 1, D_MODEL), lambda i, j: (layer, 0, 0)),
            w_spec(0), w_spec(1),
            tab_spec, tab_spec, tab_spec,
            pl.BlockSpec((ns, D_MODEL), lambda i, j: (0, 0)),
            tab_s_spec, tab_s_spec, tab_s_spec,
        ] + [c[0] for c in cast_specs],
        out_specs=[pl.BlockSpec((tm, 2 * tn), lambda i, j: (i, jnp.minimum(j, main_steps - 1))),
                   pl.BlockSpec((tm, 2 * tn), lambda i, j: (i, jnp.maximum(j - main_steps, 0))),
                   pl.BlockSpec((main_steps, ns, 2 * tn), lambda i, j: (0, 0, 0)),
                   pl.BlockSpec((qkv_steps, ns, 2 * tn), lambda i, j: (0, 0, 0))]
                  + [c[1] for c in cast_specs],
        out_shape=[jax.ShapeDtypeStruct((m, MAIN_W), BF16),
                   jax.ShapeDtypeStruct((m, 3 * QKV_W), F32),
                   jax.ShapeDtypeStruct((main_steps, ns, 2 * tn), F32),
                   jax.ShapeDtypeStruct((qkv_steps, ns, 2 * tn), F32)] + [c[2] for c in cast_specs],
        scratch_shapes=[pltpu.VMEM((tm, D_MODEL), BF16), pltpu.VMEM((ns, D_MODEL), BF16)],
        compiler_params=_params("arbitrary", "arbitrary"),
        name="inproj",
    )(x, g, w, w, *tabs, xs, *tabs_s, *[src for src, _ in casts])
    main_s = jnp.swapaxes(res[2], 0, 1).reshape(ns, MAIN_W)
    qkv_s = jnp.swapaxes(res[3], 0, 1).reshape(ns, qkv_steps * 2 * tn)[:, :3 * QKV_W]
    return (res[0], res[1], main_s, qkv_s, *res[4:])


def _rope_tables(pos):
    inv_freq = ROPE_THETA ** (-jnp.arange(0, ROT_DIM, 2, dtype=jnp.float32) / ROT_DIM)
    ang = pos.astype(jnp.float32)[:, None] * inv_freq[None, :]
    cos, sin = jnp.cos(ang), jnp.sin(ang)
    n = pos.shape[0]
    half = ROT_DIM // 2
    rest = HEAD_DIM - ROT_DIM
    c = jnp.concatenate([cos, cos, jnp.ones((n, rest), F32)], axis=1)
    s1 = jnp.concatenate([-sin, jnp.zeros((n, half + rest), F32)], axis=1)
    s2 = jnp.concatenate([jnp.zeros((n, half), F32), sin, jnp.zeros((n, rest), F32)], axis=1)
    return c, s1, s2


def _mix_ab_body(ab_ref, ac_ref, ax_ref, bu_ref, bv_ref, cw_ref, lng_ref, lnb_ref, ws_ref, bs_ref,
                 ya_ref, yb_ref, cs_ref, zs_ref):
    ts = ab_ref.shape[0]

    @pl.when(pl.program_id(1) == 0)
    def _():
        zs_ref[0:8, :] = jnp.zeros((8, W_A), F32)

    z = ac_ref[...].astype(F32) * ax_ref[...].astype(F32)
    zs_ref[8:8 + ts, :] = z
    z1 = zs_ref[7:7 + ts, :]
    z2 = zs_ref[6:6 + ts, :]
    conv = cw_ref[0:1, :] * z2 + cw_ref[1:2, :] * z1 + cw_ref[2:3, :] * z
    ya_ref[...] = (ab_ref[...].astype(F32) * conv).astype(BF16)
    tail = zs_ref[ts:ts + 8, :]
    zs_ref[0:8, :] = tail
    cs_ref[0] = tail

    v = bv_ref[...].astype(F32)
    xc = v - jnp.mean(v, axis=-1, keepdims=True)
    var = jnp.mean(xc * xc, axis=-1, keepdims=True)
    vn = (xc * lax.rsqrt(var + EPS) * lng_ref[...] + lnb_ref[...]).astype(BF16)
    row = lax.broadcasted_iota(jnp.int32, (CHUNK, CHUNK), 0)
    col = lax.broadcasted_iota(jnp.int32, (CHUNK, CHUNK), 1)
    for g in range(N_B_GROUPS):
        wg = jnp.where(row >= col, ws_ref[g], 0.0).astype(BF16)
        bcol = bs_ref[:, g:g + 1]
        gs = slice(g * B_GROUP_W, (g + 1) * B_GROUP_W)
        for c in range(ts // CHUNK):
            rs = slice(c * CHUNK, (c + 1) * CHUNK)
            sg = jnp.dot(wg, vn[rs, gs], preferred_element_type=F32) + bcol
            yb_ref[rs, gs] = (bu_ref[rs, gs].astype(F32) * sg).astype(BF16)


def _attn_body(q0, q1, q2, k0, k1, k2, v0, v1, v2, y_ref,
               qc, kc, vc, s_scr, p_scr, inv_scr, o0, o1, o2, l0, l1, l2):
    row = lax.broadcasted_iota(jnp.int32, (QB, 2 * QB), 0)
    col = lax.broadcasted_iota(jnp.int32, (QB, 2 * QB), 1)
    band = jnp.logical_and(col >= row, col <= row + QB)
    band_first = jnp.logical_and(band, col >= QB)
    scale = HEAD_DIM ** -0.5
    nt = (((1,), (1,)), ((), ()))
    zero_blk = jnp.zeros((QB, HEAD_DIM), BF16)
    for (_, dil), q_ref, k_ref, v_ref, o_ref, l_ref in zip(
            DIL_GROUPS, (q0, q1, q2), (k0, k1, k2), (v0, v1, v2), (o0, o1, o2), (l0, l1, l2)):
        L = SEQ // dil
        nblk = L // QB

        def tok_rows(r, c):
            if dil == 1:
                return pl.ds(c * QB, QB)
            return pl.ds(r + c * QB * dil, QB, stride=dil)

        for r in range(dil):
            src = pl.ds(0, L) if dil == 1 else pl.ds(r, L, stride=dil)
            base = r * (L + QB)
            qc[r * L:(r + 1) * L, :] = q_ref[src, :].astype(BF16)
            kc[base:base + QB, :] = zero_blk
            vc[base:base + QB, :] = zero_blk
            kc[base + QB:base + QB + L, :] = k_ref[src, :].astype(BF16)
            vc[base + QB:base + QB + L, :] = v_ref[src, :].astype(BF16)

        for r in range(dil):
            for c in range(nblk):
                t = r * nblk + c
                kk = kc[r * (L + QB) + c * QB:r * (L + QB) + (c + 2) * QB, :]
                s = lax.dot_general(qc[t * QB:(t + 1) * QB, :], kk, nt, preferred_element_type=F32) * scale
                s_scr[t * QB:(t + 1) * QB, :] = jnp.where(band_first if c == 0 else band, s, NEG_BIG)

        for r in range(dil):
            for u in range(L // SOFTMAX_ROWS):
                rows = slice(r * L + u * SOFTMAX_ROWS, r * L + (u + 1) * SOFTMAX_ROWS)
                s = s_scr[rows, :]
                m = jnp.max(s, axis=-1, keepdims=True)
                p = jnp.exp(s - m)
                den = jnp.sum(p, axis=-1, keepdims=True)
                p_scr[rows, :] = p.astype(BF16)
                inv_scr[rows, :] = jnp.broadcast_to(1.0 / den, (SOFTMAX_ROWS, HEAD_DIM))
                tok = (pl.ds(u * SOFTMAX_ROWS, SOFTMAX_ROWS) if dil == 1
                       else pl.ds(r + u * SOFTMAX_ROWS * dil, SOFTMAX_ROWS, stride=dil))
                l_ref[tok, :] = jnp.broadcast_to(m + jnp.log(den), (SOFTMAX_ROWS, HEAD_DIM))

        for r in range(dil):
            for c in range(nblk):
                t = r * nblk + c
                vv = vc[r * (L + QB) + c * QB:r * (L + QB) + (c + 2) * QB, :]
                o = jnp.dot(p_scr[t * QB:(t + 1) * QB, :], vv, preferred_element_type=F32)
                o_ref[tok_rows(r, c), :] = o * inv_scr[t * QB:(t + 1) * QB, :]
    for u in range(SEQ // MERGE_ROWS):
        rows = slice(u * MERGE_ROWS, (u + 1) * MERGE_ROWS)
        y_ref[rows, :] = _combine_groups(o0[rows, :], o1[rows, :], o2[rows, :],
                                         l0[rows, :], l1[rows, :], l2[rows, :]).astype(BF16)


def _attn(proj):
    def seg(off, g):
        base = (off + g * C_WIDTH) // HEAD_DIM
        return pl.BlockSpec((SEQ, HEAD_DIM), lambda b, h: (b, base + h))

    specs = [seg(off, g) for off in (OFF_Q, OFF_K, OFF_V) for g in range(N_DIL)]
    return pl.pallas_call(
        _attn_body,
        grid=(BATCH, N_C_HEADS),
        in_specs=specs,
        out_specs=pl.BlockSpec((SEQ, HEAD_DIM), lambda b, h: (b, h)),
        out_shape=jax.ShapeDtypeStruct((BATCH * SEQ, C_WIDTH), BF16),
        scratch_shapes=[pltpu.VMEM((SEQ, HEAD_DIM), BF16),
                        pltpu.VMEM((2 * SEQ, HEAD_DIM), BF16),
                        pltpu.VMEM((2 * SEQ, HEAD_DIM), BF16),
                        pltpu.VMEM((SEQ, 2 * QB), F32),
                        pltpu.VMEM((SEQ, 2 * QB), BF16),
                        pltpu.VMEM((SEQ, HEAD_DIM), F32)]
                       + [pltpu.VMEM((SEQ, HEAD_DIM), F32)] * (2 * N_DIL),
        compiler_params=_params("arbitrary", "arbitrary"),
        name="attn",
    )(*([proj] * (3 * N_DIL)))


def _kv_pack_body(*refs, fixed):
    k_refs, v_refs, out_refs = refs[0:N_DIL], refs[N_DIL:2 * N_DIL], refs[-N_DIL:]
    for k_ref, v_ref, o_ref, is_fixed in zip(k_refs, v_refs, out_refs, fixed):
        def pack(k_ref=k_ref, v_ref=v_ref, o_ref=o_ref):
            for h in range(N_C_HEADS):
                sl = slice(h * HEAD_DIM, (h + 1) * HEAD_DIM)
                o_ref[:, 0, h, :] = k_ref[:, sl]
                o_ref[:, 1, h, :] = v_ref[:, sl]

        if is_fixed:
            pl.when(pl.program_id(1) == 0)(pack)
        else:
            pack()


N_MIX_IN, N_MIX_OUT = 10, 3


def _mix_ab_kv_body(*refs, fixed):
    n_kv_in = len(refs) - N_MIX_IN - N_MIX_OUT - N_DIL - 1
    mix_in, kv_in = refs[:N_MIX_IN], refs[N_MIX_IN:N_MIX_IN + 2 * N_DIL]
    outs = refs[N_MIX_IN + n_kv_in:-1]
    _mix_ab_body(*mix_in, *outs[:N_MIX_OUT], refs[-1])
    _kv_pack_body(*kv_in, *outs[N_MIX_OUT:], fixed=fixed)


def _mix_ab_kv(main, qkv, cw, lng, lnb, ws, bs_t, layer, prev):
    ts = TS_MIX
    nb = SEQ // ts
    blk = OFF_AB // W_A
    keeps = [min(win, SEQ) for win, _ in DIL_GROUPS]

    def seg(k):
        return pl.BlockSpec((ts, W_A), lambda b, s: (b * nb + s, blk + k))

    def full(shape):
        return pl.BlockSpec(shape, lambda b, s: (0,) * len(shape))

    rows = pl.BlockSpec((ts, W_A), lambda b, s: (b * nb + s, 0))
    mix_in = [seg(0), seg(1), seg(2), seg(3), seg(4),
              full((CONV_W, W_A)), full((1, W_B)), full((1, W_B)),
              full((N_B_GROUPS, CHUNK, CHUNK)), full((CHUNK, N_B_GROUPS))]
    mix_out = [rows, rows, pl.BlockSpec((1, 8, W_A), lambda b, s: (b, 0, 0))]
    mix_shapes = [jax.ShapeDtypeStruct((BATCH * SEQ, W_A), BF16),
                  jax.ShapeDtypeStruct((BATCH * SEQ, W_B), BF16),
                  jax.ShapeDtypeStruct((BATCH, 8, W_A), F32)]
    assert len(mix_in) == N_MIX_IN and len(mix_out) == N_MIX_OUT

    def src(off, g):
        keep, col = keeps[g], (off + g * C_WIDTH) // C_WIDTH
        if keep <= ts:
            nb = SEQ // keep
            return pl.BlockSpec((keep, C_WIDTH), lambda b, s: (b * nb + nb - 1, col))
        nb, first = SEQ // ts, (SEQ - keep) // ts
        return pl.BlockSpec((ts, C_WIDTH), lambda b, s: (b * nb + jnp.maximum(s, first), col))

    def dst(keep):
        if keep <= ts:
            return pl.BlockSpec((None, None, keep, 2, N_C_HEADS, HEAD_DIM), lambda b, s: (layer, b, 0, 0, 0, 0))
        first = (SEQ - keep) // ts
        return pl.BlockSpec((None, None, ts, 2, N_C_HEADS, HEAD_DIM),
                            lambda b, s: (layer, b, jnp.maximum(s - first, 0), 0, 0, 0))

    in_specs = mix_in + [src(OFF_K, g) for g in range(N_DIL)] + [src(OFF_V, g) for g in range(N_DIL)]
    args = [main] * 5 + [cw, lng, lnb, ws, bs_t] + [qkv] * (2 * N_DIL)
    aliases = {}
    if prev is not None:
        aliases = {len(in_specs) + g: N_MIX_OUT + g for g in range(N_DIL)}
        in_specs += [pl.BlockSpec(memory_space=pl.ANY)] * N_DIL
        args += list(prev)
    res = pl.pallas_call(
        functools.partial(_mix_ab_kv_body, fixed=tuple(keep <= ts for keep in keeps)),
        grid=(BATCH, nb),
        in_specs=in_specs,
        out_specs=mix_out + [dst(keep) for keep in keeps],
        out_shape=mix_shapes + [jax.ShapeDtypeStruct((DEPTH, BATCH, keep, 2, N_C_HEADS, HEAD_DIM), F32)
                                for keep in keeps],
        input_output_aliases=aliases,
        scratch_shapes=[pltpu.VMEM((ts + 8, W_A), F32)],
        compiler_params=_params("arbitrary", "arbitrary"),
        name="mix_ab_kv",
    )(*args)
    return res[0], res[1], res[2], res[N_MIX_OUT:]


def _mix_sample_body(p_ref, qkv_ref, st_ref, cw_ref, lng_ref, lnb_ref, wsc_ref, bsc_ref, c0_ref, c1_ref, c2_ref,
                     ya_ref, yb_ref, yc_ref, cs_ref, vn_ref):
    @pl.when(pl.program_id(0) == 0)
    def _():
        def seg(off, w):
            return p_ref[:, off:off + w]

        z = seg(OFF_AB + W_A, W_A) * seg(OFF_AB + 2 * W_A, W_A)
        st0 = st_ref[0, :, 0:W_A]
        st1 = st_ref[0, :, W_A:2 * W_A]
        conv = cw_ref[0:1, :] * st0 + cw_ref[1:2, :] * st1 + cw_ref[2:3, :] * z
        ya_ref[...] = (seg(OFF_AB, W_A) * conv).astype(BF16)
        cs_ref[:, 0:W_A] = st1
        cs_ref[:, W_A:2 * W_A] = z

        v = seg(OFF_AB + 3 * W_A + W_B, W_B)
        xc = v - jnp.mean(v, axis=-1, keepdims=True)
        var = jnp.mean(xc * xc, axis=-1, keepdims=True)
        vn = xc * lax.rsqrt(var + EPS) * lng_ref[...] + lnb_ref[...]
        vn_ref[...] = vn
        yb_ref[...] = (seg(OFF_AB + 3 * W_A, W_B) * (wsc_ref[...] * vn + bsc_ref[...])).astype(BF16)

    scale = HEAD_DIM ** -0.5
    outs, lses = [], []
    for g, c_ref in enumerate((c0_ref, c1_ref, c2_ref)):
        q = qkv_ref[g]
        kn = qkv_ref[N_DIL + g]
        vnew = qkv_ref[2 * N_DIL + g]
        kc = c_ref[:, 0]
        vc = c_ref[:, 1]
        s_c = jnp.sum(q[None] * kc, axis=-1, keepdims=True) * scale
        s_n = jnp.sum(q * kn, axis=-1, keepdims=True) * scale
        m = jnp.maximum(jnp.max(s_c, axis=0), s_n)
        p_c = jnp.exp(s_c - m[None])
        p_n = jnp.exp(s_n - m)
        den = jnp.sum(p_c, axis=0) + p_n
        outs.append((jnp.sum(p_c * vc, axis=0) + p_n * vnew) / den)
        lses.append(m + jnp.log(den))
    yc_ref[...] = _combine_groups(*outs, *lses)


def _mix_sample(proj_s, qkv_s, st, cw, lng, lnb, wsc, bsc, caches, layer):
    def full(shape):
        return pl.BlockSpec(shape, lambda b: (0,) * len(shape))

    n_keys = DIL_GROUPS[0][0] // DIL_GROUPS[0][1]
    cache_specs = [pl.BlockSpec((None, None, n_keys, None, 2, N_C_HEADS, HEAD_DIM),
                                lambda b: (layer, b, 0, 0, 0, 0, 0)) for _ in caches]
    qkv = qkv_s.reshape(DEC_BATCH, 3 * N_DIL, N_C_HEADS, HEAD_DIM)
    return pl.pallas_call(
        _mix_sample_body,
        grid=(DEC_BATCH,),
        in_specs=[full((DEC_BATCH, MAIN_W)),
                  pl.BlockSpec((None, 3 * N_DIL, N_C_HEADS, HEAD_DIM), lambda b: (b, 0, 0, 0)),
                  pl.BlockSpec((1, DEC_BATCH, 2 * W_A), lambda b: (layer, 0, 0)),
                  full((CONV_W, W_A)), full((1, W_B)), full((1, W_B)), full((1, W_B)), full((1, W_B))]
                 + cache_specs,
        out_specs=[full((DEC_BATCH, W_A)), full((DEC_BATCH, W_B)),
                   pl.BlockSpec((None, N_C_HEADS, HEAD_DIM), lambda b: (b, 0, 0)),
                   full((DEC_BATCH, 2 * W_A)), full((DEC_BATCH, W_B))],
        out_shape=[jax.ShapeDtypeStruct((DEC_BATCH, W_A), BF16),
                   jax.ShapeDtypeStruct((DEC_BATCH, W_B), BF16),
                   jax.ShapeDtypeStruct((DEC_BATCH, N_C_HEADS, HEAD_DIM), F32),
                   jax.ShapeDtypeStruct((DEC_BATCH, 2 * W_A), F32),
                   jax.ShapeDtypeStruct((DEC_BATCH, W_B), F32)],
        compiler_params=_params("arbitrary"),
        name="mix_sample",
    )(proj_s, qkv, st, cw, lng, lnb, wsc, bsc, *caches)


def _combine_groups(o0, o1, o2, l0, l1, l2):
    m = jnp.maximum(jnp.maximum(l0, l1), l2)
    e0, e1, e2 = jnp.exp(l0 - m), jnp.exp(l1 - m), jnp.exp(l2 - m)
    return (e0 * o0 + e1 * o1 + e2 * o2) / (e0 + e1 + e2)


def _merge_tail(x, ga, gb, gc, ya, yb, yc, wa_ref, wb_ref, wc_ref, wo_ref, gp_ref):
    mm = (jax.nn.sigmoid(ga) * jnp.dot(ya, wa_ref[...], preferred_element_type=F32)
          + jax.nn.sigmoid(gb) * jnp.dot(yb, wb_ref[...], preferred_element_type=F32)
          + jax.nn.sigmoid(gc) * jnp.dot(yc, wc_ref[...], preferred_element_type=F32))
    r = jnp.dot(mm.astype(BF16), wo_ref[...], preferred_element_type=F32)
    return x + _rms(r, gp_ref[...])


def _merge_body(ga_ref, gb_ref, gc_ref, ya_ref, yb_ref, yc_ref, x_ref,
                ms_ref, yas_ref, ybs_ref, ycs_ref, xs_ref,
                wa_ref, wb_ref, wc_ref, wo_ref, gp_ref, out_ref, outs_ref):
    weights = (wa_ref, wb_ref, wc_ref, wo_ref, gp_ref)
    out_ref[...] = _merge_tail(x_ref[...], ga_ref[...].astype(F32), gb_ref[...].astype(F32),
                               gc_ref[...].astype(F32), ya_ref[...], yb_ref[...], yc_ref[...], *weights)

    @pl.when(pl.program_id(0) == 0)
    def _():
        gates = [ms_ref[:, k * D_MODEL:(k + 1) * D_MODEL] for k in range(3)]
        outs_ref[...] = _merge_tail(xs_ref[...], *gates, yas_ref[...], ybs_ref[...],
                                    ycs_ref[...].astype(BF16), *weights)


def _merge(proj, ya, yb, yc, x, main_s, ya_s, yb_s, yc_s, xs, wa, wb, wc, wo, gp, layer, tm):
    m = x.shape[0]

    def rows(w, k=0):
        return pl.BlockSpec((tm, w), lambda i: (i, k))

    def whole(a):
        return pl.BlockSpec(a.shape, lambda i: (0, 0))

    def resident(k):
        return pl.BlockSpec((k, D_MODEL), lambda i: (0, 0), pipeline_mode=pl.Buffered(1))

    return pl.pallas_call(
        _merge_body,
        grid=(m // tm,),
        in_specs=[rows(D_MODEL, 0), rows(D_MODEL, 1), rows(D_MODEL, 2), rows(W_A), rows(W_B), rows(C_WIDTH),
                  rows(D_MODEL),
                  whole(main_s), whole(ya_s), whole(yb_s), whole(yc_s), whole(xs),
                  resident(W_A), resident(W_B), resident(C_WIDTH), resident(D_MODEL),
                  pl.BlockSpec((None, 1, D_MODEL), lambda i: (layer, 0, 0), pipeline_mode=pl.Buffered(1))],
        out_specs=[rows(D_MODEL), whole(xs)],
        out_shape=[jax.ShapeDtypeStruct((m, D_MODEL), F32), jax.ShapeDtypeStruct(xs.shape, F32)],
        compiler_params=_params("arbitrary"),
        name="merge",
    )(proj, proj, proj, ya, yb, yc, x, main_s, ya_s, yb_s, yc_s, xs, wa, wb, wc, wo, gp)


def _swiglu_part(h, wg_ref, wu_ref, wo_ref):
    gate = jnp.dot(h, wg_ref[...], preferred_element_type=F32)
    up = jnp.dot(h, wu_ref[...], preferred_element_type=F32)
    act = (gate * jax.nn.sigmoid(gate) * up).astype(BF16)
    return jnp.dot(act, wo_ref[...], preferred_element_type=F32)


def _ffn_body(x_ref, g1_ref, wg_ref, wu_ref, wo_ref, g2_ref, xs_ref, *rest, n_cast):
    cast_src = rest[:n_cast]
    out_ref, outs_ref = rest[n_cast:n_cast + 2]
    cast_dst = rest[n_cast + 2:2 * n_cast + 2]
    h_ref, hs_ref = rest[-2:]
    i = pl.program_id(0)
    j = pl.program_id(1)
    last = pl.num_programs(1) - 1

    @pl.when(j == 0)
    def _():
        h_ref[...] = _rms(x_ref[...], g1_ref[...]).astype(BF16)
        out_ref[...] = jnp.zeros_like(out_ref)

    _cast_blocks(cast_src, cast_dst)
    out_ref[...] += _swiglu_part(h_ref[...], wg_ref, wu_ref, wo_ref)

    @pl.when(j == last)
    def _():
        out_ref[...] = x_ref[...] + _rms(out_ref[...], g2_ref[...])

    @pl.when(i == 0)
    def _():
        @pl.when(j == 0)
        def _():
            hs_ref[...] = _rms(xs_ref[...], g1_ref[...]).astype(BF16)
            outs_ref[...] = jnp.zeros_like(outs_ref)

        outs_ref[...] += _swiglu_part(hs_ref[...], wg_ref, wu_ref, wo_ref)

        @pl.when(j == last)
        def _():
            outs_ref[...] = xs_ref[...] + _rms(outs_ref[...], g2_ref[...])


def _ffn(x, xs, g1, w_in, w_out, g2, layer, tm, tf, casts=()):
    m = x.shape[0]
    ns = xs.shape[0]
    nj = D_FF // tf
    cast_specs = [_cast_specs(src, layer + 1, rows, nj, m // tm) for src, rows in casts]
    res = pl.pallas_call(
        functools.partial(_ffn_body, n_cast=len(casts)),
        grid=(m // tm, nj),
        in_specs=[
            pl.BlockSpec((tm, D_MODEL), lambda i, j: (i, 0)),
            pl.BlockSpec((None, 1, D_MODEL), lambda i, j: (layer, 0, 0)),
            pl.BlockSpec((D_MODEL, tf), lambda i, j: (0, j)),
            pl.BlockSpec((D_MODEL, tf), lambda i, j: (0, nj + j)),
            pl.BlockSpec((tf, D_MODEL), lambda i, j: (j, 0)),
            pl.BlockSpec((None, 1, D_MODEL), lambda i, j: (layer, 0, 0)),
            pl.BlockSpec((ns, D_MODEL), lambda i, j: (0, 0)),
        ] + [c[0] for c in cast_specs],
        out_specs=[pl.BlockSpec((tm, D_MODEL), lambda i, j: (i, 0)),
                   pl.BlockSpec((ns, D_MODEL), lambda i, j: (0, 0))] + [c[1] for c in cast_specs],
        out_shape=[jax.ShapeDtypeStruct((m, D_MODEL), F32),
                   jax.ShapeDtypeStruct((ns, D_MODEL), F32)] + [c[2] for c in cast_specs],
        scratch_shapes=[pltpu.VMEM((tm, D_MODEL), BF16), pltpu.VMEM((ns, D_MODEL), BF16)],
        compiler_params=_params("arbitrary", "arbitrary"),
        name="ffn",
    )(x, g1, w_in, w_in, w_out, g2, xs, *[src for src, _ in casts])
    return res[0], res[1], res[2:]


TM_INPROJ = 1024
TS_MIX = 512
TM_MERGE = 256
TM_FFN = 512
TF_FFN = 512
CAST_ROWS_FFN_IN = 32
CAST_ROWS_FFN_OUT = 64
CAST_ROWS_W_IN = 16
CAST_ROWS_MERGE = 32


def kernel(x_prompt, x_sample, state_conv, cache_kv_w128, cache_kv_w512, cache_kv_w2048, g_pre_mix, w_in, conv_w, ln_g, ln_b, w_s, b_s, w_a_out, w_b_out, w_c_out, w_o, g_post_mix, g_pre_ffn, w_ffn_in, w_ffn_out, g_post_ffn):
    w_in_l = w_in[0].astype(BF16)
    inproj_casts = ((w_ffn_in, CAST_ROWS_FFN_IN), (w_ffn_out, CAST_ROWS_FFN_OUT),
                    (w_a_out, CAST_ROWS_MERGE), (w_b_out, CAST_ROWS_MERGE), (w_c_out, CAST_ROWS_MERGE),
                    (w_o, CAST_ROWS_MERGE))
    g_pre_mix, g_post_mix, g_pre_ffn, g_post_ffn = (
        g.reshape(DEPTH, 1, D_MODEL) for g in (g_pre_mix, g_post_mix, g_pre_ffn, g_post_ffn))

    tabs_p = _rope_tables(jnp.arange(SEQ, dtype=jnp.int32))
    tabs_s = _rope_tables(jnp.full((DEC_BATCH,), PAST_LEN, dtype=jnp.int32))

    n_keys = DIL_GROUPS[0][0] // DIL_GROUPS[0][1]
    caches = tuple(c.reshape(DEPTH, DEC_BATCH, n_keys, dil, 2, N_C_HEADS, HEAD_DIM)
                   for c, (_, dil) in zip((cache_kv_w128, cache_kv_w512, cache_kv_w2048), DIL_GROUPS))
    st_all = state_conv.reshape(DEPTH, DEC_BATCH, (CONV_W - 1) * W_A)
    bs_t = jnp.swapaxes(b_s, 1, 2)
    wsc = jnp.repeat(w_s[:, :, 0, 0], B_GROUP_W, axis=1)
    bsc = jnp.repeat(b_s[:, :, 0], B_GROUP_W, axis=1)

    xp = x_prompt.reshape(BATCH * SEQ, D_MODEL)
    xs = x_sample.reshape(DEC_BATCH, D_MODEL)
    conv_p, conv_s, vchunk_s = [], [], []
    kv_p = None
    kv_s = [[] for _ in DIL_GROUPS]

    def row(a, l):
        return a[l][None, :]

    for l in range(DEPTH):
        proj, qkv, proj_s, qkv_s, wfi_l, wfo_l, wa_l, wb_l, wc_l, wo_l = _inproj(
            xp, xs, g_pre_mix, w_in_l, l, tabs_p, tabs_s, TM_INPROJ, casts=inproj_casts)
        ya, yb, ctail, kv_p = _mix_ab_kv(proj, qkv, conv_w[l], row(ln_g, l), row(ln_b, l), w_s[l], bs_t[l], l, kv_p)
        yc = _attn(qkv)
        conv_p.append(ctail[:, 8 - (CONV_W - 1):])
        ya_s, yb_s, yc_s, cst, vn = _mix_sample(proj_s, qkv_s, st_all, conv_w[l], row(ln_g, l), row(ln_b, l),
                                                row(wsc, l), row(bsc, l), caches, l)
        for g in range(N_DIL):
            k = qkv_s[:, OFF_K + g * C_WIDTH:OFF_K + (g + 1) * C_WIDTH]
            v = qkv_s[:, OFF_V + g * C_WIDTH:OFF_V + (g + 1) * C_WIDTH]
            kv_s[g].append(jnp.stack([k.reshape(DEC_BATCH, 1, N_C_HEADS, HEAD_DIM),
                                      v.reshape(DEC_BATCH, 1, N_C_HEADS, HEAD_DIM)], axis=2))
        conv_s.append(cst.reshape(DEC_BATCH, CONV_W - 1, W_A))
        vchunk_s.append(vn.reshape(DEC_BATCH, 1, W_B))
        xp, xs = _merge(proj, ya, yb, yc, xp, proj_s, ya_s, yb_s, yc_s.reshape(DEC_BATCH, C_WIDTH), xs,
                        wa_l, wb_l, wc_l, wo_l, g_post_mix, l, TM_MERGE)
        next_casts = ((w_in, CAST_ROWS_W_IN),) if l + 1 < DEPTH else ()
        xp, xs, w_in_next = _ffn(xp, xs, g_pre_ffn, wfi_l, wfo_l, g_post_ffn, l, TM_FFN, TF_FFN, casts=next_casts)
        if w_in_next:
            w_in_l = w_in_next[0]

    return (xp.reshape(BATCH, SEQ, D_MODEL), xs.reshape(DEC_BATCH, 1, D_MODEL),
            jnp.stack(conv_p, axis=0),
            kv_p[0], kv_p[1], kv_p[2],
            jnp.stack(conv_s, axis=0),
            jnp.stack(kv_s[0], axis=0), jnp.stack(kv_s[1], axis=0), jnp.stack(kv_s[2], axis=0),
            jnp.stack(vchunk_s, axis=0))
```

```python
import functools

import jax
import jax.numpy as jnp
from jax import lax
from jax.experimental import pallas as pl
from jax.experimental.pallas import tpu as pltpu

D_MODEL = 2048
BATCH = 4
SEQ = 2048
DEPTH = 4
DEC_BATCH = 8
PAST_LEN = 16384
W_A = 1024
CONV_W = 3
W_B = 1024
CHUNK = 128
N_B_GROUPS = 4
B_GROUP_W = W_B // N_B_GROUPS
N_C_HEADS = 4
HEAD_DIM = 128
ROT_DIM = HEAD_DIM // 4
ROPE_THETA = 500000.0
DIL_GROUPS = ((128, 1), (512, 4), (2048, 16))
N_DIL = len(DIL_GROUPS)
QB = 128
C_WIDTH = N_C_HEADS * HEAD_DIM
QKV_W = N_DIL * C_WIDTH
D_FF = ((-(-8 * D_MODEL // 3) + 255) // 256) * 256
IN_WIDTH = 3 * W_A + 2 * W_B + 3 * QKV_W + 3 * D_MODEL
EPS = 1e-6

GATE_W = 3 * D_MODEL
OFF_AB = GATE_W
MAIN_W = GATE_W + 3 * W_A + 2 * W_B
OFF_Q, OFF_K, OFF_V = 0, QKV_W, 2 * QKV_W
ORIG_GATE_OFF = IN_WIDTH - GATE_W

ROPE_ROWS = 256
X_PREFETCH_STEPS = (2, 4, 6, 8)
SOFTMAX_ROWS = 32
MERGE_ROWS = 64
V7X_VMEM_LIMIT = 60 * 1024 * 1024
NEG_BIG = -1e30

F32 = jnp.float32
BF16 = jnp.bfloat16


def _params(*sem):
    return pltpu.CompilerParams(dimension_semantics=sem, vmem_limit_bytes=V7X_VMEM_LIMIT)


def _rms(x, g):
    return x * lax.rsqrt(jnp.mean(x * x, axis=-1, keepdims=True) + EPS) * g


def _cast_specs(src, layer, rows, steps_per_row_tile, n_steps):
    _, r, c = src.shape
    nblk = r // rows
    assert r % rows == 0 and nblk <= n_steps

    def blk(i, j):
        return jnp.minimum(i * steps_per_row_tile + j, nblk - 1)

    return (pl.BlockSpec((None, rows, c), lambda i, j: (layer, blk(i, j), 0)),
            pl.BlockSpec((rows, c), lambda i, j: (blk(i, j), 0)),
            jax.ShapeDtypeStruct((r, c), BF16))


def _cast_blocks(src_refs, dst_refs):
    for s_ref, d_ref in zip(src_refs, dst_refs):
        d_ref[...] = s_ref[...].astype(BF16)


def _rope(a, c, s1, s2):
    return a * c + pltpu.roll(a, HEAD_DIM - ROT_DIM // 2, 1) * s1 + pltpu.roll(a, ROT_DIM // 2, 1) * s2


def _inproj_body(*refs, x_parts, main_steps, rope_steps, n_cast):
    x_refs = refs[:x_parts]
    g_ref, w0_ref, w1_ref, c_ref, s1_ref, s2_ref, xs_ref, cs_ref, s1s_ref, s2s_ref = refs[x_parts:x_parts + 10]
    rest = refs[x_parts + 10:]
    part = x_refs[0].shape[0]
    tm = x_parts * part
    cast_src = rest[:n_cast]
    main_ref, qkv_ref, mains_ref, qkvs_ref = rest[n_cast:n_cast + 4]
    cast_dst = rest[n_cast + 4:2 * n_cast + 4]
    h_ref, hs_ref = rest[-2:]
    i = pl.program_id(0)
    j = pl.program_id(1)
    tn = w0_ref.shape[1]
    heads = tn // HEAD_DIM
    halves = ((w0_ref, slice(0, tn)), (w1_ref, slice(tn, 2 * tn)))
    with_samples = i == 0

    @pl.when(j == 0)
    def _():
        for p, x_ref in enumerate(x_refs):
            h_ref[p * part:(p + 1) * part, :] = _rms(x_ref[...], g_ref[...]).astype(BF16)

    @pl.when(jnp.logical_and(with_samples, j == 0))
    def _():
        hs_ref[...] = _rms(xs_ref[...], g_ref[...]).astype(BF16)

    @pl.when(j < main_steps)
    def _():
        _cast_blocks(cast_src, cast_dst)
        for w_ref, cols in halves:
            main_ref[:, cols] = jnp.dot(h_ref[...], w_ref[...], preferred_element_type=F32).astype(main_ref.dtype)

        @pl.when(with_samples)
        def _():
            for w_ref, cols in halves:
                mains_ref[j, :, cols] = jnp.dot(hs_ref[...], w_ref[...], preferred_element_type=F32)

    @pl.when(jnp.logical_and(j >= main_steps, j < main_steps + rope_steps))
    def _():
        _cast_blocks(cast_src, cast_dst)
        rc = min(tm, ROPE_ROWS)
        for w_ref, cols in halves:
            for r in range(tm // rc):
                rs = slice(r * rc, (r + 1) * rc)
                acc = jnp.dot(h_ref[rs, :], w_ref[...], preferred_element_type=F32)
                for h in range(heads):
                    sl = slice(cols.start + h * HEAD_DIM, cols.start + (h + 1) * HEAD_DIM)
                    qkv_ref[rs, sl] = _rope(acc[:, h * HEAD_DIM:(h + 1) * HEAD_DIM],
                                            c_ref[rs, :], s1_ref[rs, :], s2_ref[rs, :])

        @pl.when(with_samples)
        def _():
            for w_ref, cols in halves:
                acc = jnp.dot(hs_ref[...], w_ref[...], preferred_element_type=F32)
                for h in range(heads):
                    sl = slice(cols.start + h * HEAD_DIM, cols.start + (h + 1) * HEAD_DIM)
                    qkvs_ref[j - main_steps, :, sl] = _rope(acc[:, h * HEAD_DIM:(h + 1) * HEAD_DIM],
                                                            cs_ref[...], s1s_ref[...], s2s_ref[...])

    @pl.when(j >= main_steps + rope_steps)
    def _():
        _cast_blocks(cast_src, cast_dst)
        for w_ref, cols in halves:
            qkv_ref[:, cols] = jnp.dot(h_ref[...], w_ref[...], preferred_element_type=F32)

        @pl.when(with_samples)
        def _():
            for w_ref, cols in halves:
                qkvs_ref[j - main_steps, :, cols] = jnp.dot(hs_ref[...], w_ref[...], preferred_element_type=F32)


def _inproj(x, xs, g, w, layer, tabs, tabs_s, tm, casts=()):
    m = x.shape[0]
    ns = xs.shape[0]
    tn = C_WIDTH
    n_tiles = IN_WIDTH // tn
    rot = ORIG_GATE_OFF // tn
    main_steps = MAIN_W // (2 * tn)
    rope_steps = 2 * QKV_W // (2 * tn)
    qkv_steps = pl.cdiv(3 * QKV_W, 2 * tn)
    steps = main_steps + qkv_steps
    tab_spec = pl.BlockSpec((tm, HEAD_DIM), lambda i, j: (i % (SEQ // tm), 0))
    tab_s_spec = pl.BlockSpec((ns, HEAD_DIM), lambda i, j: (0, 0))

    def w_spec(half):
        return pl.BlockSpec((D_MODEL, tn), lambda i, j: (0, (2 * j + half + rot) % n_tiles))

    def x_spec(p):
        def tile(i, j):
            return jnp.minimum(i + jnp.where(j >= X_PREFETCH_STEPS[p], 1, 0), m // tm - 1)
        return pl.BlockSpec((tm // x_parts, D_MODEL), lambda i, j: (tile(i, j) * x_parts + p, 0))

    x_parts = len(X_PREFETCH_STEPS)
    cast_specs = [_cast_specs(src, layer, rows, steps, (m // tm) * steps) for src, rows in casts]
    res = pl.pallas_call(
        functools.partial(_inproj_body, x_parts=x_parts, main_steps=main_steps, rope_steps=rope_steps,
                          n_cast=len(casts)),
        grid=(m // tm, steps),
        in_specs=[x_spec(p) for p in range(x_parts)] + [
            pl.BlockSpec((None, 1, D_MODEL), lambda i, j: (layer, 0, 0)),
            w_spec(0), w_spec(1),
            tab_spec, tab_spec, tab_spec,
            pl.BlockSpec((ns, D_MODEL), lambda i, j: (0, 0)),
            tab_s_spec, tab_s_spec, tab_s_spec,
        ] + [c[0] for c in cast_specs],
        out_specs=[pl.BlockSpec((tm, 2 * tn), lambda i, j: (i, jnp.minimum(j, main_steps - 1))),
                   pl.BlockSpec((tm, 2 * tn), lambda i, j: (i, jnp.maximum(j - main_steps, 0))),
                   pl.BlockSpec((main_steps, ns, 2 * tn), lambda i, j: (0, 0, 0)),
                   pl.BlockSpec((qkv_steps, ns, 2 * tn), lambda i, j: (0, 0, 0))]
                  + [c[1] for c in cast_specs],
        out_shape=[jax.ShapeDtypeStruct((m, MAIN_W), BF16),
                   jax.ShapeDtypeStruct((m, 3 * QKV_W), F32),
                   jax.ShapeDtypeStruct((main_steps, ns, 2 * tn), F32),
                   jax.ShapeDtypeStruct((qkv_steps, ns, 2 * tn), F32)] + [c[2] for c in cast_specs],
        scratch_shapes=[pltpu.VMEM((tm, D_MODEL), BF16), pltpu.VMEM((ns, D_MODEL), BF16)],
        compiler_params=_params("arbitrary", "arbitrary"),
        name="inproj",
    )(*([x] * x_parts), g, w, w, *tabs, xs, *tabs_s, *[src for src, _ in casts])
    main_s = jnp.swapaxes(res[2], 0, 1).reshape(ns, MAIN_W)
    qkv_s = jnp.swapaxes(res[3], 0, 1).reshape(ns, qkv_steps * 2 * tn)[:, :3 * QKV_W]
    return (res[0], res[1], main_s, qkv_s, *res[4:])


def _rope_tables(pos):
    inv_freq = ROPE_THETA ** (-jnp.arange(0, ROT_DIM, 2, dtype=jnp.float32) / ROT_DIM)
    ang = pos.astype(jnp.float32)[:, None] * inv_freq[None, :]
    cos, sin = jnp.cos(ang), jnp.sin(ang)
    n = pos.shape[0]
    half = ROT_DIM // 2
    rest = HEAD_DIM - ROT_DIM
    c = jnp.concatenate([cos, cos, jnp.ones((n, rest), F32)], axis=1)
    s1 = jnp.concatenate([-sin, jnp.zeros((n, half + rest), F32)], axis=1)
    s2 = jnp.concatenate([jnp.zeros((n, half), F32), sin, jnp.zeros((n, rest), F32)], axis=1)
    return c, s1, s2


def _mix_ab_body(ab_ref, ac_ref, ax_ref, bu_ref, bv_ref, cw_ref, lng_ref, lnb_ref, ws_ref, bs_ref,
                 ya_ref, yb_ref, cs_ref, zs_ref):
    ts = ab_ref.shape[0]

    @pl.when(pl.program_id(1) == 0)
    def _():
        zs_ref[0:8, :] = jnp.zeros((8, W_A), F32)

    z = ac_ref[...].astype(F32) * ax_ref[...].astype(F32)
    zs_ref[8:8 + ts, :] = z
    z1 = zs_ref[7:7 + ts, :]
    z2 = zs_ref[6:6 + ts, :]
    conv = cw_ref[0:1, :] * z2 + cw_ref[1:2, :] * z1 + cw_ref[2:3, :] * z
    ya_ref[...] = (ab_ref[...].astype(F32) * conv).astype(BF16)
    tail = zs_ref[ts:ts + 8, :]
    zs_ref[0:8, :] = tail
    cs_ref[0] = tail

    v = bv_ref[...].astype(F32)
    xc = v - jnp.mean(v, axis=-1, keepdims=True)
    var = jnp.mean(xc * xc, axis=-1, keepdims=True)
    vn = (xc * lax.rsqrt(var + EPS) * lng_ref[...] + lnb_ref[...]).astype(BF16)
    row = lax.broadcasted_iota(jnp.int32, (CHUNK, CHUNK), 0)
    col = lax.broadcasted_iota(jnp.int32, (CHUNK, CHUNK), 1)
    for g in range(N_B_GROUPS):
        wg = jnp.where(row >= col, ws_ref[g], 0.0).astype(BF16)
        bcol = bs_ref[:, g:g + 1]
        gs = slice(g * B_GROUP_W, (g + 1) * B_GROUP_W)
        for c in range(ts // CHUNK):
            rs = slice(c * CHUNK, (c + 1) * CHUNK)
            sg = jnp.dot(wg, vn[rs, gs], preferred_element_type=F32) + bcol
            yb_ref[rs, gs] = (bu_ref[rs, gs].astype(F32) * sg).astype(BF16)


def _attn_body(q0, q1, q2, k0, k1, k2, v0, v1, v2, y_ref,
               qc, kc, vc, s_scr, p_scr, inv_scr, o0, o1, o2, l0, l1, l2):
    row = lax.broadcasted_iota(jnp.int32, (QB, 2 * QB), 0)
    col = lax.broadcasted_iota(jnp.int32, (QB, 2 * QB), 1)
    band = jnp.logical_and(col >= row, col <= row + QB)
    band_first = jnp.logical_and(band, col >= QB)
    scale = HEAD_DIM ** -0.5
    nt = (((1,), (1,)), ((), ()))
    zero_blk = jnp.zeros((QB, HEAD_DIM), BF16)
    for (_, dil), q_ref, k_ref, v_ref, o_ref, l_ref in zip(
            DIL_GROUPS, (q0, q1, q2), (k0, k1, k2), (v0, v1, v2), (o0, o1, o2), (l0, l1, l2)):
        L = SEQ // dil
        nblk = L // QB

        def tok_rows(r, c):
            if dil == 1:
                return pl.ds(c * QB, QB)
            return pl.ds(r + c * QB * dil, QB, stride=dil)

        for r in range(dil):
            src = pl.ds(0, L) if dil == 1 else pl.ds(r, L, stride=dil)
            base = r * (L + QB)
            qc[r * L:(r + 1) * L, :] = q_ref[src, :].astype(BF16)
            kc[base:base + QB, :] = zero_blk
            vc[base:base + QB, :] = zero_blk
            kc[base + QB:base + QB + L, :] = k_ref[src, :].astype(BF16)
            vc[base + QB:base + QB + L, :] = v_ref[src, :].astype(BF16)

        for r in range(dil):
            for c in range(nblk):
                t = r * nblk + c
                kk = kc[r * (L + QB) + c * QB:r * (L + QB) + (c + 2) * QB, :]
                s = lax.dot_general(qc[t * QB:(t + 1) * QB, :], kk, nt, preferred_element_type=F32) * scale
                s_scr[t * QB:(t + 1) * QB, :] = jnp.where(band_first if c == 0 else band, s, NEG_BIG)

        for r in range(dil):
            for u in range(L // SOFTMAX_ROWS):
                rows = slice(r * L + u * SOFTMAX_ROWS, r * L + (u + 1) * SOFTMAX_ROWS)
                s = s_scr[rows, :]
                m = jnp.max(s, axis=-1, keepdims=True)
                p = jnp.exp(s - m)
                den = jnp.sum(p, axis=-1, keepdims=True)
                p_scr[rows, :] = p.astype(BF16)
                inv_scr[rows, :] = jnp.broadcast_to(1.0 / den, (SOFTMAX_ROWS, HEAD_DIM))
                tok = (pl.ds(u * SOFTMAX_ROWS, SOFTMAX_ROWS) if dil == 1
                       else pl.ds(r + u * SOFTMAX_ROWS * dil, SOFTMAX_ROWS, stride=dil))
                l_ref[tok, :] = jnp.broadcast_to(m + jnp.log(den), (SOFTMAX_ROWS, HEAD_DIM))

        for r in range(dil):
            for c in range(nblk):
                t = r * nblk + c
                vv = vc[r * (L + QB) + c * QB:r * (L + QB) + (c + 2) * QB, :]
                o = jnp.dot(p_scr[t * QB:(t + 1) * QB, :], vv, preferred_element_type=F32)
                o_ref[tok_rows(r, c), :] = o * inv_scr[t * QB:(t + 1) * QB, :]
    for u in range(SEQ // MERGE_ROWS):
        rows = slice(u * MERGE_ROWS, (u + 1) * MERGE_ROWS)
        y_ref[rows, :] = _combine_groups(o0[rows, :], o1[rows, :], o2[rows, :],
                                         l0[rows, :], l1[rows, :], l2[rows, :]).astype(BF16)


def _attn(proj):
    def seg(off, g):
        base = (off + g * C_WIDTH) // HEAD_DIM
        return pl.BlockSpec((SEQ, HEAD_DIM), lambda b, h: (b, base + h))

    specs = [seg(off, g) for off in (OFF_Q, OFF_K, OFF_V) for g in range(N_DIL)]
    return pl.pallas_call(
        _attn_body,
        grid=(BATCH, N_C_HEADS),
        in_specs=specs,
        out_specs=pl.BlockSpec((SEQ, HEAD_DIM), lambda b, h: (b, h)),
        out_shape=jax.ShapeDtypeStruct((BATCH * SEQ, C_WIDTH), BF16),
        scratch_shapes=[pltpu.VMEM((SEQ, HEAD_DIM), BF16),
                        pltpu.VMEM((2 * SEQ, HEAD_DIM), BF16),
                        pltpu.VMEM((2 * SEQ, HEAD_DIM), BF16),
                        pltpu.VMEM((SEQ, 2 * QB), F32),
                        pltpu.VMEM((SEQ, 2 * QB), BF16),
                        pltpu.VMEM((SEQ, HEAD_DIM), F32)]
                       + [pltpu.VMEM((SEQ, HEAD_DIM), F32)] * (2 * N_DIL),
        compiler_params=_params("arbitrary", "arbitrary"),
        name="attn",
    )(*([proj] * (3 * N_DIL)))


def _kv_pack_body(*refs, fixed):
    k_refs, v_refs, out_refs = refs[0:N_DIL], refs[N_DIL:2 * N_DIL], refs[-N_DIL:]
    for k_ref, v_ref, o_ref, is_fixed in zip(k_refs, v_refs, out_refs, fixed):
        def pack(k_ref=k_ref, v_ref=v_ref, o_ref=o_ref):
            for h in range(N_C_HEADS):
                sl = slice(h * HEAD_DIM, (h + 1) * HEAD_DIM)
                o_ref[:, 0, h, :] = k_ref[:, sl]
                o_ref[:, 1, h, :] = v_ref[:, sl]

        if is_fixed:
            pl.when(pl.program_id(1) == 0)(pack)
        else:
            pack()


N_MIX_IN, N_MIX_OUT = 10, 3


def _mix_ab_kv_body(*refs, fixed):
    n_kv_in = len(refs) - N_MIX_IN - N_MIX_OUT - N_DIL - 1
    mix_in, kv_in = refs[:N_MIX_IN], refs[N_MIX_IN:N_MIX_IN + 2 * N_DIL]
    outs = refs[N_MIX_IN + n_kv_in:-1]
    _mix_ab_body(*mix_in, *outs[:N_MIX_OUT], refs[-1])
    _kv_pack_body(*kv_in, *outs[N_MIX_OUT:], fixed=fixed)


def _mix_ab_kv(main, qkv, cw, lng, lnb, ws, bs_t, layer, prev):
    ts = TS_MIX
    nb = SEQ // ts
    blk = OFF_AB // W_A
    keeps = [min(win, SEQ) for win, _ in DIL_GROUPS]

    def seg(k):
        return pl.BlockSpec((ts, W_A), lambda b, s: (b * nb + s, blk + k))

    def full(shape):
        return pl.BlockSpec(shape, lambda b, s: (0,) * len(shape))

    rows = pl.BlockSpec((ts, W_A), lambda b, s: (b * nb + s, 0))
    mix_in = [seg(0), seg(1), seg(2), seg(3), seg(4),
              full((CONV_W, W_A)), full((1, W_B)), full((1, W_B)),
              full((N_B_GROUPS, CHUNK, CHUNK)), full((CHUNK, N_B_GROUPS))]
    mix_out = [rows, rows, pl.BlockSpec((1, 8, W_A), lambda b, s: (b, 0, 0))]
    mix_shapes = [jax.ShapeDtypeStruct((BATCH * SEQ, W_A), BF16),
                  jax.ShapeDtypeStruct((BATCH * SEQ, W_B), BF16),
                  jax.ShapeDtypeStruct((BATCH, 8, W_A), F32)]
    assert len(mix_in) == N_MIX_IN and len(mix_out) == N_MIX_OUT

    def src(off, g):
        keep, col = keeps[g], (off + g * C_WIDTH) // C_WIDTH
        if keep <= ts:
            nb = SEQ // keep
            return pl.BlockSpec((keep, C_WIDTH), lambda b, s: (b * nb + nb - 1, col))
        nb, first = SEQ // ts, (SEQ - keep) // ts
        return pl.BlockSpec((ts, C_WIDTH), lambda b, s: (b * nb + jnp.maximum(s, first), col))

    def dst(keep):
        if keep <= ts:
            return pl.BlockSpec((None, None, keep, 2, N_C_HEADS, HEAD_DIM), lambda b, s: (layer, b, 0, 0, 0, 0))
        first = (SEQ - keep) // ts
        return pl.BlockSpec((None, None, ts, 2, N_C_HEADS, HEAD_DIM),
                            lambda b, s: (layer, b, jnp.maximum(s - first, 0), 0, 0, 0))

    in_specs = mix_in + [src(OFF_K, g) for g in range(N_DIL)] + [src(OFF_V, g) for g in range(N_DIL)]
    args = [main] * 5 + [cw, lng, lnb, ws, bs_t] + [qkv] * (2 * N_DIL)
    aliases = {}
    if prev is not None:
        aliases = {len(in_specs) + g: N_MIX_OUT + g for g in range(N_DIL)}
        in_specs += [pl.BlockSpec(memory_space=pl.ANY)] * N_DIL
        args += list(prev)
    res = pl.pallas_call(
        functools.partial(_mix_ab_kv_body, fixed=tuple(keep <= ts for keep in keeps)),
        grid=(BATCH, nb),
        in_specs=in_specs,
        out_specs=mix_out + [dst(keep) for keep in keeps],
        out_shape=mix_shapes + [jax.ShapeDtypeStruct((DEPTH, BATCH, keep, 2, N_C_HEADS, HEAD_DIM), F32)
                                for keep in keeps],
        input_output_aliases=aliases,
        scratch_shapes=[pltpu.VMEM((ts + 8, W_A), F32)],
        compiler_params=_params("arbitrary", "arbitrary"),
        name="mix_ab_kv",
    )(*args)
    return res[0], res[1], res[2], res[N_MIX_OUT:]


def _mix_sample_body(p_ref, qkv_ref, st_ref, cw_ref, lng_ref, lnb_ref, wsc_ref, bsc_ref, c0_ref, c1_ref, c2_ref,
                     ya_ref, yb_ref, yc_ref, cs_ref, vn_ref):
    @pl.when(pl.program_id(0) == 0)
    def _():
        def seg(off, w):
            return p_ref[:, off:off + w]

        z = seg(OFF_AB + W_A, W_A) * seg(OFF_AB + 2 * W_A, W_A)
        st0 = st_ref[0, :, 0:W_A]
        st1 = st_ref[0, :, W_A:2 * W_A]
        conv = cw_ref[0:1, :] * st0 + cw_ref[1:2, :] * st1 + cw_ref[2:3, :] * z
        ya_ref[...] = (seg(OFF_AB, W_A) * conv).astype(BF16)
        cs_ref[:, 0:W_A] = st1
        cs_ref[:, W_A:2 * W_A] = z

        v = seg(OFF_AB + 3 * W_A + W_B, W_B)
        xc = v - jnp.mean(v, axis=-1, keepdims=True)
        var = jnp.mean(xc * xc, axis=-1, keepdims=True)
        vn = xc * lax.rsqrt(var + EPS) * lng_ref[...] + lnb_ref[...]
        vn_ref[...] = vn
        yb_ref[...] = (seg(OFF_AB + 3 * W_A, W_B) * (wsc_ref[...] * vn + bsc_ref[...])).astype(BF16)

    scale = HEAD_DIM ** -0.5
    outs, lses = [], []
    for g, c_ref in enumerate((c0_ref, c1_ref, c2_ref)):
        q = qkv_ref[g]
        kn = qkv_ref[N_DIL + g]
        vnew = qkv_ref[2 * N_DIL + g]
        kc = c_ref[:, 0]
        vc = c_ref[:, 1]
        s_c = jnp.sum(q[None] * kc, axis=-1, keepdims=True) * scale
        s_n = jnp.sum(q * kn, axis=-1, keepdims=True) * scale
        m = jnp.maximum(jnp.max(s_c, axis=0), s_n)
        p_c = jnp.exp(s_c - m[None])
        p_n = jnp.exp(s_n - m)
        den = jnp.sum(p_c, axis=0) + p_n
        outs.append((jnp.sum(p_c * vc, axis=0) + p_n * vnew) / den)
        lses.append(m + jnp.log(den))
    yc_ref[...] = _combine_groups(*outs, *lses)


def _mix_sample(proj_s, qkv_s, st, cw, lng, lnb, wsc, bsc, caches, layer):
    def full(shape):
        return pl.BlockSpec(shape, lambda b: (0,) * len(shape))

    n_keys = DIL_GROUPS[0][0] // DIL_GROUPS[0][1]
    cache_specs = [pl.BlockSpec((None, None, n_keys, None, 2, N_C_HEADS, HEAD_DIM),
                                lambda b: (layer, b, 0, 0, 0, 0, 0)) for _ in caches]
    qkv = qkv_s.reshape(DEC_BATCH, 3 * N_DIL, N_C_HEADS, HEAD_DIM)
    return pl.pallas_call(
        _mix_sample_body,
        grid=(DEC_BATCH,),
        in_specs=[full((DEC_BATCH, MAIN_W)),
                  pl.BlockSpec((None, 3 * N_DIL, N_C_HEADS, HEAD_DIM), lambda b: (b, 0, 0, 0)),
                  pl.BlockSpec((1, DEC_BATCH, 2 * W_A), lambda b: (layer, 0, 0)),
                  full((CONV_W, W_A)), full((1, W_B)), full((1, W_B)), full((1, W_B)), full((1, W_B))]
                 + cache_specs,
        out_specs=[full((DEC_BATCH, W_A)), full((DEC_BATCH, W_B)),
                   pl.BlockSpec((None, N_C_HEADS, HEAD_DIM), lambda b: (b, 0, 0)),
                   full((DEC_BATCH, 2 * W_A)), full((DEC_BATCH, W_B))],
        out_shape=[jax.ShapeDtypeStruct((DEC_BATCH, W_A), BF16),
                   jax.ShapeDtypeStruct((DEC_BATCH, W_B), BF16),
                   jax.ShapeDtypeStruct((DEC_BATCH, N_C_HEADS, HEAD_DIM), F32),
                   jax.ShapeDtypeStruct((DEC_BATCH, 2 * W_A), F32),
                   jax.ShapeDtypeStruct((DEC_BATCH, W_B), F32)],
        compiler_params=_params("arbitrary"),
        name="mix_sample",
    )(proj_s, qkv, st, cw, lng, lnb, wsc, bsc, *caches)


def _combine_groups(o0, o1, o2, l0, l1, l2):
    m = jnp.maximum(jnp.maximum(l0, l1), l2)
    e0, e1, e2 = jnp.exp(l0 - m), jnp.exp(l1 - m), jnp.exp(l2 - m)
    return (e0 * o0 + e1 * o1 + e2 * o2) / (e0 + e1 + e2)


def _merge_tail(x, ga, gb, gc, ya, yb, yc, wa_ref, wb_ref, wc_ref, wo_ref, gp_ref):
    mm = (jax.nn.sigmoid(ga) * jnp.dot(ya, wa_ref[...], preferred_element_type=F32)
          + jax.nn.sigmoid(gb) * jnp.dot(yb, wb_ref[...], preferred_element_type=F32)
          + jax.nn.sigmoid(gc) * jnp.dot(yc, wc_ref[...], preferred_element_type=F32))
    r = jnp.dot(mm.astype(BF16), wo_ref[...], preferred_element_type=F32)
    return x + _rms(r, gp_ref[...])


def _merge_body(ga_ref, gb_ref, gc_ref, ya_ref, yb_ref, yc_ref, x_ref,
                ms_ref, yas_ref, ybs_ref, ycs_ref, xs_ref,
                wa_ref, wb_ref, wc_ref, wo_ref, gp_ref, out_ref, outs_ref):
    weights = (wa_ref, wb_ref, wc_ref, wo_ref, gp_ref)
    out_ref[...] = _merge_tail(x_ref[...], ga_ref[...].astype(F32), gb_ref[...].astype(F32),
                               gc_ref[...].astype(F32), ya_ref[...], yb_ref[...], yc_ref[...], *weights)

    @pl.when(pl.program_id(0) == 0)
    def _():
        gates = [ms_ref[:, k * D_MODEL:(k + 1) * D_MODEL] for k in range(3)]
        outs_ref[...] = _merge_tail(xs_ref[...], *gates, yas_ref[...], ybs_ref[...],
                                    ycs_ref[...].astype(BF16), *weights)


def _merge(proj, ya, yb, yc, x, main_s, ya_s, yb_s, yc_s, xs, wa, wb, wc, wo, gp, layer, tm):
    m = x.shape[0]

    def rows(w, k=0):
        return pl.BlockSpec((tm, w), lambda i: (i, k))

    def whole(a):
        return pl.BlockSpec(a.shape, lambda i: (0, 0))

    def resident(k):
        return pl.BlockSpec((k, D_MODEL), lambda i: (0, 0), pipeline_mode=pl.Buffered(1))

    return pl.pallas_call(
        _merge_body,
        grid=(m // tm,),
        in_specs=[rows(D_MODEL, 0), rows(D_MODEL, 1), rows(D_MODEL, 2), rows(W_A), rows(W_B), rows(C_WIDTH),
                  rows(D_MODEL),
                  whole(main_s), whole(ya_s), whole(yb_s), whole(yc_s), whole(xs),
                  resident(W_A), resident(W_B), resident(C_WIDTH), resident(D_MODEL),
                  pl.BlockSpec((None, 1, D_MODEL), lambda i: (layer, 0, 0), pipeline_mode=pl.Buffered(1))],
        out_specs=[rows(D_MODEL), whole(xs)],
        out_shape=[jax.ShapeDtypeStruct((m, D_MODEL), F32), jax.ShapeDtypeStruct(xs.shape, F32)],
        compiler_params=_params("arbitrary"),
        name="merge",
    )(proj, proj, proj, ya, yb, yc, x, main_s, ya_s, yb_s, yc_s, xs, wa, wb, wc, wo, gp)


def _swiglu_part(h, wg_ref, wu_ref, wo_ref):
    gate = jnp.dot(h, wg_ref[...], preferred_element_type=F32)
    up = jnp.dot(h, wu_ref[...], preferred_element_type=F32)
    act = (gate * jax.nn.sigmoid(gate) * up).astype(BF16)
    return jnp.dot(act, wo_ref[...], preferred_element_type=F32)


def _ffn_body(x_ref, g1_ref, wg_ref, wu_ref, wo_ref, g2_ref, xs_ref, *rest, n_cast):
    cast_src = rest[:n_cast]
    out_ref, outs_ref = rest[n_cast:n_cast + 2]
    cast_dst = rest[n_cast + 2:2 * n_cast + 2]
    h_ref, hs_ref = rest[-2:]
    i = pl.program_id(0)
    j = pl.program_id(1)
    last = pl.num_programs(1) - 1

    @pl.when(j == 0)
    def _():
        h_ref[...] = _rms(x_ref[...], g1_ref[...]).astype(BF16)
        out_ref[...] = jnp.zeros_like(out_ref)

    _cast_blocks(cast_src, cast_dst)
    out_ref[...] += _swiglu_part(h_ref[...], wg_ref, wu_ref, wo_ref)

    @pl.when(j == last)
    def _():
        out_ref[...] = x_ref[...] + _rms(out_ref[...], g2_ref[...])

    @pl.when(i == 0)
    def _():
        @pl.when(j == 0)
        def _():
            hs_ref[...] = _rms(xs_ref[...], g1_ref[...]).astype(BF16)
            outs_ref[...] = jnp.zeros_like(outs_ref)

        outs_ref[...] += _swiglu_part(hs_ref[...], wg_ref, wu_ref, wo_ref)

        @pl.when(j == last)
        def _():
            outs_ref[...] = xs_ref[...] + _rms(outs_ref[...], g2_ref[...])


def _ffn(x, xs, g1, w_in, w_out, g2, layer, tm, tf, casts=()):
    m = x.shape[0]
    ns = xs.shape[0]
    nj = D_FF // tf
    cast_specs = [_cast_specs(src, layer + 1, rows, nj, (m // tm) * nj) for src, rows in casts]
    res = pl.pallas_call(
        functools.partial(_ffn_body, n_cast=len(casts)),
        grid=(m // tm, nj),
        in_specs=[
            pl.BlockSpec((tm, D_MODEL), lambda i, j: (i, 0)),
            pl.BlockSpec((None, 1, D_MODEL), lambda i, j: (layer, 0, 0)),
            pl.BlockSpec((D_MODEL, tf), lambda i, j: (0, j)),
            pl.BlockSpec((D_MODEL, tf), lambda i, j: (0, nj + j)),
            pl.BlockSpec((tf, D_MODEL), lambda i, j: (j, 0)),
            pl.BlockSpec((None, 1, D_MODEL), lambda i, j: (layer, 0, 0)),
            pl.BlockSpec((ns, D_MODEL), lambda i, j: (0, 0)),
        ] + [c[0] for c in cast_specs],
        out_specs=[pl.BlockSpec((tm, D_MODEL), lambda i, j: (i, 0)),
                   pl.BlockSpec((ns, D_MODEL), lambda i, j: (0, 0))] + [c[1] for c in cast_specs],
        out_shape=[jax.ShapeDtypeStruct((m, D_MODEL), F32),
                   jax.ShapeDtypeStruct((ns, D_MODEL), F32)] + [c[2] for c in cast_specs],
        scratch_shapes=[pltpu.VMEM((tm, D_MODEL), BF16), pltpu.VMEM((ns, D_MODEL), BF16)],
        compiler_params=_params("arbitrary", "arbitrary"),
        name="ffn",
    )(x, g1, w_in, w_in, w_out, g2, xs, *[src for src, _ in casts])
    return res[0], res[1], res[2:]


TM_INPROJ = 1024
TS_MIX = 512
TM_MERGE = 256
TM_FFN = 512
TF_FFN = 512
CAST_ROWS_FFN_IN = 16
CAST_ROWS_FFN_OUT = 64
CAST_ROWS_W_IN = 16
CAST_ROWS_MERGE = 16


def kernel(x_prompt, x_sample, state_conv, cache_kv_w128, cache_kv_w512, cache_kv_w2048, g_pre_mix, w_in, conv_w, ln_g, ln_b, w_s, b_s, w_a_out, w_b_out, w_c_out, w_o, g_post_mix, g_pre_ffn, w_ffn_in, w_ffn_out, g_post_ffn):
    w_in_l = w_in[0].astype(BF16)
    inproj_casts = ((w_ffn_in, CAST_ROWS_FFN_IN), (w_ffn_out, CAST_ROWS_FFN_OUT),
                    (w_a_out, CAST_ROWS_MERGE), (w_b_out, CAST_ROWS_MERGE), (w_c_out, CAST_ROWS_MERGE),
                    (w_o, CAST_ROWS_MERGE))
    g_pre_mix, g_post_mix, g_pre_ffn, g_post_ffn = (
        g.reshape(DEPTH, 1, D_MODEL) for g in (g_pre_mix, g_post_mix, g_pre_ffn, g_post_ffn))

    tabs_p = _rope_tables(jnp.arange(SEQ, dtype=jnp.int32))
    tabs_s = _rope_tables(jnp.full((DEC_BATCH,), PAST_LEN, dtype=jnp.int32))

    n_keys = DIL_GROUPS[0][0] // DIL_GROUPS[0][1]
    caches = tuple(c.reshape(DEPTH, DEC_BATCH, n_keys, dil, 2, N_C_HEADS, HEAD_DIM)
                   for c, (_, dil) in zip((cache_kv_w128, cache_kv_w512, cache_kv_w2048), DIL_GROUPS))
    st_all = state_conv.reshape(DEPTH, DEC_BATCH, (CONV_W - 1) * W_A)
    bs_t = jnp.swapaxes(b_s, 1, 2)
    wsc = jnp.repeat(w_s[:, :, 0, 0], B_GROUP_W, axis=1)
    bsc = jnp.repeat(b_s[:, :, 0], B_GROUP_W, axis=1)

    xp = x_prompt.reshape(BATCH * SEQ, D_MODEL)
    xs = x_sample.reshape(DEC_BATCH, D_MODEL)
    conv_p, conv_s, vchunk_s = [], [], []
    kv_p = None
    kv_s = [[] for _ in DIL_GROUPS]

    def row(a, l):
        return a[l][None, :]

    for l in range(DEPTH):
        proj, qkv, proj_s, qkv_s, wfi_l, wfo_l, wa_l, wb_l, wc_l, wo_l = _inproj(
            xp, xs, g_pre_mix, w_in_l, l, tabs_p, tabs_s, TM_INPROJ, casts=inproj_casts)
        ya, yb, ctail, kv_p = _mix_ab_kv(proj, qkv, conv_w[l], row(ln_g, l), row(ln_b, l), w_s[l], bs_t[l], l, kv_p)
        yc = _attn(qkv)
        conv_p.append(ctail[:, 8 - (CONV_W - 1):])
        ya_s, yb_s, yc_s, cst, vn = _mix_sample(proj_s, qkv_s, st_all, conv_w[l], row(ln_g, l), row(ln_b, l),
                                                row(wsc, l), row(bsc, l), caches, l)
        for g in range(N_DIL):
            k = qkv_s[:, OFF_K + g * C_WIDTH:OFF_K + (g + 1) * C_WIDTH]
            v = qkv_s[:, OFF_V + g * C_WIDTH:OFF_V + (g + 1) * C_WIDTH]
            kv_s[g].append(jnp.stack([k.reshape(DEC_BATCH, 1, N_C_HEADS, HEAD_DIM),
                                      v.reshape(DEC_BATCH, 1, N_C_HEADS, HEAD_DIM)], axis=2))
        conv_s.append(cst.reshape(DEC_BATCH, CONV_W - 1, W_A))
        vchunk_s.append(vn.reshape(DEC_BATCH, 1, W_B))
        xp, xs = _merge(proj, ya, yb, yc, xp, proj_s, ya_s, yb_s, yc_s.reshape(DEC_BATCH, C_WIDTH), xs,
                        wa_l, wb_l, wc_l, wo_l, g_post_mix, l, TM_MERGE)
        next_casts = ((w_in, CAST_ROWS_W_IN),) if l + 1 < DEPTH else ()
        xp, xs, w_in_next = _ffn(xp, xs, g_pre_ffn, wfi_l, wfo_l, g_post_ffn, l, TM_FFN, TF_FFN, casts=next_casts)
        if w_in_next:
            w_in_l = w_in_next[0]

    return (xp.reshape(BATCH, SEQ, D_MODEL), xs.reshape(DEC_BATCH, 1, D_MODEL),
            jnp.stack(conv_p, axis=0),
            kv_p[0], kv_p[1], kv_p[2],
            jnp.stack(conv_s, axis=0),
            jnp.stack(kv_s[0], axis=0), jnp.stack(kv_s[1], axis=0), jnp.stack(kv_s[2], axis=0),
            jnp.stack(vchunk_s, axis=0))
```

```python
import functools

import jax
import jax.numpy as jnp
from jax import lax
from jax.experimental import pallas as pl
from jax.experimental.pallas import tpu as pltpu

D_MODEL = 2048
BATCH = 4
SEQ = 2048
DEPTH = 4
DEC_BATCH = 8
PAST_LEN = 16384
W_A = 1024
CONV_W = 3
W_B = 1024
CHUNK = 128
N_B_GROUPS = 4
B_GROUP_W = W_B // N_B_GROUPS
N_C_HEADS = 4
HEAD_DIM = 128
ROT_DIM = HEAD_DIM // 4
ROPE_THETA = 500000.0
DIL_GROUPS = ((128, 1), (512, 4), (2048, 16))
N_DIL = len(DIL_GROUPS)
QB = 128
C_WIDTH = N_C_HEADS * HEAD_DIM
QKV_W = N_DIL * C_WIDTH
D_FF = ((-(-8 * D_MODEL // 3) + 255) // 256) * 256
IN_WIDTH = 3 * W_A + 2 * W_B + 3 * QKV_W + 3 * D_MODEL
EPS = 1e-6

GATE_W = 3 * D_MODEL
OFF_AB = GATE_W
MAIN_W = GATE_W + 3 * W_A + 2 * W_B
OFF_Q, OFF_K, OFF_V = 0, QKV_W, 2 * QKV_W
ORIG_GATE_OFF = IN_WIDTH - GATE_W

ROPE_ROWS = 256
SOFTMAX_ROWS = 32
MERGE_ROWS = 64
V7X_VMEM_LIMIT = 60 * 1024 * 1024
NEG_BIG = -1e30

F32 = jnp.float32
BF16 = jnp.bfloat16


def _params(*sem):
    return pltpu.CompilerParams(dimension_semantics=sem, vmem_limit_bytes=V7X_VMEM_LIMIT)


def _rms(x, g):
    return x * lax.rsqrt(jnp.mean(x * x, axis=-1, keepdims=True) + EPS) * g


def _cast_specs(src, layer, rows, steps_per_row_tile, n_steps):
    _, r, c = src.shape
    nblk = r // rows
    assert r % rows == 0 and nblk <= n_steps

    def blk(i, j):
        return jnp.minimum(i * steps_per_row_tile + j, nblk - 1)

    return (pl.BlockSpec((None, rows, c), lambda i, j: (layer, blk(i, j), 0)),
            pl.BlockSpec((rows, c), lambda i, j: (blk(i, j), 0)),
            jax.ShapeDtypeStruct((r, c), BF16))


def _cast_blocks(src_refs, dst_refs):
    for s_ref, d_ref in zip(src_refs, dst_refs):
        d_ref[...] = s_ref[...].astype(BF16)


def _rope(a, c, s1, s2):
    return a * c + pltpu.roll(a, HEAD_DIM - ROT_DIM // 2, 1) * s1 + pltpu.roll(a, ROT_DIM // 2, 1) * s2


def _inproj_body(x_ref, g_ref, w0_ref, w1_ref, c_ref, s1_ref, s2_ref, xs_ref, cs_ref, s1s_ref, s2s_ref, *rest,
                 main_steps, rope_steps, n_cast):
    cast_src = rest[:n_cast]
    main_ref, qkv_ref, mains_ref, qkvs_ref = rest[n_cast:n_cast + 4]
    cast_dst = rest[n_cast + 4:2 * n_cast + 4]
    h_ref, hs_ref = rest[-2:]
    i = pl.program_id(0)
    j = pl.program_id(1)
    tn = w0_ref.shape[1]
    heads = tn // HEAD_DIM
    halves = ((w0_ref, slice(0, tn)), (w1_ref, slice(tn, 2 * tn)))
    with_samples = i == 0

    @pl.when(j == 0)
    def _():
        h_ref[...] = _rms(x_ref[...], g_ref[...]).astype(BF16)

    @pl.when(jnp.logical_and(with_samples, j == 0))
    def _():
        hs_ref[...] = _rms(xs_ref[...], g_ref[...]).astype(BF16)

    @pl.when(j < main_steps)
    def _():
        _cast_blocks(cast_src, cast_dst)
        for w_ref, cols in halves:
            main_ref[:, cols] = jnp.dot(h_ref[...], w_ref[...], preferred_element_type=F32).astype(main_ref.dtype)

        @pl.when(with_samples)
        def _():
            for w_ref, cols in halves:
                mains_ref[j, :, cols] = jnp.dot(hs_ref[...], w_ref[...], preferred_element_type=F32)

    @pl.when(jnp.logical_and(j >= main_steps, j < main_steps + rope_steps))
    def _():
        _cast_blocks(cast_src, cast_dst)
        tm = x_ref.shape[0]
        rc = min(tm, ROPE_ROWS)
        for w_ref, cols in halves:
            for r in range(tm // rc):
                rs = slice(r * rc, (r + 1) * rc)
                acc = jnp.dot(h_ref[rs, :], w_ref[...], preferred_element_type=F32)
                for h in range(heads):
                    qkv_ref[cols.start // HEAD_DIM + h, rs, :] = _rope(
                        acc[:, h * HEAD_DIM:(h + 1) * HEAD_DIM], c_ref[rs, :], s1_ref[rs, :], s2_ref[rs, :])

        @pl.when(with_samples)
        def _():
            for w_ref, cols in halves:
                acc = jnp.dot(hs_ref[...], w_ref[...], preferred_element_type=F32)
                for h in range(heads):
                    sl = slice(cols.start + h * HEAD_DIM, cols.start + (h + 1) * HEAD_DIM)
                    qkvs_ref[j - main_steps, :, sl] = _rope(acc[:, h * HEAD_DIM:(h + 1) * HEAD_DIM],
                                                            cs_ref[...], s1s_ref[...], s2s_ref[...])

    @pl.when(j >= main_steps + rope_steps)
    def _():
        _cast_blocks(cast_src, cast_dst)
        for w_ref, cols in halves:
            acc = jnp.dot(h_ref[...], w_ref[...], preferred_element_type=F32)
            for h in range(heads):
                qkv_ref[cols.start // HEAD_DIM + h] = acc[:, h * HEAD_DIM:(h + 1) * HEAD_DIM]

        @pl.when(with_samples)
        def _():
            for w_ref, cols in halves:
                qkvs_ref[j - main_steps, :, cols] = jnp.dot(hs_ref[...], w_ref[...], preferred_element_type=F32)


def _inproj(x, xs, g, w, layer, tabs, tabs_s, tm, casts=()):
    m = x.shape[0]
    ns = xs.shape[0]
    tn = C_WIDTH
    n_tiles = IN_WIDTH // tn
    rot = ORIG_GATE_OFF // tn
    main_steps = MAIN_W // (2 * tn)
    rope_steps = 2 * QKV_W // (2 * tn)
    qkv_steps = pl.cdiv(3 * QKV_W, 2 * tn)
    steps = main_steps + qkv_steps
    tab_spec = pl.BlockSpec((tm, HEAD_DIM), lambda i, j: (i % (SEQ // tm), 0))
    tab_s_spec = pl.BlockSpec((ns, HEAD_DIM), lambda i, j: (0, 0))

    def w_spec(half):
        return pl.BlockSpec((D_MODEL, tn), lambda i, j: (0, (2 * j + half + rot) % n_tiles))

    def x_tile(i, j):
        return jnp.minimum(i + jnp.where(j >= steps // 2, 1, 0), m // tm - 1)

    cast_specs = [_cast_specs(src, layer, rows, steps, (m // tm) * steps) for src, rows in casts]
    res = pl.pallas_call(
        functools.partial(_inproj_body, main_steps=main_steps, rope_steps=rope_steps, n_cast=len(casts)),
        grid=(m // tm, steps),
        in_specs=[
            pl.BlockSpec((tm, D_MODEL), lambda i, j: (x_tile(i, j), 0)),
            pl.BlockSpec((None, 1, D_MODEL), lambda i, j: (layer, 0, 0)),
            w_spec(0), w_spec(1),
            tab_spec, tab_spec, tab_spec,
            pl.BlockSpec((ns, D_MODEL), lambda i, j: (0, 0)),
            tab_s_spec, tab_s_spec, tab_s_spec,
        ] + [c[0] for c in cast_specs],
        out_specs=[pl.BlockSpec((tm, 2 * tn), lambda i, j: (i, jnp.minimum(j, main_steps - 1))),
                   pl.BlockSpec((2 * tn // HEAD_DIM, tm, HEAD_DIM),
                                lambda i, j: (jnp.maximum(j - main_steps, 0), i, 0)),
                   pl.BlockSpec((main_steps, ns, 2 * tn), lambda i, j: (0, 0, 0)),
                   pl.BlockSpec((qkv_steps, ns, 2 * tn), lambda i, j: (0, 0, 0))]
                  + [c[1] for c in cast_specs],
        out_shape=[jax.ShapeDtypeStruct((m, MAIN_W), BF16),
                   jax.ShapeDtypeStruct((3 * QKV_W // HEAD_DIM, m, HEAD_DIM), F32),
                   jax.ShapeDtypeStruct((main_steps, ns, 2 * tn), F32),
                   jax.ShapeDtypeStruct((qkv_steps, ns, 2 * tn), F32)] + [c[2] for c in cast_specs],
        scratch_shapes=[pltpu.VMEM((tm, D_MODEL), BF16), pltpu.VMEM((ns, D_MODEL), BF16)],
        compiler_params=_params("arbitrary", "arbitrary"),
        name="inproj",
    )(x, g, w, w, *tabs, xs, *tabs_s, *[src for src, _ in casts])
    main_s = jnp.swapaxes(res[2], 0, 1).reshape(ns, MAIN_W)
    qkv_s = jnp.swapaxes(res[3], 0, 1).reshape(ns, qkv_steps * 2 * tn)[:, :3 * QKV_W]
    return (res[0], res[1], main_s, qkv_s, *res[4:])


def _rope_tables(pos):
    inv_freq = ROPE_THETA ** (-jnp.arange(0, ROT_DIM, 2, dtype=jnp.float32) / ROT_DIM)
    ang = pos.astype(jnp.float32)[:, None] * inv_freq[None, :]
    cos, sin = jnp.cos(ang), jnp.sin(ang)
    n = pos.shape[0]
    half = ROT_DIM // 2
    rest = HEAD_DIM - ROT_DIM
    c = jnp.concatenate([cos, cos, jnp.ones((n, rest), F32)], axis=1)
    s1 = jnp.concatenate([-sin, jnp.zeros((n, half + rest), F32)], axis=1)
    s2 = jnp.concatenate([jnp.zeros((n, half), F32), sin, jnp.zeros((n, rest), F32)], axis=1)
    return c, s1, s2


def _mix_ab_body(ab_ref, ac_ref, ax_ref, bu_ref, bv_ref, cw_ref, lng_ref, lnb_ref, ws_ref, bs_ref,
                 ya_ref, yb_ref, cs_ref, zs_ref):
    ts = ab_ref.shape[0]

    @pl.when(pl.program_id(1) == 0)
    def _():
        zs_ref[0:8, :] = jnp.zeros((8, W_A), F32)

    z = ac_ref[...].astype(F32) * ax_ref[...].astype(F32)
    zs_ref[8:8 + ts, :] = z
    z1 = zs_ref[7:7 + ts, :]
    z2 = zs_ref[6:6 + ts, :]
    conv = cw_ref[0:1, :] * z2 + cw_ref[1:2, :] * z1 + cw_ref[2:3, :] * z
    ya_ref[...] = (ab_ref[...].astype(F32) * conv).astype(BF16)
    tail = zs_ref[ts:ts + 8, :]
    zs_ref[0:8, :] = tail
    cs_ref[0] = tail

    v = bv_ref[...].astype(F32)
    xc = v - jnp.mean(v, axis=-1, keepdims=True)
    var = jnp.mean(xc * xc, axis=-1, keepdims=True)
    vn = (xc * lax.rsqrt(var + EPS) * lng_ref[...] + lnb_ref[...]).astype(BF16)
    row = lax.broadcasted_iota(jnp.int32, (CHUNK, CHUNK), 0)
    col = lax.broadcasted_iota(jnp.int32, (CHUNK, CHUNK), 1)
    for g in range(N_B_GROUPS):
        wg = jnp.where(row >= col, ws_ref[g], 0.0).astype(BF16)
        bcol = bs_ref[:, g:g + 1]
        gs = slice(g * B_GROUP_W, (g + 1) * B_GROUP_W)
        for c in range(ts // CHUNK):
            rs = slice(c * CHUNK, (c + 1) * CHUNK)
            sg = jnp.dot(wg, vn[rs, gs], preferred_element_type=F32) + bcol
            yb_ref[rs, gs] = (bu_ref[rs, gs].astype(F32) * sg).astype(BF16)


def _attn_body(q0, q1, q2, k0, k1, k2, v0, v1, v2, y_ref,
               qc, kc, vc, s_scr, p_scr, inv_scr, o0, o1, o2, l0, l1, l2):
    row = lax.broadcasted_iota(jnp.int32, (QB, 2 * QB), 0)
    col = lax.broadcasted_iota(jnp.int32, (QB, 2 * QB), 1)
    band = jnp.logical_and(col >= row, col <= row + QB)
    band_first = jnp.logical_and(band, col >= QB)
    scale = HEAD_DIM ** -0.5
    nt = (((1,), (1,)), ((), ()))
    zero_blk = jnp.zeros((QB, HEAD_DIM), BF16)
    for (_, dil), q_ref, k_ref, v_ref, o_ref, l_ref in zip(
            DIL_GROUPS, (q0, q1, q2), (k0, k1, k2), (v0, v1, v2), (o0, o1, o2), (l0, l1, l2)):
        L = SEQ // dil
        nblk = L // QB

        def tok_rows(r, c):
            if dil == 1:
                return pl.ds(c * QB, QB)
            return pl.ds(r + c * QB * dil, QB, stride=dil)

        for r in range(dil):
            src = pl.ds(0, L) if dil == 1 else pl.ds(r, L, stride=dil)
            base = r * (L + QB)
            qc[r * L:(r + 1) * L, :] = q_ref[src, :].astype(BF16)
            kc[base:base + QB, :] = zero_blk
            vc[base:base + QB, :] = zero_blk
            kc[base + QB:base + QB + L, :] = k_ref[src, :].astype(BF16)
            vc[base + QB:base + QB + L, :] = v_ref[src, :].astype(BF16)

        for r in range(dil):
            for c in range(nblk):
                t = r * nblk + c
                kk = kc[r * (L + QB) + c * QB:r * (L + QB) + (c + 2) * QB, :]
                s = lax.dot_general(qc[t * QB:(t + 1) * QB, :], kk, nt, preferred_element_type=F32) * scale
                s_scr[t * QB:(t + 1) * QB, :] = jnp.where(band_first if c == 0 else band, s, NEG_BIG)

        for r in range(dil):
            for u in range(L // SOFTMAX_ROWS):
                rows = slice(r * L + u * SOFTMAX_ROWS, r * L + (u + 1) * SOFTMAX_ROWS)
                s = s_scr[rows, :]
                m = jnp.max(s, axis=-1, keepdims=True)
                p = jnp.exp(s - m)
                den = jnp.sum(p, axis=-1, keepdims=True)
                p_scr[rows, :] = p.astype(BF16)
                inv_scr[rows, :] = jnp.broadcast_to(1.0 / den, (SOFTMAX_ROWS, HEAD_DIM))
                tok = (pl.ds(u * SOFTMAX_ROWS, SOFTMAX_ROWS) if dil == 1
                       else pl.ds(r + u * SOFTMAX_ROWS * dil, SOFTMAX_ROWS, stride=dil))
                l_ref[tok, :] = jnp.broadcast_to(m + jnp.log(den), (SOFTMAX_ROWS, HEAD_DIM))

        for r in range(dil):
            for c in range(nblk):
                t = r * nblk + c
                vv = vc[r * (L + QB) + c * QB:r * (L + QB) + (c + 2) * QB, :]
                o = jnp.dot(p_scr[t * QB:(t + 1) * QB, :], vv, preferred_element_type=F32)
                o_ref[tok_rows(r, c), :] = o * inv_scr[t * QB:(t + 1) * QB, :]
    for u in range(SEQ // MERGE_ROWS):
        rows = slice(u * MERGE_ROWS, (u + 1) * MERGE_ROWS)
        y_ref[rows, :] = _combine_groups(o0[rows, :], o1[rows, :], o2[rows, :],
                                         l0[rows, :], l1[rows, :], l2[rows, :]).astype(BF16)


def _attn(proj):
    def seg(off, g):
        base = (off + g * C_WIDTH) // HEAD_DIM
        return pl.BlockSpec((None, SEQ, HEAD_DIM), lambda b, h: (base + h, b, 0))

    specs = [seg(off, g) for off in (OFF_Q, OFF_K, OFF_V) for g in range(N_DIL)]
    return pl.pallas_call(
        _attn_body,
        grid=(BATCH, N_C_HEADS),
        in_specs=specs,
        out_specs=pl.BlockSpec((SEQ, HEAD_DIM), lambda b, h: (b, h)),
        out_shape=jax.ShapeDtypeStruct((BATCH * SEQ, C_WIDTH), BF16),
        scratch_shapes=[pltpu.VMEM((SEQ, HEAD_DIM), BF16),
                        pltpu.VMEM((2 * SEQ, HEAD_DIM), BF16),
                        pltpu.VMEM((2 * SEQ, HEAD_DIM), BF16),
                        pltpu.VMEM((SEQ, 2 * QB), F32),
                        pltpu.VMEM((SEQ, 2 * QB), BF16),
                        pltpu.VMEM((SEQ, HEAD_DIM), F32)]
                       + [pltpu.VMEM((SEQ, HEAD_DIM), F32)] * (2 * N_DIL),
        compiler_params=_params("arbitrary", "arbitrary"),
        name="attn",
    )(*([proj] * (3 * N_DIL)))


def _kv_pack_body(*refs, fixed):
    k_refs, v_refs, out_refs = refs[0:N_DIL], refs[N_DIL:2 * N_DIL], refs[-N_DIL:]
    for k_ref, v_ref, o_ref, is_fixed in zip(k_refs, v_refs, out_refs, fixed):
        def pack(k_ref=k_ref, v_ref=v_ref, o_ref=o_ref):
            for h in range(N_C_HEADS):
                o_ref[:, 0, h, :] = k_ref[h]
                o_ref[:, 1, h, :] = v_ref[h]

        if is_fixed:
            pl.when(pl.program_id(1) == 0)(pack)
        else:
            pack()


N_MIX_IN, N_MIX_OUT = 10, 3


def _mix_ab_kv_body(*refs, fixed):
    n_kv_in = len(refs) - N_MIX_IN - N_MIX_OUT - N_DIL - 1
    mix_in, kv_in = refs[:N_MIX_IN], refs[N_MIX_IN:N_MIX_IN + 2 * N_DIL]
    outs = refs[N_MIX_IN + n_kv_in:-1]
    _mix_ab_body(*mix_in, *outs[:N_MIX_OUT], refs[-1])
    _kv_pack_body(*kv_in, *outs[N_MIX_OUT:], fixed=fixed)


def _mix_ab_kv(main, qkv, cw, lng, lnb, ws, bs_t, layer, prev):
    ts = TS_MIX
    nb = SEQ // ts
    blk = OFF_AB // W_A
    keeps = [min(win, SEQ) for win, _ in DIL_GROUPS]

    def seg(k):
        return pl.BlockSpec((ts, W_A), lambda b, s: (b * nb + s, blk + k))

    def full(shape):
        return pl.BlockSpec(shape, lambda b, s: (0,) * len(shape))

    rows = pl.BlockSpec((ts, W_A), lambda b, s: (b * nb + s, 0))
    mix_in = [seg(0), seg(1), seg(2), seg(3), seg(4),
              full((CONV_W, W_A)), full((1, W_B)), full((1, W_B)),
              full((N_B_GROUPS, CHUNK, CHUNK)), full((CHUNK, N_B_GROUPS))]
    mix_out = [rows, rows, pl.BlockSpec((1, 8, W_A), lambda b, s: (b, 0, 0))]
    mix_shapes = [jax.ShapeDtypeStruct((BATCH * SEQ, W_A), BF16),
                  jax.ShapeDtypeStruct((BATCH * SEQ, W_B), BF16),
                  jax.ShapeDtypeStruct((BATCH, 8, W_A), F32)]
    assert len(mix_in) == N_MIX_IN and len(mix_out) == N_MIX_OUT

    def src(off, g):
        keep, col = keeps[g], (off + g * C_WIDTH) // C_WIDTH
        if keep <= ts:
            nb = SEQ // keep
            return pl.BlockSpec((N_C_HEADS, keep, HEAD_DIM), lambda b, s: (col, b * nb + nb - 1, 0))
        nb, first = SEQ // ts, (SEQ - keep) // ts
        return pl.BlockSpec((N_C_HEADS, ts, HEAD_DIM), lambda b, s: (col, b * nb + jnp.maximum(s, first), 0))

    def dst(keep):
        if keep <= ts:
            return pl.BlockSpec((None, None, keep, 2, N_C_HEADS, HEAD_DIM), lambda b, s: (layer, b, 0, 0, 0, 0))
        first = (SEQ - keep) // ts
        return pl.BlockSpec((None, None, ts, 2, N_C_HEADS, HEAD_DIM),
                            lambda b, s: (layer, b, jnp.maximum(s - first, 0), 0, 0, 0))

    in_specs = mix_in + [src(OFF_K, g) for g in range(N_DIL)] + [src(OFF_V, g) for g in range(N_DIL)]
    args = [main] * 5 + [cw, lng, lnb, ws, bs_t] + [qkv] * (2 * N_DIL)
    aliases = {}
    if prev is not None:
        aliases = {len(in_specs) + g: N_MIX_OUT + g for g in range(N_DIL)}
        in_specs += [pl.BlockSpec(memory_space=pl.ANY)] * N_DIL
        args += list(prev)
    res = pl.pallas_call(
        functools.partial(_mix_ab_kv_body, fixed=tuple(keep <= ts for keep in keeps)),
        grid=(BATCH, nb),
        in_specs=in_specs,
        out_specs=mix_out + [dst(keep) for keep in keeps],
        out_shape=mix_shapes + [jax.ShapeDtypeStruct((DEPTH, BATCH, keep, 2, N_C_HEADS, HEAD_DIM), F32)
                                for keep in keeps],
        input_output_aliases=aliases,
        scratch_shapes=[pltpu.VMEM((ts + 8, W_A), F32)],
        compiler_params=_params("arbitrary", "arbitrary"),
        name="mix_ab_kv",
    )(*args)
    return res[0], res[1], res[2], res[N_MIX_OUT:]


def _mix_sample_body(p_ref, qkv_ref, st_ref, cw_ref, lng_ref, lnb_ref, wsc_ref, bsc_ref, c0_ref, c1_ref, c2_ref,
                     ya_ref, yb_ref, yc_ref, cs_ref, vn_ref):
    @pl.when(pl.program_id(0) == 0)
    def _():
        def seg(off, w):
            return p_ref[:, off:off + w]

        z = seg(OFF_AB + W_A, W_A) * seg(OFF_AB + 2 * W_A, W_A)
        st0 = st_ref[0, :, 0:W_A]
        st1 = st_ref[0, :, W_A:2 * W_A]
        conv = cw_ref[0:1, :] * st0 + cw_ref[1:2, :] * st1 + cw_ref[2:3, :] * z
        ya_ref[...] = (seg(OFF_AB, W_A) * conv).astype(BF16)
        cs_ref[:, 0:W_A] = st1
        cs_ref[:, W_A:2 * W_A] = z

        v = seg(OFF_AB + 3 * W_A + W_B, W_B)
        xc = v - jnp.mean(v, axis=-1, keepdims=True)
        var = jnp.mean(xc * xc, axis=-1, keepdims=True)
        vn = xc * lax.rsqrt(var + EPS) * lng_ref[...] + lnb_ref[...]
        vn_ref[...] = vn
        yb_ref[...] = (seg(OFF_AB + 3 * W_A, W_B) * (wsc_ref[...] * vn + bsc_ref[...])).astype(BF16)

    scale = HEAD_DIM ** -0.5
    outs, lses = [], []
    for g, c_ref in enumerate((c0_ref, c1_ref, c2_ref)):
        q = qkv_ref[g]
        kn = qkv_ref[N_DIL + g]
        vnew = qkv_ref[2 * N_DIL + g]
        kc = c_ref[:, 0]
        vc = c_ref[:, 1]
        s_c = jnp.sum(q[None] * kc, axis=-1, keepdims=True) * scale
        s_n = jnp.sum(q * kn, axis=-1, keepdims=True) * scale
        m = jnp.maximum(jnp.max(s_c, axis=0), s_n)
        p_c = jnp.exp(s_c - m[None])
        p_n = jnp.exp(s_n - m)
        den = jnp.sum(p_c, axis=0) + p_n
        outs.append((jnp.sum(p_c * vc, axis=0) + p_n * vnew) / den)
        lses.append(m + jnp.log(den))
    yc_ref[...] = _combine_groups(*outs, *lses)


def _mix_sample(proj_s, qkv_s, st, cw, lng, lnb, wsc, bsc, caches, layer):
    def full(shape):
        return pl.BlockSpec(shape, lambda b: (0,) * len(shape))

    n_keys = DIL_GROUPS[0][0] // DIL_GROUPS[0][1]
    cache_specs = [pl.BlockSpec((None, None, n_keys, None, 2, N_C_HEADS, HEAD_DIM),
                                lambda b: (layer, b, 0, 0, 0, 0, 0)) for _ in caches]
    qkv = qkv_s.reshape(DEC_BATCH, 3 * N_DIL, N_C_HEADS, HEAD_DIM)
    return pl.pallas_call(
        _mix_sample_body,
        grid=(DEC_BATCH,),
        in_specs=[full((DEC_BATCH, MAIN_W)),
                  pl.BlockSpec((None, 3 * N_DIL, N_C_HEADS, HEAD_DIM), lambda b: (b, 0, 0, 0)),
                  pl.BlockSpec((1, DEC_BATCH, 2 * W_A), lambda b: (layer, 0, 0)),
                  full((CONV_W, W_A)), full((1, W_B)), full((1, W_B)), full((1, W_B)), full((1, W_B))]
                 + cache_specs,
        out_specs=[full((DEC_BATCH, W_A)), full((DEC_BATCH, W_B)),
                   pl.BlockSpec((None, N_C_HEADS, HEAD_DIM), lambda b: (b, 0, 0)),
                   full((DEC_BATCH, 2 * W_A)), full((DEC_BATCH, W_B))],
        out_shape=[jax.ShapeDtypeStruct((DEC_BATCH, W_A), BF16),
                   jax.ShapeDtypeStruct((DEC_BATCH, W_B), BF16),
                   jax.ShapeDtypeStruct((DEC_BATCH, N_C_HEADS, HEAD_DIM), F32),
                   jax.ShapeDtypeStruct((DEC_BATCH, 2 * W_A), F32),
                   jax.ShapeDtypeStruct((DEC_BATCH, W_B), F32)],
        compiler_params=_params("arbitrary"),
        name="mix_sample",
    )(proj_s, qkv, st, cw, lng, lnb, wsc, bsc, *caches)


def _combine_groups(o0, o1, o2, l0, l1, l2):
    m = jnp.maximum(jnp.maximum(l0, l1), l2)
    e0, e1, e2 = jnp.exp(l0 - m), jnp.exp(l1 - m), jnp.exp(l2 - m)
    return (e0 * o0 + e1 * o1 + e2 * o2) / (e0 + e1 + e2)


def _merge_tail(x, ga, gb, gc, ya, yb, yc, wa_ref, wb_ref, wc_ref, wo_ref, gp_ref):
    mm = (jax.nn.sigmoid(ga) * jnp.dot(ya, wa_ref[...], preferred_element_type=F32)
          + jax.nn.sigmoid(gb) * jnp.dot(yb, wb_ref[...], preferred_element_type=F32)
          + jax.nn.sigmoid(gc) * jnp.dot(yc, wc_ref[...], preferred_element_type=F32))
    r = jnp.dot(mm.astype(BF16), wo_ref[...], preferred_element_type=F32)
    return x + _rms(r, gp_ref[...])


def _merge_body(ga_ref, gb_ref, gc_ref, ya_ref, yb_ref, yc_ref, x_ref,
                ms_ref, yas_ref, ybs_ref, ycs_ref, xs_ref,
                wa_ref, wb_ref, wc_ref, wo_ref, gp_ref, out_ref, outs_ref):
    weights = (wa_ref, wb_ref, wc_ref, wo_ref, gp_ref)
    out_ref[...] = _merge_tail(x_ref[...], ga_ref[...].astype(F32), gb_ref[...].astype(F32),
                               gc_ref[...].astype(F32), ya_ref[...], yb_ref[...], yc_ref[...], *weights)

    @pl.when(pl.program_id(0) == 0)
    def _():
        gates = [ms_ref[:, k * D_MODEL:(k + 1) * D_MODEL] for k in range(3)]
        outs_ref[...] = _merge_tail(xs_ref[...], *gates, yas_ref[...], ybs_ref[...],
                                    ycs_ref[...].astype(BF16), *weights)


def _merge(proj, ya, yb, yc, x, main_s, ya_s, yb_s, yc_s, xs, wa, wb, wc, wo, gp, layer, tm):
    m = x.shape[0]

    def rows(w, k=0):
        return pl.BlockSpec((tm, w), lambda i: (i, k))

    def whole(a):
        return pl.BlockSpec(a.shape, lambda i: (0, 0))

    def resident(k):
        return pl.BlockSpec((k, D_MODEL), lambda i: (0, 0), pipeline_mode=pl.Buffered(1))

    return pl.pallas_call(
        _merge_body,
        grid=(m // tm,),
        in_specs=[rows(D_MODEL, 0), rows(D_MODEL, 1), rows(D_MODEL, 2), rows(W_A), rows(W_B), rows(C_WIDTH),
                  rows(D_MODEL),
                  whole(main_s), whole(ya_s), whole(yb_s), whole(yc_s), whole(xs),
                  resident(W_A), resident(W_B), resident(C_WIDTH), resident(D_MODEL),
                  pl.BlockSpec((None, 1, D_MODEL), lambda i: (layer, 0, 0), pipeline_mode=pl.Buffered(1))],
        out_specs=[rows(D_MODEL), whole(xs)],
        out_shape=[jax.ShapeDtypeStruct((m, D_MODEL), F32), jax.ShapeDtypeStruct(xs.shape, F32)],
        compiler_params=_params("arbitrary"),
        name="merge",
    )(proj, proj, proj, ya, yb, yc, x, main_s, ya_s, yb_s, yc_s, xs, wa, wb, wc, wo, gp)


def _swiglu_part(h, wg_ref, wu_ref, wo_ref):
    gate = jnp.dot(h, wg_ref[...], preferred_element_type=F32)
    up = jnp.dot(h, wu_ref[...], preferred_element_type=F32)
    act = (gate * jax.nn.sigmoid(gate) * up).astype(BF16)
    return jnp.dot(act, wo_ref[...], preferred_element_type=F32)


def _ffn_body(x_ref, g1_ref, wg_ref, wu_ref, wo_ref, g2_ref, xs_ref, *rest, n_cast):
    cast_src = rest[:n_cast]
    out_ref, outs_ref = rest[n_cast:n_cast + 2]
    cast_dst = rest[n_cast + 2:2 * n_cast + 2]
    h_ref, hs_ref = rest[-2:]
    i = pl.program_id(0)
    j = pl.program_id(1)
    last = pl.num_programs(1) - 1

    @pl.when(j == 0)
    def _():
        h_ref[...] = _rms(x_ref[...], g1_ref[...]).astype(BF16)
        out_ref[...] = jnp.zeros_like(out_ref)

    _cast_blocks(cast_src, cast_dst)
    out_ref[...] += _swiglu_part(h_ref[...], wg_ref, wu_ref, wo_ref)

    @pl.when(j == last)
    def _():
        out_ref[...] = x_ref[...] + _rms(out_ref[...], g2_ref[...])

    @pl.when(i == 0)
    def _():
        @pl.when(j == 0)
        def _():
            hs_ref[...] = _rms(xs_ref[...], g1_ref[...]).astype(BF16)
            outs_ref[...] = jnp.zeros_like(outs_ref)

        outs_ref[...] += _swiglu_part(hs_ref[...], wg_ref, wu_ref, wo_ref)

        @pl.when(j == last)
        def _():
            outs_ref[...] = xs_ref[...] + _rms(outs_ref[...], g2_ref[...])


def _ffn(x, xs, g1, w_in, w_out, g2, layer, tm, tf, casts=()):
    m = x.shape[0]
    ns = xs.shape[0]
    nj = D_FF // tf
    cast_specs = [_cast_specs(src, layer + 1, rows, nj, (m // tm) * nj) for src, rows in casts]
    res = pl.pallas_call(
        functools.partial(_ffn_body, n_cast=len(casts)),
        grid=(m // tm, nj),
        in_specs=[
            pl.BlockSpec((tm, D_MODEL), lambda i, j: (i, 0)),
            pl.BlockSpec((None, 1, D_MODEL), lambda i, j: (layer, 0, 0)),
            pl.BlockSpec((D_MODEL, tf), lambda i, j: (0, j)),
            pl.BlockSpec((D_MODEL, tf), lambda i, j: (0, nj + j)),
            pl.BlockSpec((tf, D_MODEL), lambda i, j: (j, 0)),
            pl.BlockSpec((None, 1, D_MODEL), lambda i, j: (layer, 0, 0)),
            pl.BlockSpec((ns, D_MODEL), lambda i, j: (0, 0)),
        ] + [c[0] for c in cast_specs],
        out_specs=[pl.BlockSpec((tm, D_MODEL), lambda i, j: (i, 0)),
                   pl.BlockSpec((ns, D_MODEL), lambda i, j: (0, 0))] + [c[1] for c in cast_specs],
        out_shape=[jax.ShapeDtypeStruct((m, D_MODEL), F32),
                   jax.ShapeDtypeStruct((ns, D_MODEL), F32)] + [c[2] for c in cast_specs],
        scratch_shapes=[pltpu.VMEM((tm, D_MODEL), BF16), pltpu.VMEM((ns, D_MODEL), BF16)],
        compiler_params=_params("arbitrary", "arbitrary"),
        name="ffn",
    )(x, g1, w_in, w_in, w_out, g2, xs, *[src for src, _ in casts])
    return res[0], res[1], res[2:]


TM_INPROJ = 1024
TS_MIX = 512
TM_MERGE = 256
TM_FFN = 512
TF_FFN = 512
CAST_ROWS_FFN_IN = 16
CAST_ROWS_FFN_OUT = 64
CAST_ROWS_W_IN = 16
CAST_ROWS_MERGE = 16


def kernel(x_prompt, x_sample, state_conv, cache_kv_w128, cache_kv_w512, cache_kv_w2048, g_pre_mix, w_in, conv_w, ln_g, ln_b, w_s, b_s, w_a_out, w_b_out, w_c_out, w_o, g_post_mix, g_pre_ffn, w_ffn_in, w_ffn_out, g_post_ffn):
    w_in_l = w_in[0].astype(BF16)
    inproj_casts = ((w_ffn_in, CAST_ROWS_FFN_IN), (w_ffn_out, CAST_ROWS_FFN_OUT),
                    (w_a_out, CAST_ROWS_MERGE), (w_b_out, CAST_ROWS_MERGE), (w_c_out, CAST_ROWS_MERGE),
                    (w_o, CAST_ROWS_MERGE))
    g_pre_mix, g_post_mix, g_pre_ffn, g_post_ffn = (
        g.reshape(DEPTH, 1, D_MODEL) for g in (g_pre_mix, g_post_mix, g_pre_ffn, g_post_ffn))

    tabs_p = _rope_tables(jnp.arange(SEQ, dtype=jnp.int32))
    tabs_s = _rope_tables(jnp.full((DEC_BATCH,), PAST_LEN, dtype=jnp.int32))

    n_keys = DIL_GROUPS[0][0] // DIL_GROUPS[0][1]
    caches = tuple(c.reshape(DEPTH, DEC_BATCH, n_keys, dil, 2, N_C_HEADS, HEAD_DIM)
                   for c, (_, dil) in zip((cache_kv_w128, cache_kv_w512, cache_kv_w2048), DIL_GROUPS))
    st_all = state_conv.reshape(DEPTH, DEC_BATCH, (CONV_W - 1) * W_A)
    bs_t = jnp.swapaxes(b_s, 1, 2)
    wsc = jnp.repeat(w_s[:, :, 0, 0], B_GROUP_W, axis=1)
    bsc = jnp.repeat(b_s[:, :, 0], B_GROUP_W, axis=1)

    xp = x_prompt.reshape(BATCH * SEQ, D_MODEL)
    xs = x_sample.reshape(DEC_BATCH, D_MODEL)
    conv_p, conv_s, vchunk_s = [], [], []
    kv_p = None
    kv_s = [[] for _ in DIL_GROUPS]

    def row(a, l):
        return a[l][None, :]

    for l in range(DEPTH):
        proj, qkv, proj_s, qkv_s, wfi_l, wfo_l, wa_l, wb_l, wc_l, wo_l = _inproj(
            xp, xs, g_pre_mix, w_in_l, l, tabs_p, tabs_s, TM_INPROJ, casts=inproj_casts)
        ya, yb, ctail, kv_p = _mix_ab_kv(proj, qkv, conv_w[l], row(ln_g, l), row(ln_b, l), w_s[l], bs_t[l], l, kv_p)
        yc = _attn(qkv)
        conv_p.append(ctail[:, 8 - (CONV_W - 1):])
        ya_s, yb_s, yc_s, cst, vn = _mix_sample(proj_s, qkv_s, st_all, conv_w[l], row(ln_g, l), row(ln_b, l),
                                                row(wsc, l), row(bsc, l), caches, l)
        for g in range(N_DIL):
            k = qkv_s[:, OFF_K + g * C_WIDTH:OFF_K + (g + 1) * C_WIDTH]
            v = qkv_s[:, OFF_V + g * C_WIDTH:OFF_V + (g + 1) * C_WIDTH]
            kv_s[g].append(jnp.stack([k.reshape(DEC_BATCH, 1, N_C_HEADS, HEAD_DIM),
                                      v.reshape(DEC_BATCH, 1, N_C_HEADS, HEAD_DIM)], axis=2))
        conv_s.append(cst.reshape(DEC_BATCH, CONV_W - 1, W_A))
        vchunk_s.append(vn.reshape(DEC_BATCH, 1, W_B))
        xp, xs = _merge(proj, ya, yb, yc, xp, proj_s, ya_s, yb_s, yc_s.reshape(DEC_BATCH, C_WIDTH), xs,
                        wa_l, wb_l, wc_l, wo_l, g_post_mix, l, TM_MERGE)
        next_casts = ((w_in, CAST_ROWS_W_IN),) if l + 1 < DEPTH else ()
        xp, xs, w_in_next = _ffn(xp, xs, g_pre_ffn, wfi_l, wfo_l, g_post_ffn, l, TM_FFN, TF_FFN, casts=next_casts)
        if w_in_next:
            w_in_l = w_in_next[0]

    return (xp.reshape(BATCH, SEQ, D_MODEL), xs.reshape(DEC_BATCH, 1, D_MODEL),
            jnp.stack(conv_p, axis=0),
            kv_p[0], kv_p[1], kv_p[2],
            jnp.stack(conv_s, axis=0),
            jnp.stack(kv_s[0], axis=0), jnp.stack(kv_s[1], axis=0), jnp.stack(kv_s[2], axis=0),
            jnp.stack(vchunk_s, axis=0))
```

```python
import functools

import jax
import jax.numpy as jnp
from jax import lax
from jax.experimental import pallas as pl
from jax.experimental.pallas import tpu as pltpu

D_MODEL = 2048
BATCH = 4
SEQ = 2048
DEPTH = 4
DEC_BATCH = 8
PAST_LEN = 16384
W_A = 1024
CONV_W = 3
W_B = 1024
CHUNK = 128
N_B_GROUPS = 4
B_GROUP_W = W_B // N_B_GROUPS
N_C_HEADS = 4
HEAD_DIM = 128
ROT_DIM = HEAD_DIM // 4
ROPE_THETA = 500000.0
DIL_GROUPS = ((128, 1), (512, 4), (2048, 16))
N_DIL = len(DIL_GROUPS)
QB = 128
C_WIDTH = N_C_HEADS * HEAD_DIM
QKV_W = N_DIL * C_WIDTH
D_FF = ((-(-8 * D_MODEL // 3) + 255) // 256) * 256
IN_WIDTH = 3 * W_A + 2 * W_B + 3 * QKV_W + 3 * D_MODEL
EPS = 1e-6

GATE_W = 3 * D_MODEL
OFF_AB = GATE_W
MAIN_W = GATE_W + 3 * W_A + 2 * W_B
OFF_Q, OFF_K, OFF_V = 0, QKV_W, 2 * QKV_W
ORIG_GATE_OFF = IN_WIDTH - GATE_W

ROPE_ROWS = 256
SOFTMAX_ROWS = 32
MERGE_ROWS = 64
F32_SUBLANES = 8
V7X_VMEM_LIMIT = 60 * 1024 * 1024
NEG_BIG = -1e30

F32 = jnp.float32
BF16 = jnp.bfloat16


def _params(*sem):
    return pltpu.CompilerParams(dimension_semantics=sem, vmem_limit_bytes=V7X_VMEM_LIMIT)


def _rms(x, g):
    return x * lax.rsqrt(jnp.mean(x * x, axis=-1, keepdims=True) + EPS) * g


def _cast_specs(src, layer, rows, steps_per_row_tile, n_steps):
    _, r, c = src.shape
    nblk = r // rows
    assert r % rows == 0 and nblk <= n_steps

    def blk(i, j):
        return jnp.minimum(i * steps_per_row_tile + j, nblk - 1)

    return (pl.BlockSpec((None, rows, c), lambda i, j: (layer, blk(i, j), 0)),
            pl.BlockSpec((rows, c), lambda i, j: (blk(i, j), 0)),
            jax.ShapeDtypeStruct((r, c), BF16))


def _cast_blocks(src_refs, dst_refs):
    for s_ref, d_ref in zip(src_refs, dst_refs):
        d_ref[...] = s_ref[...].astype(BF16)


def _rope(a, c, s1, s2):
    return a * c + pltpu.roll(a, HEAD_DIM - ROT_DIM // 2, 1) * s1 + pltpu.roll(a, ROT_DIM // 2, 1) * s2


def _inproj_body(x_ref, g_ref, w0_ref, w1_ref, c_ref, s1_ref, s2_ref, xs_ref, cs_ref, s1s_ref, s2s_ref, *rest,
                 main_steps, rope_steps, n_cast):
    cast_src = rest[:n_cast]
    main_ref, qkv_ref, mains_ref, qkvs_ref = rest[n_cast:n_cast + 4]
    cast_dst = rest[n_cast + 4:2 * n_cast + 4]
    h_ref, hs_ref = rest[-2:]
    i = pl.program_id(0)
    j = pl.program_id(1)
    tn = w0_ref.shape[1]
    heads = tn // HEAD_DIM
    halves = ((w0_ref, slice(0, tn)), (w1_ref, slice(tn, 2 * tn)))
    with_samples = i == 0

    @pl.when(j == 0)
    def _():
        h_ref[...] = _rms(x_ref[...], g_ref[...]).astype(BF16)

    @pl.when(jnp.logical_and(with_samples, j == 0))
    def _():
        hs_ref[...] = _rms(xs_ref[...], g_ref[...]).astype(BF16)

    @pl.when(j < main_steps)
    def _():
        _cast_blocks(cast_src, cast_dst)
        for w_ref, cols in halves:
            main_ref[:, cols] = jnp.dot(h_ref[...], w_ref[...], preferred_element_type=F32).astype(main_ref.dtype)

        @pl.when(with_samples)
        def _():
            for w_ref, cols in halves:
                mains_ref[j, :, cols] = jnp.dot(hs_ref[...], w_ref[...], preferred_element_type=F32)

    @pl.when(jnp.logical_and(j >= main_steps, j < main_steps + rope_steps))
    def _():
        _cast_blocks(cast_src, cast_dst)
        tm = x_ref.shape[0]
        rc = min(tm, ROPE_ROWS)
        for w_ref, cols in halves:
            for r in range(tm // rc):
                rs = slice(r * rc, (r + 1) * rc)
                acc = jnp.dot(h_ref[rs, :], w_ref[...], preferred_element_type=F32)
                for h in range(heads):
                    qkv_ref[cols.start // HEAD_DIM + h, rs, :] = _rope(
                        acc[:, h * HEAD_DIM:(h + 1) * HEAD_DIM], c_ref[rs, :], s1_ref[rs, :], s2_ref[rs, :])

        @pl.when(with_samples)
        def _():
            for w_ref, cols in halves:
                acc = jnp.dot(hs_ref[...], w_ref[...], preferred_element_type=F32)
                for h in range(heads):
                    sl = slice(cols.start + h * HEAD_DIM, cols.start + (h + 1) * HEAD_DIM)
                    qkvs_ref[j - main_steps, :, sl] = _rope(acc[:, h * HEAD_DIM:(h + 1) * HEAD_DIM],
                                                            cs_ref[...], s1s_ref[...], s2s_ref[...])

    @pl.when(j >= main_steps + rope_steps)
    def _():
        _cast_blocks(cast_src, cast_dst)
        for w_ref, cols in halves:
            acc = jnp.dot(h_ref[...], w_ref[...], preferred_element_type=F32)
            for h in range(heads):
                qkv_ref[cols.start // HEAD_DIM + h] = acc[:, h * HEAD_DIM:(h + 1) * HEAD_DIM]

        @pl.when(with_samples)
        def _():
            for w_ref, cols in halves:
                qkvs_ref[j - main_steps, :, cols] = jnp.dot(hs_ref[...], w_ref[...], preferred_element_type=F32)


def _inproj(x, xs, g, w, layer, tabs, tabs_s, tm, casts=()):
    m = x.shape[0]
    ns = xs.shape[0]
    tn = C_WIDTH
    n_tiles = IN_WIDTH // tn
    rot = ORIG_GATE_OFF // tn
    main_steps = MAIN_W // (2 * tn)
    rope_steps = 2 * QKV_W // (2 * tn)
    qkv_steps = pl.cdiv(3 * QKV_W, 2 * tn)
    steps = main_steps + qkv_steps
    tab_spec = pl.BlockSpec((tm, HEAD_DIM), lambda i, j: (i % (SEQ // tm), 0))
    tab_s_spec = pl.BlockSpec((ns, HEAD_DIM), lambda i, j: (0, 0))

    def w_spec(half):
        return pl.BlockSpec((D_MODEL, tn), lambda i, j: (0, (2 * j + half + rot) % n_tiles))

    def x_tile(i, j):
        return jnp.minimum(i + jnp.where(j >= steps // 2, 1, 0), m // tm - 1)

    cast_specs = [_cast_specs(src, layer, rows, steps, (m // tm) * steps) for src, rows in casts]
    res = pl.pallas_call(
        functools.partial(_inproj_body, main_steps=main_steps, rope_steps=rope_steps, n_cast=len(casts)),
        grid=(m // tm, steps),
        in_specs=[
            pl.BlockSpec((tm, D_MODEL), lambda i, j: (x_tile(i, j), 0)),
            pl.BlockSpec((None, 1, D_MODEL), lambda i, j: (layer, 0, 0)),
            w_spec(0), w_spec(1),
            tab_spec, tab_spec, tab_spec,
            pl.BlockSpec((ns, D_MODEL), lambda i, j: (0, 0)),
            tab_s_spec, tab_s_spec, tab_s_spec,
        ] + [c[0] for c in cast_specs],
        out_specs=[pl.BlockSpec((tm, 2 * tn), lambda i, j: (i, jnp.minimum(j, main_steps - 1))),
                   pl.BlockSpec((2 * tn // HEAD_DIM, tm, HEAD_DIM),
                                lambda i, j: (jnp.maximum(j - main_steps, 0), i, 0)),
                   pl.BlockSpec((main_steps, ns, 2 * tn), lambda i, j: (0, 0, 0)),
                   pl.BlockSpec((qkv_steps, ns, 2 * tn), lambda i, j: (0, 0, 0))]
                  + [c[1] for c in cast_specs],
        out_shape=[jax.ShapeDtypeStruct((m, MAIN_W), BF16),
                   jax.ShapeDtypeStruct((3 * QKV_W // HEAD_DIM, m, HEAD_DIM), F32),
                   jax.ShapeDtypeStruct((main_steps, ns, 2 * tn), F32),
                   jax.ShapeDtypeStruct((qkv_steps, ns, 2 * tn), F32)] + [c[2] for c in cast_specs],
        scratch_shapes=[pltpu.VMEM((tm, D_MODEL), BF16), pltpu.VMEM((ns, D_MODEL), BF16)],
        compiler_params=_params("arbitrary", "arbitrary"),
        name="inproj",
    )(x, g, w, w, *tabs, xs, *tabs_s, *[src for src, _ in casts])
    main_s = jnp.swapaxes(res[2], 0, 1).reshape(ns, MAIN_W)
    qkv_s = jnp.swapaxes(res[3], 0, 1).reshape(ns, qkv_steps * 2 * tn)[:, :3 * QKV_W]
    return (res[0], res[1], main_s, qkv_s, *res[4:])


def _rope_tables(pos):
    inv_freq = ROPE_THETA ** (-jnp.arange(0, ROT_DIM, 2, dtype=jnp.float32) / ROT_DIM)
    ang = pos.astype(jnp.float32)[:, None] * inv_freq[None, :]
    cos, sin = jnp.cos(ang), jnp.sin(ang)
    n = pos.shape[0]
    half = ROT_DIM // 2
    rest = HEAD_DIM - ROT_DIM
    c = jnp.concatenate([cos, cos, jnp.ones((n, rest), F32)], axis=1)
    s1 = jnp.concatenate([-sin, jnp.zeros((n, half + rest), F32)], axis=1)
    s2 = jnp.concatenate([jnp.zeros((n, half), F32), sin, jnp.zeros((n, rest), F32)], axis=1)
    return c, s1, s2


def _mix_ab_body(ab_ref, ac_ref, ax_ref, bu_ref, bv_ref, cw_ref, lng_ref, lnb_ref, ws_ref, bs_ref,
                 ya_ref, yb_ref, cs_ref, zs_ref):
    ts = ab_ref.shape[0]
    lead = F32_SUBLANES

    @pl.when(pl.program_id(1) == 0)
    def _():
        zs_ref[0:lead, :] = jnp.zeros((lead, W_A), F32)

    z = ac_ref[...].astype(F32) * ax_ref[...].astype(F32)
    zs_ref[lead:lead + ts, :] = z
    z1 = zs_ref[lead - 1:lead - 1 + ts, :]
    z2 = zs_ref[lead - 2:lead - 2 + ts, :]
    conv = cw_ref[0:1, :] * z2 + cw_ref[1:2, :] * z1 + cw_ref[2:3, :] * z
    ya_ref[...] = (ab_ref[...].astype(F32) * conv).astype(BF16)
    tail = zs_ref[ts:ts + lead, :]
    zs_ref[0:lead, :] = tail
    cs_ref[0] = tail

    v = bv_ref[...].astype(F32)
    xc = v - jnp.mean(v, axis=-1, keepdims=True)
    var = jnp.mean(xc * xc, axis=-1, keepdims=True)
    vn = (xc * lax.rsqrt(var + EPS) * lng_ref[...] + lnb_ref[...]).astype(BF16)
    row = lax.broadcasted_iota(jnp.int32, (CHUNK, CHUNK), 0)
    col = lax.broadcasted_iota(jnp.int32, (CHUNK, CHUNK), 1)
    for g in range(N_B_GROUPS):
        wg = jnp.where(row >= col, ws_ref[g], 0.0).astype(BF16)
        bcol = bs_ref[:, g:g + 1]
        gs = slice(g * B_GROUP_W, (g + 1) * B_GROUP_W)
        for c in range(ts // CHUNK):
            rs = slice(c * CHUNK, (c + 1) * CHUNK)
            sg = jnp.dot(wg, vn[rs, gs], preferred_element_type=F32) + bcol
            yb_ref[rs, gs] = (bu_ref[rs, gs].astype(F32) * sg).astype(BF16)


def _attn_body(q0, q1, q2, k0, k1, k2, v0, v1, v2, y_ref,
               qc, kc, vc, s_scr, p_scr, inv_scr, o0, o1, o2, l0, l1, l2):
    row = lax.broadcasted_iota(jnp.int32, (QB, 2 * QB), 0)
    col = lax.broadcasted_iota(jnp.int32, (QB, 2 * QB), 1)
    band = jnp.logical_and(col >= row, col <= row + QB)
    band_first = jnp.logical_and(band, col >= QB)
    scale = HEAD_DIM ** -0.5
    nt = (((1,), (1,)), ((), ()))
    zero_blk = jnp.zeros((QB, HEAD_DIM), BF16)
    for (_, dil), q_ref, k_ref, v_ref, o_ref, l_ref in zip(
            DIL_GROUPS, (q0, q1, q2), (k0, k1, k2), (v0, v1, v2), (o0, o1, o2), (l0, l1, l2)):
        L = SEQ // dil
        nblk = L // QB

        def tok_rows(r, c):
            if dil == 1:
                return pl.ds(c * QB, QB)
            return pl.ds(r + c * QB * dil, QB, stride=dil)

        for r in range(dil):
            src = pl.ds(0, L) if dil == 1 else pl.ds(r, L, stride=dil)
            base = r * (L + QB)
            qc[r * L:(r + 1) * L, :] = q_ref[src, :].astype(BF16)
            kc[base:base + QB, :] = zero_blk
            vc[base:base + QB, :] = zero_blk
            kc[base + QB:base + QB + L, :] = k_ref[src, :].astype(BF16)
            vc[base + QB:base + QB + L, :] = v_ref[src, :].astype(BF16)

        for r in range(dil):
            for c in range(nblk):
                t = r * nblk + c
                kk = kc[r * (L + QB) + c * QB:r * (L + QB) + (c + 2) * QB, :]
                s = lax.dot_general(qc[t * QB:(t + 1) * QB, :], kk, nt, preferred_element_type=F32) * scale
                s_scr[t * QB:(t + 1) * QB, :] = jnp.where(band_first if c == 0 else band, s, NEG_BIG)

        for r in range(dil):
            for u in range(L // SOFTMAX_ROWS):
                rows = slice(r * L + u * SOFTMAX_ROWS, r * L + (u + 1) * SOFTMAX_ROWS)
                s = s_scr[rows, :]
                m = jnp.max(s, axis=-1, keepdims=True)
                p = jnp.exp(s - m)
                den = jnp.sum(p, axis=-1, keepdims=True)
                p_scr[rows, :] = p.astype(BF16)
                inv_scr[rows, :] = jnp.broadcast_to(1.0 / den, (SOFTMAX_ROWS, HEAD_DIM))
                tok = (pl.ds(u * SOFTMAX_ROWS, SOFTMAX_ROWS) if dil == 1
                       else pl.ds(r + u * SOFTMAX_ROWS * dil, SOFTMAX_ROWS, stride=dil))
                l_ref[tok, :] = jnp.broadcast_to(m + jnp.log(den), (SOFTMAX_ROWS, HEAD_DIM))

        for r in range(dil):
            for c in range(nblk):
                t = r * nblk + c
                vv = vc[r * (L + QB) + c * QB:r * (L + QB) + (c + 2) * QB, :]
                o = jnp.dot(p_scr[t * QB:(t + 1) * QB, :], vv, preferred_element_type=F32)
                o_ref[tok_rows(r, c), :] = o * inv_scr[t * QB:(t + 1) * QB, :]
    for u in range(SEQ // MERGE_ROWS):
        rows = slice(u * MERGE_ROWS, (u + 1) * MERGE_ROWS)
        y_ref[rows, :] = _combine_groups(o0[rows, :], o1[rows, :], o2[rows, :],
                                         l0[rows, :], l1[rows, :], l2[rows, :]).astype(BF16)


def _attn(proj):
    def seg(off, g):
        base = (off + g * C_WIDTH) // HEAD_DIM
        return pl.BlockSpec((None, SEQ, HEAD_DIM), lambda b, h: (base + h, b, 0))

    specs = [seg(off, g) for off in (OFF_Q, OFF_K, OFF_V) for g in range(N_DIL)]
    return pl.pallas_call(
        _attn_body,
        grid=(BATCH, N_C_HEADS),
        in_specs=specs,
        out_specs=pl.BlockSpec((SEQ, HEAD_DIM), lambda b, h: (b, h)),
        out_shape=jax.ShapeDtypeStruct((BATCH * SEQ, C_WIDTH), BF16),
        scratch_shapes=[pltpu.VMEM((SEQ, HEAD_DIM), BF16),
                        pltpu.VMEM((2 * SEQ, HEAD_DIM), BF16),
                        pltpu.VMEM((2 * SEQ, HEAD_DIM), BF16),
                        pltpu.VMEM((SEQ, 2 * QB), F32),
                        pltpu.VMEM((SEQ, 2 * QB), BF16),
                        pltpu.VMEM((SEQ, HEAD_DIM), F32)]
                       + [pltpu.VMEM((SEQ, HEAD_DIM), F32)] * (2 * N_DIL),
        compiler_params=_params("arbitrary", "arbitrary"),
        name="attn",
    )(*([proj] * (3 * N_DIL)))


def _kv_pack_body(*refs, fixed):
    k_refs, v_refs, out_refs = refs[0:N_DIL], refs[N_DIL:2 * N_DIL], refs[-N_DIL:]
    for k_ref, v_ref, o_ref, is_fixed in zip(k_refs, v_refs, out_refs, fixed):
        def pack(k_ref=k_ref, v_ref=v_ref, o_ref=o_ref):
            for h in range(N_C_HEADS):
                o_ref[:, 0, h, :] = k_ref[h]
                o_ref[:, 1, h, :] = v_ref[h]

        if is_fixed:
            pl.when(pl.program_id(1) == 0)(pack)
        else:
            pack()


N_MIX_IN, N_MIX_OUT = 10, 3


def _mix_ab_kv_body(*refs, fixed):
    n_kv_in = len(refs) - N_MIX_IN - N_MIX_OUT - N_DIL - 1
    mix_in, kv_in = refs[:N_MIX_IN], refs[N_MIX_IN:N_MIX_IN + 2 * N_DIL]
    outs = refs[N_MIX_IN + n_kv_in:-1]
    _mix_ab_body(*mix_in, *outs[:N_MIX_OUT], refs[-1])
    _kv_pack_body(*kv_in, *outs[N_MIX_OUT:], fixed=fixed)


def _mix_ab_kv(main, qkv, cw, lng, lnb, ws, bs_t, layer, prev):
    ts = TS_MIX
    nb = SEQ // ts
    blk = OFF_AB // W_A
    keeps = [min(win, SEQ) for win, _ in DIL_GROUPS]

    def seg(k):
        return pl.BlockSpec((ts, W_A), lambda b, s: (b * nb + s, blk + k))

    def full(shape):
        return pl.BlockSpec(shape, lambda b, s: (0,) * len(shape))

    rows = pl.BlockSpec((ts, W_A), lambda b, s: (b * nb + s, 0))
    mix_in = [seg(0), seg(1), seg(2), seg(3), seg(4),
              full((CONV_W, W_A)), full((1, W_B)), full((1, W_B)),
              full((N_B_GROUPS, CHUNK, CHUNK)), full((CHUNK, N_B_GROUPS))]
    mix_out = [rows, rows, pl.BlockSpec((1, F32_SUBLANES, W_A), lambda b, s: (b, 0, 0))]
    mix_shapes = [jax.ShapeDtypeStruct((BATCH * SEQ, W_A), BF16),
                  jax.ShapeDtypeStruct((BATCH * SEQ, W_B), BF16),
                  jax.ShapeDtypeStruct((BATCH, F32_SUBLANES, W_A), F32)]
    assert len(mix_in) == N_MIX_IN and len(mix_out) == N_MIX_OUT

    def src(off, g):
        keep, col = keeps[g], (off + g * C_WIDTH) // C_WIDTH
        if keep <= ts:
            nb = SEQ // keep
            return pl.BlockSpec((N_C_HEADS, keep, HEAD_DIM), lambda b, s: (col, b * nb + nb - 1, 0))
        nb, first = SEQ // ts, (SEQ - keep) // ts
        return pl.BlockSpec((N_C_HEADS, ts, HEAD_DIM), lambda b, s: (col, b * nb + jnp.maximum(s, first), 0))

    def dst(keep):
        if keep <= ts:
            return pl.BlockSpec((None, None, keep, 2, N_C_HEADS, HEAD_DIM), lambda b, s: (layer, b, 0, 0, 0, 0))
        first = (SEQ - keep) // ts
        return pl.BlockSpec((None, None, ts, 2, N_C_HEADS, HEAD_DIM),
                            lambda b, s: (layer, b, jnp.maximum(s - first, 0), 0, 0, 0))

    in_specs = mix_in + [src(OFF_K, g) for g in range(N_DIL)] + [src(OFF_V, g) for g in range(N_DIL)]
    args = [main] * 5 + [cw, lng, lnb, ws, bs_t] + [qkv] * (2 * N_DIL)
    aliases = {}
    if prev is not None:
        aliases = {len(in_specs) + g: N_MIX_OUT + g for g in range(N_DIL)}
        in_specs += [pl.BlockSpec(memory_space=pl.ANY)] * N_DIL
        args += list(prev)
    res = pl.pallas_call(
        functools.partial(_mix_ab_kv_body, fixed=tuple(keep <= ts for keep in keeps)),
        grid=(BATCH, nb),
        in_specs=in_specs,
        out_specs=mix_out + [dst(keep) for keep in keeps],
        out_shape=mix_shapes + [jax.ShapeDtypeStruct((DEPTH, BATCH, keep, 2, N_C_HEADS, HEAD_DIM), F32)
                                for keep in keeps],
        input_output_aliases=aliases,
        scratch_shapes=[pltpu.VMEM((ts + F32_SUBLANES, W_A), F32)],
        compiler_params=_params("arbitrary", "arbitrary"),
        name="mix_ab_kv",
    )(*args)
    return res[0], res[1], res[2], res[N_MIX_OUT:]


def _mix_sample_body(p_ref, qkv_ref, st_ref, cw_ref, lng_ref, lnb_ref, wsc_ref, bsc_ref, c0_ref, c1_ref, c2_ref,
                     ya_ref, yb_ref, yc_ref, cs_ref, vn_ref):
    @pl.when(pl.program_id(0) == 0)
    def _():
        def seg(off, w):
            return p_ref[:, off:off + w]

        z = seg(OFF_AB + W_A, W_A) * seg(OFF_AB + 2 * W_A, W_A)
        st0 = st_ref[0, :, 0:W_A]
        st1 = st_ref[0, :, W_A:2 * W_A]
        conv = cw_ref[0:1, :] * st0 + cw_ref[1:2, :] * st1 + cw_ref[2:3, :] * z
        ya_ref[...] = (seg(OFF_AB, W_A) * conv).astype(BF16)
        cs_ref[:, 0:W_A] = st1
        cs_ref[:, W_A:2 * W_A] = z

        v = seg(OFF_AB + 3 * W_A + W_B, W_B)
        xc = v - jnp.mean(v, axis=-1, keepdims=True)
        var = jnp.mean(xc * xc, axis=-1, keepdims=True)
        vn = xc * lax.rsqrt(var + EPS) * lng_ref[...] + lnb_ref[...]
        vn_ref[...] = vn
        yb_ref[...] = (seg(OFF_AB + 3 * W_A, W_B) * (wsc_ref[...] * vn + bsc_ref[...])).astype(BF16)

    scale = HEAD_DIM ** -0.5
    outs, lses = [], []
    for g, c_ref in enumerate((c0_ref, c1_ref, c2_ref)):
        q = qkv_ref[g]
        kn = qkv_ref[N_DIL + g]
        vnew = qkv_ref[2 * N_DIL + g]
        kc = c_ref[:, 0]
        vc = c_ref[:, 1]
        s_c = jnp.sum(q[None] * kc, axis=-1, keepdims=True) * scale
        s_n = jnp.sum(q * kn, axis=-1, keepdims=True) * scale
        m = jnp.maximum(jnp.max(s_c, axis=0), s_n)
        p_c = jnp.exp(s_c - m[None])
        p_n = jnp.exp(s_n - m)
        den = jnp.sum(p_c, axis=0) + p_n
        outs.append((jnp.sum(p_c * vc, axis=0) + p_n * vnew) / den)
        lses.append(m + jnp.log(den))
    yc_ref[...] = _combine_groups(*outs, *lses)


def _mix_sample(proj_s, qkv_s, st, cw, lng, lnb, wsc, bsc, caches, layer):
    def full(shape):
        return pl.BlockSpec(shape, lambda b: (0,) * len(shape))

    n_keys = DIL_GROUPS[0][0] // DIL_GROUPS[0][1]
    cache_specs = [pl.BlockSpec((None, None, n_keys, None, 2, N_C_HEADS, HEAD_DIM),
                                lambda b: (layer, b, 0, 0, 0, 0, 0)) for _ in caches]
    qkv = qkv_s.reshape(DEC_BATCH, 3 * N_DIL, N_C_HEADS, HEAD_DIM)
    return pl.pallas_call(
        _mix_sample_body,
        grid=(DEC_BATCH,),
        in_specs=[full((DEC_BATCH, MAIN_W)),
                  pl.BlockSpec((None, 3 * N_DIL, N_C_HEADS, HEAD_DIM), lambda b: (b, 0, 0, 0)),
                  pl.BlockSpec((1, DEC_BATCH, 2 * W_A), lambda b: (layer, 0, 0)),
                  full((CONV_W, W_A)), full((1, W_B)), full((1, W_B)), full((1, W_B)), full((1, W_B))]
                 + cache_specs,
        out_specs=[full((DEC_BATCH, W_A)), full((DEC_BATCH, W_B)),
                   pl.BlockSpec((None, N_C_HEADS, HEAD_DIM), lambda b: (b, 0, 0)),
                   full((DEC_BATCH, 2 * W_A)), full((DEC_BATCH, W_B))],
        out_shape=[jax.ShapeDtypeStruct((DEC_BATCH, W_A), BF16),
                   jax.ShapeDtypeStruct((DEC_BATCH, W_B), BF16),
                   jax.ShapeDtypeStruct((DEC_BATCH, N_C_HEADS, HEAD_DIM), F32),
                   jax.ShapeDtypeStruct((DEC_BATCH, 2 * W_A), F32),
                   jax.ShapeDtypeStruct((DEC_BATCH, W_B), F32)],
        compiler_params=_params("arbitrary"),
        name="mix_sample",
    )(proj_s, qkv, st, cw, lng, lnb, wsc, bsc, *caches)


def _combine_groups(o0, o1, o2, l0, l1, l2):
    m = jnp.maximum(jnp.maximum(l0, l1), l2)
    e0, e1, e2 = jnp.exp(l0 - m), jnp.exp(l1 - m), jnp.exp(l2 - m)
    return (e0 * o0 + e1 * o1 + e2 * o2) / (e0 + e1 + e2)


def _merge_tail(x, ga, gb, gc, ya, yb, yc, wa_ref, wb_ref, wc_ref, wo_ref, gp_ref):
    mm = (jax.nn.sigmoid(ga) * jnp.dot(ya, wa_ref[...], preferred_element_type=F32)
          + jax.nn.sigmoid(gb) * jnp.dot(yb, wb_ref[...], preferred_element_type=F32)
          + jax.nn.sigmoid(gc) * jnp.dot(yc, wc_ref[...], preferred_element_type=F32))
    r = jnp.dot(mm.astype(BF16), wo_ref[...], preferred_element_type=F32)
    return x + _rms(r, gp_ref[...])


def _merge_body(ga_ref, gb_ref, gc_ref, ya_ref, yb_ref, yc_ref, x_ref,
                ms_ref, yas_ref, ybs_ref, ycs_ref, xs_ref,
                wa_ref, wb_ref, wc_ref, wo_ref, gp_ref, out_ref, outs_ref):
    weights = (wa_ref, wb_ref, wc_ref, wo_ref, gp_ref)
    out_ref[...] = _merge_tail(x_ref[...], ga_ref[...].astype(F32), gb_ref[...].astype(F32),
                               gc_ref[...].astype(F32), ya_ref[...], yb_ref[...], yc_ref[...], *weights)

    @pl.when(pl.program_id(0) == 0)
    def _():
        gates = [ms_ref[:, k * D_MODEL:(k + 1) * D_MODEL] for k in range(3)]
        outs_ref[...] = _merge_tail(xs_ref[...], *gates, yas_ref[...], ybs_ref[...],
                                    ycs_ref[...].astype(BF16), *weights)


def _merge(proj, ya, yb, yc, x, main_s, ya_s, yb_s, yc_s, xs, wa, wb, wc, wo, gp, layer, tm):
    m = x.shape[0]

    def rows(w, k=0):
        return pl.BlockSpec((tm, w), lambda i: (i, k))

    def whole(a):
        return pl.BlockSpec(a.shape, lambda i: (0, 0))

    def resident(k):
        return pl.BlockSpec((k, D_MODEL), lambda i: (0, 0), pipeline_mode=pl.Buffered(1))

    return pl.pallas_call(
        _merge_body,
        grid=(m // tm,),
        in_specs=[rows(D_MODEL, 0), rows(D_MODEL, 1), rows(D_MODEL, 2), rows(W_A), rows(W_B), rows(C_WIDTH),
                  rows(D_MODEL),
                  whole(main_s), whole(ya_s), whole(yb_s), whole(yc_s), whole(xs),
                  resident(W_A), resident(W_B), resident(C_WIDTH), resident(D_MODEL),
                  pl.BlockSpec((None, 1, D_MODEL), lambda i: (layer, 0, 0), pipeline_mode=pl.Buffered(1))],
        out_specs=[rows(D_MODEL), whole(xs)],
        out_shape=[jax.ShapeDtypeStruct((m, D_MODEL), F32), jax.ShapeDtypeStruct(xs.shape, F32)],
        compiler_params=_params("arbitrary"),
        name="merge",
    )(proj, proj, proj, ya, yb, yc, x, main_s, ya_s, yb_s, yc_s, xs, wa, wb, wc, wo, gp)


def _swiglu_part(h, wg_ref, wu_ref, wo_ref):
    gate = jnp.dot(h, wg_ref[...], preferred_element_type=F32)
    up = jnp.dot(h, wu_ref[...], preferred_element_type=F32)
    act = (gate * jax.nn.sigmoid(gate) * up).astype(BF16)
    return jnp.dot(act, wo_ref[...], preferred_element_type=F32)


def _ffn_body(x_ref, g1_ref, wg_ref, wu_ref, wo_ref, g2_ref, xs_ref, *rest, n_cast):
    cast_src = rest[:n_cast]
    out_ref, outs_ref = rest[n_cast:n_cast + 2]
    cast_dst = rest[n_cast + 2:2 * n_cast + 2]
    h_ref, hs_ref = rest[-2:]
    i = pl.program_id(0)
    j = pl.program_id(1)
    last = pl.num_programs(1) - 1

    @pl.when(j == 0)
    def _():
        h_ref[...] = _rms(x_ref[...], g1_ref[...]).astype(BF16)
        out_ref[...] = jnp.zeros_like(out_ref)

    _cast_blocks(cast_src, cast_dst)
    out_ref[...] += _swiglu_part(h_ref[...], wg_ref, wu_ref, wo_ref)

    @pl.when(j == last)
    def _():
        out_ref[...] = x_ref[...] + _rms(out_ref[...], g2_ref[...])

    @pl.when(i == 0)
    def _():
        @pl.when(j == 0)
        def _():
            hs_ref[...] = _rms(xs_ref[...], g1_ref[...]).astype(BF16)
            outs_ref[...] = jnp.zeros_like(outs_ref)

        outs_ref[...] += _swiglu_part(hs_ref[...], wg_ref, wu_ref, wo_ref)

        @pl.when(j == last)
        def _():
            outs_ref[...] = xs_ref[...] + _rms(outs_ref[...], g2_ref[...])


def _ffn(x, xs, g1, w_in, w_out, g2, layer, tm, tf, casts=()):
    m = x.shape[0]
    ns = xs.shape[0]
    nj = D_FF // tf
    cast_specs = [_cast_specs(src, layer + 1, rows, nj, (m // tm) * nj) for src, rows in casts]
    res = pl.pallas_call(
        functools.partial(_ffn_body, n_cast=len(casts)),
        grid=(m // tm, nj),
        in_specs=[
            pl.BlockSpec((tm, D_MODEL), lambda i, j: (i, 0)),
            pl.BlockSpec((None, 1, D_MODEL), lambda i, j: (layer, 0, 0)),
            pl.BlockSpec((D_MODEL, tf), lambda i, j: (0, j)),
            pl.BlockSpec((D_MODEL, tf), lambda i, j: (0, nj + j)),
            pl.BlockSpec((tf, D_MODEL), lambda i, j: (j, 0)),
            pl.BlockSpec((None, 1, D_MODEL), lambda i, j: (layer, 0, 0)),
            pl.BlockSpec((ns, D_MODEL), lambda i, j: (0, 0)),
        ] + [c[0] for c in cast_specs],
        out_specs=[pl.BlockSpec((tm, D_MODEL), lambda i, j: (i, 0)),
                   pl.BlockSpec((ns, D_MODEL), lambda i, j: (0, 0))] + [c[1] for c in cast_specs],
        out_shape=[jax.ShapeDtypeStruct((m, D_MODEL), F32),
                   jax.ShapeDtypeStruct((ns, D_MODEL), F32)] + [c[2] for c in cast_specs],
        scratch_shapes=[pltpu.VMEM((tm, D_MODEL), BF16), pltpu.VMEM((ns, D_MODEL), BF16)],
        compiler_params=_params("arbitrary", "arbitrary"),
        name="ffn",
    )(x, g1, w_in, w_in, w_out, g2, xs, *[src for src, _ in casts])
    return res[0], res[1], res[2:]


TM_INPROJ = 1024
TS_MIX = 512
TM_MERGE = 256
TM_FFN = 512
TF_FFN = 512
CAST_ROWS_FFN_IN = 16
CAST_ROWS_FFN_OUT = 64
CAST_ROWS_W_IN = 16
CAST_ROWS_MERGE = 16


def kernel(x_prompt, x_sample, state_conv, cache_kv_w128, cache_kv_w512, cache_kv_w2048, g_pre_mix, w_in, conv_w, ln_g, ln_b, w_s, b_s, w_a_out, w_b_out, w_c_out, w_o, g_post_mix, g_pre_ffn, w_ffn_in, w_ffn_out, g_post_ffn):
    w_in_l = w_in[0].astype(BF16)
    inproj_casts = ((w_ffn_in, CAST_ROWS_FFN_IN), (w_ffn_out, CAST_ROWS_FFN_OUT),
                    (w_a_out, CAST_ROWS_MERGE), (w_b_out, CAST_ROWS_MERGE), (w_c_out, CAST_ROWS_MERGE),
                    (w_o, CAST_ROWS_MERGE))
    g_pre_mix, g_post_mix, g_pre_ffn, g_post_ffn = (
        g.reshape(DEPTH, 1, D_MODEL) for g in (g_pre_mix, g_post_mix, g_pre_ffn, g_post_ffn))

    tabs_p = _rope_tables(jnp.arange(SEQ, dtype=jnp.int32))
    tabs_s = _rope_tables(jnp.full((DEC_BATCH,), PAST_LEN, dtype=jnp.int32))

    n_keys = DIL_GROUPS[0][0] // DIL_GROUPS[0][1]
    caches = tuple(c.reshape(DEPTH, DEC_BATCH, n_keys, dil, 2, N_C_HEADS, HEAD_DIM)
                   for c, (_, dil) in zip((cache_kv_w128, cache_kv_w512, cache_kv_w2048), DIL_GROUPS))
    st_all = state_conv.reshape(DEPTH, DEC_BATCH, (CONV_W - 1) * W_A)
    bs_t = jnp.swapaxes(b_s, 1, 2)
    wsc = jnp.repeat(w_s[:, :, 0, 0], B_GROUP_W, axis=1)
    bsc = jnp.repeat(b_s[:, :, 0], B_GROUP_W, axis=1)

    xp = x_prompt.reshape(BATCH * SEQ, D_MODEL)
    xs = x_sample.reshape(DEC_BATCH, D_MODEL)
    conv_p, conv_s, vchunk_s = [], [], []
    kv_p = None
    kv_s = [[] for _ in DIL_GROUPS]

    def row(a, l):
        return a[l][None, :]

    for l in range(DEPTH):
        proj, qkv, proj_s, qkv_s, wfi_l, wfo_l, wa_l, wb_l, wc_l, wo_l = _inproj(
            xp, xs, g_pre_mix, w_in_l, l, tabs_p, tabs_s, TM_INPROJ, casts=inproj_casts)
        ya, yb, ctail, kv_p = _mix_ab_kv(proj, qkv, conv_w[l], row(ln_g, l), row(ln_b, l), w_s[l], bs_t[l], l, kv_p)
        yc = _attn(qkv)
        conv_p.append(ctail[:, F32_SUBLANES - (CONV_W - 1):])
        ya_s, yb_s, yc_s, cst, vn = _mix_sample(proj_s, qkv_s, st_all, conv_w[l], row(ln_g, l), row(ln_b, l),
                                                row(wsc, l), row(bsc, l), caches, l)
        for g in range(N_DIL):
            k = qkv_s[:, OFF_K + g * C_WIDTH:OFF_K + (g + 1) * C_WIDTH]
            v = qkv_s[:, OFF_V + g * C_WIDTH:OFF_V + (g + 1) * C_WIDTH]
            kv_s[g].append(jnp.stack([k.reshape(DEC_BATCH, 1, N_C_HEADS, HEAD_DIM),
                                      v.reshape(DEC_BATCH, 1, N_C_HEADS, HEAD_DIM)], axis=2))
        conv_s.append(cst.reshape(DEC_BATCH, CONV_W - 1, W_A))
        vchunk_s.append(vn.reshape(DEC_BATCH, 1, W_B))
        xp, xs = _merge(proj, ya, yb, yc, xp, proj_s, ya_s, yb_s, yc_s.reshape(DEC_BATCH, C_WIDTH), xs,
                        wa_l, wb_l, wc_l, wo_l, g_post_mix, l, TM_MERGE)
        next_casts = ((w_in, CAST_ROWS_W_IN),) if l + 1 < DEPTH else ()
        xp, xs, w_in_next = _ffn(xp, xs, g_pre_ffn, wfi_l, wfo_l, g_post_ffn, l, TM_FFN, TF_FFN, casts=next_casts)
        if w_in_next:
            w_in_l = w_in_next[0]

    return (xp.reshape(BATCH, SEQ, D_MODEL), xs.reshape(DEC_BATCH, 1, D_MODEL),
            jnp.stack(conv_p, axis=0),
            kv_p[0], kv_p[1], kv_p[2],
            jnp.stack(conv_s, axis=0),
            jnp.stack(kv_s[0], axis=0), jnp.stack(kv_s[1], axis=0), jnp.stack(kv_s[2], axis=0),
            jnp.stack(vchunk_s, axis=0))
```

```python
import functools

import jax
import jax.numpy as jnp
from jax import lax
from jax.experimental import pallas as pl
from jax.experimental.pallas import tpu as pltpu

D_MODEL = 2048
BATCH = 4
SEQ = 2048
DEPTH = 4
DEC_BATCH = 8
PAST_LEN = 16384
W_A = 1024
CONV_W = 3
W_B = 1024
CHUNK = 128
N_B_GROUPS = 4
B_GROUP_W = W_B // N_B_GROUPS
N_C_HEADS = 4
HEAD_DIM = 128
ROT_DIM = HEAD_DIM // 4
ROPE_THETA = 500000.0
DIL_GROUPS = ((128, 1), (512, 4), (2048, 16))
N_DIL = len(DIL_GROUPS)
QB = 128
C_WIDTH = N_C_HEADS * HEAD_DIM
QKV_W = N_DIL * C_WIDTH
D_FF = ((-(-8 * D_MODEL // 3) + 255) // 256) * 256
IN_WIDTH = 3 * W_A + 2 * W_B + 3 * QKV_W + 3 * D_MODEL
EPS = 1e-6

GATE_W = 3 * D_MODEL
OFF_AB = GATE_W
MAIN_W = GATE_W + 3 * W_A + 2 * W_B
OFF_Q, OFF_K, OFF_V = 0, QKV_W, 2 * QKV_W
ORIG_GATE_OFF = IN_WIDTH - GATE_W

ROPE_ROWS = 256
SOFTMAX_ROWS = 32
MERGE_ROWS = 64
F32_SUBLANES = 8
V7X_VMEM_LIMIT = 60 * 1024 * 1024
NEG_BIG = -1e30

F32 = jnp.float32
BF16 = jnp.bfloat16


def _params(*sem):
    return pltpu.CompilerParams(dimension_semantics=sem, vmem_limit_bytes=V7X_VMEM_LIMIT)


def _rms(x, g):
    return x * lax.rsqrt(jnp.mean(x * x, axis=-1, keepdims=True) + EPS) * g


def _cast_specs(src, layer, rows, steps_per_row_tile, n_steps):
    _, r, c = src.shape
    nblk = r // rows
    assert r % rows == 0 and nblk <= n_steps

    def blk(i, j):
        return jnp.minimum(i * steps_per_row_tile + j, nblk - 1)

    return (pl.BlockSpec((None, rows, c), lambda i, j: (layer, blk(i, j), 0)),
            pl.BlockSpec((rows, c), lambda i, j: (blk(i, j), 0)),
            jax.ShapeDtypeStruct((r, c), BF16))


def _cast_blocks(src_refs, dst_refs):
    for s_ref, d_ref in zip(src_refs, dst_refs):
        d_ref[...] = s_ref[...].astype(BF16)


def _rope(a, c, s1, s2):
    return a * c + pltpu.roll(a, HEAD_DIM - ROT_DIM // 2, 1) * s1 + pltpu.roll(a, ROT_DIM // 2, 1) * s2


def _inproj_body(x_ref, g_ref, w0_ref, w1_ref, c_ref, s1_ref, s2_ref, xs_ref, cs_ref, s1s_ref, s2s_ref, *rest,
                 main_steps, rope_steps, n_cast):
    cast_src = rest[:n_cast]
    main_ref, qkv_ref, mains_ref, qkvs_ref = rest[n_cast:n_cast + 4]
    cast_dst = rest[n_cast + 4:2 * n_cast + 4]
    h_ref, hs_ref = rest[-2:]
    i = pl.program_id(0)
    j = pl.program_id(1)
    tn = w0_ref.shape[1]
    heads = tn // HEAD_DIM
    halves = ((w0_ref, slice(0, tn)), (w1_ref, slice(tn, 2 * tn)))
    with_samples = i == 0

    @pl.when(j == 0)
    def _():
        h_ref[...] = _rms(x_ref[...], g_ref[...]).astype(BF16)

    @pl.when(jnp.logical_and(with_samples, j == 0))
    def _():
        hs_ref[...] = _rms(xs_ref[...], g_ref[...]).astype(BF16)

    @pl.when(j < main_steps)
    def _():
        _cast_blocks(cast_src, cast_dst)
        for w_ref, cols in halves:
            main_ref[:, cols] = jnp.dot(h_ref[...], w_ref[...], preferred_element_type=F32).astype(main_ref.dtype)

        @pl.when(with_samples)
        def _():
            for w_ref, cols in halves:
                mains_ref[j, :, cols] = jnp.dot(hs_ref[...], w_ref[...], preferred_element_type=F32)

    @pl.when(jnp.logical_and(j >= main_steps, j < main_steps + rope_steps))
    def _():
        _cast_blocks(cast_src, cast_dst)
        tm = x_ref.shape[0]
        rc = min(tm, ROPE_ROWS)
        for w_ref, cols in halves:
            for r in range(tm // rc):
                rs = slice(r * rc, (r + 1) * rc)
                acc = jnp.dot(h_ref[rs, :], w_ref[...], preferred_element_type=F32)
                for h in range(heads):
                    qkv_ref[cols.start // HEAD_DIM + h, rs, :] = _rope(
                        acc[:, h * HEAD_DIM:(h + 1) * HEAD_DIM], c_ref[rs, :], s1_ref[rs, :], s2_ref[rs, :])

        @pl.when(with_samples)
        def _():
            for w_ref, cols in halves:
                acc = jnp.dot(hs_ref[...], w_ref[...], preferred_element_type=F32)
                for h in range(heads):
                    sl = slice(cols.start + h * HEAD_DIM, cols.start + (h + 1) * HEAD_DIM)
                    qkvs_ref[j - main_steps, :, sl] = _rope(acc[:, h * HEAD_DIM:(h + 1) * HEAD_DIM],
                                                            cs_ref[...], s1s_ref[...], s2s_ref[...])

    @pl.when(j >= main_steps + rope_steps)
    def _():
        _cast_blocks(cast_src, cast_dst)
        for w_ref, cols in halves:
            acc = jnp.dot(h_ref[...], w_ref[...], preferred_element_type=F32)
            for h in range(heads):
                qkv_ref[cols.start // HEAD_DIM + h] = acc[:, h * HEAD_DIM:(h + 1) * HEAD_DIM]

        @pl.when(with_samples)
        def _():
            for w_ref, cols in halves:
                qkvs_ref[j - main_steps, :, cols] = jnp.dot(hs_ref[...], w_ref[...], preferred_element_type=F32)


def _inproj(x, xs, g, w, layer, tabs, tabs_s, tm, casts=()):
    m = x.shape[0]
    ns = xs.shape[0]
    tn = C_WIDTH
    n_tiles = IN_WIDTH // tn
    rot = ORIG_GATE_OFF // tn
    main_steps = MAIN_W // (2 * tn)
    rope_steps = 2 * QKV_W // (2 * tn)
    qkv_steps = pl.cdiv(3 * QKV_W, 2 * tn)
    steps = main_steps + qkv_steps
    tab_spec = pl.BlockSpec((tm, HEAD_DIM), lambda i, j: (i % (SEQ // tm), 0))
    tab_s_spec = pl.BlockSpec((ns, HEAD_DIM), lambda i, j: (0, 0))

    def w_spec(half):
        return pl.BlockSpec((D_MODEL, tn), lambda i, j: (0, (2 * j + half + rot) % n_tiles))

    def x_tile(i, j):
        return jnp.minimum(i + jnp.where(j >= steps // 2, 1, 0), m // tm - 1)

    cast_specs = [_cast_specs(src, layer, rows, steps, (m // tm) * steps) for src, rows in casts]
    res = pl.pallas_call(
        functools.partial(_inproj_body, main_steps=main_steps, rope_steps=rope_steps, n_cast=len(casts)),
        grid=(m // tm, steps),
        in_specs=[
            pl.BlockSpec((tm, D_MODEL), lambda i, j: (x_tile(i, j), 0)),
            pl.BlockSpec((None, 1, D_MODEL), lambda i, j: (layer, 0, 0)),
            w_spec(0), w_spec(1),
            tab_spec, tab_spec, tab_spec,
            pl.BlockSpec((ns, D_MODEL), lambda i, j: (0, 0)),
            tab_s_spec, tab_s_spec, tab_s_spec,
        ] + [c[0] for c in cast_specs],
        out_specs=[pl.BlockSpec((tm, 2 * tn), lambda i, j: (i, jnp.minimum(j, main_steps - 1))),
                   pl.BlockSpec((2 * tn // HEAD_DIM, tm, HEAD_DIM),
                                lambda i, j: (jnp.maximum(j - main_steps, 0), i, 0)),
                   pl.BlockSpec((main_steps, ns, 2 * tn), lambda i, j: (0, 0, 0)),
                   pl.BlockSpec((qkv_steps, ns, 2 * tn), lambda i, j: (0, 0, 0))]
                  + [c[1] for c in cast_specs],
        out_shape=[jax.ShapeDtypeStruct((m, MAIN_W), BF16),
                   jax.ShapeDtypeStruct((3 * QKV_W // HEAD_DIM, m, HEAD_DIM), F32),
                   jax.ShapeDtypeStruct((main_steps, ns, 2 * tn), F32),
                   jax.ShapeDtypeStruct((qkv_steps, ns, 2 * tn), F32)] + [c[2] for c in cast_specs],
        scratch_shapes=[pltpu.VMEM((tm, D_MODEL), BF16), pltpu.VMEM((ns, D_MODEL), BF16)],
        compiler_params=_params("arbitrary", "arbitrary"),
        name="inproj",
    )(x, g, w, w, *tabs, xs, *tabs_s, *[src for src, _ in casts])
    main_s = jnp.swapaxes(res[2], 0, 1).reshape(ns, MAIN_W)
    qkv_s = jnp.swapaxes(res[3], 0, 1).reshape(ns, qkv_steps * 2 * tn)[:, :3 * QKV_W]
    return (res[0], res[1], main_s, qkv_s, *res[4:])


def _rope_tables(pos):
    inv_freq = ROPE_THETA ** (-jnp.arange(0, ROT_DIM, 2, dtype=jnp.float32) / ROT_DIM)
    ang = pos.astype(jnp.float32)[:, None] * inv_freq[None, :]
    cos, sin = jnp.cos(ang), jnp.sin(ang)
    n = pos.shape[0]
    half = ROT_DIM // 2
    rest = HEAD_DIM - ROT_DIM
    c = jnp.concatenate([cos, cos, jnp.ones((n, rest), F32)], axis=1)
    s1 = jnp.concatenate([-sin, jnp.zeros((n, half + rest), F32)], axis=1)
    s2 = jnp.concatenate([jnp.zeros((n, half), F32), sin, jnp.zeros((n, rest), F32)], axis=1)
    return c, s1, s2


def _mix_ab_body(ab_ref, ac_ref, ax_ref, bu_ref, bv_ref, cw_ref, lng_ref, lnb_ref, ws_ref, bs_ref,
                 ya_ref, yb_ref, cs_ref, zs_ref):
    ts = ab_ref.shape[0]
    lead = F32_SUBLANES

    @pl.when(pl.program_id(1) == 0)
    def _():
        zs_ref[0:lead, :] = jnp.zeros((lead, W_A), F32)

    z = ac_ref[...].astype(F32) * ax_ref[...].astype(F32)
    zs_ref[lead:lead + ts, :] = z
    z1 = zs_ref[lead - 1:lead - 1 + ts, :]
    z2 = zs_ref[lead - 2:lead - 2 + ts, :]
    conv = cw_ref[0:1, :] * z2 + cw_ref[1:2, :] * z1 + cw_ref[2:3, :] * z
    ya_ref[...] = (ab_ref[...].astype(F32) * conv).astype(BF16)
    tail = zs_ref[ts:ts + lead, :]
    zs_ref[0:lead, :] = tail
    cs_ref[0] = tail

    v = bv_ref[...].astype(F32)
    xc = v - jnp.mean(v, axis=-1, keepdims=True)
    var = jnp.mean(xc * xc, axis=-1, keepdims=True)
    vn = (xc * lax.rsqrt(var + EPS) * lng_ref[...] + lnb_ref[...]).astype(BF16)
    row = lax.broadcasted_iota(jnp.int32, (CHUNK, CHUNK), 0)
    col = lax.broadcasted_iota(jnp.int32, (CHUNK, CHUNK), 1)
    for g in range(N_B_GROUPS):
        wg = jnp.where(row >= col, ws_ref[g], 0.0).astype(BF16)
        bcol = bs_ref[:, g:g + 1]
        gs = slice(g * B_GROUP_W, (g + 1) * B_GROUP_W)
        for c in range(ts // CHUNK):
            rs = slice(c * CHUNK, (c + 1) * CHUNK)
            sg = jnp.dot(wg, vn[rs, gs], preferred_element_type=F32) + bcol
            yb_ref[rs, gs] = (bu_ref[rs, gs].astype(F32) * sg).astype(BF16)


def _attn_body(q0, q1, q2, k0, k1, k2, v0, v1, v2, y_ref,
               qc, kc, vc, s_scr, p_scr, inv_scr, o0, o1, o2, l0, l1, l2):
    row = lax.broadcasted_iota(jnp.int32, (QB, 2 * QB), 0)
    col = lax.broadcasted_iota(jnp.int32, (QB, 2 * QB), 1)
    band = jnp.logical_and(col >= row, col <= row + QB)
    band_first = jnp.logical_and(band, col >= QB)
    scale = HEAD_DIM ** -0.5
    nt = (((1,), (1,)), ((), ()))
    zero_blk = jnp.zeros((QB, HEAD_DIM), BF16)
    for (_, dil), q_ref, k_ref, v_ref, o_ref, l_ref in zip(
            DIL_GROUPS, (q0, q1, q2), (k0, k1, k2), (v0, v1, v2), (o0, o1, o2), (l0, l1, l2)):
        L = SEQ // dil
        nblk = L // QB

        def tok_rows(r, c):
            if dil == 1:
                return pl.ds(c * QB, QB)
            return pl.ds(r + c * QB * dil, QB, stride=dil)

        for r in range(dil):
            src = pl.ds(0, L) if dil == 1 else pl.ds(r, L, stride=dil)
            base = r * (L + QB)
            qc[r * L:(r + 1) * L, :] = q_ref[src, :].astype(BF16)
            kc[base:base + QB, :] = zero_blk
            vc[base:base + QB, :] = zero_blk
            kc[base + QB:base + QB + L, :] = k_ref[src, :].astype(BF16)
            vc[base + QB:base + QB + L, :] = v_ref[src, :].astype(BF16)

        for r in range(dil):
            for c in range(nblk):
                t = r * nblk + c
                kk = kc[r * (L + QB) + c * QB:r * (L + QB) + (c + 2) * QB, :]
                s = lax.dot_general(qc[t * QB:(t + 1) * QB, :], kk, nt, preferred_element_type=F32) * scale
                s_scr[t * QB:(t + 1) * QB, :] = jnp.where(band_first if c == 0 else band, s, NEG_BIG)

        for r in range(dil):
            for u in range(L // SOFTMAX_ROWS):
                rows = slice(r * L + u * SOFTMAX_ROWS, r * L + (u + 1) * SOFTMAX_ROWS)
                s = s_scr[rows, :]
                m = jnp.max(s, axis=-1, keepdims=True)
                p = jnp.exp(s - m)
                den = jnp.sum(p, axis=-1, keepdims=True)
                p_scr[rows, :] = p.astype(BF16)
                inv_scr[rows, :] = jnp.broadcast_to(1.0 / den, (SOFTMAX_ROWS, HEAD_DIM))
                tok = (pl.ds(u * SOFTMAX_ROWS, SOFTMAX_ROWS) if dil == 1
                       else pl.ds(r + u * SOFTMAX_ROWS * dil, SOFTMAX_ROWS, stride=dil))
                l_ref[tok, :] = jnp.broadcast_to(m + jnp.log(den), (SOFTMAX_ROWS, HEAD_DIM))

        for r in range(dil):
            for c in range(nblk):
                t = r * nblk + c
                vv = vc[r * (L + QB) + c * QB:r * (L + QB) + (c + 2) * QB, :]
                o = jnp.dot(p_scr[t * QB:(t + 1) * QB, :], vv, preferred_element_type=F32)
                o_ref[tok_rows(r, c), :] = o * inv_scr[t * QB:(t + 1) * QB, :]
    for u in range(SEQ // MERGE_ROWS):
        rows = slice(u * MERGE_ROWS, (u + 1) * MERGE_ROWS)
        y_ref[rows, :] = _combine_groups(o0[rows, :], o1[rows, :], o2[rows, :],
                                         l0[rows, :], l1[rows, :], l2[rows, :]).astype(BF16)


def _attn(proj):
    def seg(off, g):
        base = (off + g * C_WIDTH) // HEAD_DIM
        return pl.BlockSpec((None, SEQ, HEAD_DIM), lambda b, h: (base + h, b, 0))

    specs = [seg(off, g) for off in (OFF_Q, OFF_K, OFF_V) for g in range(N_DIL)]
    return pl.pallas_call(
        _attn_body,
        grid=(BATCH, N_C_HEADS),
        in_specs=specs,
        out_specs=pl.BlockSpec((SEQ, HEAD_DIM), lambda b, h: (b, h)),
        out_shape=jax.ShapeDtypeStruct((BATCH * SEQ, C_WIDTH), BF16),
        scratch_shapes=[pltpu.VMEM((SEQ, HEAD_DIM), BF16),
                        pltpu.VMEM((2 * SEQ, HEAD_DIM), BF16),
                        pltpu.VMEM((2 * SEQ, HEAD_DIM), BF16),
                        pltpu.VMEM((SEQ, 2 * QB), F32),
                        pltpu.VMEM((SEQ, 2 * QB), BF16),
                        pltpu.VMEM((SEQ, HEAD_DIM), F32)]
                       + [pltpu.VMEM((SEQ, HEAD_DIM), F32)] * (2 * N_DIL),
        compiler_params=_params("arbitrary", "arbitrary"),
        name="attn",
    )(*([proj] * (3 * N_DIL)))


def _kv_pack_body(*refs, fixed):
    k_refs, v_refs, out_refs = refs[0:N_DIL], refs[N_DIL:2 * N_DIL], refs[-N_DIL:]
    for k_ref, v_ref, o_ref, is_fixed in zip(k_refs, v_refs, out_refs, fixed):
        def pack(k_ref=k_ref, v_ref=v_ref, o_ref=o_ref):
            for h in range(N_C_HEADS):
                o_ref[:, 0, h, :] = k_ref[h]
                o_ref[:, 1, h, :] = v_ref[h]

        if is_fixed:
            pl.when(pl.program_id(1) == 0)(pack)
        else:
            pack()


N_MIX_IN, N_MIX_OUT = 10, 3


def _mix_ab_kv_body(*refs, fixed):
    n_kv_in = len(refs) - N_MIX_IN - N_MIX_OUT - N_DIL - 1
    mix_in, kv_in = refs[:N_MIX_IN], refs[N_MIX_IN:N_MIX_IN + 2 * N_DIL]
    outs = refs[N_MIX_IN + n_kv_in:-1]
    _mix_ab_body(*mix_in, *outs[:N_MIX_OUT], refs[-1])
    _kv_pack_body(*kv_in, *outs[N_MIX_OUT:], fixed=fixed)


def _mix_ab_kv(main, qkv, cw, lng, lnb, ws, bs_t, layer, prev):
    ts = TS_MIX
    nb = SEQ // ts
    blk = OFF_AB // W_A
    keeps = [min(win, SEQ) for win, _ in DIL_GROUPS]

    def seg(k):
        return pl.BlockSpec((ts, W_A), lambda b, s: (b * nb + s, blk + k))

    def full(shape):
        return pl.BlockSpec(shape, lambda b, s: (0,) * len(shape))

    rows = pl.BlockSpec((ts, W_A), lambda b, s: (b * nb + s, 0))
    mix_in = [seg(0), seg(1), seg(2), seg(3), seg(4),
              full((CONV_W, W_A)), full((1, W_B)), full((1, W_B)),
              full((N_B_GROUPS, CHUNK, CHUNK)), full((CHUNK, N_B_GROUPS))]
    mix_out = [rows, rows, pl.BlockSpec((1, F32_SUBLANES, W_A), lambda b, s: (b, 0, 0))]
    mix_shapes = [jax.ShapeDtypeStruct((BATCH * SEQ, W_A), BF16),
                  jax.ShapeDtypeStruct((BATCH * SEQ, W_B), BF16),
                  jax.ShapeDtypeStruct((BATCH, F32_SUBLANES, W_A), F32)]
    assert len(mix_in) == N_MIX_IN and len(mix_out) == N_MIX_OUT

    def src(off, g):
        keep, col = keeps[g], (off + g * C_WIDTH) // C_WIDTH
        if keep <= ts:
            nb = SEQ // keep
            return pl.BlockSpec((N_C_HEADS, keep, HEAD_DIM), lambda b, s: (col, b * nb + nb - 1, 0))
        nb, first = SEQ // ts, (SEQ - keep) // ts
        return pl.BlockSpec((N_C_HEADS, ts, HEAD_DIM), lambda b, s: (col, b * nb + jnp.maximum(s, first), 0))

    def dst(keep):
        if keep <= ts:
            return pl.BlockSpec((None, None, keep, 2, N_C_HEADS, HEAD_DIM), lambda b, s: (layer, b, 0, 0, 0, 0))
        first = (SEQ - keep) // ts
        return pl.BlockSpec((None, None, ts, 2, N_C_HEADS, HEAD_DIM),
                            lambda b, s: (layer, b, jnp.maximum(s - first, 0), 0, 0, 0))

    in_specs = mix_in + [src(OFF_K, g) for g in range(N_DIL)] + [src(OFF_V, g) for g in range(N_DIL)]
    args = [main] * 5 + [cw, lng, lnb, ws, bs_t] + [qkv] * (2 * N_DIL)
    aliases = {len(in_specs) + g: N_MIX_OUT + g for g in range(N_DIL)}
    in_specs += [pl.BlockSpec(memory_space=pl.ANY)] * N_DIL
    args += list(prev)
    res = pl.pallas_call(
        functools.partial(_mix_ab_kv_body, fixed=tuple(keep <= ts for keep in keeps)),
        grid=(BATCH, nb),
        in_specs=in_specs,
        out_specs=mix_out + [dst(keep) for keep in keeps],
        out_shape=mix_shapes + [jax.ShapeDtypeStruct((DEPTH, BATCH, keep, 2, N_C_HEADS, HEAD_DIM), F32)
                                for keep in keeps],
        input_output_aliases=aliases,
        scratch_shapes=[pltpu.VMEM((ts + F32_SUBLANES, W_A), F32)],
        compiler_params=_params("arbitrary", "arbitrary"),
        name="mix_ab_kv",
    )(*args)
    return res[0], res[1], res[2], res[N_MIX_OUT:]


def _mix_sample_body(p_ref, qkv_ref, st_ref, cw_ref, lng_ref, lnb_ref, wsc_ref, bsc_ref, c0_ref, c1_ref, c2_ref,
                     ya_ref, yb_ref, yc_ref, cs_ref, vn_ref):
    @pl.when(pl.program_id(0) == 0)
    def _():
        def seg(off, w):
            return p_ref[:, off:off + w]

        z = seg(OFF_AB + W_A, W_A) * seg(OFF_AB + 2 * W_A, W_A)
        st0 = st_ref[0, :, 0:W_A]
        st1 = st_ref[0, :, W_A:2 * W_A]
        conv = cw_ref[0:1, :] * st0 + cw_ref[1:2, :] * st1 + cw_ref[2:3, :] * z
        ya_ref[...] = (seg(OFF_AB, W_A) * conv).astype(BF16)
        cs_ref[:, 0:W_A] = st1
        cs_ref[:, W_A:2 * W_A] = z

        v = seg(OFF_AB + 3 * W_A + W_B, W_B)
        xc = v - jnp.mean(v, axis=-1, keepdims=True)
        var = jnp.mean(xc * xc, axis=-1, keepdims=True)
        vn = xc * lax.rsqrt(var + EPS) * lng_ref[...] + lnb_ref[...]
        vn_ref[...] = vn
        yb_ref[...] = (seg(OFF_AB + 3 * W_A, W_B) * (wsc_ref[...] * vn + bsc_ref[...])).astype(BF16)

    scale = HEAD_DIM ** -0.5
    outs, lses = [], []
    for g, c_ref in enumerate((c0_ref, c1_ref, c2_ref)):
        q = qkv_ref[g]
        kn = qkv_ref[N_DIL + g]
        vnew = qkv_ref[2 * N_DIL + g]
        kc = c_ref[:, 0]
        vc = c_ref[:, 1]
        s_c = jnp.sum(q[None] * kc, axis=-1, keepdims=True) * scale
        s_n = jnp.sum(q * kn, axis=-1, keepdims=True) * scale
        m = jnp.maximum(jnp.max(s_c, axis=0), s_n)
        p_c = jnp.exp(s_c - m[None])
        p_n = jnp.exp(s_n - m)
        den = jnp.sum(p_c, axis=0) + p_n
        outs.append((jnp.sum(p_c * vc, axis=0) + p_n * vnew) / den)
        lses.append(m + jnp.log(den))
    yc_ref[...] = _combine_groups(*outs, *lses)


def _mix_sample(proj_s, qkv_s, st, cw, lng, lnb, wsc, bsc, caches, layer):
    def full(shape):
        return pl.BlockSpec(shape, lambda b: (0,) * len(shape))

    n_keys = DIL_GROUPS[0][0] // DIL_GROUPS[0][1]
    cache_specs = [pl.BlockSpec((None, None, n_keys, None, 2, N_C_HEADS, HEAD_DIM),
                                lambda b: (layer, b, 0, 0, 0, 0, 0)) for _ in caches]
    qkv = qkv_s.reshape(DEC_BATCH, 3 * N_DIL, N_C_HEADS, HEAD_DIM)
    return pl.pallas_call(
        _mix_sample_body,
        grid=(DEC_BATCH,),
        in_specs=[full((DEC_BATCH, MAIN_W)),
                  pl.BlockSpec((None, 3 * N_DIL, N_C_HEADS, HEAD_DIM), lambda b: (b, 0, 0, 0)),
                  pl.BlockSpec((1, DEC_BATCH, 2 * W_A), lambda b: (layer, 0, 0)),
                  full((CONV_W, W_A)), full((1, W_B)), full((1, W_B)), full((1, W_B)), full((1, W_B))]
                 + cache_specs,
        out_specs=[full((DEC_BATCH, W_A)), full((DEC_BATCH, W_B)),
                   pl.BlockSpec((None, N_C_HEADS, HEAD_DIM), lambda b: (b, 0, 0)),
                   full((DEC_BATCH, 2 * W_A)), full((DEC_BATCH, W_B))],
        out_shape=[jax.ShapeDtypeStruct((DEC_BATCH, W_A), BF16),
                   jax.ShapeDtypeStruct((DEC_BATCH, W_B), BF16),
                   jax.ShapeDtypeStruct((DEC_BATCH, N_C_HEADS, HEAD_DIM), F32),
                   jax.ShapeDtypeStruct((DEC_BATCH, 2 * W_A), F32),
                   jax.ShapeDtypeStruct((DEC_BATCH, W_B), F32)],
        compiler_params=_params("arbitrary"),
        name="mix_sample",
    )(proj_s, qkv, st, cw, lng, lnb, wsc, bsc, *caches)


def _combine_groups(o0, o1, o2, l0, l1, l2):
    m = jnp.maximum(jnp.maximum(l0, l1), l2)
    e0, e1, e2 = jnp.exp(l0 - m), jnp.exp(l1 - m), jnp.exp(l2 - m)
    return (e0 * o0 + e1 * o1 + e2 * o2) / (e0 + e1 + e2)


def _merge_tail(x, ga, gb, gc, ya, yb, yc, wa_ref, wb_ref, wc_ref, wo_ref, gp_ref):
    mm = (jax.nn.sigmoid(ga) * jnp.dot(ya, wa_ref[...], preferred_element_type=F32)
          + jax.nn.sigmoid(gb) * jnp.dot(yb, wb_ref[...], preferred_element_type=F32)
          + jax.nn.sigmoid(gc) * jnp.dot(yc, wc_ref[...], preferred_element_type=F32))
    r = jnp.dot(mm.astype(BF16), wo_ref[...], preferred_element_type=F32)
    return x + _rms(r, gp_ref[...])


def _merge_body(ga_ref, gb_ref, gc_ref, ya_ref, yb_ref, yc_ref, x_ref,
                ms_ref, yas_ref, ybs_ref, ycs_ref, xs_ref,
                wa_ref, wb_ref, wc_ref, wo_ref, gp_ref, out_ref, outs_ref):
    weights = (wa_ref, wb_ref, wc_ref, wo_ref, gp_ref)
    out_ref[...] = _merge_tail(x_ref[...], ga_ref[...].astype(F32), gb_ref[...].astype(F32),
                               gc_ref[...].astype(F32), ya_ref[...], yb_ref[...], yc_ref[...], *weights)

    @pl.when(pl.program_id(0) == 0)
    def _():
        gates = [ms_ref[:, k * D_MODEL:(k + 1) * D_MODEL] for k in range(3)]
        outs_ref[...] = _merge_tail(xs_ref[...], *gates, yas_ref[...], ybs_ref[...],
                                    ycs_ref[...].astype(BF16), *weights)


def _merge(proj, ya, yb, yc, x, main_s, ya_s, yb_s, yc_s, xs, wa, wb, wc, wo, gp, layer, tm):
    m = x.shape[0]

    def rows(w, k=0):
        return pl.BlockSpec((tm, w), lambda i: (i, k))

    def whole(a):
        return pl.BlockSpec(a.shape, lambda i: (0, 0))

    def resident(k):
        return pl.BlockSpec((k, D_MODEL), lambda i: (0, 0), pipeline_mode=pl.Buffered(1))

    return pl.pallas_call(
        _merge_body,
        grid=(m // tm,),
        in_specs=[rows(D_MODEL, 0), rows(D_MODEL, 1), rows(D_MODEL, 2), rows(W_A), rows(W_B), rows(C_WIDTH),
                  rows(D_MODEL),
                  whole(main_s), whole(ya_s), whole(yb_s), whole(yc_s), whole(xs),
                  resident(W_A), resident(W_B), resident(C_WIDTH), resident(D_MODEL),
                  pl.BlockSpec((None, 1, D_MODEL), lambda i: (layer, 0, 0), pipeline_mode=pl.Buffered(1))],
        out_specs=[rows(D_MODEL), whole(xs)],
        out_shape=[jax.ShapeDtypeStruct((m, D_MODEL), F32), jax.ShapeDtypeStruct(xs.shape, F32)],
        compiler_params=_params("arbitrary"),
        name="merge",
    )(proj, proj, proj, ya, yb, yc, x, main_s, ya_s, yb_s, yc_s, xs, wa, wb, wc, wo, gp)


def _swiglu_part(h, wg_ref, wu_ref, wo_ref):
    gate = jnp.dot(h, wg_ref[...], preferred_element_type=F32)
    up = jnp.dot(h, wu_ref[...], preferred_element_type=F32)
    act = (gate * jax.nn.sigmoid(gate) * up).astype(BF16)
    return jnp.dot(act, wo_ref[...], preferred_element_type=F32)


def _ffn_body(x_ref, g1_ref, wg_ref, wu_ref, wo_ref, g2_ref, xs_ref, *rest, n_cast):
    cast_src = rest[:n_cast]
    out_ref, outs_ref = rest[n_cast:n_cast + 2]
    cast_dst = rest[n_cast + 2:2 * n_cast + 2]
    h_ref, hs_ref = rest[-2:]
    i = pl.program_id(0)
    j = pl.program_id(1)
    last = pl.num_programs(1) - 1

    @pl.when(j == 0)
    def _():
        h_ref[...] = _rms(x_ref[...], g1_ref[...]).astype(BF16)
        out_ref[...] = jnp.zeros_like(out_ref)

    _cast_blocks(cast_src, cast_dst)
    out_ref[...] += _swiglu_part(h_ref[...], wg_ref, wu_ref, wo_ref)

    @pl.when(j == last)
    def _():
        out_ref[...] = x_ref[...] + _rms(out_ref[...], g2_ref[...])

    @pl.when(i == 0)
    def _():
        @pl.when(j == 0)
        def _():
            hs_ref[...] = _rms(xs_ref[...], g1_ref[...]).astype(BF16)
            outs_ref[...] = jnp.zeros_like(outs_ref)

        outs_ref[...] += _swiglu_part(hs_ref[...], wg_ref, wu_ref, wo_ref)

        @pl.when(j == last)
        def _():
            outs_ref[...] = xs_ref[...] + _rms(outs_ref[...], g2_ref[...])


def _ffn(x, xs, g1, w_in, w_out, g2, layer, tm, tf, casts=()):
    m = x.shape[0]
    ns = xs.shape[0]
    nj = D_FF // tf
    cast_specs = [_cast_specs(src, layer + 1, rows, nj, (m // tm) * nj) for src, rows in casts]
    res = pl.pallas_call(
        functools.partial(_ffn_body, n_cast=len(casts)),
        grid=(m // tm, nj),
        in_specs=[
            pl.BlockSpec((tm, D_MODEL), lambda i, j: (i, 0)),
            pl.BlockSpec((None, 1, D_MODEL), lambda i, j: (layer, 0, 0)),
            pl.BlockSpec((D_MODEL, tf), lambda i, j: (0, j)),
            pl.BlockSpec((D_MODEL, tf), lambda i, j: (0, nj + j)),
            pl.BlockSpec((tf, D_MODEL), lambda i, j: (j, 0)),
            pl.BlockSpec((None, 1, D_MODEL), lambda i, j: (layer, 0, 0)),
            pl.BlockSpec((ns, D_MODEL), lambda i, j: (0, 0)),
        ] + [c[0] for c in cast_specs],
        out_specs=[pl.BlockSpec((tm, D_MODEL), lambda i, j: (i, 0)),
                   pl.BlockSpec((ns, D_MODEL), lambda i, j: (0, 0))] + [c[1] for c in cast_specs],
        out_shape=[jax.ShapeDtypeStruct((m, D_MODEL), F32),
                   jax.ShapeDtypeStruct((ns, D_MODEL), F32)] + [c[2] for c in cast_specs],
        scratch_shapes=[pltpu.VMEM((tm, D_MODEL), BF16), pltpu.VMEM((ns, D_MODEL), BF16)],
        compiler_params=_params("arbitrary", "arbitrary"),
        name="ffn",
    )(x, g1, w_in, w_in, w_out, g2, xs, *[src for src, _ in casts])
    return res[0], res[1], res[2:]


TM_INPROJ = 1024
TS_MIX = 512
TM_MERGE = 256
TM_FFN = 512
TF_FFN = 512
CAST_ROWS_FFN_IN = 16
CAST_ROWS_FFN_OUT = 64
CAST_ROWS_W_IN = 16
CAST_ROWS_MERGE = 16


def kernel(x_prompt, x_sample, state_conv, cache_kv_w128, cache_kv_w512, cache_kv_w2048, g_pre_mix, w_in, conv_w, ln_g, ln_b, w_s, b_s, w_a_out, w_b_out, w_c_out, w_o, g_post_mix, g_pre_ffn, w_ffn_in, w_ffn_out, g_post_ffn):
    w_in_l = w_in[0].astype(BF16)
    inproj_casts = ((w_ffn_in, CAST_ROWS_FFN_IN), (w_ffn_out, CAST_ROWS_FFN_OUT),
                    (w_a_out, CAST_ROWS_MERGE), (w_b_out, CAST_ROWS_MERGE), (w_c_out, CAST_ROWS_MERGE),
                    (w_o, CAST_ROWS_MERGE))
    g_pre_mix, g_post_mix, g_pre_ffn, g_post_ffn = (
        g.reshape(DEPTH, 1, D_MODEL) for g in (g_pre_mix, g_post_mix, g_pre_ffn, g_post_ffn))

    tabs_p = _rope_tables(jnp.arange(SEQ, dtype=jnp.int32))
    tabs_s = _rope_tables(jnp.full((DEC_BATCH,), PAST_LEN, dtype=jnp.int32))

    n_keys = DIL_GROUPS[0][0] // DIL_GROUPS[0][1]
    caches = tuple(c.reshape(DEPTH, DEC_BATCH, n_keys, dil, 2, N_C_HEADS, HEAD_DIM)
                   for c, (_, dil) in zip((cache_kv_w128, cache_kv_w512, cache_kv_w2048), DIL_GROUPS))
    st_all = state_conv.reshape(DEPTH, DEC_BATCH, (CONV_W - 1) * W_A)
    bs_t = jnp.swapaxes(b_s, 1, 2)
    wsc = jnp.repeat(w_s[:, :, 0, 0], B_GROUP_W, axis=1)
    bsc = jnp.repeat(b_s[:, :, 0], B_GROUP_W, axis=1)

    xp = x_prompt.reshape(BATCH * SEQ, D_MODEL)
    xs = x_sample.reshape(DEC_BATCH, D_MODEL)
    conv_p, conv_s, vchunk_s = [], [], []
    kv_p = tuple(jnp.zeros((DEPTH, BATCH, min(win, SEQ), 2, N_C_HEADS, HEAD_DIM), F32) for win, _ in DIL_GROUPS)
    kv_s = [[] for _ in DIL_GROUPS]

    def row(a, l):
        return a[l][None, :]

    for l in range(DEPTH):
        proj, qkv, proj_s, qkv_s, wfi_l, wfo_l, wa_l, wb_l, wc_l, wo_l = _inproj(
            xp, xs, g_pre_mix, w_in_l, l, tabs_p, tabs_s, TM_INPROJ, casts=inproj_casts)
        ya, yb, ctail, kv_p = _mix_ab_kv(proj, qkv, conv_w[l], row(ln_g, l), row(ln_b, l), w_s[l], bs_t[l], l, kv_p)
        yc = _attn(qkv)
        conv_p.append(ctail[:, F32_SUBLANES - (CONV_W - 1):])
        ya_s, yb_s, yc_s, cst, vn = _mix_sample(proj_s, qkv_s, st_all, conv_w[l], row(ln_g, l), row(ln_b, l),
                                                row(wsc, l), row(bsc, l), caches, l)
        for g in range(N_DIL):
            k = qkv_s[:, OFF_K + g * C_WIDTH:OFF_K + (g + 1) * C_WIDTH]
            v = qkv_s[:, OFF_V + g * C_WIDTH:OFF_V + (g + 1) * C_WIDTH]
            kv_s[g].append(jnp.stack([k.reshape(DEC_BATCH, 1, N_C_HEADS, HEAD_DIM),
                                      v.reshape(DEC_BATCH, 1, N_C_HEADS, HEAD_DIM)], axis=2))
        conv_s.append(cst.reshape(DEC_BATCH, CONV_W - 1, W_A))
        vchunk_s.append(vn.reshape(DEC_BATCH, 1, W_B))
        xp, xs = _merge(proj, ya, yb, yc, xp, proj_s, ya_s, yb_s, yc_s.reshape(DEC_BATCH, C_WIDTH), xs,
                        wa_l, wb_l, wc_l, wo_l, g_post_mix, l, TM_MERGE)
        next_casts = ((w_in, CAST_ROWS_W_IN),) if l + 1 < DEPTH else ()
        xp, xs, w_in_next = _ffn(xp, xs, g_pre_ffn, wfi_l, wfo_l, g_post_ffn, l, TM_FFN, TF_FFN, casts=next_casts)
        if w_in_next:
            w_in_l = w_in_next[0]

    return (xp.reshape(BATCH, SEQ, D_MODEL), xs.reshape(DEC_BATCH, 1, D_MODEL),
            jnp.stack(conv_p, axis=0),
            kv_p[0], kv_p[1], kv_p[2],
            jnp.stack(conv_s, axis=0),
            jnp.stack(kv_s[0], axis=0), jnp.stack(kv_s[1], axis=0), jnp.stack(kv_s[2], axis=0),
            jnp.stack(vchunk_s, axis=0))
```

```python
import functools

import jax
import jax.numpy as jnp
from jax import lax
from jax.experimental import pallas as pl
from jax.experimental.pallas import tpu as pltpu

D_MODEL = 2048
BATCH = 4
SEQ = 2048
DEPTH = 4
DEC_BATCH = 8
PAST_LEN = 16384
W_A = 1024
CONV_W = 3
W_B = 1024
CHUNK = 128
N_B_GROUPS = 4
B_GROUP_W = W_B // N_B_GROUPS
N_C_HEADS = 4
HEAD_DIM = 128
ROT_DIM = HEAD_DIM // 4
ROPE_THETA = 500000.0
DIL_GROUPS = ((128, 1), (512, 4), (2048, 16))
N_DIL = len(DIL_GROUPS)
QB = 128
C_WIDTH = N_C_HEADS * HEAD_DIM
QKV_W = N_DIL * C_WIDTH
D_FF = ((-(-8 * D_MODEL // 3) + 255) // 256) * 256
IN_WIDTH = 3 * W_A + 2 * W_B + 3 * QKV_W + 3 * D_MODEL
EPS = 1e-6

GATE_W = 3 * D_MODEL
OFF_AB = GATE_W
MAIN_W = GATE_W + 3 * W_A + 2 * W_B
OFF_Q, OFF_K, OFF_V = 0, QKV_W, 2 * QKV_W
ORIG_GATE_OFF = IN_WIDTH - GATE_W

ROPE_ROWS = 256
SOFTMAX_ROWS = 32
MERGE_ROWS = 64
F32_SUBLANES = 8
V7X_VMEM_LIMIT = 60 * 1024 * 1024
NEG_BIG = -1e30

F32 = jnp.float32
BF16 = jnp.bfloat16


def _params(*sem):
    return pltpu.CompilerParams(dimension_semantics=sem, vmem_limit_bytes=V7X_VMEM_LIMIT)


def _rms(x, g):
    return x * lax.rsqrt(jnp.mean(x * x, axis=-1, keepdims=True) + EPS) * g


def _cast_specs(src, layer, rows, steps_per_row_tile, n_steps):
    _, r, c = src.shape
    nblk = r // rows
    assert r % rows == 0 and nblk <= n_steps

    def blk(i, j):
        return jnp.minimum(i * steps_per_row_tile + j, nblk - 1)

    return (pl.BlockSpec((None, rows, c), lambda i, j: (layer, blk(i, j), 0)),
            pl.BlockSpec((rows, c), lambda i, j: (blk(i, j), 0)),
            jax.ShapeDtypeStruct((r, c), BF16))


def _cast_blocks(src_refs, dst_refs):
    for s_ref, d_ref in zip(src_refs, dst_refs):
        d_ref[...] = s_ref[...].astype(BF16)


def _rope(a, c, s1, s2):
    return a * c + pltpu.roll(a, HEAD_DIM - ROT_DIM // 2, 1) * s1 + pltpu.roll(a, ROT_DIM // 2, 1) * s2


def _inproj_body(x_ref, g_ref, w0_ref, w1_ref, c_ref, s1_ref, s2_ref, xs_ref, cs_ref, s1s_ref, s2s_ref, *rest,
                 main_steps, rope_steps, n_cast):
    cast_src = rest[:n_cast]
    main_ref, qkv_ref, mains_ref, qkvs_ref = rest[n_cast:n_cast + 4]
    cast_dst = rest[n_cast + 4:2 * n_cast + 4]
    h_ref, hs_ref = rest[-2:]
    i = pl.program_id(0)
    j = pl.program_id(1)
    tn = w0_ref.shape[1]
    heads = tn // HEAD_DIM
    halves = ((w0_ref, slice(0, tn)), (w1_ref, slice(tn, 2 * tn)))
    with_samples = i == 0

    @pl.when(j == 0)
    def _():
        h_ref[...] = _rms(x_ref[...], g_ref[...]).astype(BF16)

    @pl.when(jnp.logical_and(with_samples, j == 0))
    def _():
        hs_ref[...] = _rms(xs_ref[...], g_ref[...]).astype(BF16)

    @pl.when(j < main_steps)
    def _():
        _cast_blocks(cast_src, cast_dst)
        for w_ref, cols in halves:
            main_ref[:, cols] = jnp.dot(h_ref[...], w_ref[...], preferred_element_type=F32).astype(main_ref.dtype)

        @pl.when(with_samples)
        def _():
            for w_ref, cols in halves:
                mains_ref[j, :, cols] = jnp.dot(hs_ref[...], w_ref[...], preferred_element_type=F32)

    @pl.when(jnp.logical_and(j >= main_steps, j < main_steps + rope_steps))
    def _():
        _cast_blocks(cast_src, cast_dst)
        tm = x_ref.shape[0]
        rc = min(tm, ROPE_ROWS)
        for w_ref, cols in halves:
            for r in range(tm // rc):
                rs = slice(r * rc, (r + 1) * rc)
                acc = jnp.dot(h_ref[rs, :], w_ref[...], preferred_element_type=F32)
                for h in range(heads):
                    qkv_ref[cols.start // HEAD_DIM + h, rs, :] = _rope(
                        acc[:, h * HEAD_DIM:(h + 1) * HEAD_DIM], c_ref[rs, :], s1_ref[rs, :], s2_ref[rs, :])

        @pl.when(with_samples)
        def _():
            for w_ref, cols in halves:
                acc = jnp.dot(hs_ref[...], w_ref[...], preferred_element_type=F32)
                for h in range(heads):
                    sl = slice(cols.start + h * HEAD_DIM, cols.start + (h + 1) * HEAD_DIM)
                    qkvs_ref[j - main_steps, :, sl] = _rope(acc[:, h * HEAD_DIM:(h + 1) * HEAD_DIM],
                                                            cs_ref[...], s1s_ref[...], s2s_ref[...])

    @pl.when(j >= main_steps + rope_steps)
    def _():
        _cast_blocks(cast_src, cast_dst)
        for w_ref, cols in halves:
            acc = jnp.dot(h_ref[...], w_ref[...], preferred_element_type=F32)
            for h in range(heads):
                qkv_ref[cols.start // HEAD_DIM + h] = acc[:, h * HEAD_DIM:(h + 1) * HEAD_DIM]

        @pl.when(with_samples)
        def _():
            for w_ref, cols in halves:
                qkvs_ref[j - main_steps, :, cols] = jnp.dot(hs_ref[...], w_ref[...], preferred_element_type=F32)


def _inproj(x, xs, g, w, layer, tabs, tabs_s, tm, casts=()):
    m = x.shape[0]
    ns = xs.shape[0]
    tn = C_WIDTH
    n_tiles = IN_WIDTH // tn
    rot = ORIG_GATE_OFF // tn
    main_steps = MAIN_W // (2 * tn)
    rope_steps = 2 * QKV_W // (2 * tn)
    qkv_steps = pl.cdiv(3 * QKV_W, 2 * tn)
    steps = main_steps + qkv_steps
    tab_spec = pl.BlockSpec((tm, HEAD_DIM), lambda i, j: (i % (SEQ // tm), 0))
    tab_s_spec = pl.BlockSpec((ns, HEAD_DIM), lambda i, j: (0, 0))

    def w_spec(half):
        return pl.BlockSpec((D_MODEL, tn), lambda i, j: (0, (2 * j + half + rot) % n_tiles))

    def x_tile(i, j):
        return jnp.minimum(i + jnp.where(j >= steps // 2, 1, 0), m // tm - 1)

    cast_specs = [_cast_specs(src, layer, rows, steps, (m // tm) * steps) for src, rows in casts]
    res = pl.pallas_call(
        functools.partial(_inproj_body, main_steps=main_steps, rope_steps=rope_steps, n_cast=len(casts)),
        grid=(m // tm, steps),
        in_specs=[
            pl.BlockSpec((tm, D_MODEL), lambda i, j: (x_tile(i, j), 0)),
            pl.BlockSpec((None, 1, D_MODEL), lambda i, j: (layer, 0, 0)),
            w_spec(0), w_spec(1),
            tab_spec, tab_spec, tab_spec,
            pl.BlockSpec((ns, D_MODEL), lambda i, j: (0, 0)),
            tab_s_spec, tab_s_spec, tab_s_spec,
        ] + [c[0] for c in cast_specs],
        out_specs=[pl.BlockSpec((tm, 2 * tn), lambda i, j: (i, jnp.minimum(j, main_steps - 1))),
                   pl.BlockSpec((2 * tn // HEAD_DIM, tm, HEAD_DIM),
                                lambda i, j: (jnp.maximum(j - main_steps, 0), i, 0)),
                   pl.BlockSpec((main_steps, ns, 2 * tn), lambda i, j: (0, 0, 0)),
                   pl.BlockSpec((qkv_steps, ns, 2 * tn), lambda i, j: (0, 0, 0))]
                  + [c[1] for c in cast_specs],
        out_shape=[jax.ShapeDtypeStruct((m, MAIN_W), BF16),
                   jax.ShapeDtypeStruct((3 * QKV_W // HEAD_DIM, m, HEAD_DIM), F32),
                   jax.ShapeDtypeStruct((main_steps, ns, 2 * tn), F32),
                   jax.ShapeDtypeStruct((qkv_steps, ns, 2 * tn), F32)] + [c[2] for c in cast_specs],
        scratch_shapes=[pltpu.VMEM((tm, D_MODEL), BF16), pltpu.VMEM((ns, D_MODEL), BF16)],
        compiler_params=_params("arbitrary", "arbitrary"),
        name="inproj",
    )(x, g, w, w, *tabs, xs, *tabs_s, *[src for src, _ in casts])
    main_s = jnp.swapaxes(res[2], 0, 1).reshape(ns, MAIN_W)
    qkv_s = jnp.swapaxes(res[3], 0, 1).reshape(ns, qkv_steps * 2 * tn)[:, :3 * QKV_W]
    return (res[0], res[1], main_s, qkv_s, *res[4:])


def _rope_tables(pos):
    inv_freq = ROPE_THETA ** (-jnp.arange(0, ROT_DIM, 2, dtype=jnp.float32) / ROT_DIM)
    ang = pos.astype(jnp.float32)[:, None] * inv_freq[None, :]
    cos, sin = jnp.cos(ang), jnp.sin(ang)
    n = pos.shape[0]
    half = ROT_DIM // 2
    rest = HEAD_DIM - ROT_DIM
    c = jnp.concatenate([cos, cos, jnp.ones((n, rest), F32)], axis=1)
    s1 = jnp.concatenate([-sin, jnp.zeros((n, half + rest), F32)], axis=1)
    s2 = jnp.concatenate([jnp.zeros((n, half), F32), sin, jnp.zeros((n, rest), F32)], axis=1)
    return c, s1, s2


def _mix_ab_body(ab_ref, ac_ref, ax_ref, bu_ref, bv_ref, cw_ref, lng_ref, lnb_ref, ws_ref, bs_ref,
                 ya_ref, yb_ref, cs_ref, zs_ref):
    ts = ab_ref.shape[0]
    lead = F32_SUBLANES

    @pl.when(pl.program_id(1) == 0)
    def _():
        zs_ref[0:lead, :] = jnp.zeros((lead, W_A), F32)

    row = lax.broadcasted_iota(jnp.int32, (CHUNK, CHUNK), 0)
    col = lax.broadcasted_iota(jnp.int32, (CHUNK, CHUNK), 1)
    w_tril = [jnp.where(row >= col, ws_ref[g], 0.0).astype(BF16) for g in range(N_B_GROUPS)]

    for c in range(ts // CHUNK):
        rs = slice(c * CHUNK, (c + 1) * CHUNK)
        at = lead + c * CHUNK

        z = ac_ref[rs, :].astype(F32) * ax_ref[rs, :].astype(F32)
        zs_ref[at:at + CHUNK, :] = z
        z1 = zs_ref[at - 1:at - 1 + CHUNK, :]
        z2 = zs_ref[at - 2:at - 2 + CHUNK, :]
        conv = cw_ref[0:1, :] * z2 + cw_ref[1:2, :] * z1 + cw_ref[2:3, :] * z
        ya_ref[rs, :] = (ab_ref[rs, :].astype(F32) * conv).astype(BF16)

        v = bv_ref[rs, :].astype(F32)
        xc = v - jnp.mean(v, axis=-1, keepdims=True)
        var = jnp.mean(xc * xc, axis=-1, keepdims=True)
        vn = (xc * lax.rsqrt(var + EPS) * lng_ref[...] + lnb_ref[...]).astype(BF16)
        for g in range(N_B_GROUPS):
            gs = slice(g * B_GROUP_W, (g + 1) * B_GROUP_W)
            sg = jnp.dot(w_tril[g], vn[:, gs], preferred_element_type=F32) + bs_ref[:, g:g + 1]
            yb_ref[rs, gs] = (bu_ref[rs, gs].astype(F32) * sg).astype(BF16)

    tail = zs_ref[ts:ts + lead, :]
    zs_ref[0:lead, :] = tail
    cs_ref[0] = tail


def _attn_body(q0, q1, q2, k0, k1, k2, v0, v1, v2, y_ref,
               qc, kc, vc, s_scr, p_scr, inv_scr, o0, o1, o2, l0, l1, l2):
    row = lax.broadcasted_iota(jnp.int32, (QB, 2 * QB), 0)
    col = lax.broadcasted_iota(jnp.int32, (QB, 2 * QB), 1)
    band = jnp.logical_and(col >= row, col <= row + QB)
    band_first = jnp.logical_and(band, col >= QB)
    scale = HEAD_DIM ** -0.5
    nt = (((1,), (1,)), ((), ()))
    zero_blk = jnp.zeros((QB, HEAD_DIM), BF16)
    for (_, dil), q_ref, k_ref, v_ref, o_ref, l_ref in zip(
            DIL_GROUPS, (q0, q1, q2), (k0, k1, k2), (v0, v1, v2), (o0, o1, o2), (l0, l1, l2)):
        L = SEQ // dil
        nblk = L // QB

        def tok_rows(r, c):
            if dil == 1:
                return pl.ds(c * QB, QB)
            return pl.ds(r + c * QB * dil, QB, stride=dil)

        for r in range(dil):
            src = pl.ds(0, L) if dil == 1 else pl.ds(r, L, stride=dil)
            base = r * (L + QB)
            qc[r * L:(r + 1) * L, :] = q_ref[src, :].astype(BF16)
            kc[base:base + QB, :] = zero_blk
            vc[base:base + QB, :] = zero_blk
            kc[base + QB:base + QB + L, :] = k_ref[src, :].astype(BF16)
            vc[base + QB:base + QB + L, :] = v_ref[src, :].astype(BF16)

        for r in range(dil):
            for c in range(nblk):
                t = r * nblk + c
                kk = kc[r * (L + QB) + c * QB:r * (L + QB) + (c + 2) * QB, :]
                s = lax.dot_general(qc[t * QB:(t + 1) * QB, :], kk, nt, preferred_element_type=F32) * scale
                s_scr[t * QB:(t + 1) * QB, :] = jnp.where(band_first if c == 0 else band, s, NEG_BIG)

        for r in range(dil):
            for u in range(L // SOFTMAX_ROWS):
                rows = slice(r * L + u * SOFTMAX_ROWS, r * L + (u + 1) * SOFTMAX_ROWS)
                s = s_scr[rows, :]
                m = jnp.max(s, axis=-1, keepdims=True)
                p = jnp.exp(s - m)
                den = jnp.sum(p, axis=-1, keepdims=True)
                p_scr[rows, :] = p.astype(BF16)
                inv_scr[rows, :] = jnp.broadcast_to(1.0 / den, (SOFTMAX_ROWS, HEAD_DIM))
                tok = (pl.ds(u * SOFTMAX_ROWS, SOFTMAX_ROWS) if dil == 1
                       else pl.ds(r + u * SOFTMAX_ROWS * dil, SOFTMAX_ROWS, stride=dil))
                l_ref[tok, :] = jnp.broadcast_to(m + jnp.log(den), (SOFTMAX_ROWS, HEAD_DIM))

        for r in range(dil):
            for c in range(nblk):
                t = r * nblk + c
                vv = vc[r * (L + QB) + c * QB:r * (L + QB) + (c + 2) * QB, :]
                o = jnp.dot(p_scr[t * QB:(t + 1) * QB, :], vv, preferred_element_type=F32)
                o_ref[tok_rows(r, c), :] = o * inv_scr[t * QB:(t + 1) * QB, :]
    for u in range(SEQ // MERGE_ROWS):
        rows = slice(u * MERGE_ROWS, (u + 1) * MERGE_ROWS)
        y_ref[rows, :] = _combine_groups(o0[rows, :], o1[rows, :], o2[rows, :],
                                         l0[rows, :], l1[rows, :], l2[rows, :]).astype(BF16)


def _attn(proj):
    def seg(off, g):
        base = (off + g * C_WIDTH) // HEAD_DIM
        return pl.BlockSpec((None, SEQ, HEAD_DIM), lambda b, h: (base + h, b, 0))

    specs = [seg(off, g) for off in (OFF_Q, OFF_K, OFF_V) for g in range(N_DIL)]
    return pl.pallas_call(
        _attn_body,
        grid=(BATCH, N_C_HEADS),
        in_specs=specs,
        out_specs=pl.BlockSpec((SEQ, HEAD_DIM), lambda b, h: (b, h)),
        out_shape=jax.ShapeDtypeStruct((BATCH * SEQ, C_WIDTH), BF16),
        scratch_shapes=[pltpu.VMEM((SEQ, HEAD_DIM), BF16),
                        pltpu.VMEM((2 * SEQ, HEAD_DIM), BF16),
                        pltpu.VMEM((2 * SEQ, HEAD_DIM), BF16),
                        pltpu.VMEM((SEQ, 2 * QB), F32),
                        pltpu.VMEM((SEQ, 2 * QB), BF16),
                        pltpu.VMEM((SEQ, HEAD_DIM), F32)]
                       + [pltpu.VMEM((SEQ, HEAD_DIM), F32)] * (2 * N_DIL),
        compiler_params=_params("arbitrary", "arbitrary"),
        name="attn",
    )(*([proj] * (3 * N_DIL)))


def _kv_pack_body(*refs, fixed):
    k_refs, v_refs, out_refs = refs[0:N_DIL], refs[N_DIL:2 * N_DIL], refs[-N_DIL:]
    for k_ref, v_ref, o_ref, is_fixed in zip(k_refs, v_refs, out_refs, fixed):
        def pack(k_ref=k_ref, v_ref=v_ref, o_ref=o_ref):
            for h in range(N_C_HEADS):
                o_ref[:, 0, h, :] = k_ref[h]
                o_ref[:, 1, h, :] = v_ref[h]

        if is_fixed:
            pl.when(pl.program_id(1) == 0)(pack)
        else:
            pack()


N_MIX_IN, N_MIX_OUT = 10, 3


def _mix_ab_kv_body(*refs, fixed):
    n_kv_in = len(refs) - N_MIX_IN - N_MIX_OUT - N_DIL - 1
    mix_in, kv_in = refs[:N_MIX_IN], refs[N_MIX_IN:N_MIX_IN + 2 * N_DIL]
    outs = refs[N_MIX_IN + n_kv_in:-1]
    _mix_ab_body(*mix_in, *outs[:N_MIX_OUT], refs[-1])
    _kv_pack_body(*kv_in, *outs[N_MIX_OUT:], fixed=fixed)


def _mix_ab_kv(main, qkv, cw, lng, lnb, ws, bs_t, layer, prev):
    ts = TS_MIX
    nb = SEQ // ts
    blk = OFF_AB // W_A
    keeps = [min(win, SEQ) for win, _ in DIL_GROUPS]

    def seg(k):
        return pl.BlockSpec((ts, W_A), lambda b, s: (b * nb + s, blk + k))

    def full(shape):
        return pl.BlockSpec(shape, lambda b, s: (0,) * len(shape))

    rows = pl.BlockSpec((ts, W_A), lambda b, s: (b * nb + s, 0))
    mix_in = [seg(0), seg(1), seg(2), seg(3), seg(4),
              full((CONV_W, W_A)), full((1, W_B)), full((1, W_B)),
              full((N_B_GROUPS, CHUNK, CHUNK)), full((CHUNK, N_B_GROUPS))]
    mix_out = [rows, rows, pl.BlockSpec((1, F32_SUBLANES, W_A), lambda b, s: (b, 0, 0))]
    mix_shapes = [jax.ShapeDtypeStruct((BATCH * SEQ, W_A), BF16),
                  jax.ShapeDtypeStruct((BATCH * SEQ, W_B), BF16),
                  jax.ShapeDtypeStruct((BATCH, F32_SUBLANES, W_A), F32)]
    assert len(mix_in) == N_MIX_IN and len(mix_out) == N_MIX_OUT

    def src(off, g):
        keep, col = keeps[g], (off + g * C_WIDTH) // C_WIDTH
        if keep <= ts:
            nb = SEQ // keep
            return pl.BlockSpec((N_C_HEADS, keep, HEAD_DIM), lambda b, s: (col, b * nb + nb - 1, 0))
        nb, first = SEQ // ts, (SEQ - keep) // ts
        return pl.BlockSpec((N_C_HEADS, ts, HEAD_DIM), lambda b, s: (col, b * nb + jnp.maximum(s, first), 0))

    def dst(keep):
        if keep <= ts:
            return pl.BlockSpec((None, None, keep, 2, N_C_HEADS, HEAD_DIM), lambda b, s: (layer, b, 0, 0, 0, 0))
        first = (SEQ - keep) // ts
        return pl.BlockSpec((None, None, ts, 2, N_C_HEADS, HEAD_DIM),
                            lambda b, s: (layer, b, jnp.maximum(s - first, 0), 0, 0, 0))

    in_specs = mix_in + [src(OFF_K, g) for g in range(N_DIL)] + [src(OFF_V, g) for g in range(N_DIL)]
    args = [main] * 5 + [cw, lng, lnb, ws, bs_t] + [qkv] * (2 * N_DIL)
    aliases = {len(in_specs) + g: N_MIX_OUT + g for g in range(N_DIL)}
    in_specs += [pl.BlockSpec(memory_space=pl.ANY)] * N_DIL
    args += list(prev)
    res = pl.pallas_call(
        functools.partial(_mix_ab_kv_body, fixed=tuple(keep <= ts for keep in keeps)),
        grid=(BATCH, nb),
        in_specs=in_specs,
        out_specs=mix_out + [dst(keep) for keep in keeps],
        out_shape=mix_shapes + [jax.ShapeDtypeStruct((DEPTH, BATCH, keep, 2, N_C_HEADS, HEAD_DIM), F32)
                                for keep in keeps],
        input_output_aliases=aliases,
        scratch_shapes=[pltpu.VMEM((ts + F32_SUBLANES, W_A), F32)],
        compiler_params=_params("arbitrary", "arbitrary"),
        name="mix_ab_kv",
    )(*args)
    return res[0], res[1], res[2], res[N_MIX_OUT:]


def _mix_sample_body(p_ref, qkv_ref, st_ref, cw_ref, lng_ref, lnb_ref, wsc_ref, bsc_ref, c0_ref, c1_ref, c2_ref,
                     ya_ref, yb_ref, yc_ref, cs_ref, vn_ref):
    @pl.when(pl.program_id(0) == 0)
    def _():
        def seg(off, w):
            return p_ref[:, off:off + w]

        z = seg(OFF_AB + W_A, W_A) * seg(OFF_AB + 2 * W_A, W_A)
        st0 = st_ref[0, :, 0:W_A]
        st1 = st_ref[0, :, W_A:2 * W_A]
        conv = cw_ref[0:1, :] * st0 + cw_ref[1:2, :] * st1 + cw_ref[2:3, :] * z
        ya_ref[...] = (seg(OFF_AB, W_A) * conv).astype(BF16)
        cs_ref[:, 0:W_A] = st1
        cs_ref[:, W_A:2 * W_A] = z

        v = seg(OFF_AB + 3 * W_A + W_B, W_B)
        xc = v - jnp.mean(v, axis=-1, keepdims=True)
        var = jnp.mean(xc * xc, axis=-1, keepdims=True)
        vn = xc * lax.rsqrt(var + EPS) * lng_ref[...] + lnb_ref[...]
        vn_ref[...] = vn
        yb_ref[...] = (seg(OFF_AB + 3 * W_A, W_B) * (wsc_ref[...] * vn + bsc_ref[...])).astype(BF16)

    scale = HEAD_DIM ** -0.5
    outs, lses = [], []
    for g, c_ref in enumerate((c0_ref, c1_ref, c2_ref)):
        q = qkv_ref[g]
        kn = qkv_ref[N_DIL + g]
        vnew = qkv_ref[2 * N_DIL + g]
        kc = c_ref[:, 0]
        vc = c_ref[:, 1]
        s_c = jnp.sum(q[None] * kc, axis=-1, keepdims=True) * scale
        s_n = jnp.sum(q * kn, axis=-1, keepdims=True) * scale
        m = jnp.maximum(jnp.max(s_c, axis=0), s_n)
        p_c = jnp.exp(s_c - m[None])
        p_n = jnp.exp(s_n - m)
        den = jnp.sum(p_c, axis=0) + p_n
        outs.append((jnp.sum(p_c * vc, axis=0) + p_n * vnew) / den)
        lses.append(m + jnp.log(den))
    yc_ref[...] = _combine_groups(*outs, *lses)


def _mix_sample(proj_s, qkv_s, st, cw, lng, lnb, wsc, bsc, caches, layer):
    def full(shape):
        return pl.BlockSpec(shape, lambda b: (0,) * len(shape))

    n_keys = DIL_GROUPS[0][0] // DIL_GROUPS[0][1]
    cache_specs = [pl.BlockSpec((None, None, n_keys, None, 2, N_C_HEADS, HEAD_DIM),
                                lambda b: (layer, b, 0, 0, 0, 0, 0)) for _ in caches]
    qkv = qkv_s.reshape(DEC_BATCH, 3 * N_DIL, N_C_HEADS, HEAD_DIM)
    return pl.pallas_call(
        _mix_sample_body,
        grid=(DEC_BATCH,),
        in_specs=[full((DEC_BATCH, MAIN_W)),
                  pl.BlockSpec((None, 3 * N_DIL, N_C_HEADS, HEAD_DIM), lambda b: (b, 0, 0, 0)),
                  pl.BlockSpec((1, DEC_BATCH, 2 * W_A), lambda b: (layer, 0, 0)),
                  full((CONV_W, W_A)), full((1, W_B)), full((1, W_B)), full((1, W_B)), full((1, W_B))]
                 + cache_specs,
        out_specs=[full((DEC_BATCH, W_A)), full((DEC_BATCH, W_B)),
                   pl.BlockSpec((None, N_C_HEADS, HEAD_DIM), lambda b: (b, 0, 0)),
                   full((DEC_BATCH, 2 * W_A)), full((DEC_BATCH, W_B))],
        out_shape=[jax.ShapeDtypeStruct((DEC_BATCH, W_A), BF16),
                   jax.ShapeDtypeStruct((DEC_BATCH, W_B), BF16),
                   jax.ShapeDtypeStruct((DEC_BATCH, N_C_HEADS, HEAD_DIM), F32),
                   jax.ShapeDtypeStruct((DEC_BATCH, 2 * W_A), F32),
                   jax.ShapeDtypeStruct((DEC_BATCH, W_B), F32)],
        compiler_params=_params("arbitrary"),
        name="mix_sample",
    )(proj_s, qkv, st, cw, lng, lnb, wsc, bsc, *caches)


def _combine_groups(o0, o1, o2, l0, l1, l2):
    m = jnp.maximum(jnp.maximum(l0, l1), l2)
    e0, e1, e2 = jnp.exp(l0 - m), jnp.exp(l1 - m), jnp.exp(l2 - m)
    return (e0 * o0 + e1 * o1 + e2 * o2) / (e0 + e1 + e2)


def _merge_tail(x, ga, gb, gc, ya, yb, yc, wa_ref, wb_ref, wc_ref, wo_ref, gp_ref):
    mm = (jax.nn.sigmoid(ga) * jnp.dot(ya, wa_ref[...], preferred_element_type=F32)
          + jax.nn.sigmoid(gb) * jnp.dot(yb, wb_ref[...], preferred_element_type=F32)
          + jax.nn.sigmoid(gc) * jnp.dot(yc, wc_ref[...], preferred_element_type=F32))
    r = jnp.dot(mm.astype(BF16), wo_ref[...], preferred_element_type=F32)
    return x + _rms(r, gp_ref[...])


def _merge_body(ga_ref, gb_ref, gc_ref, ya_ref, yb_ref, yc_ref, x_ref,
                ms_ref, yas_ref, ybs_ref, ycs_ref, xs_ref,
                wa_ref, wb_ref, wc_ref, wo_ref, gp_ref, out_ref, outs_ref):
    weights = (wa_ref, wb_ref, wc_ref, wo_ref, gp_ref)
    out_ref[...] = _merge_tail(x_ref[...], ga_ref[...].astype(F32), gb_ref[...].astype(F32),
                               gc_ref[...].astype(F32), ya_ref[...], yb_ref[...], yc_ref[...], *weights)

    @pl.when(pl.program_id(0) == 0)
    def _():
        gates = [ms_ref[:, k * D_MODEL:(k + 1) * D_MODEL] for k in range(3)]
        outs_ref[...] = _merge_tail(xs_ref[...], *gates, yas_ref[...], ybs_ref[...],
                                    ycs_ref[...].astype(BF16), *weights)


def _merge(proj, ya, yb, yc, x, main_s, ya_s, yb_s, yc_s, xs, wa, wb, wc, wo, gp, layer, tm):
    m = x.shape[0]

    def rows(w, k=0):
        return pl.BlockSpec((tm, w), lambda i: (i, k))

    def whole(a):
        return pl.BlockSpec(a.shape, lambda i: (0, 0))

    def resident(k):
        return pl.BlockSpec((k, D_MODEL), lambda i: (0, 0), pipeline_mode=pl.Buffered(1))

    return pl.pallas_call(
        _merge_body,
        grid=(m // tm,),
        in_specs=[rows(D_MODEL, 0), rows(D_MODEL, 1), rows(D_MODEL, 2), rows(W_A), rows(W_B), rows(C_WIDTH),
                  rows(D_MODEL),
                  whole(main_s), whole(ya_s), whole(yb_s), whole(yc_s), whole(xs),
                  resident(W_A), resident(W_B), resident(C_WIDTH), resident(D_MODEL),
                  pl.BlockSpec((None, 1, D_MODEL), lambda i: (layer, 0, 0), pipeline_mode=pl.Buffered(1))],
        out_specs=[rows(D_MODEL), whole(xs)],
        out_shape=[jax.ShapeDtypeStruct((m, D_MODEL), F32), jax.ShapeDtypeStruct(xs.shape, F32)],
        compiler_params=_params("arbitrary"),
        name="merge",
    )(proj, proj, proj, ya, yb, yc, x, main_s, ya_s, yb_s, yc_s, xs, wa, wb, wc, wo, gp)


def _swiglu_part(h, wg_ref, wu_ref, wo_ref):
    gate = jnp.dot(h, wg_ref[...], preferred_element_type=F32)
    up = jnp.dot(h, wu_ref[...], preferred_element_type=F32)
    act = (gate * jax.nn.sigmoid(gate) * up).astype(BF16)
    return jnp.dot(act, wo_ref[...], preferred_element_type=F32)


def _ffn_body(x_ref, g1_ref, wg_ref, wu_ref, wo_ref, g2_ref, xs_ref, *rest, n_cast):
    cast_src = rest[:n_cast]
    out_ref, outs_ref = rest[n_cast:n_cast + 2]
    cast_dst = rest[n_cast + 2:2 * n_cast + 2]
    h_ref, hs_ref = rest[-2:]
    i = pl.program_id(0)
    j = pl.program_id(1)
    last = pl.num_programs(1) - 1

    @pl.when(j == 0)
    def _():
        h_ref[...] = _rms(x_ref[...], g1_ref[...]).astype(BF16)
        out_ref[...] = jnp.zeros_like(out_ref)

    _cast_blocks(cast_src, cast_dst)
    out_ref[...] += _swiglu_part(h_ref[...], wg_ref, wu_ref, wo_ref)

    @pl.when(j == last)
    def _():
        out_ref[...] = x_ref[...] + _rms(out_ref[...], g2_ref[...])

    @pl.when(i == 0)
    def _():
        @pl.when(j == 0)
        def _():
            hs_ref[...] = _rms(xs_ref[...], g1_ref[...]).astype(BF16)
            outs_ref[...] = jnp.zeros_like(outs_ref)

        outs_ref[...] += _swiglu_part(hs_ref[...], wg_ref, wu_ref, wo_ref)

        @pl.when(j == last)
        def _():
            outs_ref[...] = xs_ref[...] + _rms(outs_ref[...], g2_ref[...])


def _ffn(x, xs, g1, w_in, w_out, g2, layer, tm, tf, casts=()):
    m = x.shape[0]
    ns = xs.shape[0]
    nj = D_FF // tf
    cast_specs = [_cast_specs(src, layer + 1, rows, nj, (m // tm) * nj) for src, rows in casts]
    res = pl.pallas_call(
        functools.partial(_ffn_body, n_cast=len(casts)),
        grid=(m // tm, nj),
        in_specs=[
            pl.BlockSpec((tm, D_MODEL), lambda i, j: (i, 0)),
            pl.BlockSpec((None, 1, D_MODEL), lambda i, j: (layer, 0, 0)),
            pl.BlockSpec((D_MODEL, tf), lambda i, j: (0, j)),
            pl.BlockSpec((D_MODEL, tf), lambda i, j: (0, nj + j)),
            pl.BlockSpec((tf, D_MODEL), lambda i, j: (j, 0)),
            pl.BlockSpec((None, 1, D_MODEL), lambda i, j: (layer, 0, 0)),
            pl.BlockSpec((ns, D_MODEL), lambda i, j: (0, 0)),
        ] + [c[0] for c in cast_specs],
        out_specs=[pl.BlockSpec((tm, D_MODEL), lambda i, j: (i, 0)),
                   pl.BlockSpec((ns, D_MODEL), lambda i, j: (0, 0))] + [c[1] for c in cast_specs],
        out_shape=[jax.ShapeDtypeStruct((m, D_MODEL), F32),
                   jax.ShapeDtypeStruct((ns, D_MODEL), F32)] + [c[2] for c in cast_specs],
        scratch_shapes=[pltpu.VMEM((tm, D_MODEL), BF16), pltpu.VMEM((ns, D_MODEL), BF16)],
        compiler_params=_params("arbitrary", "arbitrary"),
        name="ffn",
    )(x, g1, w_in, w_in, w_out, g2, xs, *[src for src, _ in casts])
    return res[0], res[1], res[2:]


TM_INPROJ = 1024
TS_MIX = 512
TM_MERGE = 256
TM_FFN = 512
TF_FFN = 512
CAST_ROWS_FFN_IN = 16
CAST_ROWS_FFN_OUT = 64
CAST_ROWS_W_IN = 16
CAST_ROWS_MERGE = 16


def kernel(x_prompt, x_sample, state_conv, cache_kv_w128, cache_kv_w512, cache_kv_w2048, g_pre_mix, w_in, conv_w, ln_g, ln_b, w_s, b_s, w_a_out, w_b_out, w_c_out, w_o, g_post_mix, g_pre_ffn, w_ffn_in, w_ffn_out, g_post_ffn):
    w_in_l = w_in[0].astype(BF16)
    inproj_casts = ((w_ffn_in, CAST_ROWS_FFN_IN), (w_ffn_out, CAST_ROWS_FFN_OUT),
                    (w_a_out, CAST_ROWS_MERGE), (w_b_out, CAST_ROWS_MERGE), (w_c_out, CAST_ROWS_MERGE),
                    (w_o, CAST_ROWS_MERGE))
    g_pre_mix, g_post_mix, g_pre_ffn, g_post_ffn = (
        g.reshape(DEPTH, 1, D_MODEL) for g in (g_pre_mix, g_post_mix, g_pre_ffn, g_post_ffn))

    tabs_p = _rope_tables(jnp.arange(SEQ, dtype=jnp.int32))
    tabs_s = _rope_tables(jnp.full((DEC_BATCH,), PAST_LEN, dtype=jnp.int32))

    n_keys = DIL_GROUPS[0][0] // DIL_GROUPS[0][1]
    caches = tuple(c.reshape(DEPTH, DEC_BATCH, n_keys, dil, 2, N_C_HEADS, HEAD_DIM)
                   for c, (_, dil) in zip((cache_kv_w128, cache_kv_w512, cache_kv_w2048), DIL_GROUPS))
    st_all = state_conv.reshape(DEPTH, DEC_BATCH, (CONV_W - 1) * W_A)
    bs_t = jnp.swapaxes(b_s, 1, 2)
    wsc = jnp.repeat(w_s[:, :, 0, 0], B_GROUP_W, axis=1)
    bsc = jnp.repeat(b_s[:, :, 0], B_GROUP_W, axis=1)

    xp = x_prompt.reshape(BATCH * SEQ, D_MODEL)
    xs = x_sample.reshape(DEC_BATCH, D_MODEL)
    conv_p, conv_s, vchunk_s = [], [], []
    kv_p = tuple(jnp.zeros((DEPTH, BATCH, min(win, SEQ), 2, N_C_HEADS, HEAD_DIM), F32) for win, _ in DIL_GROUPS)
    kv_s = [[] for _ in DIL_GROUPS]

    def row(a, l):
        return a[l][None, :]

    for l in range(DEPTH):
        proj, qkv, proj_s, qkv_s, wfi_l, wfo_l, wa_l, wb_l, wc_l, wo_l = _inproj(
            xp, xs, g_pre_mix, w_in_l, l, tabs_p, tabs_s, TM_INPROJ, casts=inproj_casts)
        ya, yb, ctail, kv_p = _mix_ab_kv(proj, qkv, conv_w[l], row(ln_g, l), row(ln_b, l), w_s[l], bs_t[l], l, kv_p)
        yc = _attn(qkv)
        conv_p.append(ctail[:, F32_SUBLANES - (CONV_W - 1):])
        ya_s, yb_s, yc_s, cst, vn = _mix_sample(proj_s, qkv_s, st_all, conv_w[l], row(ln_g, l), row(ln_b, l),
                                                row(wsc, l), row(bsc, l), caches, l)
        for g in range(N_DIL):
            k = qkv_s[:, OFF_K + g * C_WIDTH:OFF_K + (g + 1) * C_WIDTH]
            v = qkv_s[:, OFF_V + g * C_WIDTH:OFF_V + (g + 1) * C_WIDTH]
            kv_s[g].append(jnp.stack([k.reshape(DEC_BATCH, 1, N_C_HEADS, HEAD_DIM),
                                      v.reshape(DEC_BATCH, 1, N_C_HEADS, HEAD_DIM)], axis=2))
        conv_s.append(cst.reshape(DEC_BATCH, CONV_W - 1, W_A))
        vchunk_s.append(vn.reshape(DEC_BATCH, 1, W_B))
        xp, xs = _merge(proj, ya, yb, yc, xp, proj_s, ya_s, yb_s, yc_s.reshape(DEC_BATCH, C_WIDTH), xs,
                        wa_l, wb_l, wc_l, wo_l, g_post_mix, l, TM_MERGE)
        next_casts = ((w_in, CAST_ROWS_W_IN),) if l + 1 < DEPTH else ()
        xp, xs, w_in_next = _ffn(xp, xs, g_pre_ffn, wfi_l, wfo_l, g_post_ffn, l, TM_FFN, TF_FFN, casts=next_casts)
        if w_in_next:
            w_in_l = w_in_next[0]

    return (xp.reshape(BATCH, SEQ, D_MODEL), xs.reshape(DEC_BATCH, 1, D_MODEL),
            jnp.stack(conv_p, axis=0),
            kv_p[0], kv_p[1], kv_p[2],
            jnp.stack(conv_s, axis=0),
            jnp.stack(kv_s[0], axis=0), jnp.stack(kv_s[1], axis=0), jnp.stack(kv_s[2], axis=0),
            jnp.stack(vchunk_s, axis=0))
```
